```python
import jax
import jax.numpy as jnp
from jax import lax
import numpy as np


D_MODEL = 1024
BATCH = 2
SEQ = 8192
DEPTH = 4

MIX_DIM = D_MODEL
ATTN_DIM = D_MODEL // 2
ATTN_HEAD_DIM = 64
ATTN_HEADS = ATTN_DIM // ATTN_HEAD_DIM
DILATED_PATTERNS = ((128, 1), (512, 4), (2048, 16))
ATTN_BLOCK = 128
ALIBI_MAX_EXP = 8.0
HGRN_DIM = MIX_DIM - ATTN_DIM
HGRN_EXPAND = 128
HGRN_HEADS = HGRN_DIM // HGRN_EXPAND
HGRN_CHUNK = 64
IN_PROJ_DIM = 3 * ATTN_DIM + 4 * HGRN_DIM
N_GROUPS = 4
EXPERTS_PER_GROUP = 8
N_EXPERTS = N_GROUPS * EXPERTS_PER_GROUP
TOP_K = 2
D_EXPERT = D_MODEL // 2
MOE_BLOCK = 128
NORM_EPS = 1e-6

kernel_name = 'hybrid_dilated_attn_hgrn2_hmoe_adaln'


def rms_norm(x, g):
    xf = x.astype(jnp.float32)
    y = xf * lax.rsqrt(jnp.mean(xf * xf, axis=-1, keepdims=True) + NORM_EPS)
    return (y * g.astype(jnp.float32)).astype(x.dtype)


def alibi_slopes(n_heads):
    return jnp.exp2(-ALIBI_MAX_EXP * jnp.arange(1, n_heads + 1, dtype=jnp.float32) / n_heads)


def dilated_window_attention(q, k, v, window, dilation, slopes):
    B, H, S, E = q.shape
    steps = window // dilation
    span = dilation * ATTN_BLOCK
    s_pad = -(-S // span) * span
    L = s_pad // dilation
    nb = L // ATTN_BLOCK

    def strided_blocks(t):
        t = jnp.pad(t, ((0, 0), (0, 0), (0, s_pad - S), (0, 0)))
        t = t.reshape(B, H, L, dilation, E).transpose(0, 1, 3, 2, 4)
        return t.reshape(B, H, dilation, nb, ATTN_BLOCK, E)

    def with_prev(t):
        prev = jnp.pad(t[:, :, :, :-1], ((0, 0), (0, 0), (0, 0), (1, 0), (0, 0), (0, 0)))
        return jnp.concatenate([prev, t], axis=-2)

    qb = strided_blocks(q)
    kb = with_prev(strided_blocks(k))
    vb = with_prev(strided_blocks(v))
    s = jnp.einsum('bhrnqe,bhrnke->bhrnqk', qb, kb).astype(jnp.float32) * (E ** -0.5)
    qi = jnp.arange(ATTN_BLOCK)[:, None]
    kj = jnp.arange(2 * ATTN_BLOCK)[None, :]
    dist = ATTN_BLOCK + qi - kj
    blk = jnp.arange(nb)[:, None, None]
    valid = (dist >= 0) & (dist <= steps) & ((blk > 0) | (kj >= ATTN_BLOCK))
    bias = -slopes[:, None, None] * (dist * dilation).astype(jnp.float32)
    s = jnp.where(valid[None, None, None], s + bias[None, :, None, None], -jnp.inf)
    m = jnp.max(s, axis=-1, keepdims=True)
    p = jnp.exp(s - m)
    l = jnp.sum(p, axis=-1, keepdims=True)
    o = jnp.einsum('bhrnqk,bhrnke->bhrnqe', p, vb.astype(jnp.float32)) / l
    lse = (m + jnp.log(l))[..., 0]
    o = o.reshape(B, H, dilation, L, E).transpose(0, 1, 3, 2, 4).reshape(B, H, s_pad, E)[:, :, :S]
    lse = lse.reshape(B, H, dilation, L).transpose(0, 1, 3, 2).reshape(B, H, s_pad)[:, :, :S]
    return o, lse


def hgrn2_chunk_scan(q, k, v, log_f):
    B, H, S, DK = q.shape
    DV = v.shape[-1]
    nc = S // HGRN_CHUNK
    causal = jnp.tril(jnp.ones((HGRN_CHUNK, HGRN_CHUNK), dtype=bool))

    def to_chunks(t):
        return t.astype(jnp.float32).reshape(B, H, nc, HGRN_CHUNK, t.shape[-1]).transpose(2, 0, 1, 3, 4)

    def step(state, inp):
        qc, kc, vc, lc = inp
        b = jnp.cumsum(lc, axis=-2)
        o_inter = jnp.einsum('bhtk,bhkv->bhtv', qc * jnp.exp(b), state)
        diff = b[:, :, :, None, :] - b[:, :, None, :, :]
        decay = jnp.where(causal[:, :, None], jnp.exp(jnp.minimum(diff, 0.0)), 0.0)
        a = jnp.einsum('bhtk,bhsk,bhtsk->bhts', qc, kc, decay)
        o_intra = jnp.einsum('bhts,bhsv->bhtv', a, vc)
        b_last = b[:, :, -1:, :]
        new_state = (jnp.exp(b_last[:, :, 0, :])[..., None] * state
                     + jnp.einsum('bhsk,bhsv->bhkv', kc * jnp.exp(b_last - b), vc))
        return new_state, o_inter + o_intra

    state0 = jnp.zeros((B, H, DK, DV), jnp.float32)
    _, o = lax.scan(step, state0, (to_chunks(q), to_chunks(k), to_chunks(v), to_chunks(log_f)))
    return o.transpose(1, 2, 0, 3, 4).reshape(B, H, S, DV)


def hybrid_mixer(h, w_in, attn_norm_g, lb, hgrn_norm_g, w_out):
    B, S, _ = h.shape
    proj = h @ w_in
    cuts = [ATTN_DIM, 2 * ATTN_DIM, 3 * ATTN_DIM, 3 * ATTN_DIM + HGRN_DIM,
            3 * ATTN_DIM + 2 * HGRN_DIM, 3 * ATTN_DIM + 3 * HGRN_DIM]
    a_q, a_k, a_v, r_q, r_f, r_i, r_g = jnp.split(proj, cuts, axis=-1)

    def heads(t, n, dh):
        return t.reshape(B, S, n, dh).transpose(0, 2, 1, 3)

    slopes = alibi_slopes(ATTN_HEADS)
    qh = heads(a_q, ATTN_HEADS, ATTN_HEAD_DIM)
    kh = heads(a_k, ATTN_HEADS, ATTN_HEAD_DIM)
    vh = heads(a_v, ATTN_HEADS, ATTN_HEAD_DIM)
    outs = []
    lses = []
    for window, dilation in DILATED_PATTERNS:
        o, lse = dilated_window_attention(qh, kh, vh, window, dilation, slopes)
        outs.append(o)
        lses.append(lse)
    wts = jax.nn.softmax(jnp.stack(lses), axis=0)
    attn = jnp.sum(wts[..., None] * jnp.stack(outs), axis=0)
    attn = attn.transpose(0, 2, 1, 3).reshape(B, S, ATTN_DIM)
    attn = rms_norm(attn, attn_norm_g).astype(h.dtype)

    z = heads(r_f, HGRN_HEADS, HGRN_EXPAND).astype(jnp.float32)
    lb_h = jnp.maximum(lb.astype(jnp.float32), 0.0).reshape(HGRN_HEADS, 1, HGRN_EXPAND)
    log_f = jnp.logaddexp(jnp.log(lb_h), jnp.log1p(-lb_h) + jax.nn.log_sigmoid(z))
    k_in = (1.0 - lb_h) * jax.nn.sigmoid(-z)
    rec = hgrn2_chunk_scan(heads(r_q, HGRN_HEADS, HGRN_EXPAND), k_in,
                           heads(r_i, HGRN_HEADS, HGRN_EXPAND), log_f)
    rec = rms_norm(rec.transpose(0, 2, 1, 3), hgrn_norm_g.reshape(HGRN_HEADS, HGRN_EXPAND)).reshape(B, S, HGRN_DIM)
    rec = (rec * jax.nn.silu(r_g.astype(jnp.float32))).astype(h.dtype)

    return jnp.concatenate([attn, rec], axis=-1) @ w_out


def hierarchical_moe(h, wg, bg, we, be, w1, w3, w2):
    B, S, D = h.shape
    n_tok = B * S
    xf = h.reshape(n_tok, D)
    group_logits = (xf @ wg + bg).astype(jnp.float32)
    group_idx = jnp.argmax(group_logits, axis=-1)
    group_prob = jnp.take_along_axis(jax.nn.softmax(group_logits, axis=-1), group_idx[:, None], axis=-1)
    expert_logits = (xf @ we + be).astype(jnp.float32).reshape(n_tok, N_GROUPS, EXPERTS_PER_GROUP)
    local_logits = jnp.take_along_axis(expert_logits, group_idx[:, None, None], axis=1)[:, 0]
    top_val, top_idx = lax.top_k(local_logits, TOP_K)
    gate = jax.nn.softmax(top_val, axis=-1) * group_prob
    expert_id = (group_idx[:, None] * EXPERTS_PER_GROUP + top_idx).reshape(-1).astype(jnp.int32)
    gate_flat = gate.reshape(-1)
    token_id = jnp.repeat(jnp.arange(n_tok, dtype=jnp.int32), TOP_K)
    n_assign = n_tok * TOP_K
    order = jnp.argsort(expert_id)
    sorted_e = expert_id[order]
    counts = jnp.bincount(expert_id, length=N_EXPERTS)
    padded = (counts + MOE_BLOCK - 1) // MOE_BLOCK * MOE_BLOCK
    start = jnp.cumsum(counts) - counts
    pend = jnp.cumsum(padded)
    pstart = pend - padded
    dest = pstart[sorted_e] + jnp.arange(n_assign, dtype=jnp.int32) - start[sorted_e]
    n_rows = -(-n_assign // MOE_BLOCK) * MOE_BLOCK + N_EXPERTS * MOE_BLOCK
    n_blocks = n_rows // MOE_BLOCK
    row_token = jnp.full((n_rows,), n_tok, jnp.int32).at[dest].set(token_id[order])
    row_gate = jnp.zeros((n_rows,), jnp.float32).at[dest].set(gate_flat[order])
    block_expert = jnp.minimum(
        jnp.searchsorted(pend, jnp.arange(n_blocks, dtype=pend.dtype) * MOE_BLOCK, side='right'), N_EXPERTS - 1)
    x_rows = jnp.concatenate([xf, jnp.zeros((1, D), xf.dtype)], axis=0)[row_token].reshape(n_blocks, MOE_BLOCK, D)

    def expert_block(args):
        xb, e = args
        return (jax.nn.silu(xb @ w1[e]) * (xb @ w3[e])) @ w2[e]

    y_rows = lax.map(expert_block, (x_rows, block_expert)).reshape(n_rows, D)
    y = jax.ops.segment_sum(y_rows.astype(jnp.float32) * row_gate[:, None], row_token,
                            num_segments=n_tok + 1)[:n_tok]
    return y.astype(h.dtype).reshape(B, S, D)


def setup_inputs(seed: int = 0) -> dict:
    key = jax.random.key(seed)
    ks = jax.random.split(key, 20)

    def nrm(k, shape, scale):
        return jax.random.normal(k, shape, jnp.float32) * scale

    return {
        'x': nrm(ks[0], (BATCH, SEQ, D_MODEL), 1.0),
        'c': nrm(ks[1], (BATCH, D_MODEL), 1.0),
        'w_ada': nrm(ks[2], (DEPTH, D_MODEL, 6 * D_MODEL), 0.5 * D_MODEL ** -0.5),
        'b_ada': nrm(ks[3], (DEPTH, 6 * D_MODEL), 0.02),
        'norm1_g': 1.0 + nrm(ks[4], (DEPTH, D_MODEL), 0.02),
        'w_in': nrm(ks[5], (DEPTH, D_MODEL, IN_PROJ_DIM), D_MODEL ** -0.5),
        'attn_norm_g': 1.0 + nrm(ks[6], (DEPTH, ATTN_DIM), 0.02),
        'hgrn_lb_logits': nrm(ks[7], (DEPTH, HGRN_DIM), 0.1),
        'hgrn_norm_g': 1.0 + nrm(ks[8], (DEPTH, HGRN_DIM), 0.02),
        'w_out': nrm(ks[9], (DEPTH, MIX_DIM, D_MODEL), MIX_DIM ** -0.5),
        'norm2_g': 1.0 + nrm(ks[10], (DEPTH, D_MODEL), 0.02),
        'router_group_w': nrm(ks[11], (DEPTH, D_MODEL, N_GROUPS), D_MODEL ** -0.5),
        'router_group_b': nrm(ks[12], (DEPTH, N_GROUPS), 0.01),
        'router_expert_w': nrm(ks[13], (DEPTH, D_MODEL, N_EXPERTS), D_MODEL ** -0.5),
        'router_expert_b': nrm(ks[14], (DEPTH, N_EXPERTS), 0.01),
        'moe_w1': nrm(ks[15], (DEPTH, N_EXPERTS, D_MODEL, D_EXPERT), D_MODEL ** -0.5),
        'moe_w3': nrm(ks[16], (DEPTH, N_EXPERTS, D_MODEL, D_EXPERT), D_MODEL ** -0.5),
        'moe_w2': nrm(ks[17], (DEPTH, N_EXPERTS, D_EXPERT, D_MODEL), D_EXPERT ** -0.5),
        'final_g': 1.0 + nrm(ks[18], (D_MODEL,), 0.02),
    }


def reference(x, c, w_ada, b_ada, norm1_g, w_in, attn_norm_g, hgrn_lb_logits, hgrn_norm_g, w_out,
              norm2_g, router_group_w, router_group_b, router_expert_w, router_expert_b,
              moe_w1, moe_w3, moe_w2, final_g):
    lb_w = jax.nn.softmax(hgrn_lb_logits.astype(jnp.float32), axis=0)
    lower_bounds = jnp.cumsum(lb_w, axis=0) - lb_w[0:1]
    c_act = jax.nn.silu(c)
    for layer in range(DEPTH):
        mod = c_act @ w_ada[layer] + b_ada[layer]
        shift1, scale1, gate1, shift2, scale2, gate2 = jnp.split(mod[:, None, :], 6, axis=-1)
        h = rms_norm(x, norm1_g[layer]) * (1.0 + scale1) + shift1
        x = x + gate1 * hybrid_mixer(h, w_in[layer], attn_norm_g[layer], lower_bounds[layer],
                                     hgrn_norm_g[layer], w_out[layer])
        h = rms_norm(x, norm2_g[layer]) * (1.0 + scale2) + shift2
        x = x + gate2 * hierarchical_moe(h, router_group_w[layer], router_group_b[layer],
                                         router_expert_w[layer], router_expert_b[layer],
                                         moe_w1[layer], moe_w3[layer], moe_w2[layer])
    return rms_norm(x, final_g)
```

```python
import functools

import numpy as np
import jax
import jax.numpy as jnp
from jax import lax
from jax.experimental import pallas as pl
from jax.experimental.pallas import tpu as pltpu

F32 = jnp.float32
BF16 = jnp.bfloat16

ATTN_HEAD_DIM = 64
ATTN_BLOCK = 128
DILATIONS = (1, 4, 16)
ALIBI_MAX_EXP = 8.0
HGRN_EXPAND = 128
HGRN_CHUNK = 128
N_GROUPS = 4
EXPERTS_PER_GROUP = 8
N_EXPERTS = N_GROUPS * EXPERTS_PER_GROUP
TOP_K = 2
NORM_EPS = 1e-6

LANES = 128
ATTN_SUPER = ATTN_BLOCK * max(DILATIONS)
ROW_TILE = 512
HGRN_TILE = 2048
MOE_TILE = 256
COMBINE_TILE = 256
VMEM_LIMIT = 56 << 20


def _params(semantics):
    return pltpu.CompilerParams(dimension_semantics=semantics, vmem_limit_bytes=VMEM_LIMIT)


def _rms(x, g):
    return x * lax.rsqrt(jnp.mean(x * x, axis=-1, keepdims=True) + NORM_EPS) * g


def _dot_nt(a, b):
    return lax.dot_general(a, b, (((1,), (1,)), ((), ())), preferred_element_type=F32)


def _dot_tn(a, b):
    return lax.dot_general(a, b, (((0,), (0,)), ((), ())), preferred_element_type=F32)


def _ada_kernel(c_ref, w_ref, b_ref, o_ref):
    c = c_ref[...]
    ca = c * jax.nn.sigmoid(c)
    o_ref[0] = jnp.dot(ca, w_ref[0], precision=lax.Precision.HIGHEST, preferred_element_type=F32) + b_ref[0]


def _adaln(c, w_ada, b_ada):
    depth, d, n6 = w_ada.shape
    bsz = c.shape[0]
    rows = 8
    cp = jnp.zeros((rows, d), F32).at[:bsz].set(c)
    tn = 1536
    out = pl.pallas_call(
        _ada_kernel,
        grid=(depth, n6 // tn),
        in_specs=[
            pl.BlockSpec((rows, d), lambda l, j: (0, 0)),
            pl.BlockSpec((1, d, tn), lambda l, j: (l, 0, j)),
            pl.BlockSpec((1, 1, tn), lambda l, j: (l, 0, j)),
        ],
        out_specs=pl.BlockSpec((1, rows, tn), lambda l, j: (l, 0, j)),
        out_shape=jax.ShapeDtypeStruct((depth, rows, n6), F32),
        compiler_params=_params(("parallel", "parallel")),
        name="adaln",
    )(cp, w_ada, b_ada.reshape(depth, 1, n6))
    return out[:, :bsz].reshape(depth, bsz, 6, d)


def _in_kernel(x_ref, mod_ref, g_ref, w_ref, *rest):
    outs, hb = rest[:-1], rest[-1]
    mod = mod_ref[0]
    h = _rms(x_ref[...], g_ref[...]) * (1.0 + mod[1:2, :]) + mod[0:1, :]
    hb[...] = h.astype(BF16)
    width = outs[0].shape[1]
    for i, o in enumerate(outs):
        o[...] = jnp.dot(hb[...], w_ref[:, i * width:(i + 1) * width], preferred_element_type=F32)


def _in_proj(x2, mod_l, g, w_in_bf, seq):
    n, d = x2.shape
    n_out = w_in_bf.shape[1]
    width = 512
    tm = ROW_TILE
    per_batch = seq // tm
    out_sds = jax.ShapeDtypeStruct((n, width), F32)
    return pl.pallas_call(
        _in_kernel,
        grid=(n // tm,),
        in_specs=[
            pl.BlockSpec((tm, d), lambda i: (i, 0)),
            pl.BlockSpec((1, 6, d), lambda i: (i // per_batch, 0, 0)),
            pl.BlockSpec((1, d), lambda i: (0, 0)),
            pl.BlockSpec((d, n_out), lambda i: (0, 0)),
        ],
        out_specs=[pl.BlockSpec((tm, width), lambda i: (i, 0))] * (n_out // width),
        out_shape=[out_sds] * (n_out // width),
        scratch_shapes=[pltpu.VMEM((tm, d), BF16)],
        compiler_params=_params(("parallel",)),
        name="in_proj",
    )(x2, mod_l, g.reshape(1, d), w_in_bf)


def _attn_bias(n_heads):
    qi = np.arange(ATTN_BLOCK)[:, None]
    kj = np.arange(2 * ATTN_BLOCK)[None, :]
    dist = ATTN_BLOCK + qi - kj
    valid = (dist >= 0) & (dist <= ATTN_BLOCK)
    slopes = np.exp2(-ALIBI_MAX_EXP * np.arange(1, n_heads + 1, dtype=np.float32) / n_heads)
    out = np.empty((len(DILATIONS), n_heads, ATTN_BLOCK, 2 * ATTN_BLOCK), np.float32)
    for p, dil in enumerate(DILATIONS):
        b = -slopes[:, None, None] * (dist * dil).astype(np.float32)[None]
        out[p] = np.where(valid[None], b, -np.inf)
    return jnp.asarray(out)


def _attn_kernel(q_ref, kc_ref, kp_ref, vc_ref, vp_ref, bias_ref, o_ref, kk, vv, oscr, lscr):
    n = pl.program_id(2)
    sb = q_ref.shape[0]
    kk[0:sb, :] = kp_ref[...]
    kk[sb:, :] = kc_ref[...]
    vv[0:sb, :] = vp_ref[...]
    vv[sb:, :] = vc_ref[...]
    is_lo = lax.broadcasted_iota(jnp.int32, (ATTN_BLOCK, LANES), 1) < ATTN_HEAD_DIM
    prev_half = lax.broadcasted_iota(jnp.int32, (ATTN_BLOCK, 2 * ATTN_BLOCK), 1) < ATTN_BLOCK
    ones = jnp.ones((2 * ATTN_BLOCK, LANES), BF16)
    scale = ATTN_HEAD_DIM ** -0.5

    for p, dil in enumerate(DILATIONS):
        nb = (sb // ATTN_BLOCK) // dil
        span = ATTN_BLOCK * dil

        def rows(start, dil=dil):
            if dil == 1:
                return pl.ds(pl.multiple_of(start, ATTN_BLOCK), ATTN_BLOCK)
            return pl.ds(start, ATTN_BLOCK, stride=dil)

        def body(it, carry, p=p, nb=nb, span=span, rows=rows):
            r = it // nb
            j = it % nb
            start = j * span + r
            qb = q_ref[rows(start), :] * scale
            kcat = jnp.concatenate([kk[rows(sb + start - span), :], kk[rows(sb + start), :]], axis=0).astype(BF16)
            vcat = jnp.concatenate([vv[rows(sb + start - span), :], vv[rows(sb + start), :]], axis=0).astype(BF16)
            vaug = jnp.concatenate([vcat, ones], axis=1)
            no_prev = jnp.logical_and(jnp.logical_and(n == 0, j == 0), prev_half)
            acc_o = None
            acc_l = None
            for hh in range(2):
                sel = is_lo if hh == 0 else jnp.logical_not(is_lo)
                qm = jnp.where(sel, qb, 0.0).astype(BF16)
                s = _dot_nt(qm, kcat) + bias_ref[p, hh]
                s = jnp.where(no_prev, -jnp.inf, s)
                m = jnp.max(s, axis=-1, keepdims=True)
                pe = jnp.exp(s - m).astype(BF16)
                of = jnp.dot(pe, vaug, preferred_element_type=F32)
                den = of[:, LANES:]
                on = of[:, :LANES] / den
                lse = m + jnp.log(den)
                acc_o = on if hh == 0 else jnp.where(sel, on, acc_o)
                acc_l = lse if hh == 0 else jnp.where(sel, lse, acc_l)
            oscr[p, rows(start), :] = acc_o
            lscr[p, rows(start), :] = acc_l
            return carry

        lax.fori_loop(0, sb // ATTN_BLOCK, body, 0)

    ct = 256

    def combine(t, carry):
        rs = pl.ds(pl.multiple_of(t * ct, ct), ct)
        l0, l1, l2 = lscr[0, rs, :], lscr[1, rs, :], lscr[2, rs, :]
        mx = jnp.maximum(jnp.maximum(l0, l1), l2)
        w0, w1, w2 = jnp.exp(l0 - mx), jnp.exp(l1 - mx), jnp.exp(l2 - mx)
        num = w0 * oscr[0, rs, :] + w1 * oscr[1, rs, :] + w2 * oscr[2, rs, :]
        o_ref[rs, :] = num / (w0 + w1 + w2)
        return carry

    lax.fori_loop(0, sb // ct, combine, 0)


def _attention(aq, ak, av, bias, bsz, seq):
    n, width = aq.shape
    sb = ATTN_SUPER
    nsb = seq // sb
    pairs = width // LANES
    blk = (sb, LANES)
    cur = lambda b, h, t: (b * nsb + t, h)
    prev = lambda b, h, t: (b * nsb + jnp.maximum(t - 1, 0), h)
    return pl.pallas_call(
        _attn_kernel,
        grid=(bsz, pairs, nsb),
        in_specs=[
            pl.BlockSpec(blk, cur),
            pl.BlockSpec(blk, cur),
            pl.BlockSpec(blk, prev),
            pl.BlockSpec(blk, cur),
            pl.BlockSpec(blk, prev),
            pl.BlockSpec((len(DILATIONS), 2, ATTN_BLOCK, 2 * ATTN_BLOCK), lambda b, h, t: (0, h, 0, 0)),
        ],
        out_specs=pl.BlockSpec(blk, cur),
        out_shape=jax.ShapeDtypeStruct((n, width), F32),
        scratch_shapes=[
            pltpu.VMEM((2 * sb, LANES), F32),
            pltpu.VMEM((2 * sb, LANES), F32),
            pltpu.VMEM((len(DILATIONS), sb, LANES), F32),
            pltpu.VMEM((len(DILATIONS), sb, LANES), F32),
        ],
        compiler_params=_params(("parallel", "parallel", "arbitrary")),
        name="dilated_attn",
    )(aq, ak, ak, av, av, bias)


def _hgrn_consts():
    c = HGRN_CHUNK
    t = np.arange(c)[:, None]
    u = np.arange(c)[None, :]
    blocks = [u <= t, u > t]
    masks = []
    m = c // 2
    while m >= 1:
        mid = (t // (2 * m)) * (2 * m) + m
        is_q = t >= mid
        blocks.append((is_q & (u >= mid) & (u <= t)) | (~is_q & (u > t) & (u < mid)))
        masks.append(((t // (2 * m)) == (u // (2 * m))) & (t % (2 * m) >= m) & (u % (2 * m) < m))
        m //= 2
    masks.append(t == u)
    wall = np.concatenate(blocks, axis=0).astype(np.float32)
    return jnp.asarray(wall, BF16), jnp.asarray(np.stack(masks).astype(np.float32))


def _hgrn_kernel(q_ref, z_ref, v_ref, g_ref, lb_ref, gn_ref, wall_ref, msk_ref, o_ref, st):
    c = HGRN_CHUNK

    @pl.when(pl.program_id(2) == 0)
    def _():
        st[...] = jnp.zeros_like(st)

    lb = jnp.maximum(lb_ref[...], 0.0)
    log_lb = jnp.log(lb)
    log_1m = jnp.log1p(-lb)
    one_m = 1.0 - lb
    gn = gn_ref[...]
    n_levels = msk_ref.shape[0] - 1

    def chunk(ci, carry):
        rs = pl.ds(pl.multiple_of(ci * c, c), c)
        q = q_ref[rs, :]
        z = z_ref[rs, :]
        vb = v_ref[rs, :].astype(BF16)
        ls = jnp.minimum(z, 0.0) - jnp.log1p(jnp.exp(-jnp.abs(z)))
        b2 = log_1m + ls
        lf = jnp.maximum(log_lb, b2) + jnp.log1p(jnp.exp(-jnp.abs(log_lb - b2)))
        k = one_m * jax.nn.sigmoid(-z)
        lf_hi = lf.astype(BF16)
        lf_lo = (lf - lf_hi.astype(F32)).astype(BF16)
        wall = wall_ref[...]
        e = jnp.exp(jnp.dot(wall, lf_hi, preferred_element_type=F32) + jnp.dot(wall, lf_lo, preferred_element_type=F32))
        e_q = e[0:c]
        e_k = e[c:2 * c]
        a = msk_ref[n_levels] * _dot_nt(q.astype(BF16), k.astype(BF16))
        for lv in range(n_levels):
            el = e[(2 + lv) * c:(3 + lv) * c]
            a = a + msk_ref[lv] * _dot_nt((q * el).astype(BF16), (k * el).astype(BF16))
        s_t = st[...]
        o = _dot_nt((q * e_q).astype(BF16), s_t.astype(BF16)) + jnp.dot(a.astype(BF16), vb, preferred_element_type=F32)
        st[...] = e_q[c - 1:c, :] * s_t + _dot_tn(vb, (k * e_k).astype(BF16))
        g = g_ref[rs, :]
        o_ref[rs, :] = _rms(o, gn) * (g * jax.nn.sigmoid(g))
        return carry

    lax.fori_loop(0, q_ref.shape[0] // c, chunk, 0)


def _hgrn(rq, rf, ri, rg, lb, gn, consts, bsz, seq):
    n, width = rq.shape
    heads = width // HGRN_EXPAND
    ts = HGRN_TILE
    nt = seq // ts
    wall, msk = consts
    blk = pl.BlockSpec((ts, HGRN_EXPAND), lambda b, h, t: (b * nt + t, h))
    vec = pl.BlockSpec((1, HGRN_EXPAND), lambda b, h, t: (0, h))
    return pl.pallas_call(
        _hgrn_kernel,
        grid=(bsz, heads, nt),
        in_specs=[blk, blk, blk, blk, vec, vec,
                  pl.BlockSpec(wall.shape, lambda b, h, t: (0, 0)),
                  pl.BlockSpec(msk.shape, lambda b, h, t: (0, 0, 0))],
        out_specs=blk,
        out_shape=jax.ShapeDtypeStruct((n, width), F32),
        scratch_shapes=[pltpu.VMEM((HGRN_EXPAND, HGRN_EXPAND), F32)],
        compiler_params=_params(("parallel", "parallel", "arbitrary")),
        name="hgrn2",
    )(rq, rf, ri, rg, lb.reshape(1, width), gn.reshape(1, width), wall, msk)


def _out_kernel(a_ref, r_ref, x_ref, mod_ref, ag_ref, g2_ref, wo_ref, wr_ref, br_ref, xo_ref, h2_ref, lg_ref):
    mod = mod_ref[0]
    mix = jnp.concatenate([_rms(a_ref[...], ag_ref[...]), r_ref[...]], axis=1).astype(BF16)
    xn = x_ref[...] + mod[2:3, :] * jnp.dot(mix, wo_ref[...], preferred_element_type=F32)
    xo_ref[...] = xn
    h2 = _rms(xn, g2_ref[...]) * (1.0 + mod[4:5, :]) + mod[3:4, :]
    h2_ref[...] = h2
    lg_ref[...] = jnp.dot(h2.astype(BF16), wr_ref[...], preferred_element_type=F32) + br_ref[...]


def _out_proj(attn, rec, x2, mod_l, ag, g2, wo_bf, wr_bf, br, seq):
    n, d = x2.shape
    half = attn.shape[1]
    tm = ROW_TILE
    per_batch = seq // tm
    row = lambda w: pl.BlockSpec((tm, w), lambda i: (i, 0))
    full = lambda a: pl.BlockSpec(a.shape, lambda i: (0,) * a.ndim)
    ag2, g22 = ag.reshape(1, half), g2.reshape(1, d)
    return pl.pallas_call(
        _out_kernel,
        grid=(n // tm,),
        in_specs=[row(half), row(half), row(d), pl.BlockSpec((1, 6, d), lambda i: (i // per_batch, 0, 0)),
                  full(ag2), full(g22), full(wo_bf), full(wr_bf), full(br)],
        out_specs=[row(d), row(d), row(LANES)],
        out_shape=[jax.ShapeDtypeStruct((n, d), F32), jax.ShapeDtypeStruct((n, d), F32),
                   jax.ShapeDtypeStruct((n, LANES), F32)],
        compiler_params=_params(("parallel",)),
        name="out_proj",
    )(attn, rec, x2, mod_l, ag2, g22, wo_bf, wr_bf, br)


def _route_kernel(lg_ref, e_ref, g_ref):
    lg = lg_ref[...]
    lane = lax.broadcasted_iota(jnp.int32, lg.shape, 1).astype(F32)
    big = float(LANES)

    def first_argmax(vals):
        mx = jnp.max(vals, axis=-1, keepdims=True)
        return mx, jnp.min(jnp.where(vals == mx, lane, big), axis=-1, keepdims=True)

    is_group = lane < N_GROUPS
    gmax, gidx = first_argmax(jnp.where(is_group, lg, -jnp.inf))
    gprob = 1.0 / jnp.sum(jnp.where(is_group, jnp.exp(lg - gmax), 0.0), axis=-1, keepdims=True)
    lo = N_GROUPS + EXPERTS_PER_GROUP * gidx
    el = jnp.where(jnp.logical_and(lane >= lo, lane < lo + EXPERTS_PER_GROUP), lg, -jnp.inf)
    t1, i1 = first_argmax(el)
    t2, i2 = first_argmax(jnp.where(lane == i1, -jnp.inf, el))
    ex = jnp.exp(t2 - t1)
    p1 = 1.0 / (1.0 + ex)
    p2 = ex / (1.0 + ex)
    e_ref[...] = jnp.where(lane == 0, i1 - N_GROUPS, jnp.where(lane == 1, i2 - N_GROUPS, 0.0)).astype(jnp.int32)
    g_ref[...] = jnp.where(lane == 0, p1 * gprob, jnp.where(lane == 1, p2 * gprob, 0.0))


def _route(logits):
    n = logits.shape[0]
    tm = ROW_TILE
    spec = pl.BlockSpec((tm, LANES), lambda i: (i, 0))
    return pl.pallas_call(
        _route_kernel,
        grid=(n // tm,),
        in_specs=[spec],
        out_specs=[spec, spec],
        out_shape=[jax.ShapeDtypeStruct((n, LANES), jnp.int32), jax.ShapeDtypeStruct((n, LANES), F32)],
        compiler_params=_params(("parallel",)),
        name="route",
    )(logits)


def _dispatch_plan(expert_ids, n_rows):
    flat = expert_ids.reshape(-1)
    n_assign = flat.shape[0]
    onehot = (flat[:, None] == jnp.arange(N_EXPERTS, dtype=jnp.int32)[None, :]).astype(jnp.int32)
    csum = jnp.cumsum(onehot, axis=0)
    rank = jnp.take_along_axis(csum, flat[:, None], axis=1)[:, 0] - 1
    counts = csum[-1]
    padded = (counts + MOE_TILE - 1) // MOE_TILE * MOE_TILE
    pend = jnp.cumsum(padded)
    dest = ((pend - padded)[flat] + rank).astype(jnp.int32)
    token_id = jnp.arange(n_assign, dtype=jnp.int32) // TOP_K
    row_token = jnp.zeros((n_rows,), jnp.int32).at[dest].set(token_id)
    n_blocks = n_rows // MOE_TILE
    block_expert = jnp.minimum(
        jnp.searchsorted(pend, jnp.arange(n_blocks, dtype=pend.dtype) * MOE_TILE, side='right'),
        N_EXPERTS - 1).astype(jnp.int32)
    n_used = (pend[-1:] // MOE_TILE).astype(jnp.int32)
    return dest, row_token, block_expert, n_used


def _expert_kernel(be_ref, rt_ref, nu_ref, h_hbm, w1_ref, w3_ref, w2_ref, y_ref, xbuf, w1b, w3b, w2b, sem):
    i = pl.program_id(0)
    tb = y_ref.shape[0]
    n_used = nu_ref[0]
    slot = i % 2

    def gather(blk, sl):
        def body(r, carry):
            tok = rt_ref[blk * tb + r]
            pltpu.make_async_copy(h_hbm.at[pl.ds(tok, 1)], xbuf.at[sl, pl.ds(r, 1)], sem.at[sl]).start()
            return carry
        lax.fori_loop(0, tb, body, 0, unroll=8)

    @pl.when(i == 0)
    def _():
        gather(0, 0)

    @pl.when(i + 1 < n_used)
    def _():
        gather(i + 1, 1 - slot)

    @pl.when(i < n_used)
    def _():
        pltpu.make_async_copy(h_hbm.at[pl.ds(0, tb)], xbuf.at[slot], sem.at[slot]).wait()
        e = be_ref[i]
        e_prev = be_ref[jnp.maximum(i - 1, 0)]

        @pl.when(jnp.logical_or(i == 0, e != e_prev))
        def _():
            w1b[...] = w1_ref[0, 0].astype(BF16)
            w3b[...] = w3_ref[0, 0].astype(BF16)
            w2b[...] = w2_ref[0, 0].astype(BF16)

        x = xbuf[slot].astype(BF16)
        a = jnp.dot(x, w1b[...], preferred_element_type=F32)
        b = jnp.dot(x, w3b[...], preferred_element_type=F32)
        hm = (a * jax.nn.sigmoid(a) * b).astype(BF16)
        y_ref[...] = jnp.dot(hm, w2b[...], preferred_element_type=F32)

    @pl.when(i >= n_used)
    def _():
        y_ref[...] = jnp.zeros_like(y_ref)


def _experts(h2, w1, w3, w2, layer, block_expert, row_token, n_used):
    n, d = h2.shape
    de = w1.shape[-1]
    n_rows = row_token.shape[0]
    tb = MOE_TILE
    wspec = lambda r, c: pl.BlockSpec((1, 1, r, c), lambda i, be, rt, nu: (layer, be[i], 0, 0))
    grid_spec = pltpu.PrefetchScalarGridSpec(
        num_scalar_prefetch=3,
        grid=(n_rows // tb,),
        in_specs=[pl.BlockSpec(memory_space=pl.ANY), wspec(d, de), wspec(d, de), wspec(de, d)],
        out_specs=pl.BlockSpec((tb, d), lambda i, be, rt, nu: (i, 0)),
        scratch_shapes=[pltpu.VMEM((2, tb, d), F32), pltpu.VMEM((d, de), BF16), pltpu.VMEM((d, de), BF16),
                        pltpu.VMEM((de, d), BF16), pltpu.SemaphoreType.DMA((2,))],
    )
    return pl.pallas_call(
        _expert_kernel,
        grid_spec=grid_spec,
        out_shape=jax.ShapeDtypeStruct((n_rows, d), F32),
        compiler_params=_params(("arbitrary",)),
        name="experts",
    )(block_expert, row_token, n_used, h2, w1, w3, w2)


def _combine_kernel(pos_ref, y_hbm, x_ref, gt_ref, mod_ref, fg_ref, o_ref, ybuf, sem, *, final):
    i = pl.program_id(0)
    tt = x_ref.shape[0]
    slot = i % 2

    def gather(blk, sl):
        def body(t, carry):
            for kk in range(TOP_K):
                row = pos_ref[(blk * tt + t) * TOP_K + kk]
                pltpu.make_async_copy(y_hbm.at[pl.ds(row, 1)], ybuf.at[sl, pl.ds(kk * tt + t, 1)], sem.at[sl]).start()
            return carry
        lax.fori_loop(0, tt, body, 0, unroll=8)

    @pl.when(i == 0)
    def _():
        gather(0, 0)

    @pl.when(i + 1 < pl.num_programs(0))
    def _():
        gather(i + 1, 1 - slot)

    pltpu.make_async_copy(y_hbm.at[pl.ds(0, TOP_K * tt)], ybuf.at[slot], sem.at[slot]).wait()
    gt = gt_ref[...]
    y = gt[:, 0:1] * ybuf[slot, 0:tt, :] + gt[:, 1:2] * ybuf[slot, tt:2 * tt, :]
    xn = x_ref[...] + mod_ref[0][5:6, :] * y
    o_ref[...] = _rms(xn, fg_ref[...]) if final else xn


def _combine(y_rows, pos, x2, gates, mod_l, final_g, seq, final):
    n, d = x2.shape
    tt = COMBINE_TILE
    per_batch = seq // tt
    grid_spec = pltpu.PrefetchScalarGridSpec(
        num_scalar_prefetch=1,
        grid=(n // tt,),
        in_specs=[pl.BlockSpec(memory_space=pl.ANY),
                  pl.BlockSpec((tt, d), lambda i, pos: (i, 0)),
                  pl.BlockSpec((tt, LANES), lambda i, pos: (i, 0)),
                  pl.BlockSpec((1, 6, d), lambda i, pos: (i // per_batch, 0, 0)),
                  pl.BlockSpec((1, d), lambda i, pos: (0, 0))],
        out_specs=pl.BlockSpec((tt, d), lambda i, pos: (i, 0)),
        scratch_shapes=[pltpu.VMEM((2, TOP_K * tt, d), F32), pltpu.SemaphoreType.DMA((2,))],
    )
    return pl.pallas_call(
        functools.partial(_combine_kernel, final=final),
        grid_spec=grid_spec,
        out_shape=jax.ShapeDtypeStruct((n, d), F32),
        compiler_params=_params(("arbitrary",)),
        name="combine",
    )(pos, y_rows, x2, gates, mod_l, final_g.reshape(1, d))


def kernel(x, c, w_ada, b_ada, norm1_g, w_in, attn_norm_g, hgrn_lb_logits, hgrn_norm_g, w_out, norm2_g, router_group_w, router_group_b, router_expert_w, router_expert_b, moe_w1, moe_w3, moe_w2, final_g):
    bsz, seq, d = x.shape
    depth = w_ada.shape[0]
    n = bsz * seq
    attn_dim = attn_norm_g.shape[1]
    n_heads = attn_dim // ATTN_HEAD_DIM

    lb_w = jax.nn.softmax(hgrn_lb_logits.astype(F32), axis=0)
    lower_bounds = jnp.cumsum(lb_w, axis=0) - lb_w[0:1]
    mod = _adaln(c, w_ada, b_ada)
    bias = _attn_bias(n_heads)
    hconsts = _hgrn_consts()
    n_rows = n * TOP_K + N_EXPERTS * MOE_TILE
    pad = LANES - N_GROUPS - N_EXPERTS

    x2 = x.reshape(n, d)
    for layer in range(depth):
        aq, ak, av, rq, rf, ri, rg = _in_proj(x2, mod[layer], norm1_g[layer], w_in[layer].astype(BF16), seq)
        attn = _attention(aq, ak, av, bias, bsz, seq)
        rec = _hgrn(rq, rf, ri, rg, lower_bounds[layer], hgrn_norm_g[layer], hconsts, bsz, seq)
        wr = jnp.concatenate([router_group_w[layer], router_expert_w[layer], jnp.zeros((d, pad), F32)], axis=1)
        br = jnp.concatenate([router_group_b[layer], router_expert_b[layer], jnp.zeros((pad,), F32)]).reshape(1, LANES)
        x2, h2, logits = _out_proj(attn, rec, x2, mod[layer], attn_norm_g[layer], norm2_g[layer],
                                   w_out[layer].astype(BF16), wr.astype(BF16), br, seq)
        eids, gates = _route(logits)
        pos, row_token, block_expert, n_used = _dispatch_plan(eids[:, :TOP_K], n_rows)
        y_rows = _experts(h2, moe_w1, moe_w3, moe_w2, layer, block_expert, row_token, n_used)
        x2 = _combine(y_rows, pos, x2, gates, mod[layer], final_g, seq, final=(layer == depth - 1))
    return x2.reshape(bsz, seq, d)
```

```python
import functools

import numpy as np
import jax
import jax.numpy as jnp
from jax import lax
from jax.experimental import pallas as pl
from jax.experimental.pallas import tpu as pltpu
from jax.experimental.pallas import tpu_sc as plsc

F32 = jnp.float32
BF16 = jnp.bfloat16
U32 = jnp.uint32

ATTN_HEAD_DIM = 64
ATTN_BLOCK = 128
DILATIONS = (1, 4, 16)
ALIBI_MAX_EXP = 8.0
HGRN_EXPAND = 128
HGRN_CHUNK = 128
N_GROUPS = 4
EXPERTS_PER_GROUP = 8
N_EXPERTS = N_GROUPS * EXPERTS_PER_GROUP
TOP_K = 2
NORM_EPS = 1e-6

LANES = 128
ATTN_SUPER = ATTN_BLOCK * max(DILATIONS)
ROW_TILE = 512
ATTN_UNROLL = 4
HGRN_TILE = 2048
HGRN_UNROLL = 2
MOE_TILE = 256
PLAN_TILE = 512
PLAN_LANES = 256
PACK_WIDTH = 256
SC_WINDOW = 128
VMEM_LIMIT = 56 << 20


def _params(semantics):
    return pltpu.CompilerParams(dimension_semantics=semantics, vmem_limit_bytes=VMEM_LIMIT)


def _rms(x, g):
    return x * lax.rsqrt(jnp.mean(x * x, axis=-1, keepdims=True) + NORM_EPS) * g


def _dot_nt(a, b):
    return lax.dot_general(a, b, (((1,), (1,)), ((), ())), preferred_element_type=F32)


def _dot_tn(a, b):
    return lax.dot_general(a, b, (((0,), (0,)), ((), ())), preferred_element_type=F32)


def _pack_pair(hi, lo):
    hb = lax.bitcast_convert_type(hi.astype(BF16).astype(F32), U32)
    lb = lax.bitcast_convert_type(lo.astype(BF16).astype(F32), U32)
    return hb | (lb >> 16)


def _unpack_pair(u):
    hi = lax.bitcast_convert_type(u & jnp.uint32(0xFFFF0000), F32)
    lo = lax.bitcast_convert_type(u << 16, F32)
    return hi, lo


def _pack_rows(y):
    w = PACK_WIDTH
    return _pack_pair(y[:, 0:w], y[:, 2 * w:3 * w]), _pack_pair(y[:, w:2 * w], y[:, 3 * w:4 * w])


def _unpack_rows(ua, ub):
    ha, la = _unpack_pair(ua)
    hb, lb = _unpack_pair(ub)
    return jnp.concatenate([ha, hb, la, lb], axis=1)


def _moe_residual(x_ref, ga0, ga1, gb0, gb1, gt_ref, gate_row):
    gt = gt_ref[...]
    y = gt[:, 0:1] * _unpack_rows(ga0[...], gb0[...]) + gt[:, 1:2] * _unpack_rows(ga1[...], gb1[...])
    return x_ref[...] + gate_row * y


def _ada_kernel(c_ref, w_ref, b_ref, o_ref):
    c = c_ref[...]
    ca = c * jax.nn.sigmoid(c)
    o_ref[0] = jnp.dot(ca, w_ref[0], precision=lax.Precision.HIGHEST, preferred_element_type=F32) + b_ref[0]


def _adaln(c, w_ada, b_ada):
    depth, d, n6 = w_ada.shape
    bsz = c.shape[0]
    rows = 8
    cp = jnp.zeros((rows, d), F32).at[:bsz].set(c)
    tn = 1536
    out = pl.pallas_call(
        _ada_kernel,
        grid=(depth, n6 // tn),
        in_specs=[
            pl.BlockSpec((rows, d), lambda l, j: (0, 0)),
            pl.BlockSpec((1, d, tn), lambda l, j: (l, 0, j)),
            pl.BlockSpec((1, 1, tn), lambda l, j: (l, 0, j)),
        ],
        out_specs=pl.BlockSpec((1, rows, tn), lambda l, j: (l, 0, j)),
        out_shape=jax.ShapeDtypeStruct((depth, rows, n6), F32),
        compiler_params=_params(("parallel", "parallel")),
        name="adaln",
    )(cp, w_ada, b_ada.reshape(depth, 1, n6))
    return out[:, :bsz].reshape(depth, bsz, 6, d)


def _in_kernel(*refs, n_moe):
    moe, (x_ref, mod_ref, g_ref, w_ref), rest = refs[:n_moe], refs[n_moe:n_moe + 4], refs[n_moe + 4:]
    hb = rest[-1]
    if n_moe:
        ga0, ga1, gb0, gb1, gt_ref, pmod_ref = moe
        x = _moe_residual(x_ref, ga0, ga1, gb0, gb1, gt_ref, pmod_ref[0][5:6, :])
        rest[0][...] = x
        outs = rest[1:-1]
    else:
        x = x_ref[...]
        outs = rest[:-1]
    mod = mod_ref[0]
    h = _rms(x, g_ref[...]) * (1.0 + mod[1:2, :]) + mod[0:1, :]
    hb[...] = h.astype(BF16)
    width = outs[0].shape[1]
    for i, o in enumerate(outs):
        o[...] = jnp.dot(hb[...], w_ref[:, i * width:(i + 1) * width], preferred_element_type=F32)


def _in_proj(x2, mod_l, g, w_in_bf, seq, moe=None):
    n, d = x2.shape
    n_out = w_in_bf.shape[1]
    width = 512
    tm = ROW_TILE
    per_batch = seq // tm
    row = lambda w: pl.BlockSpec((tm, w), lambda i: (i, 0))
    mod_spec = pl.BlockSpec((1, 6, d), lambda i: (i // per_batch, 0, 0))
    in_specs, args, out_specs, out_shape = [], [], [], []
    if moe is not None:
        ga, gb, gates, mod_prev = moe
        second = pl.BlockSpec((tm, PACK_WIDTH), lambda i: (i + n // tm, 0))
        in_specs += [row(PACK_WIDTH), second, row(PACK_WIDTH), second, row(LANES), mod_spec]
        args += [ga, ga, gb, gb, gates, mod_prev]
        out_specs.append(row(d))
        out_shape.append(jax.ShapeDtypeStruct((n, d), F32))
    in_specs += [row(d), mod_spec, pl.BlockSpec((1, d), lambda i: (0, 0)), pl.BlockSpec((d, n_out), lambda i: (0, 0))]
    args += [x2, mod_l, g.reshape(1, d), w_in_bf]
    out_specs += [row(width)] * (n_out // width)
    out_shape += [jax.ShapeDtypeStruct((n, width), F32)] * (n_out // width)
    outs = pl.pallas_call(
        functools.partial(_in_kernel, n_moe=0 if moe is None else 6),
        grid=(n // tm,),
        in_specs=in_specs,
        out_specs=out_specs,
        out_shape=out_shape,
        scratch_shapes=[pltpu.VMEM((tm, d), BF16)],
        compiler_params=_params(("parallel",)),
        name="in_proj",
    )(*args)
    return (x2, outs) if moe is None else (outs[0], outs[1:])


def _attn_bias(n_heads):
    qi = np.arange(ATTN_BLOCK)[:, None]
    kj = np.arange(2 * ATTN_BLOCK)[None, :]
    dist = ATTN_BLOCK + qi - kj
    valid = (dist >= 0) & (dist <= ATTN_BLOCK)
    slopes = np.exp2(-ALIBI_MAX_EXP * np.arange(1, n_heads + 1, dtype=np.float32) / n_heads)
    out = np.empty((len(DILATIONS), n_heads, ATTN_BLOCK, 2 * ATTN_BLOCK), np.float32)
    for p, dil in enumerate(DILATIONS):
        b = -slopes[:, None, None] * (dist * dil).astype(np.float32)[None]
        out[p] = np.where(valid[None], b, -np.inf)
    return jnp.asarray(out)


def _attn_kernel(q_ref, kc_ref, kp_ref, vc_ref, vp_ref, bias_ref, o_ref, kk, vv, oscr, lscr):
    n = pl.program_id(2)
    sb = q_ref.shape[0]
    kk[0:sb, :] = kp_ref[...]
    kk[sb:, :] = kc_ref[...]
    vv[0:sb, :] = vp_ref[...]
    vv[sb:, :] = vc_ref[...]
    is_lo = lax.broadcasted_iota(jnp.int32, (ATTN_BLOCK, LANES), 1) < ATTN_HEAD_DIM
    prev_half = lax.broadcasted_iota(jnp.int32, (ATTN_BLOCK, 2 * ATTN_BLOCK), 1) < ATTN_BLOCK
    ones = jnp.ones((2 * ATTN_BLOCK, LANES), BF16)
    scale = ATTN_HEAD_DIM ** -0.5

    for p, dil in enumerate(DILATIONS):
        nb = (sb // ATTN_BLOCK) // dil
        span = ATTN_BLOCK * dil

        def rows(start, dil=dil):
            if dil == 1:
                return pl.ds(pl.multiple_of(start, ATTN_BLOCK), ATTN_BLOCK)
            return pl.ds(start, ATTN_BLOCK, stride=dil)

        def body(it, carry, p=p, nb=nb, span=span, rows=rows):
            r = it // nb
            j = it % nb
            start = j * span + r
            qb = q_ref[rows(start), :] * scale
            kcat = jnp.concatenate([kk[rows(sb + start - span), :], kk[rows(sb + start), :]], axis=0).astype(BF16)
            vcat = jnp.concatenate([vv[rows(sb + start - span), :], vv[rows(sb + start), :]], axis=0).astype(BF16)
            vaug = jnp.concatenate([vcat, ones], axis=1)
            no_prev = jnp.logical_and(jnp.logical_and(n == 0, j == 0), prev_half)
            acc_o = None
            acc_l = None
            for hh in range(2):
                sel = is_lo if hh == 0 else jnp.logical_not(is_lo)
                qm = jnp.where(sel, qb, 0.0).astype(BF16)
                s = _dot_nt(qm, kcat) + bias_ref[p, hh]
                s = jnp.where(no_prev, -jnp.inf, s)
                m = jnp.max(s, axis=-1, keepdims=True)
                pe = jnp.exp(s - m).astype(BF16)
                of = jnp.dot(pe, vaug, preferred_element_type=F32)
                den = of[:, LANES:]
                on = of[:, :LANES] / den
                lse = m + jnp.log(den)
                acc_o = on if hh == 0 else jnp.where(sel, on, acc_o)
                acc_l = lse if hh == 0 else jnp.where(sel, lse, acc_l)
            oscr[p, rows(start), :] = acc_o
            lscr[p, rows(start), :] = acc_l
            return carry

        lax.fori_loop(0, sb // ATTN_BLOCK, body, 0, unroll=ATTN_UNROLL)

    ct = 256

    def combine(t, carry):
        rs = pl.ds(pl.multiple_of(t * ct, ct), ct)
        l0, l1, l2 = lscr[0, rs, :], lscr[1, rs, :], lscr[2, rs, :]
        mx = jnp.maximum(jnp.maximum(l0, l1), l2)
        w0, w1, w2 = jnp.exp(l0 - mx), jnp.exp(l1 - mx), jnp.exp(l2 - mx)
        num = w0 * oscr[0, rs, :] + w1 * oscr[1, rs, :] + w2 * oscr[2, rs, :]
        o_ref[rs, :] = num / (w0 + w1 + w2)
        return carry

    lax.fori_loop(0, sb // ct, combine, 0)


def _attention(aq, ak, av, bias, bsz, seq):
    n, width = aq.shape
    sb = ATTN_SUPER
    nsb = seq // sb
    pairs = width // LANES
    blk = (sb, LANES)
    cur = lambda b, h, t: (b * nsb + t, h)
    prev = lambda b, h, t: (b * nsb + jnp.maximum(t - 1, 0), h)
    return pl.pallas_call(
        _attn_kernel,
        grid=(bsz, pairs, nsb),
        in_specs=[
            pl.BlockSpec(blk, cur),
            pl.BlockSpec(blk, cur),
            pl.BlockSpec(blk, prev),
            pl.BlockSpec(blk, cur),
            pl.BlockSpec(blk, prev),
            pl.BlockSpec((len(DILATIONS), 2, ATTN_BLOCK, 2 * ATTN_BLOCK), lambda b, h, t: (0, h, 0, 0)),
        ],
        out_specs=pl.BlockSpec(blk, cur),
        out_shape=jax.ShapeDtypeStruct((n, width), F32),
        scratch_shapes=[
            pltpu.VMEM((2 * sb, LANES), F32),
            pltpu.VMEM((2 * sb, LANES), F32),
            pltpu.VMEM((len(DILATIONS), sb, LANES), F32),
            pltpu.VMEM((len(DILATIONS), sb, LANES), F32),
        ],
        compiler_params=_params(("parallel", "parallel", "arbitrary")),
        name="dilated_attn",
    )(aq, ak, ak, av, av, bias)


def _hgrn_consts():
    c = HGRN_CHUNK
    t = np.arange(c)[:, None]
    u = np.arange(c)[None, :]
    blocks = [u <= t, u > t]
    masks = []
    m = c // 2
    while m >= 1:
        mid = (t // (2 * m)) * (2 * m) + m
        is_q = t >= mid
        blocks.append((is_q & (u >= mid) & (u <= t)) | (~is_q & (u > t) & (u < mid)))
        masks.append(((t // (2 * m)) == (u // (2 * m))) & (t % (2 * m) >= m) & (u % (2 * m) < m))
        m //= 2
    masks.append(t == u)
    wall = np.concatenate(blocks, axis=0).astype(np.float32)
    return jnp.asarray(wall, BF16), jnp.asarray(np.stack(masks).astype(np.float32))


def _hgrn_kernel(q_ref, z_ref, v_ref, g_ref, lb_ref, gn_ref, wall_ref, msk_ref, o_ref, st):
    c = HGRN_CHUNK

    @pl.when(pl.program_id(2) == 0)
    def _():
        st[...] = jnp.zeros_like(st)

    lb = jnp.maximum(lb_ref[...], 0.0)
    log_lb = jnp.log(lb)
    log_1m = jnp.log1p(-lb)
    one_m = 1.0 - lb
    gn = gn_ref[...]
    n_levels = msk_ref.shape[0] - 1

    def chunk(ci, carry):
        rs = pl.ds(pl.multiple_of(ci * c, c), c)
        q = q_ref[rs, :]
        z = z_ref[rs, :]
        vb = v_ref[rs, :].astype(BF16)
        ls = jnp.minimum(z, 0.0) - jnp.log1p(jnp.exp(-jnp.abs(z)))
        b2 = log_1m + ls
        lf = jnp.maximum(log_lb, b2) + jnp.log1p(jnp.exp(-jnp.abs(log_lb - b2)))
        k = one_m * jax.nn.sigmoid(-z)
        lf_hi = lf.astype(BF16)
        lf_lo = (lf - lf_hi.astype(F32)).astype(BF16)
        wall = wall_ref[...]
        e = jnp.exp(jnp.dot(wall, lf_hi, preferred_element_type=F32) + jnp.dot(wall, lf_lo, preferred_element_type=F32))
        e_q = e[0:c]
        e_k = e[c:2 * c]
        a = msk_ref[n_levels] * _dot_nt(q.astype(BF16), k.astype(BF16))
        for lv in range(n_levels):
            el = e[(2 + lv) * c:(3 + lv) * c]
            a = a + msk_ref[lv] * _dot_nt((q * el).astype(BF16), (k * el).astype(BF16))
        s_t = st[...]
        o = _dot_nt((q * e_q).astype(BF16), s_t.astype(BF16)) + jnp.dot(a.astype(BF16), vb, preferred_element_type=F32)
        st[...] = e_q[c - 1:c, :] * s_t + _dot_tn(vb, (k * e_k).astype(BF16))
        g = g_ref[rs, :]
        o_ref[rs, :] = _rms(o, gn) * (g * jax.nn.sigmoid(g))
        return carry

    lax.fori_loop(0, q_ref.shape[0] // c, chunk, 0, unroll=HGRN_UNROLL)


def _hgrn(rq, rf, ri, rg, lb, gn, consts, bsz, seq):
    n, width = rq.shape
    heads = width // HGRN_EXPAND
    ts = HGRN_TILE
    nt = seq // ts
    wall, msk = consts
    blk = pl.BlockSpec((ts, HGRN_EXPAND), lambda b, h, t: (b * nt + t, h))
    vec = pl.BlockSpec((1, HGRN_EXPAND), lambda b, h, t: (0, h))
    return pl.pallas_call(
        _hgrn_kernel,
        grid=(bsz, heads, nt),
        in_specs=[blk, blk, blk, blk, vec, vec,
                  pl.BlockSpec(wall.shape, lambda b, h, t: (0, 0)),
                  pl.BlockSpec(msk.shape, lambda b, h, t: (0, 0, 0))],
        out_specs=blk,
        out_shape=jax.ShapeDtypeStruct((n, width), F32),
        scratch_shapes=[pltpu.VMEM((HGRN_EXPAND, HGRN_EXPAND), F32)],
        compiler_params=_params(("parallel", "parallel", "arbitrary")),
        name="hgrn2",
    )(rq, rf, ri, rg, lb.reshape(1, width), gn.reshape(1, width), wall, msk)


def _out_kernel(a_ref, r_ref, x_ref, mod_ref, ag_ref, g2_ref, wo_ref, wr_ref, br_ref, xo_ref, ha_ref, hb_ref, lg_ref):
    mod = mod_ref[0]
    mix = jnp.concatenate([_rms(a_ref[...], ag_ref[...]), r_ref[...]], axis=1).astype(BF16)
    xn = x_ref[...] + mod[2:3, :] * jnp.dot(mix, wo_ref[...], preferred_element_type=F32)
    xo_ref[...] = xn
    h2 = _rms(xn, g2_ref[...]) * (1.0 + mod[4:5, :]) + mod[3:4, :]
    ha_ref[...], hb_ref[...] = _pack_rows(h2)
    lg_ref[...] = jnp.dot(h2.astype(BF16), wr_ref[...], preferred_element_type=F32) + br_ref[...]


def _out_proj(attn, rec, x2, mod_l, ag, g2, wo_bf, wr_bf, br, seq):
    n, d = x2.shape
    half = attn.shape[1]
    tm = ROW_TILE
    per_batch = seq // tm
    row = lambda w: pl.BlockSpec((tm, w), lambda i: (i, 0))
    full = lambda a: pl.BlockSpec(a.shape, lambda i: (0,) * a.ndim)
    ag2, g22 = ag.reshape(1, half), g2.reshape(1, d)
    return pl.pallas_call(
        _out_kernel,
        grid=(n // tm,),
        in_specs=[row(half), row(half), row(d), pl.BlockSpec((1, 6, d), lambda i: (i // per_batch, 0, 0)),
                  full(ag2), full(g22), full(wo_bf), full(wr_bf), full(br)],
        out_specs=[row(d), row(PACK_WIDTH), row(PACK_WIDTH), row(LANES)],
        out_shape=[jax.ShapeDtypeStruct((n, d), F32), jax.ShapeDtypeStruct((n, PACK_WIDTH), U32),
                   jax.ShapeDtypeStruct((n, PACK_WIDTH), U32), jax.ShapeDtypeStruct((n, LANES), F32)],
        compiler_params=_params(("parallel",)),
        name="out_proj",
    )(attn, rec, x2, mod_l, ag2, g22, wo_bf, wr_bf, br)


def _route_kernel(lg_ref, e_ref, g_ref):
    lg = lg_ref[...]
    lane = lax.broadcasted_iota(jnp.int32, lg.shape, 1).astype(F32)
    big = float(LANES)

    def first_argmax(vals):
        mx = jnp.max(vals, axis=-1, keepdims=True)
        return mx, jnp.min(jnp.where(vals == mx, lane, big), axis=-1, keepdims=True)

    is_group = lane < N_GROUPS
    gmax, gidx = first_argmax(jnp.where(is_group, lg, -jnp.inf))
    gprob = 1.0 / jnp.sum(jnp.where(is_group, jnp.exp(lg - gmax), 0.0), axis=-1, keepdims=True)
    lo = N_GROUPS + EXPERTS_PER_GROUP * gidx
    el = jnp.where(jnp.logical_and(lane >= lo, lane < lo + EXPERTS_PER_GROUP), lg, -jnp.inf)
    t1, i1 = first_argmax(el)
    t2, i2 = first_argmax(jnp.where(lane == i1, -jnp.inf, el))
    ex = jnp.exp(t2 - t1)
    p1 = 1.0 / (1.0 + ex)
    p2 = ex / (1.0 + ex)
    e_ref[...] = jnp.where(lane == 0, i1 - N_GROUPS, jnp.where(lane == 1, i2 - N_GROUPS, 0.0)).astype(jnp.int32)
    g_ref[...] = jnp.where(lane == 0, p1 * gprob, jnp.where(lane == 1, p2 * gprob, 0.0))


def _route(logits):
    n = logits.shape[0]
    tm = ROW_TILE
    spec = pl.BlockSpec((tm, LANES), lambda i: (i, 0))
    return pl.pallas_call(
        _route_kernel,
        grid=(n // tm,),
        in_specs=[spec],
        out_specs=[spec, spec],
        out_shape=[jax.ShapeDtypeStruct((n, LANES), jnp.int32), jax.ShapeDtypeStruct((n, LANES), F32)],
        compiler_params=_params(("parallel",)),
        name="route",
    )(logits)


def _plan_kernel(e_ref, ut_ref, pos_ref, be_ref, bv_ref, nu_ref, cnt, off, run):
    phase = pl.program_id(0)
    i = pl.program_id(1)
    t = e_ref.shape[1]
    eio = lax.broadcasted_iota(jnp.int32, (N_EXPERTS, t), 0)
    oh1 = e_ref[0:1, :] == eio
    oh2 = e_ref[1:2, :] == eio
    oh = jnp.logical_or(oh1, oh2).astype(F32)
    tot = jnp.sum(oh, axis=1, keepdims=True)

    @pl.when(jnp.logical_and(phase == 0, i == 0))
    def _():
        cnt[...] = jnp.zeros_like(cnt)

    @pl.when(phase == 0)
    def _():
        cnt[...] += tot

    @pl.when(jnp.logical_and(phase == 1, i == 0))
    def _():
        counts = cnt[...]
        nb = jnp.floor((counts + (MOE_TILE - 1)) * (1.0 / MOE_TILE))
        er = lax.broadcasted_iota(jnp.int32, (N_EXPERTS, N_EXPERTS), 0)
        ec = lax.broadcasted_iota(jnp.int32, (N_EXPERTS, N_EXPERTS), 1)
        cum_excl = jnp.dot((ec < er).astype(BF16), nb.astype(BF16), preferred_element_type=F32)
        off[...] = cum_excl * MOE_TILE
        run[...] = jnp.zeros_like(run)
        ce, ci, cn = cum_excl[:, 0:1], (cum_excl + nb)[:, 0:1], counts[:, 0:1]
        b = lax.broadcasted_iota(jnp.int32, (N_EXPERTS, be_ref.shape[1]), 1).astype(F32)
        member = jnp.logical_and(b >= ce, b < ci)
        be = jnp.minimum(jnp.sum((b >= ci).astype(F32), axis=0, keepdims=True), N_EXPERTS - 1.0)
        bv = jnp.sum(jnp.where(member, jnp.minimum(cn - (b - ce) * MOE_TILE, float(MOE_TILE)), 0.0), axis=0, keepdims=True)
        be_ref[...] = be.astype(jnp.int32)
        bv_ref[...] = bv.astype(jnp.int32)
        nu_ref[...] = jnp.broadcast_to(ci[N_EXPERTS - 1:N_EXPERTS, :], nu_ref.shape).astype(jnp.int32)

    @pl.when(phase == 1)
    def _():
        c = jnp.dot(oh.astype(BF16), ut_ref[...], preferred_element_type=F32)
        r = off[:, 0:1] + run[:, 0:1] + c - 1.0
        pos_ref[0:1, :] = jnp.sum(jnp.where(oh1, r, 0.0), axis=0, keepdims=True).astype(jnp.int32)
        pos_ref[1:2, :] = jnp.sum(jnp.where(oh2, r, 0.0), axis=0, keepdims=True).astype(jnp.int32)
        run[...] += tot


def _plan(expert_ids_t, n_rows):
    n = expert_ids_t.shape[1]
    t = PLAN_TILE
    n_blocks = n_rows // MOE_TILE
    assert n_blocks <= PLAN_LANES and TOP_K == 2
    ut = jnp.asarray(np.triu(np.ones((t, t), np.float32)), BF16)
    tab = jax.ShapeDtypeStruct((1, PLAN_LANES), jnp.int32)
    const = lambda shape: pl.BlockSpec(shape, lambda p, i: (0, 0))
    pos, be, bv, nu = pl.pallas_call(
        _plan_kernel,
        grid=(2, n // t),
        in_specs=[pl.BlockSpec((TOP_K, t), lambda p, i: (0, i)), const((t, t))],
        out_specs=[pl.BlockSpec((TOP_K, t), lambda p, i: (0, i * p)), const((1, PLAN_LANES)), const((1, PLAN_LANES)),
                   const((1, LANES))],
        out_shape=[jax.ShapeDtypeStruct((TOP_K, n), jnp.int32), tab, tab, jax.ShapeDtypeStruct((1, LANES), jnp.int32)],
        scratch_shapes=[pltpu.VMEM((N_EXPERTS, LANES), F32)] * 3,
        compiler_params=_params(("arbitrary", "arbitrary")),
        name="plan",
    )(expert_ids_t, ut)
    return pos, be.reshape(-1), bv.reshape(-1), nu.reshape(-1)[:1]


def _sc_mesh():
    return plsc.VectorSubcoreMesh(core_axis_name="c", subcore_axis_name="s")


def _sc_scatter_rows(x, pos, n_rows):
    n, d = x.shape
    w = SC_WINDOW

    @functools.partial(pl.kernel, out_type=jax.ShapeDtypeStruct((n_rows, d), x.dtype), mesh=_sc_mesh(), scratch_types=[])
    def scatter(x_hbm, p_hbm, o_hbm):
        def body(x_vmem, i_vmem):
            for k in range(TOP_K):
                pltpu.sync_copy(x_vmem, o_hbm.at[i_vmem.at[k]])

        pltpu.emit_pipeline(
            body,
            grid=(n // w,),
            in_specs=[pl.BlockSpec((w, d), lambda i: (i, 0)), pl.BlockSpec((TOP_K, w), lambda i: (0, i))],
            out_specs=[],
            core_axis_name=("c", "s"),
            dimension_semantics=(pltpu.PARALLEL,),
        )(x_hbm, p_hbm)

    return scatter(x, pos)


def _sc_gather_rows(y, pos):
    m = pos.shape[0] * pos.shape[1]
    d = y.shape[1]
    w = SC_WINDOW

    @functools.partial(pl.kernel, out_type=jax.ShapeDtypeStruct((m, d), y.dtype), mesh=_sc_mesh(), scratch_types=[])
    def gather(y_hbm, p_hbm, o_hbm):
        def body(i_vmem, o_vmem):
            pltpu.sync_copy(y_hbm.at[i_vmem.at[0]], o_vmem)

        pltpu.emit_pipeline(
            body,
            grid=(m // w,),
            in_specs=[pl.BlockSpec((1, w), lambda i: (0, i))],
            out_specs=[pl.BlockSpec((w, d), lambda i: (i, 0))],
            core_axis_name=("c", "s"),
            dimension_semantics=(pltpu.PARALLEL,),
        )(p_hbm, o_hbm)

    return gather(y, pos.reshape(1, m))


def _expert_kernel(be_ref, bv_ref, nu_ref, xa_ref, xb_ref, w1_ref, w3_ref, w2_ref, ya_ref, yb_ref, w1b, w3b, w2b):
    i = pl.program_id(0)
    tb = xa_ref.shape[0]

    @pl.when(i < nu_ref[0])
    def _():
        e = be_ref[i]
        e_prev = be_ref[jnp.maximum(i - 1, 0)]

        @pl.when(jnp.logical_or(i == 0, e != e_prev))
        def _():
            w1b[...] = w1_ref[0, 0].astype(BF16)
            w3b[...] = w3_ref[0, 0].astype(BF16)
            w2b[...] = w2_ref[0, 0].astype(BF16)

        live = lax.broadcasted_iota(jnp.int32, (tb, 1), 0) < bv_ref[i]
        x = jnp.where(live, _unpack_rows(xa_ref[...], xb_ref[...]), 0.0).astype(BF16)
        a = jnp.dot(x, w1b[...], preferred_element_type=F32)
        b = jnp.dot(x, w3b[...], preferred_element_type=F32)
        hm = (a * jax.nn.sigmoid(a) * b).astype(BF16)
        ya_ref[...], yb_ref[...] = _pack_rows(jnp.dot(hm, w2b[...], preferred_element_type=F32))

    @pl.when(i >= nu_ref[0])
    def _():
        ya_ref[...] = jnp.zeros_like(ya_ref)
        yb_ref[...] = jnp.zeros_like(yb_ref)


def _experts(xa, xb, w1, w3, w2, layer, block_expert, block_valid, n_used):
    n_rows = xa.shape[0]
    d, de = w1.shape[-2:]
    tb = MOE_TILE
    wspec = lambda r, c: pl.BlockSpec((1, 1, r, c), lambda i, be, bv, nu: (layer, be[i], 0, 0))
    xspec = pl.BlockSpec((tb, PACK_WIDTH), lambda i, be, bv, nu: (jnp.minimum(i, nu[0] - 1), 0))
    yspec = pl.BlockSpec((tb, PACK_WIDTH), lambda i, be, bv, nu: (i, 0))
    grid_spec = pltpu.PrefetchScalarGridSpec(
        num_scalar_prefetch=3,
        grid=(n_rows // tb,),
        in_specs=[xspec, xspec, wspec(d, de), wspec(d, de), wspec(de, d)],
        out_specs=[yspec, yspec],
        scratch_shapes=[pltpu.VMEM((d, de), BF16), pltpu.VMEM((d, de), BF16), pltpu.VMEM((de, d), BF16)],
    )
    rows = jax.ShapeDtypeStruct((n_rows, PACK_WIDTH), U32)
    return pl.pallas_call(
        _expert_kernel,
        grid_spec=grid_spec,
        out_shape=[rows, rows],
        compiler_params=_params(("arbitrary",)),
        name="experts",
    )(block_expert, block_valid, n_used, xa, xb, w1, w3, w2)


def _final_kernel(ga0, ga1, gb0, gb1, gt_ref, mod_ref, x_ref, fg_ref, o_ref):
    o_ref[...] = _rms(_moe_residual(x_ref, ga0, ga1, gb0, gb1, gt_ref, mod_ref[0][5:6, :]), fg_ref[...])


def _final(x2, ga, gb, gates, mod_l, final_g, seq):
    n, d = x2.shape
    tm = ROW_TILE
    per_batch = seq // tm
    row = lambda w: pl.BlockSpec((tm, w), lambda i: (i, 0))
    second = pl.BlockSpec((tm, PACK_WIDTH), lambda i: (i + n // tm, 0))
    return pl.pallas_call(
        _final_kernel,
        grid=(n // tm,),
        in_specs=[row(PACK_WIDTH), second, row(PACK_WIDTH), second, row(LANES),
                  pl.BlockSpec((1, 6, d), lambda i: (i // per_batch, 0, 0)), row(d), pl.BlockSpec((1, d), lambda i: (0, 0))],
        out_specs=row(d),
        out_shape=jax.ShapeDtypeStruct((n, d), F32),
        compiler_params=_params(("parallel",)),
        name="final_norm",
    )(ga, ga, gb, gb, gates, mod_l, x2, final_g.reshape(1, d))


def kernel(x, c, w_ada, b_ada, norm1_g, w_in, attn_norm_g, hgrn_lb_logits, hgrn_norm_g, w_out, norm2_g, router_group_w, router_group_b, router_expert_w, router_expert_b, moe_w1, moe_w3, moe_w2, final_g):
    bsz, seq, d = x.shape
    depth = w_ada.shape[0]
    n = bsz * seq
    attn_dim = attn_norm_g.shape[1]
    n_heads = attn_dim // ATTN_HEAD_DIM

    lb_w = jax.nn.softmax(hgrn_lb_logits.astype(F32), axis=0)
    lower_bounds = jnp.cumsum(lb_w, axis=0) - lb_w[0:1]
    mod = _adaln(c, w_ada, b_ada)
    bias = _attn_bias(n_heads)
    hconsts = _hgrn_consts()
    n_rows = n * TOP_K + N_EXPERTS * MOE_TILE
    pad = LANES - N_GROUPS - N_EXPERTS

    x2 = x.reshape(n, d)
    moe = None
    for layer in range(depth):
        x2, (aq, ak, av, rq, rf, ri, rg) = _in_proj(x2, mod[layer], norm1_g[layer], w_in[layer].astype(BF16), seq, moe)
        attn = _attention(aq, ak, av, bias, bsz, seq)
        rec = _hgrn(rq, rf, ri, rg, lower_bounds[layer], hgrn_norm_g[layer], hconsts, bsz, seq)
        wr = jnp.concatenate([router_group_w[layer], router_expert_w[layer], jnp.zeros((d, pad), F32)], axis=1)
        br = jnp.concatenate([router_group_b[layer], router_expert_b[layer], jnp.zeros((pad,), F32)]).reshape(1, LANES)
        x2, ha, hb, logits = _out_proj(attn, rec, x2, mod[layer], attn_norm_g[layer], norm2_g[layer],
                                       w_out[layer].astype(BF16), wr.astype(BF16), br, seq)
        eids, gates = _route(logits)
        pos, block_expert, block_valid, n_used = _plan(eids[:, :TOP_K].T, n_rows)
        ya, yb = _experts(_sc_scatter_rows(ha, pos, n_rows), _sc_scatter_rows(hb, pos, n_rows),
                          moe_w1, moe_w3, moe_w2, layer, block_expert, block_valid, n_used)
        moe = (_sc_gather_rows(ya, pos), _sc_gather_rows(yb, pos), gates, mod[layer])
    return _final(x2, moe[0], moe[1], moe[2], moe[3], final_g, seq).reshape(bsz, seq, d)
```

```python
import functools

import numpy as np
import jax
import jax.numpy as jnp
from jax import lax
from jax.experimental import pallas as pl
from jax.experimental.pallas import tpu as pltpu
from jax.experimental.pallas import tpu_sc as plsc

F32 = jnp.float32
BF16 = jnp.bfloat16
U32 = jnp.uint32

ATTN_HEAD_DIM = 64
ATTN_BLOCK = 128
DILATIONS = (1, 4, 16)
ALIBI_MAX_EXP = 8.0
HGRN_EXPAND = 128
HGRN_CHUNK = 128
N_GROUPS = 4
EXPERTS_PER_GROUP = 8
N_EXPERTS = N_GROUPS * EXPERTS_PER_GROUP
TOP_K = 2
NORM_EPS = 1e-6
LOG2E = 1.4426950408889634

LANES = 128
ATTN_SUPER = ATTN_BLOCK * max(DILATIONS)
ROW_TILE = 512
ATTN_UNROLL = 16
HGRN_TILE = 2048
HGRN_UNROLL = 4
MOE_TILE = 512
PLAN_TILE = 512
PLAN_LANES = 256
PACK_WIDTH = 256
SC_WINDOW = 128
VMEM_LIMIT = 56 << 20


def _params(semantics):
    return pltpu.CompilerParams(dimension_semantics=semantics, vmem_limit_bytes=VMEM_LIMIT)


def _rms(x, g):
    return x * lax.rsqrt(jnp.mean(x * x, axis=-1, keepdims=True) + NORM_EPS) * g


def _dot_nt(a, b):
    return lax.dot_general(a, b, (((1,), (1,)), ((), ())), preferred_element_type=F32)


def _dot_tn(a, b):
    return lax.dot_general(a, b, (((0,), (0,)), ((), ())), preferred_element_type=F32)


def _pack_pair(hi, lo):
    hb = lax.bitcast_convert_type(hi.astype(BF16).astype(F32), U32)
    lb = lax.bitcast_convert_type(lo.astype(BF16).astype(F32), U32)
    return hb | (lb >> 16)


def _unpack_pair(u):
    hi = lax.bitcast_convert_type(u & jnp.uint32(0xFFFF0000), F32)
    lo = lax.bitcast_convert_type(u << 16, F32)
    return hi, lo


def _pack_rows(y):
    w = PACK_WIDTH
    return _pack_pair(y[:, 0:w], y[:, 2 * w:3 * w]), _pack_pair(y[:, w:2 * w], y[:, 3 * w:4 * w])


def _unpack_rows(ua, ub):
    ha, la = _unpack_pair(ua)
    hb, lb = _unpack_pair(ub)
    return jnp.concatenate([ha, hb, la, lb], axis=1)


def _moe_residual(x_ref, ga0, ga1, gb0, gb1, gt_ref, gate_row):
    gt = gt_ref[...]
    y = gt[:, 0:1] * _unpack_rows(ga0[...], gb0[...]) + gt[:, 1:2] * _unpack_rows(ga1[...], gb1[...])
    return x_ref[...] + gate_row * y


def _ada_kernel(c_ref, w_ref, b_ref, o_ref):
    c = c_ref[...]
    ca = c * jax.nn.sigmoid(c)
    o_ref[0] = jnp.dot(ca, w_ref[0], precision=lax.Precision.HIGHEST, preferred_element_type=F32) + b_ref[0]


def _adaln(c, w_ada, b_ada):
    depth, d, n6 = w_ada.shape
    bsz = c.shape[0]
    rows = 8
    cp = jnp.zeros((rows, d), F32).at[:bsz].set(c)
    tn = 1536
    out = pl.pallas_call(
        _ada_kernel,
        grid=(depth, n6 // tn),
        in_specs=[
            pl.BlockSpec((rows, d), lambda l, j: (0, 0)),
            pl.BlockSpec((1, d, tn), lambda l, j: (l, 0, j)),
            pl.BlockSpec((1, 1, tn), lambda l, j: (l, 0, j)),
        ],
        out_specs=pl.BlockSpec((1, rows, tn), lambda l, j: (l, 0, j)),
        out_shape=jax.ShapeDtypeStruct((depth, rows, n6), F32),
        compiler_params=_params(("parallel", "parallel")),
        name="adaln",
    )(cp, w_ada, b_ada.reshape(depth, 1, n6))
    return out[:, :bsz].reshape(depth, bsz, 6, d)


def _in_kernel(*refs, n_moe):
    moe, (x_ref, mod_ref, g_ref, w_ref), rest = refs[:n_moe], refs[n_moe:n_moe + 4], refs[n_moe + 4:]
    hb = rest[-1]
    if n_moe:
        ga0, ga1, gb0, gb1, gt_ref, pmod_ref = moe
        x = _moe_residual(x_ref, ga0, ga1, gb0, gb1, gt_ref, pmod_ref[0][5:6, :])
        rest[0][...] = x
        outs = rest[1:-1]
    else:
        x = x_ref[...]
        outs = rest[:-1]
    mod = mod_ref[0]
    h = _rms(x, g_ref[...]) * (1.0 + mod[1:2, :]) + mod[0:1, :]
    hb[...] = h.astype(BF16)
    width = outs[0].shape[1]
    for i, o in enumerate(outs):
        o[...] = jnp.dot(hb[...], w_ref[:, i * width:(i + 1) * width], preferred_element_type=F32)


def _in_proj(x2, mod_l, g, w_in_bf, seq, moe=None):
    n, d = x2.shape
    n_out = w_in_bf.shape[1]
    width = 512
    tm = ROW_TILE
    per_batch = seq // tm
    row = lambda w: pl.BlockSpec((tm, w), lambda i: (i, 0))
    mod_spec = pl.BlockSpec((1, 6, d), lambda i: (i // per_batch, 0, 0))
    in_specs, args, out_specs, out_shape = [], [], [], []
    if moe is not None:
        ga, gb, gates, mod_prev = moe
        second = pl.BlockSpec((tm, PACK_WIDTH), lambda i: (i + n // tm, 0))
        in_specs += [row(PACK_WIDTH), second, row(PACK_WIDTH), second, row(LANES), mod_spec]
        args += [ga, ga, gb, gb, gates, mod_prev]
        out_specs.append(row(d))
        out_shape.append(jax.ShapeDtypeStruct((n, d), F32))
    in_specs += [row(d), mod_spec, pl.BlockSpec((1, d), lambda i: (0, 0)), pl.BlockSpec((d, n_out), lambda i: (0, 0))]
    args += [x2, mod_l, g.reshape(1, d), w_in_bf]
    out_specs += [row(width)] * (n_out // width)
    out_shape += [jax.ShapeDtypeStruct((n, width), F32)] * (n_out // width)
    outs = pl.pallas_call(
        functools.partial(_in_kernel, n_moe=0 if moe is None else 6),
        grid=(n // tm,),
        in_specs=in_specs,
        out_specs=out_specs,
        out_shape=out_shape,
        scratch_shapes=[pltpu.VMEM((tm, d), BF16)],
        compiler_params=_params(("parallel",)),
        name="in_proj",
    )(*args)
    return (x2, outs) if moe is None else (outs[0], outs[1:])


def _attn_bias(n_heads):
    qi = np.arange(ATTN_BLOCK)[:, None]
    kj = np.arange(2 * ATTN_BLOCK)[None, :]
    dist = ATTN_BLOCK + qi - kj
    valid = (dist >= 0) & (dist <= ATTN_BLOCK)
    slopes = np.exp2(-ALIBI_MAX_EXP * np.arange(1, n_heads + 1, dtype=np.float32) / n_heads)
    out = np.empty((2, len(DILATIONS), n_heads, ATTN_BLOCK, 2 * ATTN_BLOCK), np.float32)
    for p, dil in enumerate(DILATIONS):
        b = -slopes[:, None, None] * (dist * dil).astype(np.float32)[None] * np.float32(LOG2E)
        out[0, p] = np.where(valid[None], b, -np.inf)
        out[1, p] = np.where((valid & (kj >= ATTN_BLOCK))[None], b, -np.inf)
    return jnp.asarray(out)


def _attn_kernel(q_ref, k_ref, v_ref, bias_ref, o_ref, kcar, vcar, pbuf, vbuf, oscr, mscr, dscr):
    n = pl.program_id(2)
    sb = q_ref.shape[0]
    is_lo = lax.broadcasted_iota(jnp.int32, (ATTN_BLOCK, LANES), 1) < ATTN_HEAD_DIM
    scale = ATTN_HEAD_DIM ** -0.5 * LOG2E

    @pl.when(n == 0)
    def _():
        kcar[...] = jnp.zeros_like(kcar)
        vcar[...] = jnp.zeros_like(vcar)

    pbuf[...] = jnp.zeros_like(pbuf)
    vbuf[...] = jnp.zeros_like(vbuf)
    lo2 = lax.broadcasted_iota(jnp.int32, (2 * ATTN_BLOCK, LANES), 1) < ATTN_HEAD_DIM
    zero = jnp.zeros((2 * ATTN_BLOCK, LANES), BF16)
    den_lo = lo2.astype(BF16)
    den_hi = jnp.logical_not(lo2).astype(BF16)

    for p, dil in enumerate(DILATIONS):
        nb = (sb // ATTN_BLOCK) // dil

        def rows_of(it, dil=dil, nb=nb):
            start = (it % nb) * (ATTN_BLOCK * dil) + it // nb
            return pl.ds(start, ATTN_BLOCK, stride=dil) if dil > 1 else pl.ds(pl.multiple_of(start, ATTN_BLOCK), ATTN_BLOCK)

        def pv(it, probs, vaug, p=p, rows_of=rows_of):
            rows = rows_of(it)
            of = jnp.dot(probs, vaug, preferred_element_type=F32)
            oscr[p, rows, :] = of[:, :LANES]
            dscr[p, rows, :] = of[:, LANES:]

        def qk(it, p=p, nb=nb, rows_of=rows_of):
            r = it // nb
            rows = rows_of(it)
            crow = pl.ds(pl.multiple_of(r * ATTN_BLOCK, ATTN_BLOCK), ATTN_BLOCK)
            qb = q_ref[rows, :] * scale
            kc = k_ref[rows, :].astype(BF16)
            vc = v_ref[rows, :].astype(BF16)
            kcat = jnp.concatenate([kcar[p, crow, :], kc], axis=0)
            vcat = jnp.concatenate([vcar[p, crow, :], vc], axis=0)
            vaug = jnp.concatenate([jnp.concatenate([jnp.where(lo2, vcat, zero), den_lo], axis=1),
                                    jnp.concatenate([jnp.where(lo2, zero, vcat), den_hi], axis=1)], axis=0)
            kcar[p, crow, :] = kc
            vcar[p, crow, :] = vc
            first = jnp.logical_and(n == 0, it % nb == 0).astype(jnp.int32)
            probs, acc_m = [], None
            for hh in range(2):
                sel = is_lo if hh == 0 else jnp.logical_not(is_lo)
                qm = jnp.where(sel, qb, 0.0).astype(BF16)
                s = _dot_nt(qm, kcat) + bias_ref[first, p, hh]
                m = jnp.max(s, axis=-1, keepdims=True)
                probs.append(jnp.exp2(s - m).astype(BF16))
                mb = jnp.broadcast_to(m, (ATTN_BLOCK, LANES))
                acc_m = mb if hh == 0 else jnp.where(sel, mb, acc_m)
            mscr[p, rows, :] = acc_m
            return jnp.concatenate(probs, axis=1), vaug

        def body(g, carry, pv=pv, qk=qk):
            it0 = g * ATTN_UNROLL
            pend = (pbuf[...], vbuf[...])
            for u in range(ATTN_UNROLL):
                cur = qk(it0 + u)
                pv(jnp.maximum(it0 + u - 1, 0), *pend)
                pend = cur
            pbuf[...] = pend[0]
            vbuf[...] = pend[1]
            return carry

        n_it = sb // ATTN_BLOCK
        lax.fori_loop(0, n_it // ATTN_UNROLL, body, 0)
        pv(n_it - 1, pbuf[...], vbuf[...])

    ct = 256

    def combine(t, carry):
        rs = pl.ds(pl.multiple_of(t * ct, ct), ct)
        m0, m1, m2 = mscr[0, rs, :], mscr[1, rs, :], mscr[2, rs, :]
        mx = jnp.maximum(jnp.maximum(m0, m1), m2)
        w0, w1, w2 = jnp.exp2(m0 - mx), jnp.exp2(m1 - mx), jnp.exp2(m2 - mx)
        num = w0 * oscr[0, rs, :] + w1 * oscr[1, rs, :] + w2 * oscr[2, rs, :]
        den = w0 * dscr[0, rs, :] + w1 * dscr[1, rs, :] + w2 * dscr[2, rs, :]
        o_ref[rs, :] = num / den
        return carry

    lax.fori_loop(0, sb // ct, combine, 0)


def _attention(aq, ak, av, bias, bsz, seq):
    n, width = aq.shape
    sb = ATTN_SUPER
    nsb = seq // sb
    pairs = width // LANES
    npat = len(DILATIONS)
    blk = pl.BlockSpec((sb, LANES), lambda b, h, t: (b * nsb + t, h))
    return pl.pallas_call(
        _attn_kernel,
        grid=(bsz, pairs, nsb),
        in_specs=[blk, blk, blk,
                  pl.BlockSpec((2, npat, 2, ATTN_BLOCK, 2 * ATTN_BLOCK), lambda b, h, t: (0, 0, h, 0, 0))],
        out_specs=blk,
        out_shape=jax.ShapeDtypeStruct((n, width), F32),
        scratch_shapes=[
            pltpu.VMEM((npat, max(DILATIONS) * ATTN_BLOCK, LANES), BF16),
            pltpu.VMEM((npat, max(DILATIONS) * ATTN_BLOCK, LANES), BF16),
            pltpu.VMEM((ATTN_BLOCK, 4 * ATTN_BLOCK), BF16),
            pltpu.VMEM((4 * ATTN_BLOCK, 2 * LANES), BF16),
            pltpu.VMEM((npat, sb, LANES), F32),
            pltpu.VMEM((npat, sb, LANES), F32),
            pltpu.VMEM((npat, sb, LANES), F32),
        ],
        compiler_params=_params(("parallel", "parallel", "arbitrary")),
        name="dilated_attn",
    )(aq, ak, av, bias)


def _hgrn_consts():
    c = HGRN_CHUNK
    t = np.arange(c)[:, None]
    u = np.arange(c)[None, :]
    low, masks, sgn = [], [], []
    m = c // 2
    while m >= 1:
        mid = (t // (2 * m)) * (2 * m) + m
        is_q = t >= mid
        if m in (4, 2):
            low.append((is_q & (u >= mid) & (u <= t)) | (~is_q & (u > t) & (u < mid)))
        if m >= 8:
            sgn.append(np.broadcast_to(np.where(is_q, 1.0, -1.0), (c, c)))
        if m == 1:
            sgn.append(np.broadcast_to(np.where(is_q, 1.0, 0.0), (c, c)))
        masks.append(((t // (2 * m)) == (u // (2 * m))) & (t % (2 * m) >= m) & (u % (2 * m) < m))
        m //= 2
    masks.append(t == u)
    f = lambda a: np.asarray(a, np.float32)
    return (jnp.asarray(f(u <= t), BF16), jnp.asarray(f(np.concatenate(low, axis=0)), BF16),
            jnp.asarray(f(np.stack(sgn))), jnp.asarray(f(np.stack(masks))))


def _hgrn_kernel(q_ref, z_ref, v_ref, g_ref, lb_ref, gn_ref, tri_ref, low_ref, sgn_ref, msk_ref, o_ref, st):
    c = HGRN_CHUNK

    @pl.when(pl.program_id(2) == 0)
    def _():
        st[...] = jnp.zeros_like(st)

    lb = jnp.maximum(lb_ref[...], 0.0)
    log_lb = jnp.log(lb)
    log_1m = jnp.log1p(-lb)
    one_m = 1.0 - lb
    gn = gn_ref[...]
    n_levels = msk_ref.shape[0] - 1
    coarse = [c >> (i + 1) for i in range(n_levels) if (c >> (i + 1)) >= 8]

    def chunk(ci, carry):
        rs = pl.ds(pl.multiple_of(ci * c, c), c)
        z = z_ref[rs, :]
        qb = q_ref[rs, :].astype(BF16)
        vb = v_ref[rs, :].astype(BF16)
        ez = jnp.exp(-jnp.abs(z))
        ls = jnp.minimum(z, 0.0) - jnp.log(1.0 + ez)
        b2 = log_1m + ls
        lf = jnp.maximum(log_lb, b2) + jnp.log(1.0 + jnp.exp(-jnp.abs(log_lb - b2)))
        rz = 1.0 / (1.0 + ez)
        kb = (one_m * jnp.where(z > 0.0, ez * rz, rz)).astype(BF16)
        l2 = lf * LOG2E
        hi = l2.astype(BF16)
        hl = jnp.concatenate([hi, (l2 - hi.astype(F32)).astype(BF16)], axis=1)
        bb = jnp.dot(tri_ref[...], hl, preferred_element_type=F32)
        b = bb[:, :HGRN_EXPAND] + bb[:, HGRN_EXPAND:]
        dl = jnp.dot(low_ref[...], hl, preferred_element_type=F32)
        dl = dl[:, :HGRN_EXPAND] + dl[:, HGRN_EXPAND:]

        def level(d):
            e = jnp.exp2(d).astype(BF16)
            return _dot_nt(qb * e, kb * e)

        a = msk_ref[n_levels] * _dot_nt(qb, kb)
        for i in range(n_levels):
            m = c >> (i + 1)
            if m >= 8:
                ref = jnp.concatenate([jnp.broadcast_to(b[r0 + m - 1:r0 + m, :], (2 * m, HGRN_EXPAND))
                                       for r0 in range(0, c, 2 * m)], axis=0)
                d = (b - ref) * sgn_ref[i]
            elif m > 1:
                d = dl[(i - len(coarse)) * c:(i - len(coarse) + 1) * c]
            else:
                d = l2 * sgn_ref[len(coarse)]
            a = a + msk_ref[i] * level(d)
        b_last = b[c - 1:c, :]
        s_t = st[...]
        o = _dot_nt(qb * jnp.exp2(b).astype(BF16), s_t.astype(BF16)) + jnp.dot(a.astype(BF16), vb, preferred_element_type=F32)
        st[...] = jnp.exp2(b_last) * s_t + _dot_tn(vb, kb * jnp.exp2(b_last - b).astype(BF16))
        g = g_ref[rs, :]
        o_ref[rs, :] = _rms(o, gn) * (g * jax.nn.sigmoid(g))
        return carry

    lax.fori_loop(0, q_ref.shape[0] // c, chunk, 0, unroll=HGRN_UNROLL)


def _hgrn(rq, rf, ri, rg, lb, gn, consts, bsz, seq):
    n, width = rq.shape
    heads = width // HGRN_EXPAND
    ts = HGRN_TILE
    nt = seq // ts
    blk = pl.BlockSpec((ts, HGRN_EXPAND), lambda b, h, t: (b * nt + t, h))
    vec = pl.BlockSpec((1, HGRN_EXPAND), lambda b, h, t: (0, h))
    const = lambda a: pl.BlockSpec(a.shape, lambda b, h, t: (0,) * a.ndim)
    return pl.pallas_call(
        _hgrn_kernel,
        grid=(bsz, heads, nt),
        in_specs=[blk, blk, blk, blk, vec, vec] + [const(a) for a in consts],
        out_specs=blk,
        out_shape=jax.ShapeDtypeStruct((n, width), F32),
        scratch_shapes=[pltpu.VMEM((HGRN_EXPAND, HGRN_EXPAND), F32)],
        compiler_params=_params(("parallel", "parallel", "arbitrary")),
        name="hgrn2",
    )(rq, rf, ri, rg, lb.reshape(1, width), gn.reshape(1, width), *consts)


def _out_kernel(a_ref, r_ref, x_ref, mod_ref, ag_ref, g2_ref, wo_ref, wr_ref, br_ref, xo_ref, ha_ref, hb_ref, lg_ref):
    mod = mod_ref[0]
    mix = jnp.concatenate([_rms(a_ref[...], ag_ref[...]), r_ref[...]], axis=1).astype(BF16)
    xn = x_ref[...] + mod[2:3, :] * jnp.dot(mix, wo_ref[...], preferred_element_type=F32)
    xo_ref[...] = xn
    h2 = _rms(xn, g2_ref[...]) * (1.0 + mod[4:5, :]) + mod[3:4, :]
    ha_ref[...], hb_ref[...] = _pack_rows(h2)
    lg_ref[...] = jnp.dot(h2.astype(BF16), wr_ref[...], preferred_element_type=F32) + br_ref[...]


def _out_proj(attn, rec, x2, mod_l, ag, g2, wo_bf, wr_bf, br, seq):
    n, d = x2.shape
    half = attn.shape[1]
    tm = ROW_TILE
    per_batch = seq // tm
    row = lambda w: pl.BlockSpec((tm, w), lambda i: (i, 0))
    full = lambda a: pl.BlockSpec(a.shape, lambda i: (0,) * a.ndim)
    ag2, g22 = ag.reshape(1, half), g2.reshape(1, d)
    return pl.pallas_call(
        _out_kernel,
        grid=(n // tm,),
        in_specs=[row(half), row(half), row(d), pl.BlockSpec((1, 6, d), lambda i: (i // per_batch, 0, 0)),
                  full(ag2), full(g22), full(wo_bf), full(wr_bf), full(br)],
        out_specs=[row(d), row(PACK_WIDTH), row(PACK_WIDTH), row(LANES)],
        out_shape=[jax.ShapeDtypeStruct((n, d), F32), jax.ShapeDtypeStruct((n, PACK_WIDTH), U32),
                   jax.ShapeDtypeStruct((n, PACK_WIDTH), U32), jax.ShapeDtypeStruct((n, LANES), F32)],
        compiler_params=_params(("parallel",)),
        name="out_proj",
    )(attn, rec, x2, mod_l, ag2, g22, wo_bf, wr_bf, br)


def _route_kernel(lg_ref, e_ref, g_ref):
    lg = lg_ref[...]
    lane = lax.broadcasted_iota(jnp.int32, lg.shape, 1).astype(F32)
    big = float(LANES)

    def first_argmax(vals):
        mx = jnp.max(vals, axis=-1, keepdims=True)
        return mx, jnp.min(jnp.where(vals == mx, lane, big), axis=-1, keepdims=True)

    is_group = lane < N_GROUPS
    gmax, gidx = first_argmax(jnp.where(is_group, lg, -jnp.inf))
    gprob = 1.0 / jnp.sum(jnp.where(is_group, jnp.exp(lg - gmax), 0.0), axis=-1, keepdims=True)
    lo = N_GROUPS + EXPERTS_PER_GROUP * gidx
    el = jnp.where(jnp.logical_and(lane >= lo, lane < lo + EXPERTS_PER_GROUP), lg, -jnp.inf)
    t1, i1 = first_argmax(el)
    t2, i2 = first_argmax(jnp.where(lane == i1, -jnp.inf, el))
    ex = jnp.exp(t2 - t1)
    p1 = 1.0 / (1.0 + ex)
    p2 = ex / (1.0 + ex)
    e_ref[...] = jnp.where(lane == 0, i1 - N_GROUPS, jnp.where(lane == 1, i2 - N_GROUPS, 0.0)).astype(jnp.int32)
    g_ref[...] = jnp.where(lane == 0, p1 * gprob, jnp.where(lane == 1, p2 * gprob, 0.0))


def _route(logits):
    n = logits.shape[0]
    tm = ROW_TILE
    spec = pl.BlockSpec((tm, LANES), lambda i: (i, 0))
    return pl.pallas_call(
        _route_kernel,
        grid=(n // tm,),
        in_specs=[spec],
        out_specs=[spec, spec],
        out_shape=[jax.ShapeDtypeStruct((n, LANES), jnp.int32), jax.ShapeDtypeStruct((n, LANES), F32)],
        compiler_params=_params(("parallel",)),
        name="route",
    )(logits)


def _plan_kernel(e_ref, ut_ref, pos_ref, be_ref, bv_ref, nu_ref, cnt, off, run):
    phase = pl.program_id(0)
    i = pl.program_id(1)
    t = e_ref.shape[1]
    eio = lax.broadcasted_iota(jnp.int32, (N_EXPERTS, t), 0)
    oh1 = e_ref[0:1, :] == eio
    oh2 = e_ref[1:2, :] == eio
    oh = jnp.logical_or(oh1, oh2).astype(F32)
    tot = jnp.sum(oh, axis=1, keepdims=True)

    @pl.when(jnp.logical_and(phase == 0, i == 0))
    def _():
        cnt[...] = jnp.zeros_like(cnt)

    @pl.when(phase == 0)
    def _():
        cnt[...] += tot

    @pl.when(jnp.logical_and(phase == 1, i == 0))
    def _():
        counts = cnt[...]
        nb = jnp.floor((counts + (MOE_TILE - 1)) * (1.0 / MOE_TILE))
        er = lax.broadcasted_iota(jnp.int32, (N_EXPERTS, N_EXPERTS), 0)
        ec = lax.broadcasted_iota(jnp.int32, (N_EXPERTS, N_EXPERTS), 1)
        cum_excl = jnp.dot((ec < er).astype(BF16), nb.astype(BF16), preferred_element_type=F32)
        off[...] = cum_excl * MOE_TILE
        run[...] = jnp.zeros_like(run)
        ce, ci, cn = cum_excl[:, 0:1], (cum_excl + nb)[:, 0:1], counts[:, 0:1]
        b = lax.broadcasted_iota(jnp.int32, (N_EXPERTS, be_ref.shape[1]), 1).astype(F32)
        member = jnp.logical_and(b >= ce, b < ci)
        be = jnp.minimum(jnp.sum((b >= ci).astype(F32), axis=0, keepdims=True), N_EXPERTS - 1.0)
        bv = jnp.sum(jnp.where(member, jnp.minimum(cn - (b - ce) * MOE_TILE, float(MOE_TILE)), 0.0), axis=0, keepdims=True)
        be_ref[...] = be.astype(jnp.int32)
        bv_ref[...] = bv.astype(jnp.int32)
        nu_ref[...] = jnp.broadcast_to(ci[N_EXPERTS - 1:N_EXPERTS, :], nu_ref.shape).astype(jnp.int32)

    @pl.when(phase == 1)
    def _():
        c = jnp.dot(oh.astype(BF16), ut_ref[...], preferred_element_type=F32)
        r = off[:, 0:1] + run[:, 0:1] + c - 1.0
        pos_ref[0:1, :] = jnp.sum(jnp.where(oh1, r, 0.0), axis=0, keepdims=True).astype(jnp.int32)
        pos_ref[1:2, :] = jnp.sum(jnp.where(oh2, r, 0.0), axis=0, keepdims=True).astype(jnp.int32)
        run[...] += tot


def _plan(expert_ids_t, n_rows):
    n = expert_ids_t.shape[1]
    t = PLAN_TILE
    n_blocks = n_rows // MOE_TILE
    assert n_blocks <= PLAN_LANES and TOP_K == 2
    ut = jnp.asarray(np.triu(np.ones((t, t), np.float32)), BF16)
    tab = jax.ShapeDtypeStruct((1, PLAN_LANES), jnp.int32)
    const = lambda shape: pl.BlockSpec(shape, lambda p, i: (0, 0))
    pos, be, bv, nu = pl.pallas_call(
        _plan_kernel,
        grid=(2, n // t),
        in_specs=[pl.BlockSpec((TOP_K, t), lambda p, i: (0, i)), const((t, t))],
        out_specs=[pl.BlockSpec((TOP_K, t), lambda p, i: (0, i * p)), const((1, PLAN_LANES)), const((1, PLAN_LANES)),
                   const((1, LANES))],
        out_shape=[jax.ShapeDtypeStruct((TOP_K, n), jnp.int32), tab, tab, jax.ShapeDtypeStruct((1, LANES), jnp.int32)],
        scratch_shapes=[pltpu.VMEM((N_EXPERTS, LANES), F32)] * 3,
        compiler_params=_params(("arbitrary", "arbitrary")),
        name="plan",
    )(expert_ids_t, ut)
    return pos, be.reshape(-1), bv.reshape(-1), nu.reshape(-1)[:1]


def _sc_mesh():
    return plsc.VectorSubcoreMesh(core_axis_name="c", subcore_axis_name="s")


def _sc_scatter_rows(x, pos, n_rows):
    n, d = x.shape
    w = SC_WINDOW

    @functools.partial(pl.kernel, out_type=jax.ShapeDtypeStruct((n_rows, d), x.dtype), mesh=_sc_mesh(), scratch_types=[])
    def scatter(x_hbm, p_hbm, o_hbm):
        def body(x_vmem, i_vmem):
            for k in range(TOP_K):
                pltpu.sync_copy(x_vmem, o_hbm.at[i_vmem.at[k]])

        pltpu.emit_pipeline(
            body,
            grid=(n // w,),
            in_specs=[pl.BlockSpec((w, d), lambda i: (i, 0)), pl.BlockSpec((TOP_K, w), lambda i: (0, i))],
            out_specs=[],
            core_axis_name=("c", "s"),
            dimension_semantics=(pltpu.PARALLEL,),
        )(x_hbm, p_hbm)

    return scatter(x, pos)


def _sc_gather_rows(y, pos):
    m = pos.shape[0] * pos.shape[1]
    d = y.shape[1]
    w = SC_WINDOW

    @functools.partial(pl.kernel, out_type=jax.ShapeDtypeStruct((m, d), y.dtype), mesh=_sc_mesh(), scratch_types=[])
    def gather(y_hbm, p_hbm, o_hbm):
        def body(i_vmem, o_vmem):
            pltpu.sync_copy(y_hbm.at[i_vmem.at[0]], o_vmem)

        pltpu.emit_pipeline(
            body,
            grid=(m // w,),
            in_specs=[pl.BlockSpec((1, w), lambda i: (0, i))],
            out_specs=[pl.BlockSpec((w, d), lambda i: (i, 0))],
            core_axis_name=("c", "s"),
            dimension_semantics=(pltpu.PARALLEL,),
        )(p_hbm, o_hbm)

    return gather(y, pos.reshape(1, m))


def _expert_kernel(be_ref, bv_ref, nu_ref, xa_ref, xb_ref, w1_ref, w3_ref, w2_ref, ya_ref, yb_ref, w1b, w3b, w2b):
    i = pl.program_id(0)
    tb = xa_ref.shape[0]

    @pl.when(i < nu_ref[0])
    def _():
        e = be_ref[i]
        e_prev = be_ref[jnp.maximum(i - 1, 0)]

        @pl.when(jnp.logical_or(i == 0, e != e_prev))
        def _():
            w1b[...] = w1_ref[0, 0].astype(BF16)
            w3b[...] = w3_ref[0, 0].astype(BF16)
            w2b[...] = w2_ref[0, 0].astype(BF16)

        live = lax.broadcasted_iota(jnp.int32, (tb, 1), 0) < bv_ref[i]
        x = jnp.where(live, _unpack_rows(xa_ref[...], xb_ref[...]), 0.0).astype(BF16)
        a = jnp.dot(x, w1b[...], preferred_element_type=F32)
        b = jnp.dot(x, w3b[...], preferred_element_type=F32)
        hm = (a * jax.nn.sigmoid(a) * b).astype(BF16)
        ya_ref[...], yb_ref[...] = _pack_rows(jnp.dot(hm, w2b[...], preferred_element_type=F32))

    @pl.when(i >= nu_ref[0])
    def _():
        ya_ref[...] = jnp.zeros_like(ya_ref)
        yb_ref[...] = jnp.zeros_like(yb_ref)


def _experts(xa, xb, w1, w3, w2, layer, block_expert, block_valid, n_used):
    n_rows = xa.shape[0]
    d, de = w1.shape[-2:]
    tb = MOE_TILE
    wspec = lambda r, c: pl.BlockSpec((1, 1, r, c), lambda i, be, bv, nu: (layer, be[i], 0, 0))
    xspec = pl.BlockSpec((tb, PACK_WIDTH), lambda i, be, bv, nu: (jnp.minimum(i, nu[0] - 1), 0))
    yspec = pl.BlockSpec((tb, PACK_WIDTH), lambda i, be, bv, nu: (i, 0))
    grid_spec = pltpu.PrefetchScalarGridSpec(
        num_scalar_prefetch=3,
        grid=(n_rows // tb,),
        in_specs=[xspec, xspec, wspec(d, de), wspec(d, de), wspec(de, d)],
        out_specs=[yspec, yspec],
        scratch_shapes=[pltpu.VMEM((d, de), BF16), pltpu.VMEM((d, de), BF16), pltpu.VMEM((de, d), BF16)],
    )
    rows = jax.ShapeDtypeStruct((n_rows, PACK_WIDTH), U32)
    return pl.pallas_call(
        _expert_kernel,
        grid_spec=grid_spec,
        out_shape=[rows, rows],
        compiler_params=_params(("arbitrary",)),
        name="experts",
    )(block_expert, block_valid, n_used, xa, xb, w1, w3, w2)


def _final_kernel(ga0, ga1, gb0, gb1, gt_ref, mod_ref, x_ref, fg_ref, o_ref):
    o_ref[...] = _rms(_moe_residual(x_ref, ga0, ga1, gb0, gb1, gt_ref, mod_ref[0][5:6, :]), fg_ref[...])


def _final(x2, ga, gb, gates, mod_l, final_g, seq):
    n, d = x2.shape
    tm = ROW_TILE
    per_batch = seq // tm
    row = lambda w: pl.BlockSpec((tm, w), lambda i: (i, 0))
    second = pl.BlockSpec((tm, PACK_WIDTH), lambda i: (i + n // tm, 0))
    return pl.pallas_call(
        _final_kernel,
        grid=(n // tm,),
        in_specs=[row(PACK_WIDTH), second, row(PACK_WIDTH), second, row(LANES),
                  pl.BlockSpec((1, 6, d), lambda i: (i // per_batch, 0, 0)), row(d), pl.BlockSpec((1, d), lambda i: (0, 0))],
        out_specs=row(d),
        out_shape=jax.ShapeDtypeStruct((n, d), F32),
        compiler_params=_params(("parallel",)),
        name="final_norm",
    )(ga, ga, gb, gb, gates, mod_l, x2, final_g.reshape(1, d))


def kernel(x, c, w_ada, b_ada, norm1_g, w_in, attn_norm_g, hgrn_lb_logits, hgrn_norm_g, w_out, norm2_g, router_group_w, router_group_b, router_expert_w, router_expert_b, moe_w1, moe_w3, moe_w2, final_g):
    bsz, seq, d = x.shape
    depth = w_ada.shape[0]
    n = bsz * seq
    attn_dim = attn_norm_g.shape[1]
    n_heads = attn_dim // ATTN_HEAD_DIM

    lb_w = jax.nn.softmax(hgrn_lb_logits.astype(F32), axis=0)
    lower_bounds = jnp.cumsum(lb_w, axis=0) - lb_w[0:1]
    mod = _adaln(c, w_ada, b_ada)
    bias = _attn_bias(n_heads)
    hconsts = _hgrn_consts()
    n_rows = n * TOP_K + N_EXPERTS * MOE_TILE
    pad = LANES - N_GROUPS - N_EXPERTS

    x2 = x.reshape(n, d)
    moe = None
    for layer in range(depth):
        x2, (aq, ak, av, rq, rf, ri, rg) = _in_proj(x2, mod[layer], norm1_g[layer], w_in[layer].astype(BF16), seq, moe)
        attn = _attention(aq, ak, av, bias, bsz, seq)
        rec = _hgrn(rq, rf, ri, rg, lower_bounds[layer], hgrn_norm_g[layer], hconsts, bsz, seq)
        wr = jnp.concatenate([router_group_w[layer], router_expert_w[layer], jnp.zeros((d, pad), F32)], axis=1)
        br = jnp.concatenate([router_group_b[layer], router_expert_b[layer], jnp.zeros((pad,), F32)]).reshape(1, LANES)
        x2, ha, hb, logits = _out_proj(attn, rec, x2, mod[layer], attn_norm_g[layer], norm2_g[layer],
                                       w_out[layer].astype(BF16), wr.astype(BF16), br, seq)
        eids, gates = _route(logits)
        pos, block_expert, block_valid, n_used = _plan(eids[:, :TOP_K].T, n_rows)
        ya, yb = _experts(_sc_scatter_rows(ha, pos, n_rows), _sc_scatter_rows(hb, pos, n_rows),
                          moe_w1, moe_w3, moe_w2, layer, block_expert, block_valid, n_used)
        moe = (_sc_gather_rows(ya, pos), _sc_gather_rows(yb, pos), gates, mod[layer])
    return _final(x2, moe[0], moe[1], moe[2], moe[3], final_g, seq).reshape(bsz, seq, d)
```

```python
import functools

import numpy as np
import jax
import jax.numpy as jnp
from jax import lax
from jax.experimental import pallas as pl
from jax.experimental.pallas import tpu as pltpu
from jax.experimental.pallas import tpu_sc as plsc

F32 = jnp.float32
BF16 = jnp.bfloat16
U32 = jnp.uint32

ATTN_HEAD_DIM = 64
ATTN_BLOCK = 128
DILATIONS = (1, 4, 16)
ALIBI_MAX_EXP = 8.0
HGRN_EXPAND = 128
HGRN_CHUNK = 128
N_GROUPS = 4
EXPERTS_PER_GROUP = 8
N_EXPERTS = N_GROUPS * EXPERTS_PER_GROUP
TOP_K = 2
NORM_EPS = 1e-6
LOG2E = 1.4426950408889634

LANES = 128
ATTN_SUPER = ATTN_BLOCK * max(DILATIONS)
ROW_TILE = 512
ATTN_UNROLL = 16
HGRN_TILE = 2048
HGRN_UNROLL = 4
MOE_TILE = 256
PLAN_TILE = 512
PACK_WIDTH = 256
SC_WINDOW = 128
VMEM_LIMIT = 56 << 20


def _params(semantics):
    return pltpu.CompilerParams(dimension_semantics=semantics, vmem_limit_bytes=VMEM_LIMIT)


def _rms(x, g):
    return x * lax.rsqrt(jnp.mean(x * x, axis=-1, keepdims=True) + NORM_EPS) * g


def _dot_nt(a, b):
    return lax.dot_general(a, b, (((1,), (1,)), ((), ())), preferred_element_type=F32)


def _dot_tn(a, b):
    return lax.dot_general(a, b, (((0,), (0,)), ((), ())), preferred_element_type=F32)


def _pack_pair(hi, lo):
    hb = lax.bitcast_convert_type(hi.astype(BF16).astype(F32), U32)
    lb = lax.bitcast_convert_type(lo.astype(BF16).astype(F32), U32)
    return hb | (lb >> 16)


def _unpack_pair(u):
    hi = lax.bitcast_convert_type(u & jnp.uint32(0xFFFF0000), F32)
    lo = lax.bitcast_convert_type(u << 16, F32)
    return hi, lo


def _pack_rows(y):
    w = PACK_WIDTH
    return _pack_pair(y[:, 0:w], y[:, 2 * w:3 * w]), _pack_pair(y[:, w:2 * w], y[:, 3 * w:4 * w])


def _unpack_rows(ua, ub):
    ha, la = _unpack_pair(ua)
    hb, lb = _unpack_pair(ub)
    return jnp.concatenate([ha, hb, la, lb], axis=1)


def _moe_residual(x_ref, ga0, ga1, gb0, gb1, gt_ref, gate_row):
    gt = gt_ref[...]
    y = gt[:, 0:1] * _unpack_rows(ga0[...], gb0[...]) + gt[:, 1:2] * _unpack_rows(ga1[...], gb1[...])
    return x_ref[...] + gate_row * y


def _ada_kernel(c_ref, w_ref, b_ref, o_ref):
    c = c_ref[...]
    ca = c * jax.nn.sigmoid(c)
    o_ref[0] = jnp.dot(ca, w_ref[0], precision=lax.Precision.HIGHEST, preferred_element_type=F32) + b_ref[0]


def _adaln(c, w_ada, b_ada):
    depth, d, n6 = w_ada.shape
    bsz = c.shape[0]
    rows = 8
    cp = jnp.zeros((rows, d), F32).at[:bsz].set(c)
    tn = 1536
    out = pl.pallas_call(
        _ada_kernel,
        grid=(depth, n6 // tn),
        in_specs=[
            pl.BlockSpec((rows, d), lambda l, j: (0, 0)),
            pl.BlockSpec((1, d, tn), lambda l, j: (l, 0, j)),
            pl.BlockSpec((1, 1, tn), lambda l, j: (l, 0, j)),
        ],
        out_specs=pl.BlockSpec((1, rows, tn), lambda l, j: (l, 0, j)),
        out_shape=jax.ShapeDtypeStruct((depth, rows, n6), F32),
        compiler_params=_params(("parallel", "parallel")),
        name="adaln",
    )(cp, w_ada, b_ada.reshape(depth, 1, n6))
    return out[:, :bsz].reshape(depth, bsz, 6, d)


def _in_kernel(*refs, n_moe):
    moe, (x_ref, mod_ref, g_ref, w_ref), rest = refs[:n_moe], refs[n_moe:n_moe + 4], refs[n_moe + 4:]
    hb = rest[-1]
    if n_moe:
        ga0, ga1, gb0, gb1, gt_ref, pmod_ref = moe
        x = _moe_residual(x_ref, ga0, ga1, gb0, gb1, gt_ref, pmod_ref[0][5:6, :])
        rest[0][...] = x
        outs = rest[1:-1]
    else:
        x = x_ref[...]
        outs = rest[:-1]
    mod = mod_ref[0]
    h = _rms(x, g_ref[...]) * (1.0 + mod[1:2, :]) + mod[0:1, :]
    hb[...] = h.astype(BF16)
    width = outs[0].shape[1]
    for i, o in enumerate(outs):
        o[...] = jnp.dot(hb[...], w_ref[:, i * width:(i + 1) * width], preferred_element_type=F32)


def _in_proj(x2, mod_l, g, w_in_bf, seq, moe=None):
    n, d = x2.shape
    n_out = w_in_bf.shape[1]
    width = 512
    tm = ROW_TILE
    per_batch = seq // tm
    row = lambda w: pl.BlockSpec((tm, w), lambda i: (i, 0))
    mod_spec = pl.BlockSpec((1, 6, d), lambda i: (i // per_batch, 0, 0))
    in_specs, args, out_specs, out_shape = [], [], [], []
    if moe is not None:
        ga, gb, gates, mod_prev = moe
        second = pl.BlockSpec((tm, PACK_WIDTH), lambda i: (i + n // tm, 0))
        in_specs += [row(PACK_WIDTH), second, row(PACK_WIDTH), second, row(LANES), mod_spec]
        args += [ga, ga, gb, gb, gates, mod_prev]
        out_specs.append(row(d))
        out_shape.append(jax.ShapeDtypeStruct((n, d), F32))
    in_specs += [row(d), mod_spec, pl.BlockSpec((1, d), lambda i: (0, 0)), pl.BlockSpec((d, n_out), lambda i: (0, 0))]
    args += [x2, mod_l, g.reshape(1, d), w_in_bf]
    out_specs += [row(width)] * (n_out // width)
    out_shape += [jax.ShapeDtypeStruct((n, width), F32)] * (n_out // width)
    outs = pl.pallas_call(
        functools.partial(_in_kernel, n_moe=0 if moe is None else 6),
        grid=(n // tm,),
        in_specs=in_specs,
        out_specs=out_specs,
        out_shape=out_shape,
        scratch_shapes=[pltpu.VMEM((tm, d), BF16)],
        compiler_params=_params(("parallel",)),
        name="in_proj",
    )(*args)
    return (x2, outs) if moe is None else (outs[0], outs[1:])


def _attn_bias(n_heads):
    qi = np.arange(ATTN_BLOCK)[:, None]
    kj = np.arange(2 * ATTN_BLOCK)[None, :]
    dist = ATTN_BLOCK + qi - kj
    valid = (dist >= 0) & (dist <= ATTN_BLOCK)
    slopes = np.exp2(-ALIBI_MAX_EXP * np.arange(1, n_heads + 1, dtype=np.float32) / n_heads)
    out = np.empty((2, len(DILATIONS), n_heads, ATTN_BLOCK, 2 * ATTN_BLOCK), np.float32)
    for p, dil in enumerate(DILATIONS):
        b = -slopes[:, None, None] * (dist * dil).astype(np.float32)[None] * np.float32(LOG2E)
        out[0, p] = np.where(valid[None], b, -np.inf)
        out[1, p] = np.where((valid & (kj >= ATTN_BLOCK))[None], b, -np.inf)
    return jnp.asarray(out)


def _attn_kernel(q_ref, k_ref, v_ref, bias_ref, o_ref, kcar, vcar, pbuf, vbuf, oscr, mscr, dscr):
    n = pl.program_id(2)
    sb = q_ref.shape[0]
    is_lo = lax.broadcasted_iota(jnp.int32, (ATTN_BLOCK, LANES), 1) < ATTN_HEAD_DIM
    scale = ATTN_HEAD_DIM ** -0.5 * LOG2E

    @pl.when(n == 0)
    def _():
        kcar[...] = jnp.zeros_like(kcar)
        vcar[...] = jnp.zeros_like(vcar)

    pbuf[...] = jnp.zeros_like(pbuf)
    vbuf[...] = jnp.zeros_like(vbuf)
    lo2 = lax.broadcasted_iota(jnp.int32, (2 * ATTN_BLOCK, LANES), 1) < ATTN_HEAD_DIM
    zero = jnp.zeros((2 * ATTN_BLOCK, LANES), BF16)
    den_lo = lo2.astype(BF16)
    den_hi = jnp.logical_not(lo2).astype(BF16)

    for p, dil in enumerate(DILATIONS):
        nb = (sb // ATTN_BLOCK) // dil

        def rows_of(it, dil=dil, nb=nb):
            start = (it % nb) * (ATTN_BLOCK * dil) + it // nb
            return pl.ds(start, ATTN_BLOCK, stride=dil) if dil > 1 else pl.ds(pl.multiple_of(start, ATTN_BLOCK), ATTN_BLOCK)

        def pv(it, probs, vaug, p=p, rows_of=rows_of):
            rows = rows_of(it)
            of = jnp.dot(probs, vaug, preferred_element_type=F32)
            oscr[p, rows, :] = of[:, :LANES]
            dscr[p, rows, :] = of[:, LANES:]

        def qk(it, p=p, nb=nb, rows_of=rows_of):
            r = it // nb
            rows = rows_of(it)
            crow = pl.ds(pl.multiple_of(r * ATTN_BLOCK, ATTN_BLOCK), ATTN_BLOCK)
            qb = q_ref[rows, :] * scale
            kc = k_ref[rows, :].astype(BF16)
            vc = v_ref[rows, :].astype(BF16)
            kcat = jnp.concatenate([kcar[p, crow, :], kc], axis=0)
            vcat = jnp.concatenate([vcar[p, crow, :], vc], axis=0)
            vaug = jnp.concatenate([jnp.concatenate([jnp.where(lo2, vcat, zero), den_lo], axis=1),
                                    jnp.concatenate([jnp.where(lo2, zero, vcat), den_hi], axis=1)], axis=0)
            kcar[p, crow, :] = kc
            vcar[p, crow, :] = vc
            first = jnp.logical_and(n == 0, it % nb == 0).astype(jnp.int32)
            probs, acc_m = [], None
            for hh in range(2):
                sel = is_lo if hh == 0 else jnp.logical_not(is_lo)
                qm = jnp.where(sel, qb, 0.0).astype(BF16)
                s = _dot_nt(qm, kcat) + bias_ref[first, p, hh]
                m = jnp.max(s, axis=-1, keepdims=True)
                probs.append(jnp.exp2(s - m).astype(BF16))
                mb = jnp.broadcast_to(m, (ATTN_BLOCK, LANES))
                acc_m = mb if hh == 0 else jnp.where(sel, mb, acc_m)
            mscr[p, rows, :] = acc_m
            return jnp.concatenate(probs, axis=1), vaug

        def body(g, carry, pv=pv, qk=qk):
            it0 = g * ATTN_UNROLL
            pend = (pbuf[...], vbuf[...])
            for u in range(ATTN_UNROLL):
                cur = qk(it0 + u)
                pv(jnp.maximum(it0 + u - 1, 0), *pend)
                pend = cur
            pbuf[...] = pend[0]
            vbuf[...] = pend[1]
            return carry

        n_it = sb // ATTN_BLOCK
        lax.fori_loop(0, n_it // ATTN_UNROLL, body, 0)
        pv(n_it - 1, pbuf[...], vbuf[...])

    ct = 256

    def combine(t, carry):
        rs = pl.ds(pl.multiple_of(t * ct, ct), ct)
        m0, m1, m2 = mscr[0, rs, :], mscr[1, rs, :], mscr[2, rs, :]
        mx = jnp.maximum(jnp.maximum(m0, m1), m2)
        w0, w1, w2 = jnp.exp2(m0 - mx), jnp.exp2(m1 - mx), jnp.exp2(m2 - mx)
        num = w0 * oscr[0, rs, :] + w1 * oscr[1, rs, :] + w2 * oscr[2, rs, :]
        den = w0 * dscr[0, rs, :] + w1 * dscr[1, rs, :] + w2 * dscr[2, rs, :]
        o_ref[rs, :] = num / den
        return carry

    lax.fori_loop(0, sb // ct, combine, 0)


def _attention(aq, ak, av, bias, bsz, seq):
    n, width = aq.shape
    sb = ATTN_SUPER
    nsb = seq // sb
    pairs = width // LANES
    npat = len(DILATIONS)
    blk = pl.BlockSpec((sb, LANES), lambda b, h, t: (b * nsb + t, h))
    return pl.pallas_call(
        _attn_kernel,
        grid=(bsz, pairs, nsb),
        in_specs=[blk, blk, blk,
                  pl.BlockSpec((2, npat, 2, ATTN_BLOCK, 2 * ATTN_BLOCK), lambda b, h, t: (0, 0, h, 0, 0))],
        out_specs=blk,
        out_shape=jax.ShapeDtypeStruct((n, width), F32),
        scratch_shapes=[
            pltpu.VMEM((npat, max(DILATIONS) * ATTN_BLOCK, LANES), BF16),
            pltpu.VMEM((npat, max(DILATIONS) * ATTN_BLOCK, LANES), BF16),
            pltpu.VMEM((ATTN_BLOCK, 4 * ATTN_BLOCK), BF16),
            pltpu.VMEM((4 * ATTN_BLOCK, 2 * LANES), BF16),
            pltpu.VMEM((npat, sb, LANES), F32),
            pltpu.VMEM((npat, sb, LANES), F32),
            pltpu.VMEM((npat, sb, LANES), F32),
        ],
        compiler_params=_params(("parallel", "parallel", "arbitrary")),
        name="dilated_attn",
    )(aq, ak, av, bias)


def _hgrn_consts():
    c = HGRN_CHUNK
    t = np.arange(c)[:, None]
    u = np.arange(c)[None, :]
    low, masks, sgn = [], [], []
    m = c // 2
    while m >= 1:
        mid = (t // (2 * m)) * (2 * m) + m
        is_q = t >= mid
        if m in (4, 2):
            low.append((is_q & (u >= mid) & (u <= t)) | (~is_q & (u > t) & (u < mid)))
        if m >= 8:
            sgn.append(np.broadcast_to(np.where(is_q, 1.0, -1.0), (c, c)))
        if m == 1:
            sgn.append(np.broadcast_to(np.where(is_q, 1.0, 0.0), (c, c)))
        masks.append(((t // (2 * m)) == (u // (2 * m))) & (t % (2 * m) >= m) & (u % (2 * m) < m))
        m //= 2
    masks.append(t == u)
    f = lambda a: np.asarray(a, np.float32)
    return (jnp.asarray(f(u <= t), BF16), jnp.asarray(f(np.concatenate(low, axis=0)), BF16),
            jnp.asarray(f(np.stack(sgn))), jnp.asarray(f(np.stack(masks))))


def _hgrn_kernel(q_ref, z_ref, v_ref, g_ref, lb_ref, gn_ref, tri_ref, low_ref, sgn_ref, msk_ref, o_ref, st):
    c = HGRN_CHUNK

    @pl.when(pl.program_id(2) == 0)
    def _():
        st[...] = jnp.zeros_like(st)

    lb = jnp.maximum(lb_ref[...], 0.0)
    log_lb = jnp.log(lb)
    log_1m = jnp.log1p(-lb)
    one_m = 1.0 - lb
    gn = gn_ref[...]
    n_levels = msk_ref.shape[0] - 1
    coarse = [c >> (i + 1) for i in range(n_levels) if (c >> (i + 1)) >= 8]

    def chunk(ci, carry):
        rs = pl.ds(pl.multiple_of(ci * c, c), c)
        z = z_ref[rs, :]
        qb = q_ref[rs, :].astype(BF16)
        vb = v_ref[rs, :].astype(BF16)
        ez = jnp.exp(-jnp.abs(z))
        ls = jnp.minimum(z, 0.0) - jnp.log(1.0 + ez)
        b2 = log_1m + ls
        lf = jnp.maximum(log_lb, b2) + jnp.log(1.0 + jnp.exp(-jnp.abs(log_lb - b2)))
        rz = 1.0 / (1.0 + ez)
        kb = (one_m * jnp.where(z > 0.0, ez * rz, rz)).astype(BF16)
        l2 = lf * LOG2E
        hi = l2.astype(BF16)
        hl = jnp.concatenate([hi, (l2 - hi.astype(F32)).astype(BF16)], axis=1)
        bb = jnp.dot(tri_ref[...], hl, preferred_element_type=F32)
        b = bb[:, :HGRN_EXPAND] + bb[:, HGRN_EXPAND:]
        dl = jnp.dot(low_ref[...], hl, preferred_element_type=F32)
        dl = dl[:, :HGRN_EXPAND] + dl[:, HGRN_EXPAND:]

        def level(d):
            e = jnp.exp2(d).astype(BF16)
            return _dot_nt(qb * e, kb * e)

        a = msk_ref[n_levels] * _dot_nt(qb, kb)
        for i in range(n_levels):
            m = c >> (i + 1)
            if m >= 8:
                ref = jnp.concatenate([jnp.broadcast_to(b[r0 + m - 1:r0 + m, :], (2 * m, HGRN_EXPAND))
                                       for r0 in range(0, c, 2 * m)], axis=0)
                d = (b - ref) * sgn_ref[i]
            elif m > 1:
                d = dl[(i - len(coarse)) * c:(i - len(coarse) + 1) * c]
            else:
                d = l2 * sgn_ref[len(coarse)]
            a = a + msk_ref[i] * level(d)
        b_last = b[c - 1:c, :]
        s_t = st[...]
        o = _dot_nt(qb * jnp.exp2(b).astype(BF16), s_t.astype(BF16)) + jnp.dot(a.astype(BF16), vb, preferred_element_type=F32)
        st[...] = jnp.exp2(b_last) * s_t + _dot_tn(vb, kb * jnp.exp2(b_last - b).astype(BF16))
        g = g_ref[rs, :]
        o_ref[rs, :] = _rms(o, gn) * (g * jax.nn.sigmoid(g))
        return carry

    lax.fori_loop(0, q_ref.shape[0] // c, chunk, 0, unroll=HGRN_UNROLL)


def _hgrn(rq, rf, ri, rg, lb, gn, consts, bsz, seq):
    n, width = rq.shape
    heads = width // HGRN_EXPAND
    ts = HGRN_TILE
    nt = seq // ts
    blk = pl.BlockSpec((ts, HGRN_EXPAND), lambda b, h, t: (b * nt + t, h))
    vec = pl.BlockSpec((1, HGRN_EXPAND), lambda b, h, t: (0, h))
    const = lambda a: pl.BlockSpec(a.shape, lambda b, h, t: (0,) * a.ndim)
    return pl.pallas_call(
        _hgrn_kernel,
        grid=(bsz, heads, nt),
        in_specs=[blk, blk, blk, blk, vec, vec] + [const(a) for a in consts],
        out_specs=blk,
        out_shape=jax.ShapeDtypeStruct((n, width), F32),
        scratch_shapes=[pltpu.VMEM((HGRN_EXPAND, HGRN_EXPAND), F32)],
        compiler_params=_params(("parallel", "parallel", "arbitrary")),
        name="hgrn2",
    )(rq, rf, ri, rg, lb.reshape(1, width), gn.reshape(1, width), *consts)


def _out_kernel(a_ref, r_ref, x_ref, mod_ref, ag_ref, g2_ref, wo_ref, wr_ref, br_ref, xo_ref, ha_ref, hb_ref, lg_ref):
    mod = mod_ref[0]
    mix = jnp.concatenate([_rms(a_ref[...], ag_ref[...]), r_ref[...]], axis=1).astype(BF16)
    xn = x_ref[...] + mod[2:3, :] * jnp.dot(mix, wo_ref[...], preferred_element_type=F32)
    xo_ref[...] = xn
    h2 = _rms(xn, g2_ref[...]) * (1.0 + mod[4:5, :]) + mod[3:4, :]
    ha_ref[...], hb_ref[...] = _pack_rows(h2)
    lg_ref[...] = jnp.dot(h2.astype(BF16), wr_ref[...], preferred_element_type=F32) + br_ref[...]


def _out_proj(attn, rec, x2, mod_l, ag, g2, wo_bf, wr_bf, br, seq):
    n, d = x2.shape
    half = attn.shape[1]
    tm = ROW_TILE
    per_batch = seq // tm
    row = lambda w: pl.BlockSpec((tm, w), lambda i: (i, 0))
    full = lambda a: pl.BlockSpec(a.shape, lambda i: (0,) * a.ndim)
    ag2, g22 = ag.reshape(1, half), g2.reshape(1, d)
    return pl.pallas_call(
        _out_kernel,
        grid=(n // tm,),
        in_specs=[row(half), row(half), row(d), pl.BlockSpec((1, 6, d), lambda i: (i // per_batch, 0, 0)),
                  full(ag2), full(g22), full(wo_bf), full(wr_bf), full(br)],
        out_specs=[row(d), row(PACK_WIDTH), row(PACK_WIDTH), row(LANES)],
        out_shape=[jax.ShapeDtypeStruct((n, d), F32), jax.ShapeDtypeStruct((n, PACK_WIDTH), U32),
                   jax.ShapeDtypeStruct((n, PACK_WIDTH), U32), jax.ShapeDtypeStruct((n, LANES), F32)],
        compiler_params=_params(("parallel",)),
        name="out_proj",
    )(attn, rec, x2, mod_l, ag2, g22, wo_bf, wr_bf, br)


def _route_kernel(lg_ref, e_ref, g_ref):
    lg = lg_ref[...]
    lane = lax.broadcasted_iota(jnp.int32, lg.shape, 1).astype(F32)
    big = float(LANES)

    def first_argmax(vals):
        mx = jnp.max(vals, axis=-1, keepdims=True)
        return mx, jnp.min(jnp.where(vals == mx, lane, big), axis=-1, keepdims=True)

    is_group = lane < N_GROUPS
    gmax, gidx = first_argmax(jnp.where(is_group, lg, -jnp.inf))
    gprob = 1.0 / jnp.sum(jnp.where(is_group, jnp.exp(lg - gmax), 0.0), axis=-1, keepdims=True)
    lo = N_GROUPS + EXPERTS_PER_GROUP * gidx
    el = jnp.where(jnp.logical_and(lane >= lo, lane < lo + EXPERTS_PER_GROUP), lg, -jnp.inf)
    t1, i1 = first_argmax(el)
    t2, i2 = first_argmax(jnp.where(lane == i1, -jnp.inf, el))
    ex = jnp.exp(t2 - t1)
    p1 = 1.0 / (1.0 + ex)
    p2 = ex / (1.0 + ex)
    e_ref[...] = jnp.where(lane == 0, i1 - N_GROUPS, jnp.where(lane == 1, i2 - N_GROUPS, 0.0)).astype(jnp.int32)
    g_ref[...] = jnp.where(lane == 0, p1 * gprob, jnp.where(lane == 1, p2 * gprob, 0.0))


def _route(logits):
    n = logits.shape[0]
    tm = ROW_TILE
    spec = pl.BlockSpec((tm, LANES), lambda i: (i, 0))
    return pl.pallas_call(
        _route_kernel,
        grid=(n // tm,),
        in_specs=[spec],
        out_specs=[spec, spec],
        out_shape=[jax.ShapeDtypeStruct((n, LANES), jnp.int32), jax.ShapeDtypeStruct((n, LANES), F32)],
        compiler_params=_params(("parallel",)),
        name="route",
    )(logits)


def _plan_kernel(e_ref, ut_ref, pos_ref, off_ref, cnt_ref, cnt, off, run):
    phase = pl.program_id(0)
    i = pl.program_id(1)
    t = e_ref.shape[1]
    eio = lax.broadcasted_iota(jnp.int32, (N_EXPERTS, t), 0)
    oh1 = e_ref[0:1, :] == eio
    oh2 = e_ref[1:2, :] == eio
    oh = jnp.logical_or(oh1, oh2).astype(F32)
    tot = jnp.sum(oh, axis=1, keepdims=True)

    @pl.when(jnp.logical_and(phase == 0, i == 0))
    def _():
        cnt[...] = jnp.zeros_like(cnt)

    @pl.when(phase == 0)
    def _():
        cnt[...] += tot

    @pl.when(jnp.logical_and(phase == 1, i == 0))
    def _():
        counts = cnt[...]
        nb = jnp.floor((counts + (MOE_TILE - 1)) * (1.0 / MOE_TILE))
        er = lax.broadcasted_iota(jnp.int32, (N_EXPERTS, N_EXPERTS), 0)
        ec = lax.broadcasted_iota(jnp.int32, (N_EXPERTS, N_EXPERTS), 1)
        off[...] = jnp.dot((ec < er).astype(BF16), nb.astype(BF16), preferred_element_type=F32) * MOE_TILE
        run[...] = jnp.zeros_like(run)
        off_ref[...] = off[...].astype(jnp.int32)
        cnt_ref[...] = counts.astype(jnp.int32)

    @pl.when(phase == 1)
    def _():
        c = jnp.dot(oh.astype(BF16), ut_ref[...], preferred_element_type=F32)
        r = off[:, 0:1] + run[:, 0:1] + c - 1.0
        pos_ref[0:1, :] = jnp.sum(jnp.where(oh1, r, 0.0), axis=0, keepdims=True).astype(jnp.int32)
        pos_ref[1:2, :] = jnp.sum(jnp.where(oh2, r, 0.0), axis=0, keepdims=True).astype(jnp.int32)
        run[...] += tot


def _plan(expert_ids_t):
    n = expert_ids_t.shape[1]
    t = PLAN_TILE
    assert TOP_K == 2
    ut = jnp.asarray(np.triu(np.ones((t, t), np.float32)), BF16)
    tab = jax.ShapeDtypeStruct((N_EXPERTS, LANES), jnp.int32)
    const = lambda shape: pl.BlockSpec(shape, lambda p, i: (0, 0))
    pos, off, cnt = pl.pallas_call(
        _plan_kernel,
        grid=(2, n // t),
        in_specs=[pl.BlockSpec((TOP_K, t), lambda p, i: (0, i)), const((t, t))],
        out_specs=[pl.BlockSpec((TOP_K, t), lambda p, i: (0, i * p)), const((N_EXPERTS, LANES)), const((N_EXPERTS, LANES))],
        out_shape=[jax.ShapeDtypeStruct((TOP_K, n), jnp.int32), tab, tab],
        scratch_shapes=[pltpu.VMEM((N_EXPERTS, LANES), F32)] * 3,
        compiler_params=_params(("arbitrary", "arbitrary")),
        name="plan",
    )(expert_ids_t, ut)
    return pos, off[:, 0], cnt[:, 0]


def _sc_mesh():
    return plsc.VectorSubcoreMesh(core_axis_name="c", subcore_axis_name="s")


def _sc_scatter_rows(x, pos, n_rows):
    n, d = x.shape
    w = SC_WINDOW

    @functools.partial(pl.kernel, out_type=jax.ShapeDtypeStruct((n_rows, d), x.dtype), mesh=_sc_mesh(), scratch_types=[])
    def scatter(x_hbm, p_hbm, o_hbm):
        def body(x_vmem, i_vmem):
            for k in range(TOP_K):
                pltpu.sync_copy(x_vmem, o_hbm.at[i_vmem.at[k]])

        pltpu.emit_pipeline(
            body,
            grid=(n // w,),
            in_specs=[pl.BlockSpec((w, d), lambda i: (i, 0)), pl.BlockSpec((TOP_K, w), lambda i: (0, i))],
            out_specs=[],
            core_axis_name=("c", "s"),
            dimension_semantics=(pltpu.PARALLEL,),
        )(x_hbm, p_hbm)

    return scatter(x, pos)


def _sc_gather_rows(y, pos):
    m = pos.shape[0] * pos.shape[1]
    d = y.shape[1]
    w = SC_WINDOW

    @functools.partial(pl.kernel, out_type=jax.ShapeDtypeStruct((m, d), y.dtype), mesh=_sc_mesh(), scratch_types=[])
    def gather(y_hbm, p_hbm, o_hbm):
        def body(i_vmem, o_vmem):
            pltpu.sync_copy(y_hbm.at[i_vmem.at[0]], o_vmem)

        pltpu.emit_pipeline(
            body,
            grid=(m // w,),
            in_specs=[pl.BlockSpec((1, w), lambda i: (0, i))],
            out_specs=[pl.BlockSpec((w, d), lambda i: (i, 0))],
            core_axis_name=("c", "s"),
            dimension_semantics=(pltpu.PARALLEL,),
        )(p_hbm, o_hbm)

    return gather(y, pos.reshape(1, m))


def _expert_kernel(off_ref, cnt_ref, xa_hbm, xb_hbm, w1_ref, w3_ref, w2_ref, ya_hbm, yb_hbm,
                   w1b, w3b, w2b, xbuf, ybuf, lsem, ssem):
    e = pl.program_id(0)
    tb = MOE_TILE
    cnt = cnt_ref[e]
    base = off_ref[e]
    nblk = lax.shift_right_logical(cnt + (tb - 1), tb.bit_length() - 1)
    w1b[...] = w1_ref[0, 0].astype(BF16)
    w3b[...] = w3_ref[0, 0].astype(BF16)
    w2b[...] = w2_ref[0, 0].astype(BF16)

    def rows(j):
        return pl.ds(pl.multiple_of(base + j * tb, tb), tb)

    def loads(j, slot):
        return (pltpu.make_async_copy(xa_hbm.at[rows(j)], xbuf.at[slot, 0], lsem.at[slot, 0]),
                pltpu.make_async_copy(xb_hbm.at[rows(j)], xbuf.at[slot, 1], lsem.at[slot, 1]))

    def stores(j, slot):
        return (pltpu.make_async_copy(ybuf.at[slot, 0], ya_hbm.at[rows(j)], ssem.at[slot, 0]),
                pltpu.make_async_copy(ybuf.at[slot, 1], yb_hbm.at[rows(j)], ssem.at[slot, 1]))

    @pl.when(nblk > 0)
    def _():
        for c in loads(0, 0):
            c.start()

    def block(j, carry):
        slot = j % 2
        for c in loads(j, slot):
            c.wait()

        @pl.when(j + 1 < nblk)
        def _():
            for c in loads(j + 1, 1 - slot):
                c.start()

        @pl.when(j >= 2)
        def _():
            for c in stores(j - 2, slot):
                c.wait()

        live = lax.broadcasted_iota(jnp.int32, (tb, 1), 0) < cnt - j * tb
        x = jnp.where(live, _unpack_rows(xbuf[slot, 0], xbuf[slot, 1]), 0.0).astype(BF16)
        a = jnp.dot(x, w1b[...], preferred_element_type=F32)
        b = jnp.dot(x, w3b[...], preferred_element_type=F32)
        hm = (a * jax.nn.sigmoid(a) * b).astype(BF16)
        ybuf[slot, 0], ybuf[slot, 1] = _pack_rows(jnp.dot(hm, w2b[...], preferred_element_type=F32))
        for c in stores(j, slot):
            c.start()
        return carry

    lax.fori_loop(0, nblk, block, 0)

    @pl.when(nblk >= 2)
    def _():
        for c in stores(nblk - 2, nblk % 2):
            c.wait()

    @pl.when(nblk >= 1)
    def _():
        for c in stores(nblk - 1, (nblk - 1) % 2):
            c.wait()

    @pl.when(e == pl.num_programs(0) - 1)
    def _():
        ybuf[0] = jnp.zeros_like(ybuf[0])

        def fill(j, carry):
            for c in stores(j, 0):
                c.start()
            for c in stores(j, 0):
                c.wait()
            return carry

        lax.fori_loop(nblk, (ya_hbm.shape[0] - base) // tb, fill, 0)


def _experts(xa, xb, w1, w3, w2, layer, row_off, row_cnt):
    n_rows = xa.shape[0]
    d, de = w1.shape[-2:]
    tb = MOE_TILE
    wspec = lambda r, c: pl.BlockSpec((1, 1, r, c), lambda e, off, cnt: (layer, e, 0, 0))
    anyspec = pl.BlockSpec(memory_space=pl.ANY)
    grid_spec = pltpu.PrefetchScalarGridSpec(
        num_scalar_prefetch=2,
        grid=(N_EXPERTS,),
        in_specs=[anyspec, anyspec, wspec(d, de), wspec(d, de), wspec(de, d)],
        out_specs=[anyspec, anyspec],
        scratch_shapes=[pltpu.VMEM((d, de), BF16), pltpu.VMEM((d, de), BF16), pltpu.VMEM((de, d), BF16),
                        pltpu.VMEM((2, 2, tb, PACK_WIDTH), U32), pltpu.VMEM((2, 2, tb, PACK_WIDTH), U32),
                        pltpu.SemaphoreType.DMA((2, 2)), pltpu.SemaphoreType.DMA((2, 2))],
    )
    rows = jax.ShapeDtypeStruct((n_rows, PACK_WIDTH), U32)
    return pl.pallas_call(
        _expert_kernel,
        grid_spec=grid_spec,
        out_shape=[rows, rows],
        compiler_params=_params(("arbitrary",)),
        name="experts",
    )(row_off, row_cnt, xa, xb, w1, w3, w2)


def _final_kernel(ga0, ga1, gb0, gb1, gt_ref, mod_ref, x_ref, fg_ref, o_ref):
    o_ref[...] = _rms(_moe_residual(x_ref, ga0, ga1, gb0, gb1, gt_ref, mod_ref[0][5:6, :]), fg_ref[...])


def _final(x2, ga, gb, gates, mod_l, final_g, seq):
    n, d = x2.shape
    tm = ROW_TILE
    per_batch = seq // tm
    row = lambda w: pl.BlockSpec((tm, w), lambda i: (i, 0))
    second = pl.BlockSpec((tm, PACK_WIDTH), lambda i: (i + n // tm, 0))
    return pl.pallas_call(
        _final_kernel,
        grid=(n // tm,),
        in_specs=[row(PACK_WIDTH), second, row(PACK_WIDTH), second, row(LANES),
                  pl.BlockSpec((1, 6, d), lambda i: (i // per_batch, 0, 0)), row(d), pl.BlockSpec((1, d), lambda i: (0, 0))],
        out_specs=row(d),
        out_shape=jax.ShapeDtypeStruct((n, d), F32),
        compiler_params=_params(("parallel",)),
        name="final_norm",
    )(ga, ga, gb, gb, gates, mod_l, x2, final_g.reshape(1, d))


def kernel(x, c, w_ada, b_ada, norm1_g, w_in, attn_norm_g, hgrn_lb_logits, hgrn_norm_g, w_out, norm2_g, router_group_w, router_group_b, router_expert_w, router_expert_b, moe_w1, moe_w3, moe_w2, final_g):
    bsz, seq, d = x.shape
    depth = w_ada.shape[0]
    n = bsz * seq
    attn_dim = attn_norm_g.shape[1]
    n_heads = attn_dim // ATTN_HEAD_DIM

    lb_w = jax.nn.softmax(hgrn_lb_logits.astype(F32), axis=0)
    lower_bounds = jnp.cumsum(lb_w, axis=0) - lb_w[0:1]
    mod = _adaln(c, w_ada, b_ada)
    bias = _attn_bias(n_heads)
    hconsts = _hgrn_consts()
    n_rows = n * TOP_K + N_EXPERTS * MOE_TILE
    pad = LANES - N_GROUPS - N_EXPERTS

    x2 = x.reshape(n, d)
    moe = None
    for layer in range(depth):
        x2, (aq, ak, av, rq, rf, ri, rg) = _in_proj(x2, mod[layer], norm1_g[layer], w_in[layer].astype(BF16), seq, moe)
        attn = _attention(aq, ak, av, bias, bsz, seq)
        rec = _hgrn(rq, rf, ri, rg, lower_bounds[layer], hgrn_norm_g[layer], hconsts, bsz, seq)
        wr = jnp.concatenate([router_group_w[layer], router_expert_w[layer], jnp.zeros((d, pad), F32)], axis=1)
        br = jnp.concatenate([router_group_b[layer], router_expert_b[layer], jnp.zeros((pad,), F32)]).reshape(1, LANES)
        x2, ha, hb, logits = _out_proj(attn, rec, x2, mod[layer], attn_norm_g[layer], norm2_g[layer],
                                       w_out[layer].astype(BF16), wr.astype(BF16), br, seq)
        eids, gates = _route(logits)
        pos, row_off, row_cnt = _plan(eids[:, :TOP_K].T)
        ya, yb = _experts(_sc_scatter_rows(ha, pos, n_rows), _sc_scatter_rows(hb, pos, n_rows),
                          moe_w1, moe_w3, moe_w2, layer, row_off, row_cnt)
        moe = (_sc_gather_rows(ya, pos), _sc_gather_rows(yb, pos), gates, mod[layer])
    return _final(x2, moe[0], moe[1], moe[2], moe[3], final_g, seq).reshape(bsz, seq, d)
```

```python
import functools

import numpy as np
import jax
import jax.numpy as jnp
from jax import lax
from jax.experimental import pallas as pl
from jax.experimental.pallas import tpu as pltpu
from jax.experimental.pallas import tpu_sc as plsc

F32 = jnp.float32
BF16 = jnp.bfloat16
U32 = jnp.uint32

ATTN_HEAD_DIM = 64
ATTN_BLOCK = 128
DILATIONS = (1, 4, 16)
ALIBI_MAX_EXP = 8.0
HGRN_EXPAND = 128
HGRN_CHUNK = 128
N_GROUPS = 4
EXPERTS_PER_GROUP = 8
N_EXPERTS = N_GROUPS * EXPERTS_PER_GROUP
TOP_K = 2
NORM_EPS = 1e-6
LOG2E = 1.4426950408889634

LANES = 128
ATTN_SUPER = ATTN_BLOCK * max(DILATIONS)
ROW_TILE = 512
ATTN_UNROLL = 16
HGRN_TILE = 2048
HGRN_UNROLL = 4
MOE_TILE = 256
LOAD_AHEAD = 2
PLAN_TILE = 512
PACK_WIDTH = 256
SC_WINDOW = 128
VMEM_LIMIT = 56 << 20


def _params(semantics):
    return pltpu.CompilerParams(dimension_semantics=semantics, vmem_limit_bytes=VMEM_LIMIT)


def _rms(x, g):
    return x * lax.rsqrt(jnp.mean(x * x, axis=-1, keepdims=True) + NORM_EPS) * g


def _dot_nt(a, b):
    return lax.dot_general(a, b, (((1,), (1,)), ((), ())), preferred_element_type=F32)


def _dot_tn(a, b):
    return lax.dot_general(a, b, (((0,), (0,)), ((), ())), preferred_element_type=F32)


def _pack_pair(hi, lo):
    hb = lax.bitcast_convert_type(hi.astype(BF16).astype(F32), U32)
    lb = lax.bitcast_convert_type(lo.astype(BF16).astype(F32), U32)
    return hb | (lb >> 16)


def _unpack_pair(u):
    hi = lax.bitcast_convert_type(u & jnp.uint32(0xFFFF0000), F32)
    lo = lax.bitcast_convert_type(u << 16, F32)
    return hi, lo


def _pack_rows(y):
    w = PACK_WIDTH
    return _pack_pair(y[:, 0:w], y[:, 2 * w:3 * w]), _pack_pair(y[:, w:2 * w], y[:, 3 * w:4 * w])


def _unpack_rows(ua, ub):
    ha, la = _unpack_pair(ua)
    hb, lb = _unpack_pair(ub)
    return jnp.concatenate([ha, hb, la, lb], axis=1)


def _moe_residual(x_ref, ga0, ga1, gb0, gb1, gt_ref, gate_row):
    gt = gt_ref[...]
    y = gt[:, 0:1] * _unpack_rows(ga0[...], gb0[...]) + gt[:, 1:2] * _unpack_rows(ga1[...], gb1[...])
    return x_ref[...] + gate_row * y


def _ada_kernel(c_ref, w_ref, b_ref, o_ref):
    c = c_ref[...]
    ca = c * jax.nn.sigmoid(c)
    o_ref[0] = jnp.dot(ca, w_ref[0], precision=lax.Precision.HIGHEST, preferred_element_type=F32) + b_ref[0]


def _adaln(c, w_ada, b_ada):
    depth, d, n6 = w_ada.shape
    bsz = c.shape[0]
    rows = 8
    cp = jnp.zeros((rows, d), F32).at[:bsz].set(c)
    tn = 1536
    out = pl.pallas_call(
        _ada_kernel,
        grid=(depth, n6 // tn),
        in_specs=[
            pl.BlockSpec((rows, d), lambda l, j: (0, 0)),
            pl.BlockSpec((1, d, tn), lambda l, j: (l, 0, j)),
            pl.BlockSpec((1, 1, tn), lambda l, j: (l, 0, j)),
        ],
        out_specs=pl.BlockSpec((1, rows, tn), lambda l, j: (l, 0, j)),
        out_shape=jax.ShapeDtypeStruct((depth, rows, n6), F32),
        compiler_params=_params(("parallel", "parallel")),
        name="adaln",
    )(cp, w_ada, b_ada.reshape(depth, 1, n6))
    return out[:, :bsz].reshape(depth, bsz, 6, d)


def _in_kernel(*refs, n_moe):
    moe, (x_ref, mod_ref, g_ref, w_ref), rest = refs[:n_moe], refs[n_moe:n_moe + 4], refs[n_moe + 4:]
    hb = rest[-1]
    if n_moe:
        ga0, ga1, gb0, gb1, gt_ref, pmod_ref = moe
        x = _moe_residual(x_ref, ga0, ga1, gb0, gb1, gt_ref, pmod_ref[0][5:6, :])
        rest[0][...] = x
        outs = rest[1:-1]
    else:
        x = x_ref[...]
        outs = rest[:-1]
    mod = mod_ref[0]
    h = _rms(x, g_ref[...]) * (1.0 + mod[1:2, :]) + mod[0:1, :]
    hb[...] = h.astype(BF16)
    width = outs[0].shape[1]
    for i, o in enumerate(outs):
        o[...] = jnp.dot(hb[...], w_ref[:, i * width:(i + 1) * width], preferred_element_type=F32).astype(o.dtype)


def _in_proj(x2, mod_l, g, w_in_bf, seq, moe=None):
    n, d = x2.shape
    n_out = w_in_bf.shape[1]
    width = 512
    tm = ROW_TILE
    per_batch = seq // tm
    row = lambda w: pl.BlockSpec((tm, w), lambda i: (i, 0))
    mod_spec = pl.BlockSpec((1, 6, d), lambda i: (i // per_batch, 0, 0))
    in_specs, args, out_specs, out_shape = [], [], [], []
    if moe is not None:
        ga, gb, gates, mod_prev = moe
        second = pl.BlockSpec((tm, PACK_WIDTH), lambda i: (i + n // tm, 0))
        in_specs += [row(PACK_WIDTH), second, row(PACK_WIDTH), second, row(LANES), mod_spec]
        args += [ga, ga, gb, gb, gates, mod_prev]
        out_specs.append(row(d))
        out_shape.append(jax.ShapeDtypeStruct((n, d), F32))
    in_specs += [row(d), mod_spec, pl.BlockSpec((1, d), lambda i: (0, 0)), pl.BlockSpec((d, n_out), lambda i: (0, 0))]
    args += [x2, mod_l, g.reshape(1, d), w_in_bf]
    out_dtypes = (F32, F32, F32, BF16, F32, BF16, F32)
    out_specs += [row(width)] * len(out_dtypes)
    out_shape += [jax.ShapeDtypeStruct((n, width), dt) for dt in out_dtypes]
    outs = pl.pallas_call(
        functools.partial(_in_kernel, n_moe=0 if moe is None else 6),
        grid=(n // tm,),
        in_specs=in_specs,
        out_specs=out_specs,
        out_shape=out_shape,
        scratch_shapes=[pltpu.VMEM((tm, d), BF16)],
        compiler_params=_params(("parallel",)),
        name="in_proj",
    )(*args)
    return (x2, outs) if moe is None else (outs[0], outs[1:])


def _attn_bias(n_heads):
    qi = np.arange(ATTN_BLOCK)[:, None]
    kj = np.arange(2 * ATTN_BLOCK)[None, :]
    dist = ATTN_BLOCK + qi - kj
    valid = (dist >= 0) & (dist <= ATTN_BLOCK)
    slopes = np.exp2(-ALIBI_MAX_EXP * np.arange(1, n_heads + 1, dtype=np.float32) / n_heads)
    out = np.empty((2, len(DILATIONS), n_heads, ATTN_BLOCK, 2 * ATTN_BLOCK), np.float32)
    for p, dil in enumerate(DILATIONS):
        b = -slopes[:, None, None] * (dist * dil).astype(np.float32)[None] * np.float32(LOG2E)
        out[0, p] = np.where(valid[None], b, -np.inf)
        out[1, p] = np.where((valid & (kj >= ATTN_BLOCK))[None], b, -np.inf)
    return jnp.asarray(out)


def _attn_kernel(q_ref, k_ref, v_ref, bias_ref, o_ref, kcar, vcar, pbuf, vbuf, oscr, mscr, dscr):
    n = pl.program_id(2)
    sb = q_ref.shape[0]
    is_lo = lax.broadcasted_iota(jnp.int32, (ATTN_BLOCK, LANES), 1) < ATTN_HEAD_DIM
    scale = ATTN_HEAD_DIM ** -0.5 * LOG2E

    @pl.when(n == 0)
    def _():
        kcar[...] = jnp.zeros_like(kcar)
        vcar[...] = jnp.zeros_like(vcar)

    pbuf[...] = jnp.zeros_like(pbuf)
    vbuf[...] = jnp.zeros_like(vbuf)
    lo2 = lax.broadcasted_iota(jnp.int32, (2 * ATTN_BLOCK, LANES), 1) < ATTN_HEAD_DIM
    zero = jnp.zeros((2 * ATTN_BLOCK, LANES), BF16)
    den_lo = lo2.astype(BF16)
    den_hi = jnp.logical_not(lo2).astype(BF16)

    for p, dil in enumerate(DILATIONS):
        nb = (sb // ATTN_BLOCK) // dil

        def rows_of(it, dil=dil, nb=nb):
            start = (it % nb) * (ATTN_BLOCK * dil) + it // nb
            return pl.ds(start, ATTN_BLOCK, stride=dil) if dil > 1 else pl.ds(pl.multiple_of(start, ATTN_BLOCK), ATTN_BLOCK)

        def pv(it, probs, vaug, p=p, rows_of=rows_of):
            rows = rows_of(it)
            of = jnp.dot(probs, vaug, preferred_element_type=F32)
            oscr[p, rows, :] = of[:, :LANES]
            dscr[p, rows, :] = of[:, LANES:]

        def qk(it, p=p, nb=nb, rows_of=rows_of):
            r = it // nb
            rows = rows_of(it)
            crow = pl.ds(pl.multiple_of(r * ATTN_BLOCK, ATTN_BLOCK), ATTN_BLOCK)
            qb = q_ref[rows, :] * scale
            kc = k_ref[rows, :].astype(BF16)
            vc = v_ref[rows, :].astype(BF16)
            kcat = jnp.concatenate([kcar[p, crow, :], kc], axis=0)
            vcat = jnp.concatenate([vcar[p, crow, :], vc], axis=0)
            vaug = jnp.concatenate([jnp.concatenate([jnp.where(lo2, vcat, zero), den_lo], axis=1),
                                    jnp.concatenate([jnp.where(lo2, zero, vcat), den_hi], axis=1)], axis=0)
            kcar[p, crow, :] = kc
            vcar[p, crow, :] = vc
            first = jnp.logical_and(n == 0, it % nb == 0).astype(jnp.int32)
            probs, acc_m = [], None
            for hh in range(2):
                sel = is_lo if hh == 0 else jnp.logical_not(is_lo)
                qm = jnp.where(sel, qb, 0.0).astype(BF16)
                s = _dot_nt(qm, kcat) + bias_ref[first, p, hh]
                m = jnp.max(s, axis=-1, keepdims=True)
                probs.append(jnp.exp2(s - m).astype(BF16))
                mb = jnp.broadcast_to(m, (ATTN_BLOCK, LANES))
                acc_m = mb if hh == 0 else jnp.where(sel, mb, acc_m)
            mscr[p, rows, :] = acc_m
            return jnp.concatenate(probs, axis=1), vaug

        def body(g, carry, pv=pv, qk=qk):
            it0 = g * ATTN_UNROLL
            pend = (pbuf[...], vbuf[...])
            for u in range(ATTN_UNROLL):
                cur = qk(it0 + u)
                pv(jnp.maximum(it0 + u - 1, 0), *pend)
                pend = cur
            pbuf[...] = pend[0]
            vbuf[...] = pend[1]
            return carry

        n_it = sb // ATTN_BLOCK
        lax.fori_loop(0, n_it // ATTN_UNROLL, body, 0)
        pv(n_it - 1, pbuf[...], vbuf[...])

    ct = 256

    def combine(t, carry):
        rs = pl.ds(pl.multiple_of(t * ct, ct), ct)
        m0, m1, m2 = mscr[0, rs, :], mscr[1, rs, :], mscr[2, rs, :]
        mx = jnp.maximum(jnp.maximum(m0, m1), m2)
        w0, w1, w2 = jnp.exp2(m0 - mx), jnp.exp2(m1 - mx), jnp.exp2(m2 - mx)
        num = w0 * oscr[0, rs, :] + w1 * oscr[1, rs, :] + w2 * oscr[2, rs, :]
        den = w0 * dscr[0, rs, :] + w1 * dscr[1, rs, :] + w2 * dscr[2, rs, :]
        o_ref[rs, :] = num / den
        return carry

    lax.fori_loop(0, sb // ct, combine, 0)


def _attention(aq, ak, av, bias, bsz, seq):
    n, width = aq.shape
    sb = ATTN_SUPER
    nsb = seq // sb
    pairs = width // LANES
    npat = len(DILATIONS)
    blk = pl.BlockSpec((sb, LANES), lambda b, h, t: (b * nsb + t, h))
    return pl.pallas_call(
        _attn_kernel,
        grid=(bsz, pairs, nsb),
        in_specs=[blk, blk, blk,
                  pl.BlockSpec((2, npat, 2, ATTN_BLOCK, 2 * ATTN_BLOCK), lambda b, h, t: (0, 0, h, 0, 0))],
        out_specs=blk,
        out_shape=jax.ShapeDtypeStruct((n, width), F32),
        scratch_shapes=[
            pltpu.VMEM((npat, max(DILATIONS) * ATTN_BLOCK, LANES), BF16),
            pltpu.VMEM((npat, max(DILATIONS) * ATTN_BLOCK, LANES), BF16),
            pltpu.VMEM((ATTN_BLOCK, 4 * ATTN_BLOCK), BF16),
            pltpu.VMEM((4 * ATTN_BLOCK, 2 * LANES), BF16),
            pltpu.VMEM((npat, sb, LANES), F32),
            pltpu.VMEM((npat, sb, LANES), F32),
            pltpu.VMEM((npat, sb, LANES), F32),
        ],
        compiler_params=_params(("parallel", "parallel", "arbitrary")),
        name="dilated_attn",
    )(aq, ak, av, bias)


def _hgrn_consts():
    c = HGRN_CHUNK
    t = np.arange(c)[:, None]
    u = np.arange(c)[None, :]
    low, masks, sgn = [], [], []
    m = c // 2
    while m >= 1:
        mid = (t // (2 * m)) * (2 * m) + m
        is_q = t >= mid
        if m in (4, 2):
            low.append((is_q & (u >= mid) & (u <= t)) | (~is_q & (u > t) & (u < mid)))
        if m >= 8:
            sgn.append(np.broadcast_to(np.where(is_q, 1.0, -1.0), (c, c)))
        if m == 1:
            sgn.append(np.broadcast_to(np.where(is_q, 1.0, 0.0), (c, c)))
        masks.append(((t // (2 * m)) == (u // (2 * m))) & (t % (2 * m) >= m) & (u % (2 * m) < m))
        m //= 2
    masks.append(t == u)
    f = lambda a: np.asarray(a, np.float32)
    return (jnp.asarray(f(u <= t), BF16), jnp.asarray(f(np.concatenate(low, axis=0)), BF16),
            jnp.asarray(f(np.stack(sgn))), jnp.asarray(f(np.stack(masks))))


def _hgrn_kernel(q_ref, z_ref, v_ref, g_ref, lb_ref, gn_ref, tri_ref, low_ref, sgn_ref, msk_ref, o_ref, st):
    c = HGRN_CHUNK

    @pl.when(pl.program_id(2) == 0)
    def _():
        st[...] = jnp.zeros_like(st)

    lb = jnp.maximum(lb_ref[...], 0.0)
    log_lb = jnp.log(lb)
    log_1m = jnp.log1p(-lb)
    one_m = 1.0 - lb
    gn = gn_ref[...]
    n_levels = msk_ref.shape[0] - 1
    coarse = [c >> (i + 1) for i in range(n_levels) if (c >> (i + 1)) >= 8]

    def chunk(ci, carry):
        rs = pl.ds(pl.multiple_of(ci * c, c), c)
        z = z_ref[rs, :]
        qb = q_ref[rs, :]
        vb = v_ref[rs, :]
        ez = jnp.exp(-jnp.abs(z))
        ls = jnp.minimum(z, 0.0) - jnp.log(1.0 + ez)
        b2 = log_1m + ls
        lf = jnp.maximum(log_lb, b2) + jnp.log(1.0 + jnp.exp(-jnp.abs(log_lb - b2)))
        rz = 1.0 / (1.0 + ez)
        kb = (one_m * jnp.where(z > 0.0, ez * rz, rz)).astype(BF16)
        l2 = lf * LOG2E
        hi = l2.astype(BF16)
        hl = jnp.concatenate([hi, (l2 - hi.astype(F32)).astype(BF16)], axis=1)
        bb = jnp.dot(tri_ref[...], hl, preferred_element_type=F32)
        b = bb[:, :HGRN_EXPAND] + bb[:, HGRN_EXPAND:]
        dl = jnp.dot(low_ref[...], hl, preferred_element_type=F32)
        dl = dl[:, :HGRN_EXPAND] + dl[:, HGRN_EXPAND:]

        def level(d):
            e = jnp.exp2(d).astype(BF16)
            return _dot_nt(qb * e, kb * e)

        a = msk_ref[n_levels] * _dot_nt(qb, kb)
        for i in range(n_levels):
            m = c >> (i + 1)
            if m >= 8:
                ref = jnp.concatenate([jnp.broadcast_to(b[r0 + m - 1:r0 + m, :], (2 * m, HGRN_EXPAND))
                                       for r0 in range(0, c, 2 * m)], axis=0)
                d = (b - ref) * sgn_ref[i]
            elif m > 1:
                d = dl[(i - len(coarse)) * c:(i - len(coarse) + 1) * c]
            else:
                d = l2 * sgn_ref[len(coarse)]
            a = a + msk_ref[i] * level(d)
        b_last = b[c - 1:c, :]
        s_t = st[...]
        o = _dot_nt(qb * jnp.exp2(b).astype(BF16), s_t.astype(BF16)) + jnp.dot(a.astype(BF16), vb, preferred_element_type=F32)
        st[...] = jnp.exp2(b_last) * s_t + _dot_tn(vb, kb * jnp.exp2(b_last - b).astype(BF16))
        g = g_ref[rs, :]
        o_ref[rs, :] = (_rms(o, gn) * (g * jax.nn.sigmoid(g))).astype(o_ref.dtype)
        return carry

    lax.fori_loop(0, q_ref.shape[0] // c, chunk, 0, unroll=HGRN_UNROLL)


def _hgrn(rq, rf, ri, rg, lb, gn, consts, bsz, seq):
    n, width = rq.shape
    heads = width // HGRN_EXPAND
    ts = HGRN_TILE
    nt = seq // ts
    blk = pl.BlockSpec((ts, HGRN_EXPAND), lambda b, h, t: (b * nt + t, h))
    vec = pl.BlockSpec((1, HGRN_EXPAND), lambda b, h, t: (0, h))
    const = lambda a: pl.BlockSpec(a.shape, lambda b, h, t: (0,) * a.ndim)
    return pl.pallas_call(
        _hgrn_kernel,
        grid=(bsz, heads, nt),
        in_specs=[blk, blk, blk, blk, vec, vec] + [const(a) for a in consts],
        out_specs=blk,
        out_shape=jax.ShapeDtypeStruct((n, width), BF16),
        scratch_shapes=[pltpu.VMEM((HGRN_EXPAND, HGRN_EXPAND), F32)],
        compiler_params=_params(("parallel", "parallel", "arbitrary")),
        name="hgrn2",
    )(rq, rf, ri, rg, lb.reshape(1, width), gn.reshape(1, width), *consts)


def _route_math(lg):
    lane = lax.broadcasted_iota(jnp.int32, lg.shape, 1).astype(F32)
    big = float(LANES)

    def first_argmax(vals):
        mx = jnp.max(vals, axis=-1, keepdims=True)
        return mx, jnp.min(jnp.where(vals == mx, lane, big), axis=-1, keepdims=True)

    is_group = lane < N_GROUPS
    gmax, gidx = first_argmax(jnp.where(is_group, lg, -jnp.inf))
    gprob = 1.0 / jnp.sum(jnp.where(is_group, jnp.exp(lg - gmax), 0.0), axis=-1, keepdims=True)
    lo = N_GROUPS + EXPERTS_PER_GROUP * gidx
    el = jnp.where(jnp.logical_and(lane >= lo, lane < lo + EXPERTS_PER_GROUP), lg, -jnp.inf)
    t1, i1 = first_argmax(el)
    t2, i2 = first_argmax(jnp.where(lane == i1, -jnp.inf, el))
    ex = jnp.exp(t2 - t1)
    p1 = 1.0 / (1.0 + ex)
    p2 = ex / (1.0 + ex)
    eids = jnp.where(lane == 0, i1 - N_GROUPS, jnp.where(lane == 1, i2 - N_GROUPS, 0.0)).astype(jnp.int32)
    gates = jnp.where(lane == 0, p1 * gprob, jnp.where(lane == 1, p2 * gprob, 0.0))
    return eids, gates


def _out_kernel(a_ref, r_ref, x_ref, mod_ref, ag_ref, g2_ref, wo_ref, wr_ref, br_ref, xo_ref, ha_ref, hb_ref, e_ref, gt_ref):
    mod = mod_ref[0]
    mix = jnp.concatenate([_rms(a_ref[...], ag_ref[...]).astype(BF16), r_ref[...]], axis=1)
    xn = x_ref[...] + mod[2:3, :] * jnp.dot(mix, wo_ref[...], preferred_element_type=F32)
    xo_ref[...] = xn
    h2 = _rms(xn, g2_ref[...]) * (1.0 + mod[4:5, :]) + mod[3:4, :]
    ha_ref[...], hb_ref[...] = _pack_rows(h2)
    e_ref[...], gt_ref[...] = _route_math(jnp.dot(h2.astype(BF16), wr_ref[...], preferred_element_type=F32) + br_ref[...])


def _out_proj(attn, rec, x2, mod_l, ag, g2, wo_bf, wr_bf, br, seq):
    n, d = x2.shape
    half = attn.shape[1]
    tm = ROW_TILE
    per_batch = seq // tm
    row = lambda w: pl.BlockSpec((tm, w), lambda i: (i, 0))
    full = lambda a: pl.BlockSpec(a.shape, lambda i: (0,) * a.ndim)
    ag2, g22 = ag.reshape(1, half), g2.reshape(1, d)
    return pl.pallas_call(
        _out_kernel,
        grid=(n // tm,),
        in_specs=[row(half), row(half), row(d), pl.BlockSpec((1, 6, d), lambda i: (i // per_batch, 0, 0)),
                  full(ag2), full(g22), full(wo_bf), full(wr_bf), full(br)],
        out_specs=[row(d), row(PACK_WIDTH), row(PACK_WIDTH), row(LANES), row(LANES)],
        out_shape=[jax.ShapeDtypeStruct((n, d), F32), jax.ShapeDtypeStruct((n, PACK_WIDTH), U32),
                   jax.ShapeDtypeStruct((n, PACK_WIDTH), U32), jax.ShapeDtypeStruct((n, LANES), jnp.int32),
                   jax.ShapeDtypeStruct((n, LANES), F32)],
        compiler_params=_params(("parallel",)),
        name="out_proj",
    )(attn, rec, x2, mod_l, ag2, g22, wo_bf, wr_bf, br)


def _plan_kernel(e_ref, ut_ref, pos_ref, off_ref, cnt_ref, cnt, off, run):
    phase = pl.program_id(0)
    i = pl.program_id(1)
    t = e_ref.shape[1]
    eio = lax.broadcasted_iota(jnp.int32, (N_EXPERTS, t), 0)
    oh1 = e_ref[0:1, :] == eio
    oh2 = e_ref[1:2, :] == eio
    oh = jnp.logical_or(oh1, oh2).astype(F32)
    tot = jnp.sum(oh, axis=1, keepdims=True)

    @pl.when(jnp.logical_and(phase == 0, i == 0))
    def _():
        cnt[...] = jnp.zeros_like(cnt)

    @pl.when(phase == 0)
    def _():
        cnt[...] += tot

    @pl.when(jnp.logical_and(phase == 1, i == 0))
    def _():
        counts = cnt[...]
        nb = jnp.floor((counts + (MOE_TILE - 1)) * (1.0 / MOE_TILE))
        er = lax.broadcasted_iota(jnp.int32, (N_EXPERTS, N_EXPERTS), 0)
        ec = lax.broadcasted_iota(jnp.int32, (N_EXPERTS, N_EXPERTS), 1)
        off[...] = jnp.dot((ec < er).astype(BF16), nb.astype(BF16), preferred_element_type=F32) * MOE_TILE
        run[...] = jnp.zeros_like(run)
        off_ref[...] = off[...].astype(jnp.int32)
        cnt_ref[...] = counts.astype(jnp.int32)

    @pl.when(phase == 1)
    def _():
        c = jnp.dot(oh.astype(BF16), ut_ref[...], preferred_element_type=F32)
        r = off[:, 0:1] + run[:, 0:1] + c - 1.0
        pos_ref[0:1, :] = jnp.sum(jnp.where(oh1, r, 0.0), axis=0, keepdims=True).astype(jnp.int32)
        pos_ref[1:2, :] = jnp.sum(jnp.where(oh2, r, 0.0), axis=0, keepdims=True).astype(jnp.int32)
        run[...] += tot


def _plan(expert_ids_t):
    n = expert_ids_t.shape[1]
    t = PLAN_TILE
    assert TOP_K == 2
    ut = jnp.asarray(np.triu(np.ones((t, t), np.float32)), BF16)
    tab = jax.ShapeDtypeStruct((N_EXPERTS, LANES), jnp.int32)
    const = lambda shape: pl.BlockSpec(shape, lambda p, i: (0, 0))
    pos, off, cnt = pl.pallas_call(
        _plan_kernel,
        grid=(2, n // t),
        in_specs=[pl.BlockSpec((TOP_K, t), lambda p, i: (0, i)), const((t, t))],
        out_specs=[pl.BlockSpec((TOP_K, t), lambda p, i: (0, i * p)), const((N_EXPERTS, LANES)), const((N_EXPERTS, LANES))],
        out_shape=[jax.ShapeDtypeStruct((TOP_K, n), jnp.int32), tab, tab],
        scratch_shapes=[pltpu.VMEM((N_EXPERTS, LANES), F32)] * 3,
        compiler_params=_params(("arbitrary", "arbitrary")),
        name="plan",
    )(expert_ids_t, ut)
    return pos, off[:, 0], cnt[:, 0]


def _sc_mesh():
    return plsc.VectorSubcoreMesh(core_axis_name="c", subcore_axis_name="s")


def _sc_scatter_rows(x, pos, n_rows):
    n, d = x.shape
    w = SC_WINDOW

    @functools.partial(pl.kernel, out_type=jax.ShapeDtypeStruct((n_rows, d), x.dtype), mesh=_sc_mesh(), scratch_types=[])
    def scatter(x_hbm, p_hbm, o_hbm):
        def body(x_vmem, i_vmem):
            for k in range(TOP_K):
                pltpu.sync_copy(x_vmem, o_hbm.at[i_vmem.at[k]])

        pltpu.emit_pipeline(
            body,
            grid=(n // w,),
            in_specs=[pl.BlockSpec((w, d), lambda i: (i, 0)), pl.BlockSpec((TOP_K, w), lambda i: (0, i))],
            out_specs=[],
            core_axis_name=("c", "s"),
            dimension_semantics=(pltpu.PARALLEL,),
        )(x_hbm, p_hbm)

    return scatter(x, pos)


def _sc_gather_rows(y, pos):
    m = pos.shape[0] * pos.shape[1]
    d = y.shape[1]
    w = SC_WINDOW

    @functools.partial(pl.kernel, out_type=jax.ShapeDtypeStruct((m, d), y.dtype), mesh=_sc_mesh(), scratch_types=[])
    def gather(y_hbm, p_hbm, o_hbm):
        def body(i_vmem, o_vmem):
            pltpu.sync_copy(y_hbm.at[i_vmem.at[0]], o_vmem)

        pltpu.emit_pipeline(
            body,
            grid=(m // w,),
            in_specs=[pl.BlockSpec((1, w), lambda i: (0, i))],
            out_specs=[pl.BlockSpec((w, d), lambda i: (i, 0))],
            core_axis_name=("c", "s"),
            dimension_semantics=(pltpu.PARALLEL,),
        )(p_hbm, o_hbm)

    return gather(y, pos.reshape(1, m))


def _expert_kernel(off_ref, cnt_ref, xa_hbm, xb_hbm, w1_ref, w3_ref, w2_ref, ya_hbm, yb_hbm,
                   w1b, w3b, w2b, xbuf, ybuf, lsem, ssem):
    e = pl.program_id(0)
    tb = MOE_TILE
    cnt = cnt_ref[e]
    base = off_ref[e]
    nblk = lax.shift_right_logical(cnt + (tb - 1), tb.bit_length() - 1)
    w1b[...] = w1_ref[0, 0].astype(BF16)
    w3b[...] = w3_ref[0, 0].astype(BF16)
    w2b[...] = w2_ref[0, 0].astype(BF16)

    def rows(j):
        return pl.ds(pl.multiple_of(base + j * tb, tb), tb)

    def loads(j, slot):
        return (pltpu.make_async_copy(xa_hbm.at[rows(j)], xbuf.at[slot, 0], lsem.at[slot, 0]),
                pltpu.make_async_copy(xb_hbm.at[rows(j)], xbuf.at[slot, 1], lsem.at[slot, 1]))

    def stores(j, slot):
        return (pltpu.make_async_copy(ybuf.at[slot, 0], ya_hbm.at[rows(j)], ssem.at[slot, 0]),
                pltpu.make_async_copy(ybuf.at[slot, 1], yb_hbm.at[rows(j)], ssem.at[slot, 1]))

    for ahead in range(LOAD_AHEAD):
        @pl.when(nblk > ahead)
        def _(ahead=ahead):
            for c in loads(ahead, ahead):
                c.start()

    def block(j, carry):
        slot = j % 2
        lslot = j % (LOAD_AHEAD + 1)
        for c in loads(j, lslot):
            c.wait()

        @pl.when(j + LOAD_AHEAD < nblk)
        def _():
            for c in loads(j + LOAD_AHEAD, (j + LOAD_AHEAD) % (LOAD_AHEAD + 1)):
                c.start()

        @pl.when(j >= 2)
        def _():
            for c in stores(j - 2, slot):
                c.wait()

        live = lax.broadcasted_iota(jnp.int32, (tb, 1), 0) < cnt - j * tb
        x = jnp.where(live, _unpack_rows(xbuf[lslot, 0], xbuf[lslot, 1]), 0.0).astype(BF16)
        a = jnp.dot(x, w1b[...], preferred_element_type=F32)
        b = jnp.dot(x, w3b[...], preferred_element_type=F32)
        hm = (a * jax.nn.sigmoid(a) * b).astype(BF16)
        ybuf[slot, 0], ybuf[slot, 1] = _pack_rows(jnp.dot(hm, w2b[...], preferred_element_type=F32))
        for c in stores(j, slot):
            c.start()
        return carry

    lax.fori_loop(0, nblk, block, 0)

    @pl.when(nblk >= 2)
    def _():
        for c in stores(nblk - 2, nblk % 2):
            c.wait()

    @pl.when(nblk >= 1)
    def _():
        for c in stores(nblk - 1, (nblk - 1) % 2):
            c.wait()

    @pl.when(e == pl.num_programs(0) - 1)
    def _():
        ybuf[0] = jnp.zeros_like(ybuf[0])

        def fill(j, carry):
            for c in stores(j, 0):
                c.start()
            for c in stores(j, 0):
                c.wait()
            return carry

        lax.fori_loop(nblk, (ya_hbm.shape[0] - base) // tb, fill, 0)


def _experts(xa, xb, w1, w3, w2, layer, row_off, row_cnt):
    n_rows = xa.shape[0]
    d, de = w1.shape[-2:]
    tb = MOE_TILE
    wspec = lambda r, c: pl.BlockSpec((1, 1, r, c), lambda e, off, cnt: (layer, e, 0, 0))
    anyspec = pl.BlockSpec(memory_space=pl.ANY)
    grid_spec = pltpu.PrefetchScalarGridSpec(
        num_scalar_prefetch=2,
        grid=(N_EXPERTS,),
        in_specs=[anyspec, anyspec, wspec(d, de), wspec(d, de), wspec(de, d)],
        out_specs=[anyspec, anyspec],
        scratch_shapes=[pltpu.VMEM((d, de), BF16), pltpu.VMEM((d, de), BF16), pltpu.VMEM((de, d), BF16),
                        pltpu.VMEM((LOAD_AHEAD + 1, 2, tb, PACK_WIDTH), U32), pltpu.VMEM((2, 2, tb, PACK_WIDTH), U32),
                        pltpu.SemaphoreType.DMA((LOAD_AHEAD + 1, 2)), pltpu.SemaphoreType.DMA((2, 2))],
    )
    rows = jax.ShapeDtypeStruct((n_rows, PACK_WIDTH), U32)
    return pl.pallas_call(
        _expert_kernel,
        grid_spec=grid_spec,
        out_shape=[rows, rows],
        compiler_params=_params(("arbitrary",)),
        name="experts",
    )(row_off, row_cnt, xa, xb, w1, w3, w2)


def _final_kernel(ga0, ga1, gb0, gb1, gt_ref, mod_ref, x_ref, fg_ref, o_ref):
    o_ref[...] = _rms(_moe_residual(x_ref, ga0, ga1, gb0, gb1, gt_ref, mod_ref[0][5:6, :]), fg_ref[...])


def _final(x2, ga, gb, gates, mod_l, final_g, seq):
    n, d = x2.shape
    tm = ROW_TILE
    per_batch = seq // tm
    row = lambda w: pl.BlockSpec((tm, w), lambda i: (i, 0))
    second = pl.BlockSpec((tm, PACK_WIDTH), lambda i: (i + n // tm, 0))
    return pl.pallas_call(
        _final_kernel,
        grid=(n // tm,),
        in_specs=[row(PACK_WIDTH), second, row(PACK_WIDTH), second, row(LANES),
                  pl.BlockSpec((1, 6, d), lambda i: (i // per_batch, 0, 0)), row(d), pl.BlockSpec((1, d), lambda i: (0, 0))],
        out_specs=row(d),
        out_shape=jax.ShapeDtypeStruct((n, d), F32),
        compiler_params=_params(("parallel",)),
        name="final_norm",
    )(ga, ga, gb, gb, gates, mod_l, x2, final_g.reshape(1, d))


def kernel(x, c, w_ada, b_ada, norm1_g, w_in, attn_norm_g, hgrn_lb_logits, hgrn_norm_g, w_out, norm2_g, router_group_w, router_group_b, router_expert_w, router_expert_b, moe_w1, moe_w3, moe_w2, final_g):
    bsz, seq, d = x.shape
    depth = w_ada.shape[0]
    n = bsz * seq
    attn_dim = attn_norm_g.shape[1]
    n_heads = attn_dim // ATTN_HEAD_DIM

    lb_w = jax.nn.softmax(hgrn_lb_logits.astype(F32), axis=0)
    lower_bounds = jnp.cumsum(lb_w, axis=0) - lb_w[0:1]
    mod = _adaln(c, w_ada, b_ada)
    bias = _attn_bias(n_heads)
    hconsts = _hgrn_consts()
    n_rows = n * TOP_K + N_EXPERTS * MOE_TILE
    pad = LANES - N_GROUPS - N_EXPERTS

    x2 = x.reshape(n, d)
    moe = None
    for layer in range(depth):
        x2, (aq, ak, av, rq, rf, ri, rg) = _in_proj(x2, mod[layer], norm1_g[layer], w_in[layer].astype(BF16), seq, moe)
        attn = _attention(aq, ak, av, bias, bsz, seq)
        rec = _hgrn(rq, rf, ri, rg, lower_bounds[layer], hgrn_norm_g[layer], hconsts, bsz, seq)
        wr = jnp.concatenate([router_group_w[layer], router_expert_w[layer], jnp.zeros((d, pad), F32)], axis=1)
        br = jnp.concatenate([router_group_b[layer], router_expert_b[layer], jnp.zeros((pad,), F32)]).reshape(1, LANES)
        x2, ha, hb, eids, gates = _out_proj(attn, rec, x2, mod[layer], attn_norm_g[layer], norm2_g[layer],
                                            w_out[layer].astype(BF16), wr.astype(BF16), br, seq)
        pos, row_off, row_cnt = _plan(eids[:, :TOP_K].T)
        ya, yb = _experts(_sc_scatter_rows(ha, pos, n_rows), _sc_scatter_rows(hb, pos, n_rows),
                          moe_w1, moe_w3, moe_w2, layer, row_off, row_cnt)
        moe = (_sc_gather_rows(ya, pos), _sc_gather_rows(yb, pos), gates, mod[layer])
    return _final(x2, moe[0], moe[1], moe[2], moe[3], final_g, seq).reshape(bsz, seq, d)
```

```python
import functools

import numpy as np
import jax
import jax.numpy as jnp
from jax import lax
from jax.experimental import pallas as pl
from jax.experimental.pallas import tpu as pltpu
from jax.experimental.pallas import tpu_sc as plsc

F32 = jnp.float32
BF16 = jnp.bfloat16
U32 = jnp.uint32

ATTN_HEAD_DIM = 64
ATTN_BLOCK = 128
DILATIONS = (1, 4, 16)
ALIBI_MAX_EXP = 8.0
HGRN_EXPAND = 128
HGRN_CHUNK = 128
N_GROUPS = 4
EXPERTS_PER_GROUP = 8
N_EXPERTS = N_GROUPS * EXPERTS_PER_GROUP
TOP_K = 2
NORM_EPS = 1e-6
LOG2E = 1.4426950408889634

LANES = 128
ATTN_SUPER = ATTN_BLOCK * max(DILATIONS)
ROW_TILE = 512
ATTN_UNROLL = 16
HGRN_TILE = 2048
HGRN_UNROLL = 4
MOE_TILE = 512
PLAN_TILE = 2048
PLAN_LANES = 256
PACK_WIDTH = 256
SC_WINDOW = 128
VMEM_LIMIT = 56 << 20


def _params(semantics):
    return pltpu.CompilerParams(dimension_semantics=semantics, vmem_limit_bytes=VMEM_LIMIT)


def _rms(x, g):
    return x * lax.rsqrt(jnp.mean(x * x, axis=-1, keepdims=True) + NORM_EPS) * g


def _dot_nt(a, b):
    return lax.dot_general(a, b, (((1,), (1,)), ((), ())), preferred_element_type=F32)


def _dot_tn(a, b):
    return lax.dot_general(a, b, (((0,), (0,)), ((), ())), preferred_element_type=F32)


def _pack_pair(hi, lo):
    hb = lax.bitcast_convert_type(hi.astype(BF16).astype(F32), U32)
    lb = lax.bitcast_convert_type(lo.astype(BF16).astype(F32), U32)
    return hb | (lb >> 16)


def _unpack_pair(u):
    hi = lax.bitcast_convert_type(u & jnp.uint32(0xFFFF0000), F32)
    lo = lax.bitcast_convert_type(u << 16, F32)
    return hi, lo


def _pack_rows(y):
    w = PACK_WIDTH
    return _pack_pair(y[:, 0:w], y[:, 2 * w:3 * w]), _pack_pair(y[:, w:2 * w], y[:, 3 * w:4 * w])


def _unpack_rows(ua, ub):
    ha, la = _unpack_pair(ua)
    hb, lb = _unpack_pair(ub)
    return jnp.concatenate([ha, hb, la, lb], axis=1)


def _moe_residual(x_ref, ga0, ga1, gb0, gb1, gt_ref, gate_row):
    gt = gt_ref[...]
    y = gt[:, 0:1] * _unpack_rows(ga0[...], gb0[...]) + gt[:, 1:2] * _unpack_rows(ga1[...], gb1[...])
    return x_ref[...] + gate_row * y


def _ada_kernel(c_ref, w_ref, b_ref, o_ref):
    c = c_ref[...]
    ca = c * jax.nn.sigmoid(c)
    o_ref[0] = jnp.dot(ca, w_ref[0], precision=lax.Precision.HIGHEST, preferred_element_type=F32) + b_ref[0]


def _adaln(c, w_ada, b_ada):
    depth, d, n6 = w_ada.shape
    bsz = c.shape[0]
    rows = 8
    cp = jnp.zeros((rows, d), F32).at[:bsz].set(c)
    tn = 1536
    out = pl.pallas_call(
        _ada_kernel,
        grid=(depth, n6 // tn),
        in_specs=[
            pl.BlockSpec((rows, d), lambda l, j: (0, 0)),
            pl.BlockSpec((1, d, tn), lambda l, j: (l, 0, j)),
            pl.BlockSpec((1, 1, tn), lambda l, j: (l, 0, j)),
        ],
        out_specs=pl.BlockSpec((1, rows, tn), lambda l, j: (l, 0, j)),
        out_shape=jax.ShapeDtypeStruct((depth, rows, n6), F32),
        compiler_params=_params(("parallel", "parallel")),
        name="adaln",
    )(cp, w_ada, b_ada.reshape(depth, 1, n6))
    return out[:, :bsz].reshape(depth, bsz, 6, d)


def _in_kernel(*refs, n_moe):
    moe, (x_ref, mod_ref, g_ref, w_ref), rest = refs[:n_moe], refs[n_moe:n_moe + 4], refs[n_moe + 4:]
    hb = rest[-1]
    if n_moe:
        ga0, ga1, gb0, gb1, gt_ref, pmod_ref = moe
        x = _moe_residual(x_ref, ga0, ga1, gb0, gb1, gt_ref, pmod_ref[0][5:6, :])
        rest[0][...] = x
        outs = rest[1:-1]
    else:
        x = x_ref[...]
        outs = rest[:-1]
    mod = mod_ref[0]
    h = _rms(x, g_ref[...]) * (1.0 + mod[1:2, :]) + mod[0:1, :]
    hb[...] = h.astype(BF16)
    width = outs[0].shape[1]
    for i, o in enumerate(outs):
        w = w_ref[:, i * width:(i + 1) * width].astype(BF16)
        o[...] = jnp.dot(hb[...], w, preferred_element_type=F32).astype(o.dtype)


def _in_proj(x2, mod_l, g, w_in, seq, moe=None):
    n, d = x2.shape
    n_out = w_in.shape[1]
    width = 512
    tm = ROW_TILE
    per_batch = seq // tm
    row = lambda w: pl.BlockSpec((tm, w), lambda i: (i, 0))
    mod_spec = pl.BlockSpec((1, 6, d), lambda i: (i // per_batch, 0, 0))
    in_specs, args, out_specs, out_shape = [], [], [], []
    if moe is not None:
        ga, gb, gates, mod_prev = moe
        second = pl.BlockSpec((tm, PACK_WIDTH), lambda i: (i + n // tm, 0))
        in_specs += [row(PACK_WIDTH), second, row(PACK_WIDTH), second, row(LANES), mod_spec]
        args += [ga, ga, gb, gb, gates, mod_prev]
        out_specs.append(row(d))
        out_shape.append(jax.ShapeDtypeStruct((n, d), F32))
    in_specs += [row(d), mod_spec, pl.BlockSpec((1, d), lambda i: (0, 0)),
                 pl.BlockSpec((d, n_out), lambda i: (0, 0), pipeline_mode=pl.Buffered(1))]
    args += [x2, mod_l, g.reshape(1, d), w_in]
    out_dtypes = (F32, F32, F32, BF16, F32, BF16, F32)
    out_specs += [row(width)] * len(out_dtypes)
    out_shape += [jax.ShapeDtypeStruct((n, width), dt) for dt in out_dtypes]
    outs = pl.pallas_call(
        functools.partial(_in_kernel, n_moe=0 if moe is None else 6),
        grid=(n // tm,),
        in_specs=in_specs,
        out_specs=out_specs,
        out_shape=out_shape,
        scratch_shapes=[pltpu.VMEM((tm, d), BF16)],
        compiler_params=_params(("parallel",)),
        name="in_proj",
    )(*args)
    return (x2, outs) if moe is None else (outs[0], outs[1:])


def _attn_bias(n_heads):
    qi = np.arange(ATTN_BLOCK)[:, None]
    kj = np.arange(2 * ATTN_BLOCK)[None, :]
    dist = ATTN_BLOCK + qi - kj
    valid = (dist >= 0) & (dist <= ATTN_BLOCK)
    slopes = np.exp2(-ALIBI_MAX_EXP * np.arange(1, n_heads + 1, dtype=np.float32) / n_heads)
    out = np.empty((2, len(DILATIONS), n_heads, ATTN_BLOCK, 2 * ATTN_BLOCK), np.float32)
    for p, dil in enumerate(DILATIONS):
        b = -slopes[:, None, None] * (dist * dil).astype(np.float32)[None] * np.float32(LOG2E)
        out[0, p] = np.where(valid[None], b, -np.inf)
        out[1, p] = np.where((valid & (kj >= ATTN_BLOCK))[None], b, -np.inf)
    return jnp.asarray(out)


def _attn_kernel(q_ref, k_ref, v_ref, bias_ref, o_ref, kcar, vcar, pbuf, vbuf, oscr, mscr, dscr):
    n = pl.program_id(2)
    sb = q_ref.shape[0]
    is_lo = lax.broadcasted_iota(jnp.int32, (ATTN_BLOCK, LANES), 1) < ATTN_HEAD_DIM
    scale = ATTN_HEAD_DIM ** -0.5 * LOG2E

    @pl.when(n == 0)
    def _():
        kcar[...] = jnp.zeros_like(kcar)
        vcar[...] = jnp.zeros_like(vcar)

    pbuf[...] = jnp.zeros_like(pbuf)
    vbuf[...] = jnp.zeros_like(vbuf)
    lo2 = lax.broadcasted_iota(jnp.int32, (2 * ATTN_BLOCK, LANES), 1) < ATTN_HEAD_DIM
    zero = jnp.zeros((2 * ATTN_BLOCK, LANES), BF16)
    den_lo = lo2.astype(BF16)
    den_hi = jnp.logical_not(lo2).astype(BF16)

    for p, dil in enumerate(DILATIONS):
        nb = (sb // ATTN_BLOCK) // dil

        def rows_of(it, dil=dil, nb=nb):
            start = (it % nb) * (ATTN_BLOCK * dil) + it // nb
            return pl.ds(start, ATTN_BLOCK, stride=dil) if dil > 1 else pl.ds(pl.multiple_of(start, ATTN_BLOCK), ATTN_BLOCK)

        def pv(it, probs, vaug, p=p, rows_of=rows_of):
            rows = rows_of(it)
            of = jnp.dot(probs, vaug, preferred_element_type=F32)
            oscr[p, rows, :] = of[:, :LANES]
            dscr[p, rows, :] = of[:, LANES:]

        def qk(it, p=p, nb=nb, rows_of=rows_of):
            r = it // nb
            rows = rows_of(it)
            crow = pl.ds(pl.multiple_of(r * ATTN_BLOCK, ATTN_BLOCK), ATTN_BLOCK)
            qb = q_ref[rows, :] * scale
            kc = k_ref[rows, :].astype(BF16)
            vc = v_ref[rows, :].astype(BF16)
            kcat = jnp.concatenate([kcar[p, crow, :], kc], axis=0)
            vcat = jnp.concatenate([vcar[p, crow, :], vc], axis=0)
            vaug = jnp.concatenate([jnp.concatenate([jnp.where(lo2, vcat, zero), den_lo], axis=1),
                                    jnp.concatenate([jnp.where(lo2, zero, vcat), den_hi], axis=1)], axis=0)
            kcar[p, crow, :] = kc
            vcar[p, crow, :] = vc
            first = jnp.logical_and(n == 0, it % nb == 0).astype(jnp.int32)
            probs, acc_m = [], None
            for hh in range(2):
                sel = is_lo if hh == 0 else jnp.logical_not(is_lo)
                qm = jnp.where(sel, qb, 0.0).astype(BF16)
                s = _dot_nt(qm, kcat) + bias_ref[first, p, hh]
                m = jnp.max(s, axis=-1, keepdims=True)
                probs.append(jnp.exp2(s - m).astype(BF16))
                mb = jnp.broadcast_to(m, (ATTN_BLOCK, LANES))
                acc_m = mb if hh == 0 else jnp.where(sel, mb, acc_m)
            mscr[p, rows, :] = acc_m
            return jnp.concatenate(probs, axis=1), vaug

        def body(g, carry, pv=pv, qk=qk):
            it0 = g * ATTN_UNROLL
            pend = (pbuf[...], vbuf[...])
            for u in range(ATTN_UNROLL):
                cur = qk(it0 + u)
                pv(jnp.maximum(it0 + u - 1, 0), *pend)
                pend = cur
            pbuf[...] = pend[0]
            vbuf[...] = pend[1]
            return carry

        n_it = sb // ATTN_BLOCK
        lax.fori_loop(0, n_it // ATTN_UNROLL, body, 0)
        pv(n_it - 1, pbuf[...], vbuf[...])

    ct = 256

    def combine(t, carry):
        rs = pl.ds(pl.multiple_of(t * ct, ct), ct)
        m0, m1, m2 = mscr[0, rs, :], mscr[1, rs, :], mscr[2, rs, :]
        mx = jnp.maximum(jnp.maximum(m0, m1), m2)
        w0, w1, w2 = jnp.exp2(m0 - mx), jnp.exp2(m1 - mx), jnp.exp2(m2 - mx)
        num = w0 * oscr[0, rs, :] + w1 * oscr[1, rs, :] + w2 * oscr[2, rs, :]
        den = w0 * dscr[0, rs, :] + w1 * dscr[1, rs, :] + w2 * dscr[2, rs, :]
        o_ref[rs, :] = num / den
        return carry

    lax.fori_loop(0, sb // ct, combine, 0)


def _attention(aq, ak, av, bias, bsz, seq):
    n, width = aq.shape
    sb = ATTN_SUPER
    nsb = seq // sb
    pairs = width // LANES
    npat = len(DILATIONS)
    blk = pl.BlockSpec((sb, LANES), lambda b, h, t: (b * nsb + t, h))
    return pl.pallas_call(
        _attn_kernel,
        grid=(bsz, pairs, nsb),
        in_specs=[blk, blk, blk,
                  pl.BlockSpec((2, npat, 2, ATTN_BLOCK, 2 * ATTN_BLOCK), lambda b, h, t: (0, 0, h, 0, 0))],
        out_specs=blk,
        out_shape=jax.ShapeDtypeStruct((n, width), F32),
        scratch_shapes=[
            pltpu.VMEM((npat, max(DILATIONS) * ATTN_BLOCK, LANES), BF16),
            pltpu.VMEM((npat, max(DILATIONS) * ATTN_BLOCK, LANES), BF16),
            pltpu.VMEM((ATTN_BLOCK, 4 * ATTN_BLOCK), BF16),
            pltpu.VMEM((4 * ATTN_BLOCK, 2 * LANES), BF16),
            pltpu.VMEM((npat, sb, LANES), F32),
            pltpu.VMEM((npat, sb, LANES), F32),
            pltpu.VMEM((npat, sb, LANES), F32),
        ],
        compiler_params=_params(("parallel", "parallel", "arbitrary")),
        name="dilated_attn",
    )(aq, ak, av, bias)


def _hgrn_consts():
    c = HGRN_CHUNK
    t = np.arange(c)[:, None]
    u = np.arange(c)[None, :]
    low, masks, sgn = [], [], []
    m = c // 2
    while m >= 1:
        mid = (t // (2 * m)) * (2 * m) + m
        is_q = t >= mid
        if m in (4, 2):
            low.append((is_q & (u >= mid) & (u <= t)) | (~is_q & (u > t) & (u < mid)))
        if m >= 8:
            sgn.append(np.broadcast_to(np.where(is_q, 1.0, -1.0), (c, c)))
        if m == 1:
            sgn.append(np.broadcast_to(np.where(is_q, 1.0, 0.0), (c, c)))
        masks.append(((t // (2 * m)) == (u // (2 * m))) & (t % (2 * m) >= m) & (u % (2 * m) < m))
        m //= 2
    masks.append(t == u)
    f = lambda a: np.asarray(a, np.float32)
    return (jnp.asarray(f(u <= t), BF16), jnp.asarray(f(np.concatenate(low, axis=0)), BF16),
            jnp.asarray(f(np.stack(sgn))), jnp.asarray(f(np.stack(masks))))


def _hgrn_kernel(q_ref, z_ref, v_ref, g_ref, lb_ref, gn_ref, tri_ref, low_ref, sgn_ref, msk_ref, o_ref, st):
    c = HGRN_CHUNK

    @pl.when(pl.program_id(2) == 0)
    def _():
        st[...] = jnp.zeros_like(st)

    lb = jnp.maximum(lb_ref[...], 0.0)
    log_lb = jnp.log(lb)
    log_1m = jnp.log1p(-lb)
    one_m = 1.0 - lb
    gn = gn_ref[...]
    n_levels = msk_ref.shape[0] - 1
    coarse = [c >> (i + 1) for i in range(n_levels) if (c >> (i + 1)) >= 8]

    def chunk(ci, carry):
        rs = pl.ds(pl.multiple_of(ci * c, c), c)
        z = z_ref[rs, :]
        qb = q_ref[rs, :]
        vb = v_ref[rs, :]
        ez = jnp.exp(-jnp.abs(z))
        ls = jnp.minimum(z, 0.0) - jnp.log(1.0 + ez)
        b2 = log_1m + ls
        lf = jnp.maximum(log_lb, b2) + jnp.log(1.0 + jnp.exp(-jnp.abs(log_lb - b2)))
        rz = 1.0 / (1.0 + ez)
        kb = (one_m * jnp.where(z > 0.0, ez * rz, rz)).astype(BF16)
        l2 = lf * LOG2E
        hi = l2.astype(BF16)
        hl = jnp.concatenate([hi, (l2 - hi.astype(F32)).astype(BF16)], axis=1)
        bb = jnp.dot(tri_ref[...], hl, preferred_element_type=F32)
        b = bb[:, :HGRN_EXPAND] + bb[:, HGRN_EXPAND:]
        dl = jnp.dot(low_ref[...], hl, preferred_element_type=F32)
        dl = dl[:, :HGRN_EXPAND] + dl[:, HGRN_EXPAND:]

        def level(d):
            e = jnp.exp2(d).astype(BF16)
            return _dot_nt(qb * e, kb * e)

        a = msk_ref[n_levels] * _dot_nt(qb, kb)
        for i in range(n_levels):
            m = c >> (i + 1)
            if m >= 8:
                ref = jnp.concatenate([jnp.broadcast_to(b[r0 + m - 1:r0 + m, :], (2 * m, HGRN_EXPAND))
                                       for r0 in range(0, c, 2 * m)], axis=0)
                d = (b - ref) * sgn_ref[i]
            elif m > 1:
                d = dl[(i - len(coarse)) * c:(i - len(coarse) + 1) * c]
            else:
                d = l2 * sgn_ref[len(coarse)]
            a = a + msk_ref[i] * level(d)
        b_last = b[c - 1:c, :]
        s_t = st[...]
        o = _dot_nt(qb * jnp.exp2(b).astype(BF16), s_t.astype(BF16)) + jnp.dot(a.astype(BF16), vb, preferred_element_type=F32)
        st[...] = jnp.exp2(b_last) * s_t + _dot_tn(vb, kb * jnp.exp2(b_last - b).astype(BF16))
        g = g_ref[rs, :]
        o_ref[rs, :] = (_rms(o, gn) * (g * jax.nn.sigmoid(g))).astype(o_ref.dtype)
        return carry

    lax.fori_loop(0, q_ref.shape[0] // c, chunk, 0, unroll=HGRN_UNROLL)


def _hgrn(rq, rf, ri, rg, lb, gn, consts, bsz, seq):
    n, width = rq.shape
    heads = width // HGRN_EXPAND
    ts = HGRN_TILE
    nt = seq // ts
    blk = pl.BlockSpec((ts, HGRN_EXPAND), lambda b, h, t: (b * nt + t, h))
    vec = pl.BlockSpec((1, HGRN_EXPAND), lambda b, h, t: (0, h))
    const = lambda a: pl.BlockSpec(a.shape, lambda b, h, t: (0,) * a.ndim)
    return pl.pallas_call(
        _hgrn_kernel,
        grid=(bsz, heads, nt),
        in_specs=[blk, blk, blk, blk, vec, vec] + [const(a) for a in consts],
        out_specs=blk,
        out_shape=jax.ShapeDtypeStruct((n, width), BF16),
        scratch_shapes=[pltpu.VMEM((HGRN_EXPAND, HGRN_EXPAND), F32)],
        compiler_params=_params(("parallel", "parallel", "arbitrary")),
        name="hgrn2",
    )(rq, rf, ri, rg, lb.reshape(1, width), gn.reshape(1, width), *consts)


def _route_math(lg):
    lane = lax.broadcasted_iota(jnp.int32, lg.shape, 1).astype(F32)
    big = float(LANES)

    def first_argmax(vals):
        mx = jnp.max(vals, axis=-1, keepdims=True)
        return mx, jnp.min(jnp.where(vals == mx, lane, big), axis=-1, keepdims=True)

    is_group = lane < N_GROUPS
    gmax, gidx = first_argmax(jnp.where(is_group, lg, -jnp.inf))
    gprob = 1.0 / jnp.sum(jnp.where(is_group, jnp.exp(lg - gmax), 0.0), axis=-1, keepdims=True)
    lo = N_GROUPS + EXPERTS_PER_GROUP * gidx
    el = jnp.where(jnp.logical_and(lane >= lo, lane < lo + EXPERTS_PER_GROUP), lg, -jnp.inf)
    t1, i1 = first_argmax(el)
    t2, i2 = first_argmax(jnp.where(lane == i1, -jnp.inf, el))
    ex = jnp.exp(t2 - t1)
    p1 = 1.0 / (1.0 + ex)
    p2 = ex / (1.0 + ex)
    eids = jnp.where(lane == 0, i1 - N_GROUPS, jnp.where(lane == 1, i2 - N_GROUPS, 0.0)).astype(jnp.int32)
    gates = jnp.where(lane == 0, p1 * gprob, jnp.where(lane == 1, p2 * gprob, 0.0))
    return eids, gates


def _out_kernel(a_ref, r_ref, x_ref, mod_ref, ag_ref, g2_ref, wo_ref, wr_ref, br_ref, xo_ref, ha_ref, hb_ref, e_ref, gt_ref):
    mod = mod_ref[0]
    mix = jnp.concatenate([_rms(a_ref[...], ag_ref[...]).astype(BF16), r_ref[...]], axis=1)
    xn = x_ref[...] + mod[2:3, :] * jnp.dot(mix, wo_ref[...].astype(BF16), preferred_element_type=F32)
    xo_ref[...] = xn
    h2 = _rms(xn, g2_ref[...]) * (1.0 + mod[4:5, :]) + mod[3:4, :]
    ha_ref[...], hb_ref[...] = _pack_rows(h2)
    e_ref[...], gt_ref[...] = _route_math(jnp.dot(h2.astype(BF16), wr_ref[...], preferred_element_type=F32) + br_ref[...])


def _out_proj(attn, rec, x2, mod_l, ag, g2, wo, wr_bf, br, seq):
    n, d = x2.shape
    half = attn.shape[1]
    tm = ROW_TILE
    per_batch = seq // tm
    row = lambda w: pl.BlockSpec((tm, w), lambda i: (i, 0))
    full = lambda a: pl.BlockSpec(a.shape, lambda i: (0,) * a.ndim, pipeline_mode=pl.Buffered(1))
    ag2, g22 = ag.reshape(1, half), g2.reshape(1, d)
    return pl.pallas_call(
        _out_kernel,
        grid=(n // tm,),
        in_specs=[row(half), row(half), row(d), pl.BlockSpec((1, 6, d), lambda i: (i // per_batch, 0, 0)),
                  full(ag2), full(g22), full(wo), full(wr_bf), full(br)],
        out_specs=[row(d), row(PACK_WIDTH), row(PACK_WIDTH), row(LANES), row(LANES)],
        out_shape=[jax.ShapeDtypeStruct((n, d), F32), jax.ShapeDtypeStruct((n, PACK_WIDTH), U32),
                   jax.ShapeDtypeStruct((n, PACK_WIDTH), U32), jax.ShapeDtypeStruct((n, LANES), jnp.int32),
                   jax.ShapeDtypeStruct((n, LANES), F32)],
        compiler_params=_params(("parallel",)),
        name="out_proj",
    )(attn, rec, x2, mod_l, ag2, g22, wo, wr_bf, br)


def _plan_kernel(e_ref, ut_ref, pos_ref, be_ref, bv_ref, nu_ref, cnt, off, run):
    phase = pl.program_id(0)
    i = pl.program_id(1)
    t = e_ref.shape[1]
    eio = lax.broadcasted_iota(jnp.int32, (N_EXPERTS, t), 0)
    oh1 = e_ref[0:1, :] == eio
    oh2 = e_ref[1:2, :] == eio
    oh = jnp.logical_or(oh1, oh2).astype(F32)
    tot = jnp.sum(oh, axis=1, keepdims=True)

    @pl.when(jnp.logical_and(phase == 0, i == 0))
    def _():
        cnt[...] = jnp.zeros_like(cnt)

    @pl.when(phase == 0)
    def _():
        cnt[...] += tot

    @pl.when(jnp.logical_and(phase == 1, i == 0))
    def _():
        counts = cnt[...]
        nb = jnp.floor((counts + (MOE_TILE - 1)) * (1.0 / MOE_TILE))
        er = lax.broadcasted_iota(jnp.int32, (N_EXPERTS, N_EXPERTS), 0)
        ec = lax.broadcasted_iota(jnp.int32, (N_EXPERTS, N_EXPERTS), 1)
        cum_excl = jnp.dot((ec < er).astype(BF16), nb.astype(BF16), preferred_element_type=F32)
        off[...] = cum_excl * MOE_TILE
        run[...] = jnp.zeros_like(run)
        ce, ci, cn = cum_excl[:, 0:1], (cum_excl + nb)[:, 0:1], counts[:, 0:1]
        b = lax.broadcasted_iota(jnp.int32, (N_EXPERTS, be_ref.shape[1]), 1).astype(F32)
        member = jnp.logical_and(b >= ce, b < ci)
        be = jnp.minimum(jnp.sum((b >= ci).astype(F32), axis=0, keepdims=True), N_EXPERTS - 1.0)
        bv = jnp.sum(jnp.where(member, jnp.minimum(cn - (b - ce) * MOE_TILE, float(MOE_TILE)), 0.0), axis=0, keepdims=True)
        be_ref[...] = be.astype(jnp.int32)
        bv_ref[...] = bv.astype(jnp.int32)
        nu_ref[...] = jnp.broadcast_to(ci[N_EXPERTS - 1:N_EXPERTS, :], nu_ref.shape).astype(jnp.int32)

    @pl.when(phase == 1)
    def _():
        c = jnp.dot(oh.astype(BF16), ut_ref[...], preferred_element_type=F32)
        r = off[:, 0:1] + run[:, 0:1] + c - 1.0
        pos_ref[0:1, :] = jnp.sum(jnp.where(oh1, r, 0.0), axis=0, keepdims=True).astype(jnp.int32)
        pos_ref[1:2, :] = jnp.sum(jnp.where(oh2, r, 0.0), axis=0, keepdims=True).astype(jnp.int32)
        run[...] += tot


def _plan(expert_ids_t, n_rows):
    n = expert_ids_t.shape[1]
    t = PLAN_TILE
    n_blocks = n_rows // MOE_TILE
    assert n_blocks <= PLAN_LANES and TOP_K == 2
    ut = jnp.asarray(np.triu(np.ones((t, t), np.float32)), BF16)
    tab = jax.ShapeDtypeStruct((1, PLAN_LANES), jnp.int32)
    const = lambda shape: pl.BlockSpec(shape, lambda p, i: (0, 0))
    pos, be, bv, nu = pl.pallas_call(
        _plan_kernel,
        grid=(2, n // t),
        in_specs=[pl.BlockSpec((TOP_K, t), lambda p, i: (0, i)), const((t, t))],
        out_specs=[pl.BlockSpec((TOP_K, t), lambda p, i: (0, i * p)), const((1, PLAN_LANES)), const((1, PLAN_LANES)),
                   const((1, LANES))],
        out_shape=[jax.ShapeDtypeStruct((TOP_K, n), jnp.int32), tab, tab, jax.ShapeDtypeStruct((1, LANES), jnp.int32)],
        scratch_shapes=[pltpu.VMEM((N_EXPERTS, LANES), F32)] * 3,
        compiler_params=_params(("arbitrary", "arbitrary")),
        name="plan",
    )(expert_ids_t, ut)
    return pos, be.reshape(-1), bv.reshape(-1), nu.reshape(-1)[:1]


def _sc_mesh():
    return plsc.VectorSubcoreMesh(core_axis_name="c", subcore_axis_name="s")


def _sc_scatter_rows(x, pos, n_rows):
    n, d = x.shape
    w = SC_WINDOW

    @functools.partial(pl.kernel, out_type=jax.ShapeDtypeStruct((n_rows, d), x.dtype), mesh=_sc_mesh(), scratch_types=[])
    def scatter(x_hbm, p_hbm, o_hbm):
        def body(x_vmem, i_vmem):
            for k in range(TOP_K):
                pltpu.sync_copy(x_vmem, o_hbm.at[i_vmem.at[k]])

        pltpu.emit_pipeline(
            body,
            grid=(n // w,),
            in_specs=[pl.BlockSpec((w, d), lambda i: (i, 0)), pl.BlockSpec((TOP_K, w), lambda i: (0, i))],
            out_specs=[],
            core_axis_name=("c", "s"),
            dimension_semantics=(pltpu.PARALLEL,),
        )(x_hbm, p_hbm)

    return scatter(x, pos)


def _sc_gather_rows(y, pos):
    m = pos.shape[0] * pos.shape[1]
    d = y.shape[1]
    w = SC_WINDOW

    @functools.partial(pl.kernel, out_type=jax.ShapeDtypeStruct((m, d), y.dtype), mesh=_sc_mesh(), scratch_types=[])
    def gather(y_hbm, p_hbm, o_hbm):
        def body(i_vmem, o_vmem):
            pltpu.sync_copy(y_hbm.at[i_vmem.at[0]], o_vmem)

        pltpu.emit_pipeline(
            body,
            grid=(m // w,),
            in_specs=[pl.BlockSpec((1, w), lambda i: (0, i))],
            out_specs=[pl.BlockSpec((w, d), lambda i: (i, 0))],
            core_axis_name=("c", "s"),
            dimension_semantics=(pltpu.PARALLEL,),
        )(p_hbm, o_hbm)

    return gather(y, pos.reshape(1, m))


def _expert_kernel(be_ref, bv_ref, nu_ref, xa_ref, xb_ref, w1_ref, w3_ref, w2_ref, ya_ref, yb_ref, w1b, w3b, w2b):
    i = pl.program_id(0)
    tb = xa_ref.shape[0]

    @pl.when(i < nu_ref[0])
    def _():
        e = be_ref[i]
        e_prev = be_ref[jnp.maximum(i - 1, 0)]

        @pl.when(jnp.logical_or(i == 0, e != e_prev))
        def _():
            w1b[...] = w1_ref[0, 0].astype(BF16)
            w3b[...] = w3_ref[0, 0].astype(BF16)
            w2b[...] = w2_ref[0, 0].astype(BF16)

        live = lax.broadcasted_iota(jnp.int32, (tb, 1), 0) < bv_ref[i]
        x = jnp.where(live, _unpack_rows(xa_ref[...], xb_ref[...]), 0.0).astype(BF16)
        a = jnp.dot(x, w1b[...], preferred_element_type=F32)
        b = jnp.dot(x, w3b[...], preferred_element_type=F32)
        hm = (a * jax.nn.sigmoid(a) * b).astype(BF16)
        ya_ref[...], yb_ref[...] = _pack_rows(jnp.dot(hm, w2b[...], preferred_element_type=F32))

    @pl.when(i >= nu_ref[0])
    def _():
        ya_ref[...] = jnp.zeros_like(ya_ref)
        yb_ref[...] = jnp.zeros_like(yb_ref)


def _experts(xa, xb, w1, w3, w2, layer, block_expert, block_valid, n_used):
    n_rows = xa.shape[0]
    d, de = w1.shape[-2:]
    tb = MOE_TILE
    wspec = lambda r, c: pl.BlockSpec((1, 1, r, c), lambda i, be, bv, nu: (layer, be[i], 0, 0))
    xspec = pl.BlockSpec((tb, PACK_WIDTH), lambda i, be, bv, nu: (jnp.minimum(i, nu[0] - 1), 0))
    yspec = pl.BlockSpec((tb, PACK_WIDTH), lambda i, be, bv, nu: (i, 0))
    grid_spec = pltpu.PrefetchScalarGridSpec(
        num_scalar_prefetch=3,
        grid=(n_rows // tb,),
        in_specs=[xspec, xspec, wspec(d, de), wspec(d, de), wspec(de, d)],
        out_specs=[yspec, yspec],
        scratch_shapes=[pltpu.VMEM((d, de), BF16), pltpu.VMEM((d, de), BF16), pltpu.VMEM((de, d), BF16)],
    )
    rows = jax.ShapeDtypeStruct((n_rows, PACK_WIDTH), U32)
    return pl.pallas_call(
        _expert_kernel,
        grid_spec=grid_spec,
        out_shape=[rows, rows],
        compiler_params=_params(("arbitrary",)),
        name="experts",
    )(block_expert, block_valid, n_used, xa, xb, w1, w3, w2)


def _final_kernel(ga0, ga1, gb0, gb1, gt_ref, mod_ref, x_ref, fg_ref, o_ref):
    o_ref[...] = _rms(_moe_residual(x_ref, ga0, ga1, gb0, gb1, gt_ref, mod_ref[0][5:6, :]), fg_ref[...])


def _final(x2, ga, gb, gates, mod_l, final_g, seq):
    n, d = x2.shape
    tm = ROW_TILE
    per_batch = seq // tm
    row = lambda w: pl.BlockSpec((tm, w), lambda i: (i, 0))
    second = pl.BlockSpec((tm, PACK_WIDTH), lambda i: (i + n // tm, 0))
    return pl.pallas_call(
        _final_kernel,
        grid=(n // tm,),
        in_specs=[row(PACK_WIDTH), second, row(PACK_WIDTH), second, row(LANES),
                  pl.BlockSpec((1, 6, d), lambda i: (i // per_batch, 0, 0)), row(d), pl.BlockSpec((1, d), lambda i: (0, 0))],
        out_specs=row(d),
        out_shape=jax.ShapeDtypeStruct((n, d), F32),
        compiler_params=_params(("parallel",)),
        name="final_norm",
    )(ga, ga, gb, gb, gates, mod_l, x2, final_g.reshape(1, d))


def kernel(x, c, w_ada, b_ada, norm1_g, w_in, attn_norm_g, hgrn_lb_logits, hgrn_norm_g, w_out, norm2_g, router_group_w, router_group_b, router_expert_w, router_expert_b, moe_w1, moe_w3, moe_w2, final_g):
    bsz, seq, d = x.shape
    depth = w_ada.shape[0]
    n = bsz * seq
    attn_dim = attn_norm_g.shape[1]
    n_heads = attn_dim // ATTN_HEAD_DIM

    lb_w = jax.nn.softmax(hgrn_lb_logits.astype(F32), axis=0)
    lower_bounds = jnp.cumsum(lb_w, axis=0) - lb_w[0:1]
    mod = _adaln(c, w_ada, b_ada)
    bias = _attn_bias(n_heads)
    hconsts = _hgrn_consts()
    n_rows = n * TOP_K + N_EXPERTS * MOE_TILE
    pad = LANES - N_GROUPS - N_EXPERTS

    x2 = x.reshape(n, d)
    moe = None
    for layer in range(depth):
        x2, (aq, ak, av, rq, rf, ri, rg) = _in_proj(x2, mod[layer], norm1_g[layer], w_in[layer], seq, moe)
        attn = _attention(aq, ak, av, bias, bsz, seq)
        rec = _hgrn(rq, rf, ri, rg, lower_bounds[layer], hgrn_norm_g[layer], hconsts, bsz, seq)
        wr = jnp.concatenate([router_group_w[layer], router_expert_w[layer], jnp.zeros((d, pad), F32)], axis=1)
        br = jnp.concatenate([router_group_b[layer], router_expert_b[layer], jnp.zeros((pad,), F32)]).reshape(1, LANES)
        x2, ha, hb, eids, gates = _out_proj(attn, rec, x2, mod[layer], attn_norm_g[layer], norm2_g[layer],
                                            w_out[layer], wr.astype(BF16), br, seq)
        pos, block_expert, block_valid, n_used = _plan(eids[:, :TOP_K].T, n_rows)
        ya, yb = _experts(_sc_scatter_rows(ha, pos, n_rows), _sc_scatter_rows(hb, pos, n_rows),
                          moe_w1, moe_w3, moe_w2, layer, block_expert, block_valid, n_used)
        moe = (_sc_gather_rows(ya, pos), _sc_gather_rows(yb, pos), gates, mod[layer])
    return _final(x2, moe[0], moe[1], moe[2], moe[3], final_g, seq).reshape(bsz, seq, d)
```

```python
import functools

import numpy as np
import jax
import jax.numpy as jnp
from jax import lax
from jax.experimental import pallas as pl
from jax.experimental.pallas import tpu as pltpu
from jax.experimental.pallas import tpu_sc as plsc

F32 = jnp.float32
BF16 = jnp.bfloat16
U32 = jnp.uint32

ATTN_HEAD_DIM = 64
ATTN_BLOCK = 128
DILATIONS = (1, 4, 16)
ALIBI_MAX_EXP = 8.0
HGRN_EXPAND = 128
HGRN_CHUNK = 128
N_GROUPS = 4
EXPERTS_PER_GROUP = 8
N_EXPERTS = N_GROUPS * EXPERTS_PER_GROUP
TOP_K = 2
NORM_EPS = 1e-6
LOG2E = 1.4426950408889634

LANES = 128
ATTN_SUPER = ATTN_BLOCK * max(DILATIONS)
ROW_TILE = 512
ATTN_UNROLL = 16
HGRN_TILE = 2048
HGRN_UNROLL = 8
MOE_TILE = 512
FFN_SPLIT = 2
PLAN_TILE = 2048
PLAN_LANES = 256
PACK_WIDTH = 256
SC_WINDOW = 128
VMEM_LIMIT = 56 << 20


def _params(semantics):
    return pltpu.CompilerParams(dimension_semantics=semantics, vmem_limit_bytes=VMEM_LIMIT)


def _rms(x, g):
    return x * lax.rsqrt(jnp.mean(x * x, axis=-1, keepdims=True) + NORM_EPS) * g


def _dot_nt(a, b):
    return lax.dot_general(a, b, (((1,), (1,)), ((), ())), preferred_element_type=F32)


def _dot_tn(a, b):
    return lax.dot_general(a, b, (((0,), (0,)), ((), ())), preferred_element_type=F32)


def _pack_pair(hi, lo):
    hb = lax.bitcast_convert_type(hi.astype(BF16).astype(F32), U32)
    lb = lax.bitcast_convert_type(lo.astype(BF16).astype(F32), U32)
    return hb | (lb >> 16)


def _unpack_pair(u):
    hi = lax.bitcast_convert_type(u & jnp.uint32(0xFFFF0000), F32)
    lo = lax.bitcast_convert_type(u << 16, F32)
    return hi, lo


def _pack_rows(y):
    w = PACK_WIDTH
    return _pack_pair(y[:, 0:w], y[:, 2 * w:3 * w]), _pack_pair(y[:, w:2 * w], y[:, 3 * w:4 * w])


def _unpack_rows(ua, ub):
    ha, la = _unpack_pair(ua)
    hb, lb = _unpack_pair(ub)
    return jnp.concatenate([ha, hb, la, lb], axis=1)


def _moe_residual(x_ref, ga0, ga1, gb0, gb1, gt_ref, gate_row):
    gt = gt_ref[...]
    y = gt[:, 0:1] * _unpack_rows(ga0[...], gb0[...]) + gt[:, 1:2] * _unpack_rows(ga1[...], gb1[...])
    return x_ref[...] + gate_row * y


def _ada_kernel(c_ref, w_ref, b_ref, o_ref):
    c = c_ref[...]
    ca = c * jax.nn.sigmoid(c)
    o_ref[0] = jnp.dot(ca, w_ref[0], precision=lax.Precision.HIGHEST, preferred_element_type=F32) + b_ref[0]


def _adaln(c, w_ada, b_ada):
    depth, d, n6 = w_ada.shape
    bsz = c.shape[0]
    rows = 8
    cp = jnp.zeros((rows, d), F32).at[:bsz].set(c)
    tn = 1536
    out = pl.pallas_call(
        _ada_kernel,
        grid=(depth, n6 // tn),
        in_specs=[
            pl.BlockSpec((rows, d), lambda l, j: (0, 0)),
            pl.BlockSpec((1, d, tn), lambda l, j: (l, 0, j)),
            pl.BlockSpec((1, 1, tn), lambda l, j: (l, 0, j)),
        ],
        out_specs=pl.BlockSpec((1, rows, tn), lambda l, j: (l, 0, j)),
        out_shape=jax.ShapeDtypeStruct((depth, rows, n6), F32),
        compiler_params=_params(("parallel", "parallel")),
        name="adaln",
    )(cp, w_ada, b_ada.reshape(depth, 1, n6))
    return out[:, :bsz].reshape(depth, bsz, 6, d)


def _in_kernel(*refs, n_moe):
    moe, (x_ref, mod_ref, g_ref, w_ref), rest = refs[:n_moe], refs[n_moe:n_moe + 4], refs[n_moe + 4:]
    hb = rest[-1]
    if n_moe:
        ga0, ga1, gb0, gb1, gt_ref, pmod_ref = moe
        x = _moe_residual(x_ref, ga0, ga1, gb0, gb1, gt_ref, pmod_ref[0][5:6, :])
        rest[0][...] = x
        outs = rest[1:-1]
    else:
        x = x_ref[...]
        outs = rest[:-1]
    mod = mod_ref[0]
    h = _rms(x, g_ref[...]) * (1.0 + mod[1:2, :]) + mod[0:1, :]
    hb[...] = h.astype(BF16)
    width = outs[0].shape[1]
    for i, o in enumerate(outs):
        w = w_ref[0, :, i * width:(i + 1) * width].astype(BF16)
        o[...] = jnp.dot(hb[...], w, preferred_element_type=F32).astype(o.dtype)


def _in_proj(x2, mod_l, g, w_in, layer, seq, moe=None):
    n, d = x2.shape
    n_out = w_in.shape[2]
    width = 512
    tm = ROW_TILE
    per_batch = seq // tm
    row = lambda w: pl.BlockSpec((tm, w), lambda i: (i, 0))
    mod_spec = pl.BlockSpec((1, 6, d), lambda i: (i // per_batch, 0, 0))
    in_specs, args, out_specs, out_shape = [], [], [], []
    if moe is not None:
        ga, gb, gates, mod_prev = moe
        second = pl.BlockSpec((tm, PACK_WIDTH), lambda i: (i + n // tm, 0))
        in_specs += [row(PACK_WIDTH), second, row(PACK_WIDTH), second, row(LANES), mod_spec]
        args += [ga, ga, gb, gb, gates, mod_prev]
        out_specs.append(row(d))
        out_shape.append(jax.ShapeDtypeStruct((n, d), F32))
    in_specs += [row(d), mod_spec, pl.BlockSpec((1, d), lambda i: (0, 0)),
                 pl.BlockSpec((1, d, n_out), lambda i: (layer, 0, 0), pipeline_mode=pl.Buffered(1))]
    args += [x2, mod_l, g.reshape(1, d), w_in]
    out_dtypes = (F32, F32, F32, BF16, F32, BF16, F32)
    out_specs += [row(width)] * len(out_dtypes)
    out_shape += [jax.ShapeDtypeStruct((n, width), dt) for dt in out_dtypes]
    outs = pl.pallas_call(
        functools.partial(_in_kernel, n_moe=0 if moe is None else 6),
        grid=(n // tm,),
        in_specs=in_specs,
        out_specs=out_specs,
        out_shape=out_shape,
        scratch_shapes=[pltpu.VMEM((tm, d), BF16)],
        compiler_params=_params(("parallel",)),
        name="in_proj",
    )(*args)
    return (x2, outs) if moe is None else (outs[0], outs[1:])


def _attn_bias(n_heads):
    qi = np.arange(ATTN_BLOCK)[:, None]
    kj = np.arange(2 * ATTN_BLOCK)[None, :]
    dist = ATTN_BLOCK + qi - kj
    valid = (dist >= 0) & (dist <= ATTN_BLOCK)
    slopes = np.exp2(-ALIBI_MAX_EXP * np.arange(1, n_heads + 1, dtype=np.float32) / n_heads)
    out = np.empty((2, len(DILATIONS), n_heads, ATTN_BLOCK, 2 * ATTN_BLOCK), np.float32)
    for p, dil in enumerate(DILATIONS):
        b = -slopes[:, None, None] * (dist * dil).astype(np.float32)[None] * np.float32(LOG2E)
        out[0, p] = np.where(valid[None], b, -np.inf)
        out[1, p] = np.where((valid & (kj >= ATTN_BLOCK))[None], b, -np.inf)
    return jnp.asarray(out)


def _attn_kernel(q_ref, k_ref, v_ref, bias_ref, o_ref, kcar, vcar, pbuf, vbuf, oscr, mscr, dscr):
    n = pl.program_id(2)
    sb = q_ref.shape[0]
    is_lo = lax.broadcasted_iota(jnp.int32, (ATTN_BLOCK, LANES), 1) < ATTN_HEAD_DIM
    scale = ATTN_HEAD_DIM ** -0.5 * LOG2E

    @pl.when(n == 0)
    def _():
        kcar[...] = jnp.zeros_like(kcar)
        vcar[...] = jnp.zeros_like(vcar)

    pbuf[...] = jnp.zeros_like(pbuf)
    vbuf[...] = jnp.zeros_like(vbuf)
    lo2 = lax.broadcasted_iota(jnp.int32, (2 * ATTN_BLOCK, LANES), 1) < ATTN_HEAD_DIM
    zero = jnp.zeros((2 * ATTN_BLOCK, LANES), BF16)
    den_lo = lo2.astype(BF16)
    den_hi = jnp.logical_not(lo2).astype(BF16)

    for p, dil in enumerate(DILATIONS):
        nb = (sb // ATTN_BLOCK) // dil

        def rows_of(it, dil=dil, nb=nb):
            start = (it % nb) * (ATTN_BLOCK * dil) + it // nb
            return pl.ds(start, ATTN_BLOCK, stride=dil) if dil > 1 else pl.ds(pl.multiple_of(start, ATTN_BLOCK), ATTN_BLOCK)

        def pv(it, probs, vaug, p=p, rows_of=rows_of):
            rows = rows_of(it)
            of = jnp.dot(probs, vaug, preferred_element_type=F32)
            oscr[p, rows, :] = of[:, :LANES]
            dscr[p, rows, :] = of[:, LANES:]

        def qk(it, p=p, nb=nb, rows_of=rows_of):
            r = it // nb
            rows = rows_of(it)
            crow = pl.ds(pl.multiple_of(r * ATTN_BLOCK, ATTN_BLOCK), ATTN_BLOCK)
            qb = q_ref[rows, :] * scale
            kc = k_ref[rows, :].astype(BF16)
            vc = v_ref[rows, :].astype(BF16)
            kcat = jnp.concatenate([kcar[p, crow, :], kc], axis=0)
            vcat = jnp.concatenate([vcar[p, crow, :], vc], axis=0)
            vaug = jnp.concatenate([jnp.concatenate([jnp.where(lo2, vcat, zero), den_lo], axis=1),
                                    jnp.concatenate([jnp.where(lo2, zero, vcat), den_hi], axis=1)], axis=0)
            kcar[p, crow, :] = kc
            vcar[p, crow, :] = vc
            first = jnp.logical_and(n == 0, it % nb == 0).astype(jnp.int32)
            probs, acc_m = [], None
            for hh in range(2):
                sel = is_lo if hh == 0 else jnp.logical_not(is_lo)
                qm = jnp.where(sel, qb, 0.0).astype(BF16)
                s = _dot_nt(qm, kcat) + bias_ref[first, p, hh]
                m = jnp.max(s, axis=-1, keepdims=True)
                probs.append(jnp.exp2(s - m).astype(BF16))
                mb = jnp.broadcast_to(m, (ATTN_BLOCK, LANES))
                acc_m = mb if hh == 0 else jnp.where(sel, mb, acc_m)
            mscr[p, rows, :] = acc_m
            return jnp.concatenate(probs, axis=1), vaug

        def body(g, carry, pv=pv, qk=qk):
            it0 = g * ATTN_UNROLL
            pend = (pbuf[...], vbuf[...])
            for u in range(ATTN_UNROLL):
                cur = qk(it0 + u)
                pv(jnp.maximum(it0 + u - 1, 0), *pend)
                pend = cur
            pbuf[...] = pend[0]
            vbuf[...] = pend[1]
            return carry

        n_it = sb // ATTN_BLOCK
        lax.fori_loop(0, n_it // ATTN_UNROLL, body, 0)
        pv(n_it - 1, pbuf[...], vbuf[...])

    ct = 256

    def combine(t, carry):
        rs = pl.ds(pl.multiple_of(t * ct, ct), ct)
        m0, m1, m2 = mscr[0, rs, :], mscr[1, rs, :], mscr[2, rs, :]
        mx = jnp.maximum(jnp.maximum(m0, m1), m2)
        w0, w1, w2 = jnp.exp2(m0 - mx), jnp.exp2(m1 - mx), jnp.exp2(m2 - mx)
        num = w0 * oscr[0, rs, :] + w1 * oscr[1, rs, :] + w2 * oscr[2, rs, :]
        den = w0 * dscr[0, rs, :] + w1 * dscr[1, rs, :] + w2 * dscr[2, rs, :]
        o_ref[rs, :] = num / den
        return carry

    lax.fori_loop(0, sb // ct, combine, 0)


def _attention(aq, ak, av, bias, bsz, seq):
    n, width = aq.shape
    sb = ATTN_SUPER
    nsb = seq // sb
    pairs = width // LANES
    npat = len(DILATIONS)
    blk = pl.BlockSpec((sb, LANES), lambda b, h, t: (b * nsb + t, h))
    return pl.pallas_call(
        _attn_kernel,
        grid=(bsz, pairs, nsb),
        in_specs=[blk, blk, blk,
                  pl.BlockSpec((2, npat, 2, ATTN_BLOCK, 2 * ATTN_BLOCK), lambda b, h, t: (0, 0, h, 0, 0))],
        out_specs=blk,
        out_shape=jax.ShapeDtypeStruct((n, width), F32),
        scratch_shapes=[
            pltpu.VMEM((npat, max(DILATIONS) * ATTN_BLOCK, LANES), BF16),
            pltpu.VMEM((npat, max(DILATIONS) * ATTN_BLOCK, LANES), BF16),
            pltpu.VMEM((ATTN_BLOCK, 4 * ATTN_BLOCK), BF16),
            pltpu.VMEM((4 * ATTN_BLOCK, 2 * LANES), BF16),
            pltpu.VMEM((npat, sb, LANES), F32),
            pltpu.VMEM((npat, sb, LANES), F32),
            pltpu.VMEM((npat, sb, LANES), F32),
        ],
        compiler_params=_params(("parallel", "parallel", "arbitrary")),
        name="dilated_attn",
    )(aq, ak, av, bias)


def _hgrn_consts():
    c = HGRN_CHUNK
    t = np.arange(c)[:, None]
    u = np.arange(c)[None, :]
    low, masks, sgn = [], [], []
    m = c // 2
    while m >= 1:
        mid = (t // (2 * m)) * (2 * m) + m
        is_q = t >= mid
        if m in (4, 2):
            low.append((is_q & (u >= mid) & (u <= t)) | (~is_q & (u > t) & (u < mid)))
        if m >= 8:
            sgn.append(np.broadcast_to(np.where(is_q, 1.0, -1.0), (c, c)))
        if m == 1:
            sgn.append(np.broadcast_to(np.where(is_q, 1.0, 0.0), (c, c)))
        masks.append(((t // (2 * m)) == (u // (2 * m))) & (t % (2 * m) >= m) & (u % (2 * m) < m))
        m //= 2
    masks.append(t == u)
    f = lambda a: np.asarray(a, np.float32)
    return (jnp.asarray(f(u <= t), BF16), jnp.asarray(f(np.concatenate(low, axis=0)), BF16),
            jnp.asarray(f(np.stack(sgn))), jnp.asarray(f(np.stack(masks))))


def _hgrn_kernel(q_ref, z_ref, v_ref, g_ref, lb_ref, gn_ref, tri_ref, low_ref, sgn_ref, msk_ref, o_ref, st):
    c = HGRN_CHUNK

    @pl.when(pl.program_id(2) == 0)
    def _():
        st[...] = jnp.zeros_like(st)

    lb = jnp.maximum(lb_ref[...], 0.0)
    log_lb = jnp.log(lb)
    log_1m = jnp.log1p(-lb)
    one_m = 1.0 - lb
    gn = gn_ref[...]
    n_levels = msk_ref.shape[0] - 1
    coarse = [c >> (i + 1) for i in range(n_levels) if (c >> (i + 1)) >= 8]

    def chunk(ci, carry):
        rs = pl.ds(pl.multiple_of(ci * c, c), c)
        z = z_ref[rs, :]
        qb = q_ref[rs, :]
        vb = v_ref[rs, :]
        ez = jnp.exp(-jnp.abs(z))
        ls = jnp.minimum(z, 0.0) - jnp.log(1.0 + ez)
        b2 = log_1m + ls
        lf = jnp.maximum(log_lb, b2) + jnp.log(1.0 + jnp.exp(-jnp.abs(log_lb - b2)))
        rz = 1.0 / (1.0 + ez)
        kb = (one_m * jnp.where(z > 0.0, ez * rz, rz)).astype(BF16)
        l2 = lf * LOG2E
        hi = l2.astype(BF16)
        hl = jnp.concatenate([hi, (l2 - hi.astype(F32)).astype(BF16)], axis=1)
        bb = jnp.dot(tri_ref[...], hl, preferred_element_type=F32)
        b = bb[:, :HGRN_EXPAND] + bb[:, HGRN_EXPAND:]
        dl = jnp.dot(low_ref[...], hl, preferred_element_type=F32)
        dl = dl[:, :HGRN_EXPAND] + dl[:, HGRN_EXPAND:]

        def level(d):
            e = jnp.exp2(d).astype(BF16)
            return _dot_nt(qb * e, kb * e)

        a = msk_ref[n_levels] * _dot_nt(qb, kb)
        for i in range(n_levels):
            m = c >> (i + 1)
            if m >= 8:
                ref = jnp.concatenate([jnp.broadcast_to(b[r0 + m - 1:r0 + m, :], (2 * m, HGRN_EXPAND))
                                       for r0 in range(0, c, 2 * m)], axis=0)
                d = (b - ref) * sgn_ref[i]
            elif m > 1:
                d = dl[(i - len(coarse)) * c:(i - len(coarse) + 1) * c]
            else:
                d = l2 * sgn_ref[len(coarse)]
            a = a + msk_ref[i] * level(d)
        b_last = b[c - 1:c, :]
        s_t = st[...]
        o = _dot_nt(qb * jnp.exp2(b).astype(BF16), s_t.astype(BF16)) + jnp.dot(a.astype(BF16), vb, preferred_element_type=F32)
        st[...] = jnp.exp2(b_last) * s_t + _dot_tn(vb, kb * jnp.exp2(b_last - b).astype(BF16))
        g = g_ref[rs, :]
        o_ref[rs, :] = (_rms(o, gn) * (g * jax.nn.sigmoid(g))).astype(o_ref.dtype)
        return carry

    lax.fori_loop(0, q_ref.shape[0] // c, chunk, 0, unroll=HGRN_UNROLL)


def _hgrn(rq, rf, ri, rg, lb, gn, consts, bsz, seq):
    n, width = rq.shape
    heads = width // HGRN_EXPAND
    ts = HGRN_TILE
    nt = seq // ts
    blk = pl.BlockSpec((ts, HGRN_EXPAND), lambda b, h, t: (b * nt + t, h))
    vec = pl.BlockSpec((1, HGRN_EXPAND), lambda b, h, t: (0, h))
    const = lambda a: pl.BlockSpec(a.shape, lambda b, h, t: (0,) * a.ndim)
    return pl.pallas_call(
        _hgrn_kernel,
        grid=(bsz, heads, nt),
        in_specs=[blk, blk, blk, blk, vec, vec] + [const(a) for a in consts],
        out_specs=blk,
        out_shape=jax.ShapeDtypeStruct((n, width), BF16),
        scratch_shapes=[pltpu.VMEM((HGRN_EXPAND, HGRN_EXPAND), F32)],
        compiler_params=_params(("parallel", "parallel", "arbitrary")),
        name="hgrn2",
    )(rq, rf, ri, rg, lb.reshape(1, width), gn.reshape(1, width), *consts)


def _route_math(lg):
    lane = lax.broadcasted_iota(jnp.int32, lg.shape, 1).astype(F32)
    big = float(LANES)

    def first_argmax(vals):
        mx = jnp.max(vals, axis=-1, keepdims=True)
        return mx, jnp.min(jnp.where(vals == mx, lane, big), axis=-1, keepdims=True)

    is_group = lane < N_GROUPS
    gmax, gidx = first_argmax(jnp.where(is_group, lg, -jnp.inf))
    gprob = 1.0 / jnp.sum(jnp.where(is_group, jnp.exp(lg - gmax), 0.0), axis=-1, keepdims=True)
    lo = N_GROUPS + EXPERTS_PER_GROUP * gidx
    el = jnp.where(jnp.logical_and(lane >= lo, lane < lo + EXPERTS_PER_GROUP), lg, -jnp.inf)
    t1, i1 = first_argmax(el)
    t2, i2 = first_argmax(jnp.where(lane == i1, -jnp.inf, el))
    ex = jnp.exp(t2 - t1)
    p1 = 1.0 / (1.0 + ex)
    p2 = ex / (1.0 + ex)
    eids = jnp.where(lane == 0, i1 - N_GROUPS, jnp.where(lane == 1, i2 - N_GROUPS, 0.0)).astype(jnp.int32)
    gates = jnp.where(lane == 0, p1 * gprob, jnp.where(lane == 1, p2 * gprob, 0.0))
    return eids, gates


def _out_kernel(a_ref, r_ref, x_ref, mod_ref, ag_ref, g2_ref, wo_ref, wr_ref, br_ref, xo_ref, ha_ref, hb_ref, e_ref, gt_ref):
    mod = mod_ref[0]
    mix = jnp.concatenate([_rms(a_ref[...], ag_ref[...]).astype(BF16), r_ref[...]], axis=1)
    xn = x_ref[...] + mod[2:3, :] * jnp.dot(mix, wo_ref[0].astype(BF16), preferred_element_type=F32)
    xo_ref[...] = xn
    h2 = _rms(xn, g2_ref[...]) * (1.0 + mod[4:5, :]) + mod[3:4, :]
    ha_ref[...], hb_ref[...] = _pack_rows(h2)
    e_ref[...], gt_ref[...] = _route_math(jnp.dot(h2.astype(BF16), wr_ref[...], preferred_element_type=F32) + br_ref[...])


def _out_proj(attn, rec, x2, mod_l, ag, g2, w_out, layer, wr_bf, br, seq):
    n, d = x2.shape
    half = attn.shape[1]
    tm = ROW_TILE
    per_batch = seq // tm
    row = lambda w: pl.BlockSpec((tm, w), lambda i: (i, 0))
    full = lambda a: pl.BlockSpec(a.shape, lambda i: (0,) * a.ndim, pipeline_mode=pl.Buffered(1))
    ag2, g22 = ag.reshape(1, half), g2.reshape(1, d)
    return pl.pallas_call(
        _out_kernel,
        grid=(n // tm,),
        in_specs=[row(half), row(half), row(d), pl.BlockSpec((1, 6, d), lambda i: (i // per_batch, 0, 0)),
                  full(ag2), full(g22), pl.BlockSpec((1, d, d), lambda i: (layer, 0, 0), pipeline_mode=pl.Buffered(1)),
                  full(wr_bf), full(br)],
        out_specs=[row(d), row(PACK_WIDTH), row(PACK_WIDTH), row(LANES), row(LANES)],
        out_shape=[jax.ShapeDtypeStruct((n, d), F32), jax.ShapeDtypeStruct((n, PACK_WIDTH), U32),
                   jax.ShapeDtypeStruct((n, PACK_WIDTH), U32), jax.ShapeDtypeStruct((n, LANES), jnp.int32),
                   jax.ShapeDtypeStruct((n, LANES), F32)],
        compiler_params=_params(("parallel",)),
        name="out_proj",
    )(attn, rec, x2, mod_l, ag2, g22, w_out, wr_bf, br)


def _plan_kernel(e_ref, ut_ref, pos_ref, be_ref, bv_ref, nu_ref, cnt, off, run):
    phase = pl.program_id(0)
    i = pl.program_id(1)
    t = e_ref.shape[1]
    eio = lax.broadcasted_iota(jnp.int32, (N_EXPERTS, t), 0)
    oh1 = e_ref[0:1, :] == eio
    oh2 = e_ref[1:2, :] == eio
    oh = jnp.logical_or(oh1, oh2).astype(F32)
    tot = jnp.sum(oh, axis=1, keepdims=True)

    @pl.when(jnp.logical_and(phase == 0, i == 0))
    def _():
        cnt[...] = jnp.zeros_like(cnt)

    @pl.when(phase == 0)
    def _():
        cnt[...] += tot

    @pl.when(jnp.logical_and(phase == 1, i == 0))
    def _():
        counts = cnt[...]
        nb = jnp.floor((counts + (MOE_TILE - 1)) * (1.0 / MOE_TILE))
        er = lax.broadcasted_iota(jnp.int32, (N_EXPERTS, N_EXPERTS), 0)
        ec = lax.broadcasted_iota(jnp.int32, (N_EXPERTS, N_EXPERTS), 1)
        cum_excl = jnp.dot((ec < er).astype(BF16), nb.astype(BF16), preferred_element_type=F32)
        off[...] = cum_excl * MOE_TILE
        run[...] = jnp.zeros_like(run)
        ce, ci, cn = cum_excl[:, 0:1], (cum_excl + nb)[:, 0:1], counts[:, 0:1]
        b = lax.broadcasted_iota(jnp.int32, (N_EXPERTS, be_ref.shape[1]), 1).astype(F32)
        member = jnp.logical_and(b >= ce, b < ci)
        be = jnp.minimum(jnp.sum((b >= ci).astype(F32), axis=0, keepdims=True), N_EXPERTS - 1.0)
        bv = jnp.sum(jnp.where(member, jnp.minimum(cn - (b - ce) * MOE_TILE, float(MOE_TILE)), 0.0), axis=0, keepdims=True)
        be_ref[...] = be.astype(jnp.int32)
        bv_ref[...] = bv.astype(jnp.int32)
        nu_ref[...] = jnp.broadcast_to(ci[N_EXPERTS - 1:N_EXPERTS, :], nu_ref.shape).astype(jnp.int32)

    @pl.when(phase == 1)
    def _():
        c = jnp.dot(oh.astype(BF16), ut_ref[...], preferred_element_type=F32)
        r = off[:, 0:1] + run[:, 0:1] + c - 1.0
        pos_ref[0:1, :] = jnp.sum(jnp.where(oh1, r, 0.0), axis=0, keepdims=True).astype(jnp.int32)
        pos_ref[1:2, :] = jnp.sum(jnp.where(oh2, r, 0.0), axis=0, keepdims=True).astype(jnp.int32)
        run[...] += tot


def _plan(expert_ids_t, n_rows):
    n = expert_ids_t.shape[1]
    t = PLAN_TILE
    n_blocks = n_rows // MOE_TILE
    assert n_blocks <= PLAN_LANES and TOP_K == 2
    ut = jnp.asarray(np.triu(np.ones((t, t), np.float32)), BF16)
    tab = jax.ShapeDtypeStruct((1, PLAN_LANES), jnp.int32)
    const = lambda shape: pl.BlockSpec(shape, lambda p, i: (0, 0))
    pos, be, bv, nu = pl.pallas_call(
        _plan_kernel,
        grid=(2, n // t),
        in_specs=[pl.BlockSpec((TOP_K, t), lambda p, i: (0, i)), const((t, t))],
        out_specs=[pl.BlockSpec((TOP_K, t), lambda p, i: (0, i * p)), const((1, PLAN_LANES)), const((1, PLAN_LANES)),
                   const((1, LANES))],
        out_shape=[jax.ShapeDtypeStruct((TOP_K, n), jnp.int32), tab, tab, jax.ShapeDtypeStruct((1, LANES), jnp.int32)],
        scratch_shapes=[pltpu.VMEM((N_EXPERTS, LANES), F32)] * 3,
        compiler_params=_params(("arbitrary", "arbitrary")),
        name="plan",
    )(expert_ids_t, ut)
    return pos, be.reshape(-1), bv.reshape(-1), nu.reshape(-1)[:1]


def _sc_mesh():
    return plsc.VectorSubcoreMesh(core_axis_name="c", subcore_axis_name="s")


def _sc_scatter_rows(x, pos, n_rows):
    n, d = x.shape
    w = SC_WINDOW

    @functools.partial(pl.kernel, out_type=jax.ShapeDtypeStruct((n_rows, d), x.dtype), mesh=_sc_mesh(), scratch_types=[])
    def scatter(x_hbm, p_hbm, o_hbm):
        def body(x_vmem, i_vmem):
            for k in range(TOP_K):
                pltpu.sync_copy(x_vmem, o_hbm.at[i_vmem.at[k]])

        pltpu.emit_pipeline(
            body,
            grid=(n // w,),
            in_specs=[pl.BlockSpec((w, d), lambda i: (i, 0)), pl.BlockSpec((TOP_K, w), lambda i: (0, i))],
            out_specs=[],
            core_axis_name=("c", "s"),
            dimension_semantics=(pltpu.PARALLEL,),
        )(x_hbm, p_hbm)

    return scatter(x, pos)


def _sc_gather_rows(y, pos):
    m = pos.shape[0] * pos.shape[1]
    d = y.shape[1]
    w = SC_WINDOW

    @functools.partial(pl.kernel, out_type=jax.ShapeDtypeStruct((m, d), y.dtype), mesh=_sc_mesh(), scratch_types=[])
    def gather(y_hbm, p_hbm, o_hbm):
        def body(i_vmem, o_vmem):
            pltpu.sync_copy(y_hbm.at[i_vmem.at[0]], o_vmem)

        pltpu.emit_pipeline(
            body,
            grid=(m // w,),
            in_specs=[pl.BlockSpec((1, w), lambda i: (0, i))],
            out_specs=[pl.BlockSpec((w, d), lambda i: (i, 0))],
            core_axis_name=("c", "s"),
            dimension_semantics=(pltpu.PARALLEL,),
        )(p_hbm, o_hbm)

    return gather(y, pos.reshape(1, m))


def _expert_kernel(be_ref, bv_ref, nu_ref, xa_ref, xb_ref, w1_ref, w3_ref, w2_ref, ya_ref, yb_ref, w1b, w3b, w2b):
    i = pl.program_id(0)
    tb = xa_ref.shape[0]

    @pl.when(i < nu_ref[0])
    def _():
        e = be_ref[i]
        e_prev = be_ref[jnp.maximum(i - 1, 0)]

        @pl.when(jnp.logical_or(i == 0, e != e_prev))
        def _():
            w1b[...] = w1_ref[0, 0].astype(BF16)
            w3b[...] = w3_ref[0, 0].astype(BF16)
            w2b[...] = w2_ref[0, 0].astype(BF16)

        live = lax.broadcasted_iota(jnp.int32, (tb, 1), 0) < bv_ref[i]
        x = jnp.where(live, _unpack_rows(xa_ref[...], xb_ref[...]), 0.0).astype(BF16)
        y = None
        hc = w1b.shape[1] // FFN_SPLIT
        for c in range(FFN_SPLIT):
            cs = slice(c * hc, (c + 1) * hc)
            a = jnp.dot(x, w1b[:, cs], preferred_element_type=F32)
            b = jnp.dot(x, w3b[:, cs], preferred_element_type=F32)
            hm = (a * jax.nn.sigmoid(a) * b).astype(BF16)
            part = jnp.dot(hm, w2b[cs, :], preferred_element_type=F32)
            y = part if y is None else y + part
        ya_ref[...], yb_ref[...] = _pack_rows(y)

    @pl.when(i >= nu_ref[0])
    def _():
        ya_ref[...] = jnp.zeros_like(ya_ref)
        yb_ref[...] = jnp.zeros_like(yb_ref)


def _experts(xa, xb, w1, w3, w2, layer, block_expert, block_valid, n_used):
    n_rows = xa.shape[0]
    d, de = w1.shape[-2:]
    tb = MOE_TILE
    wspec = lambda r, c: pl.BlockSpec((1, 1, r, c), lambda i, be, bv, nu: (layer, be[i], 0, 0))
    xspec = pl.BlockSpec((tb, PACK_WIDTH), lambda i, be, bv, nu: (jnp.minimum(i, nu[0] - 1), 0))
    yspec = pl.BlockSpec((tb, PACK_WIDTH), lambda i, be, bv, nu: (i, 0))
    grid_spec = pltpu.PrefetchScalarGridSpec(
        num_scalar_prefetch=3,
        grid=(n_rows // tb,),
        in_specs=[xspec, xspec, wspec(d, de), wspec(d, de), wspec(de, d)],
        out_specs=[yspec, yspec],
        scratch_shapes=[pltpu.VMEM((d, de), BF16), pltpu.VMEM((d, de), BF16), pltpu.VMEM((de, d), BF16)],
    )
    rows = jax.ShapeDtypeStruct((n_rows, PACK_WIDTH), U32)
    return pl.pallas_call(
        _expert_kernel,
        grid_spec=grid_spec,
        out_shape=[rows, rows],
        compiler_params=_params(("arbitrary",)),
        name="experts",
    )(block_expert, block_valid, n_used, xa, xb, w1, w3, w2)


def _final_kernel(ga0, ga1, gb0, gb1, gt_ref, mod_ref, x_ref, fg_ref, o_ref):
    o_ref[...] = _rms(_moe_residual(x_ref, ga0, ga1, gb0, gb1, gt_ref, mod_ref[0][5:6, :]), fg_ref[...])


def _final(x2, ga, gb, gates, mod_l, final_g, seq):
    n, d = x2.shape
    tm = ROW_TILE
    per_batch = seq // tm
    row = lambda w: pl.BlockSpec((tm, w), lambda i: (i, 0))
    second = pl.BlockSpec((tm, PACK_WIDTH), lambda i: (i + n // tm, 0))
    return pl.pallas_call(
        _final_kernel,
        grid=(n // tm,),
        in_specs=[row(PACK_WIDTH), second, row(PACK_WIDTH), second, row(LANES),
                  pl.BlockSpec((1, 6, d), lambda i: (i // per_batch, 0, 0)), row(d), pl.BlockSpec((1, d), lambda i: (0, 0))],
        out_specs=row(d),
        out_shape=jax.ShapeDtypeStruct((n, d), F32),
        compiler_params=_params(("parallel",)),
        name="final_norm",
    )(ga, ga, gb, gb, gates, mod_l, x2, final_g.reshape(1, d))


def kernel(x, c, w_ada, b_ada, norm1_g, w_in, attn_norm_g, hgrn_lb_logits, hgrn_norm_g, w_out, norm2_g, router_group_w, router_group_b, router_expert_w, router_expert_b, moe_w1, moe_w3, moe_w2, final_g):
    bsz, seq, d = x.shape
    depth = w_ada.shape[0]
    n = bsz * seq
    attn_dim = attn_norm_g.shape[1]
    n_heads = attn_dim // ATTN_HEAD_DIM

    lb_w = jax.nn.softmax(hgrn_lb_logits.astype(F32), axis=0)
    lower_bounds = jnp.cumsum(lb_w, axis=0) - lb_w[0:1]
    mod = _adaln(c, w_ada, b_ada)
    bias = _attn_bias(n_heads)
    hconsts = _hgrn_consts()
    n_rows = n * TOP_K + N_EXPERTS * MOE_TILE
    pad = LANES - N_GROUPS - N_EXPERTS

    x2 = x.reshape(n, d)
    moe = None
    for layer in range(depth):
        x2, (aq, ak, av, rq, rf, ri, rg) = _in_proj(x2, mod[layer], norm1_g[layer], w_in, layer, seq, moe)
        attn = _attention(aq, ak, av, bias, bsz, seq)
        rec = _hgrn(rq, rf, ri, rg, lower_bounds[layer], hgrn_norm_g[layer], hconsts, bsz, seq)
        wr = jnp.concatenate([router_group_w[layer], router_expert_w[layer], jnp.zeros((d, pad), F32)], axis=1)
        br = jnp.concatenate([router_group_b[layer], router_expert_b[layer], jnp.zeros((pad,), F32)]).reshape(1, LANES)
        x2, ha, hb, eids, gates = _out_proj(attn, rec, x2, mod[layer], attn_norm_g[layer], norm2_g[layer],
                                            w_out, layer, wr.astype(BF16), br, seq)
        pos, block_expert, block_valid, n_used = _plan(eids[:, :TOP_K].T, n_rows)
        ya, yb = _experts(_sc_scatter_rows(ha, pos, n_rows), _sc_scatter_rows(hb, pos, n_rows),
                          moe_w1, moe_w3, moe_w2, layer, block_expert, block_valid, n_used)
        moe = (_sc_gather_rows(ya, pos), _sc_gather_rows(yb, pos), gates, mod[layer])
    return _final(x2, moe[0], moe[1], moe[2], moe[3], final_g, seq).reshape(bsz, seq, d)
```

```python
import functools

import numpy as np
import jax
import jax.numpy as jnp
from jax import lax
from jax.experimental import pallas as pl
from jax.experimental.pallas import tpu as pltpu
from jax.experimental.pallas import tpu_sc as plsc

F32 = jnp.float32
BF16 = jnp.bfloat16
U32 = jnp.uint32

ATTN_HEAD_DIM = 64
ATTN_BLOCK = 128
DILATIONS = (1, 4, 16)
ALIBI_MAX_EXP = 8.0
HGRN_EXPAND = 128
HGRN_CHUNK = 128
N_GROUPS = 4
EXPERTS_PER_GROUP = 8
N_EXPERTS = N_GROUPS * EXPERTS_PER_GROUP
TOP_K = 2
NORM_EPS = 1e-6
LOG2E = 1.4426950408889634

LANES = 128
ATTN_SUPER = ATTN_BLOCK * max(DILATIONS)
ROW_TILE = 512
ATTN_UNROLL = 16
HGRN_TILE = 2048
HGRN_UNROLL = 16
MOE_TILE = 512
FFN_SPLIT = 2
PLAN_TILE = 2048
PLAN_LANES = 256
PACK_WIDTH = 256
SC_WINDOW = 128
VMEM_LIMIT = 56 << 20


def _params(semantics):
    return pltpu.CompilerParams(dimension_semantics=semantics, vmem_limit_bytes=VMEM_LIMIT)


def _rms(x, g):
    return x * lax.rsqrt(jnp.mean(x * x, axis=-1, keepdims=True) + NORM_EPS) * g


def _dot_nt(a, b):
    return lax.dot_general(a, b, (((1,), (1,)), ((), ())), preferred_element_type=F32)


def _dot_tn(a, b):
    return lax.dot_general(a, b, (((0,), (0,)), ((), ())), preferred_element_type=F32)


def _pack_pair(hi, lo):
    hb = lax.bitcast_convert_type(hi.astype(BF16).astype(F32), U32)
    lb = lax.bitcast_convert_type(lo.astype(BF16).astype(F32), U32)
    return hb | (lb >> 16)


def _unpack_pair(u):
    hi = lax.bitcast_convert_type(u & jnp.uint32(0xFFFF0000), F32)
    lo = lax.bitcast_convert_type(u << 16, F32)
    return hi, lo


def _pack_rows(y):
    w = PACK_WIDTH
    return _pack_pair(y[:, 0:w], y[:, 2 * w:3 * w]), _pack_pair(y[:, w:2 * w], y[:, 3 * w:4 * w])


def _unpack_rows(ua, ub):
    ha, la = _unpack_pair(ua)
    hb, lb = _unpack_pair(ub)
    return jnp.concatenate([ha, hb, la, lb], axis=1)


def _moe_residual(x_ref, ga0, ga1, gb0, gb1, gt_ref, gate_row):
    gt = gt_ref[...]
    y = gt[:, 0:1] * _unpack_rows(ga0[...], gb0[...]) + gt[:, 1:2] * _unpack_rows(ga1[...], gb1[...])
    return x_ref[...] + gate_row * y


def _ada_kernel(c_ref, w_ref, b_ref, o_ref):
    c = c_ref[...]
    ca = c * jax.nn.sigmoid(c)
    o_ref[0] = jnp.dot(ca, w_ref[0], precision=lax.Precision.HIGHEST, preferred_element_type=F32) + b_ref[0]


def _adaln(c, w_ada, b_ada):
    depth, d, n6 = w_ada.shape
    bsz = c.shape[0]
    rows = 8
    cp = jnp.zeros((rows, d), F32).at[:bsz].set(c)
    tn = 1536
    out = pl.pallas_call(
        _ada_kernel,
        grid=(depth, n6 // tn),
        in_specs=[
            pl.BlockSpec((rows, d), lambda l, j: (0, 0)),
            pl.BlockSpec((1, d, tn), lambda l, j: (l, 0, j)),
            pl.BlockSpec((1, 1, tn), lambda l, j: (l, 0, j)),
        ],
        out_specs=pl.BlockSpec((1, rows, tn), lambda l, j: (l, 0, j)),
        out_shape=jax.ShapeDtypeStruct((depth, rows, n6), F32),
        compiler_params=_params(("parallel", "parallel")),
        name="adaln",
    )(cp, w_ada, b_ada.reshape(depth, 1, n6))
    return out[:, :bsz].reshape(depth, bsz, 6, d)


def _in_kernel(*refs, n_moe):
    moe, (x_ref, mod_ref, g_ref, w_ref), rest = refs[:n_moe], refs[n_moe:n_moe + 4], refs[n_moe + 4:]
    hb = rest[-1]
    if n_moe:
        ga0, ga1, gb0, gb1, gt_ref, pmod_ref = moe
        x = _moe_residual(x_ref, ga0, ga1, gb0, gb1, gt_ref, pmod_ref[0][5:6, :])
        rest[0][...] = x
        outs = rest[1:-1]
    else:
        x = x_ref[...]
        outs = rest[:-1]
    mod = mod_ref[0]
    h = _rms(x, g_ref[...]) * (1.0 + mod[1:2, :]) + mod[0:1, :]
    hb[...] = h.astype(BF16)
    width = outs[0].shape[1]
    for i, o in enumerate(outs):
        w = w_ref[0, :, i * width:(i + 1) * width].astype(BF16)
        o[...] = jnp.dot(hb[...], w, preferred_element_type=F32).astype(o.dtype)


def _in_proj(x2, mod_l, g, w_in, layer, seq, moe=None):
    n, d = x2.shape
    n_out = w_in.shape[2]
    width = 512
    tm = ROW_TILE
    per_batch = seq // tm
    row = lambda w: pl.BlockSpec((tm, w), lambda i: (i, 0))
    mod_spec = pl.BlockSpec((1, 6, d), lambda i: (i // per_batch, 0, 0))
    in_specs, args, out_specs, out_shape = [], [], [], []
    if moe is not None:
        ga, gb, gates, mod_prev = moe
        second = pl.BlockSpec((tm, PACK_WIDTH), lambda i: (i + n // tm, 0))
        in_specs += [row(PACK_WIDTH), second, row(PACK_WIDTH), second, row(LANES), mod_spec]
        args += [ga, ga, gb, gb, gates, mod_prev]
        out_specs.append(row(d))
        out_shape.append(jax.ShapeDtypeStruct((n, d), F32))
    in_specs += [row(d), mod_spec, pl.BlockSpec((1, d), lambda i: (0, 0)),
                 pl.BlockSpec((1, d, n_out), lambda i: (layer, 0, 0), pipeline_mode=pl.Buffered(1))]
    args += [x2, mod_l, g.reshape(1, d), w_in]
    out_dtypes = (F32, F32, F32, BF16, F32, BF16, F32)
    out_specs += [row(width)] * len(out_dtypes)
    out_shape += [jax.ShapeDtypeStruct((n, width), dt) for dt in out_dtypes]
    outs = pl.pallas_call(
        functools.partial(_in_kernel, n_moe=0 if moe is None else 6),
        grid=(n // tm,),
        in_specs=in_specs,
        out_specs=out_specs,
        out_shape=out_shape,
        scratch_shapes=[pltpu.VMEM((tm, d), BF16)],
        compiler_params=_params(("parallel",)),
        name="in_proj",
    )(*args)
    return (x2, outs) if moe is None else (outs[0], outs[1:])


def _attn_bias(n_heads):
    qi = np.arange(ATTN_BLOCK)[:, None]
    kj = np.arange(2 * ATTN_BLOCK)[None, :]
    dist = ATTN_BLOCK + qi - kj
    valid = (dist >= 0) & (dist <= ATTN_BLOCK)
    slopes = np.exp2(-ALIBI_MAX_EXP * np.arange(1, n_heads + 1, dtype=np.float32) / n_heads)
    out = np.empty((2, len(DILATIONS), n_heads, ATTN_BLOCK, 2 * ATTN_BLOCK), np.float32)
    for p, dil in enumerate(DILATIONS):
        b = -slopes[:, None, None] * (dist * dil).astype(np.float32)[None] * np.float32(LOG2E)
        out[0, p] = np.where(valid[None], b, -np.inf)
        out[1, p] = np.where((valid & (kj >= ATTN_BLOCK))[None], b, -np.inf)
    return jnp.asarray(out)


def _attn_kernel(q_ref, k_ref, v_ref, bias_ref, o_ref, kcar, vcar, pbuf, vbuf, oscr, mscr, dscr):
    n = pl.program_id(2)
    sb = q_ref.shape[0]
    is_lo = lax.broadcasted_iota(jnp.int32, (ATTN_BLOCK, LANES), 1) < ATTN_HEAD_DIM
    scale = ATTN_HEAD_DIM ** -0.5 * LOG2E

    @pl.when(n == 0)
    def _():
        kcar[...] = jnp.zeros_like(kcar)
        vcar[...] = jnp.zeros_like(vcar)

    pbuf[...] = jnp.zeros_like(pbuf)
    vbuf[...] = jnp.zeros_like(vbuf)
    lo2 = lax.broadcasted_iota(jnp.int32, (2 * ATTN_BLOCK, LANES), 1) < ATTN_HEAD_DIM
    zero = jnp.zeros((2 * ATTN_BLOCK, LANES), BF16)
    den_lo = lo2.astype(BF16)
    den_hi = jnp.logical_not(lo2).astype(BF16)

    for p, dil in enumerate(DILATIONS):
        nb = (sb // ATTN_BLOCK) // dil

        def rows_of(it, dil=dil, nb=nb):
            start = (it % nb) * (ATTN_BLOCK * dil) + it // nb
            return pl.ds(start, ATTN_BLOCK, stride=dil) if dil > 1 else pl.ds(pl.multiple_of(start, ATTN_BLOCK), ATTN_BLOCK)

        def pv(it, probs, vaug, p=p, rows_of=rows_of):
            rows = rows_of(it)
            of = jnp.dot(probs, vaug, preferred_element_type=F32)
            oscr[p, rows, :] = of[:, :LANES]
            dscr[p, rows, :] = of[:, LANES:]

        def qk(it, p=p, nb=nb, rows_of=rows_of):
            r = it // nb
            rows = rows_of(it)
            crow = pl.ds(pl.multiple_of(r * ATTN_BLOCK, ATTN_BLOCK), ATTN_BLOCK)
            qb = q_ref[rows, :] * scale
            kc = k_ref[rows, :].astype(BF16)
            vc = v_ref[rows, :].astype(BF16)
            kcat = jnp.concatenate([kcar[p, crow, :], kc], axis=0)
            vcat = jnp.concatenate([vcar[p, crow, :], vc], axis=0)
            vaug = jnp.concatenate([jnp.concatenate([jnp.where(lo2, vcat, zero), den_lo], axis=1),
                                    jnp.concatenate([jnp.where(lo2, zero, vcat), den_hi], axis=1)], axis=0)
            kcar[p, crow, :] = kc
            vcar[p, crow, :] = vc
            first = jnp.logical_and(n == 0, it % nb == 0).astype(jnp.int32)
            probs, acc_m = [], None
            for hh in range(2):
                sel = is_lo if hh == 0 else jnp.logical_not(is_lo)
                qm = jnp.where(sel, qb, 0.0).astype(BF16)
                s = _dot_nt(qm, kcat) + bias_ref[first, p, hh]
                m = jnp.max(s, axis=-1, keepdims=True)
                probs.append(jnp.exp2(s - m).astype(BF16))
                mb = jnp.broadcast_to(m, (ATTN_BLOCK, LANES))
                acc_m = mb if hh == 0 else jnp.where(sel, mb, acc_m)
            mscr[p, rows, :] = acc_m
            return jnp.concatenate(probs, axis=1), vaug

        def body(g, carry, pv=pv, qk=qk):
            it0 = g * ATTN_UNROLL
            pend = (pbuf[...], vbuf[...])
            for u in range(ATTN_UNROLL):
                cur = qk(it0 + u)
                pv(jnp.maximum(it0 + u - 1, 0), *pend)
                pend = cur
            pbuf[...] = pend[0]
            vbuf[...] = pend[1]
            return carry

        n_it = sb // ATTN_BLOCK
        lax.fori_loop(0, n_it // ATTN_UNROLL, body, 0)
        pv(n_it - 1, pbuf[...], vbuf[...])

    ct = 256

    def combine(t, carry):
        rs = pl.ds(pl.multiple_of(t * ct, ct), ct)
        m0, m1, m2 = mscr[0, rs, :], mscr[1, rs, :], mscr[2, rs, :]
        mx = jnp.maximum(jnp.maximum(m0, m1), m2)
        w0, w1, w2 = jnp.exp2(m0 - mx), jnp.exp2(m1 - mx), jnp.exp2(m2 - mx)
        num = w0 * oscr[0, rs, :] + w1 * oscr[1, rs, :] + w2 * oscr[2, rs, :]
        den = w0 * dscr[0, rs, :] + w1 * dscr[1, rs, :] + w2 * dscr[2, rs, :]
        o_ref[rs, :] = num / den
        return carry

    lax.fori_loop(0, sb // ct, combine, 0)


def _attention(aq, ak, av, bias, bsz, seq):
    n, width = aq.shape
    sb = ATTN_SUPER
    nsb = seq // sb
    pairs = width // LANES
    npat = len(DILATIONS)
    blk = pl.BlockSpec((sb, LANES), lambda b, h, t: (b * nsb + t, h))
    return pl.pallas_call(
        _attn_kernel,
        grid=(bsz, pairs, nsb),
        in_specs=[blk, blk, blk,
                  pl.BlockSpec((2, npat, 2, ATTN_BLOCK, 2 * ATTN_BLOCK), lambda b, h, t: (0, 0, h, 0, 0))],
        out_specs=blk,
        out_shape=jax.ShapeDtypeStruct((n, width), F32),
        scratch_shapes=[
            pltpu.VMEM((npat, max(DILATIONS) * ATTN_BLOCK, LANES), BF16),
            pltpu.VMEM((npat, max(DILATIONS) * ATTN_BLOCK, LANES), BF16),
            pltpu.VMEM((ATTN_BLOCK, 4 * ATTN_BLOCK), BF16),
            pltpu.VMEM((4 * ATTN_BLOCK, 2 * LANES), BF16),
            pltpu.VMEM((npat, sb, LANES), F32),
            pltpu.VMEM((npat, sb, LANES), F32),
            pltpu.VMEM((npat, sb, LANES), F32),
        ],
        compiler_params=_params(("parallel", "parallel", "arbitrary")),
        name="dilated_attn",
    )(aq, ak, av, bias)


def _hgrn_consts():
    c = HGRN_CHUNK
    t = np.arange(c)[:, None]
    u = np.arange(c)[None, :]
    low, masks = [], []
    m = c // 2
    while m >= 1:
        mid = (t // (2 * m)) * (2 * m) + m
        is_q = t >= mid
        if m in (4, 2):
            low.append((is_q & (u >= mid) & (u <= t)) | (~is_q & (u > t) & (u < mid)))
        masks.append(((t // (2 * m)) == (u // (2 * m))) & (t % (2 * m) >= m) & (u % (2 * m) < m))
        m //= 2
    masks.append(t == u)
    f = lambda a: np.asarray(a, np.float32)
    return (jnp.asarray(f(u <= t), BF16), jnp.asarray(f(np.concatenate(low, axis=0)), BF16),
            jnp.asarray(f(np.broadcast_to(t % 2 == 1, (c, c)))), jnp.asarray(f(np.stack(masks))))


def _hgrn_kernel(q_ref, z_ref, v_ref, g_ref, lb_ref, gn_ref, tri_ref, low_ref, odd_ref, msk_ref, o_ref, st):
    c = HGRN_CHUNK

    @pl.when(pl.program_id(2) == 0)
    def _():
        st[...] = jnp.zeros_like(st)

    lb = jnp.maximum(lb_ref[...], 0.0)
    log_lb = jnp.log(lb)
    log_1m = jnp.log1p(-lb)
    one_m = 1.0 - lb
    gn = gn_ref[...]
    n_levels = msk_ref.shape[0] - 1
    coarse = [c >> (i + 1) for i in range(n_levels) if (c >> (i + 1)) >= 8]

    def chunk(ci, carry):
        rs = pl.ds(pl.multiple_of(ci * c, c), c)
        z = z_ref[rs, :]
        qb = q_ref[rs, :]
        vb = v_ref[rs, :]
        ez = jnp.exp(-jnp.abs(z))
        ls = jnp.minimum(z, 0.0) - jnp.log(1.0 + ez)
        b2 = log_1m + ls
        lf = jnp.maximum(log_lb, b2) + jnp.log(1.0 + jnp.exp(-jnp.abs(log_lb - b2)))
        rz = 1.0 / (1.0 + ez)
        kb = (one_m * jnp.where(z > 0.0, ez * rz, rz)).astype(BF16)
        l2 = lf * LOG2E
        hi = l2.astype(BF16)
        hl = jnp.concatenate([hi, (l2 - hi.astype(F32)).astype(BF16)], axis=1)
        bb = jnp.dot(tri_ref[...], hl, preferred_element_type=F32)
        b = bb[:, :HGRN_EXPAND] + bb[:, HGRN_EXPAND:]
        dl = jnp.dot(low_ref[...], hl, preferred_element_type=F32)
        dl = dl[:, :HGRN_EXPAND] + dl[:, HGRN_EXPAND:]

        def level(d):
            e = jnp.exp2(d).astype(BF16)
            return _dot_nt(qb * e, kb * e)

        a = msk_ref[n_levels] * _dot_nt(qb, kb)
        for i in range(n_levels):
            m = c >> (i + 1)
            if m >= 8:
                parts = []
                for r0 in range(0, c, 2 * m):
                    mid = b[r0 + m - 1:r0 + m, :]
                    parts += [mid - b[r0:r0 + m, :], b[r0 + m:r0 + 2 * m, :] - mid]
                d = jnp.concatenate(parts, axis=0)
            elif m > 1:
                d = dl[(i - len(coarse)) * c:(i - len(coarse) + 1) * c]
            else:
                d = l2 * odd_ref[...]
            a = a + msk_ref[i] * level(d)
        b_last = b[c - 1:c, :]
        s_t = st[...]
        o = _dot_nt(qb * jnp.exp2(b).astype(BF16), s_t.astype(BF16)) + jnp.dot(a.astype(BF16), vb, preferred_element_type=F32)
        st[...] = jnp.exp2(b_last) * s_t + _dot_tn(vb, kb * jnp.exp2(b_last - b).astype(BF16))
        g = g_ref[rs, :]
        o_ref[rs, :] = (_rms(o, gn) * (g * jax.nn.sigmoid(g))).astype(o_ref.dtype)
        return carry

    lax.fori_loop(0, q_ref.shape[0] // c, chunk, 0, unroll=HGRN_UNROLL)


def _hgrn(rq, rf, ri, rg, lb, gn, consts, bsz, seq):
    n, width = rq.shape
    heads = width // HGRN_EXPAND
    ts = HGRN_TILE
    nt = seq // ts
    blk = pl.BlockSpec((ts, HGRN_EXPAND), lambda b, h, t: (b * nt + t, h))
    vec = pl.BlockSpec((1, HGRN_EXPAND), lambda b, h, t: (0, h))
    const = lambda a: pl.BlockSpec(a.shape, lambda b, h, t: (0,) * a.ndim)
    return pl.pallas_call(
        _hgrn_kernel,
        grid=(bsz, heads, nt),
        in_specs=[blk, blk, blk, blk, vec, vec] + [const(a) for a in consts],
        out_specs=blk,
        out_shape=jax.ShapeDtypeStruct((n, width), BF16),
        scratch_shapes=[pltpu.VMEM((HGRN_EXPAND, HGRN_EXPAND), F32)],
        compiler_params=_params(("parallel", "parallel", "arbitrary")),
        name="hgrn2",
    )(rq, rf, ri, rg, lb.reshape(1, width), gn.reshape(1, width), *consts)


def _route_math(lg):
    row = lax.broadcasted_iota(jnp.int32, (EXPERTS_PER_GROUP, lg.shape[1]), 0).astype(F32)
    none = float(EXPERTS_PER_GROUP)

    def first_argmax(vals):
        mx = jnp.max(vals, axis=0, keepdims=True)
        return mx, jnp.min(jnp.where(vals == mx, row, none), axis=0, keepdims=True)

    g = lg[0:EXPERTS_PER_GROUP, :]
    is_group = row < N_GROUPS
    gmax, gidx = first_argmax(jnp.where(is_group, g, -jnp.inf))
    gprob = 1.0 / jnp.sum(jnp.where(is_group, jnp.exp(g - gmax), 0.0), axis=0, keepdims=True)
    el = lg[EXPERTS_PER_GROUP:2 * EXPERTS_PER_GROUP, :]
    for k in range(1, N_GROUPS):
        el = jnp.where(gidx == k, lg[(k + 1) * EXPERTS_PER_GROUP:(k + 2) * EXPERTS_PER_GROUP, :], el)
    t1, i1 = first_argmax(el)
    t2, i2 = first_argmax(jnp.where(row == i1, -jnp.inf, el))
    ex = jnp.exp(t2 - t1)
    base = gidx * EXPERTS_PER_GROUP
    eids = jnp.concatenate([base + i1, base + i2], axis=0).astype(jnp.int32)
    gates = jnp.concatenate([gprob / (1.0 + ex), gprob * ex / (1.0 + ex)], axis=0)
    return eids, gates


def _out_kernel(a_ref, r_ref, x_ref, mod_ref, ag_ref, g2_ref, wo_ref, wr_ref, br_ref, xo_ref, ha_ref, hb_ref, e_ref, gt_ref):
    mod = mod_ref[0]
    mix = jnp.concatenate([_rms(a_ref[...], ag_ref[...]).astype(BF16), r_ref[...]], axis=1)
    xn = x_ref[...] + mod[2:3, :] * jnp.dot(mix, wo_ref[0].astype(BF16), preferred_element_type=F32)
    xo_ref[...] = xn
    h2 = _rms(xn, g2_ref[...]) * (1.0 + mod[4:5, :]) + mod[3:4, :]
    ha_ref[...], hb_ref[...] = _pack_rows(h2)
    eids, gates = _route_math(_dot_nt(wr_ref[...], h2.astype(BF16)) + br_ref[...])
    e_ref[...] = eids
    pad = jnp.zeros((LANES - TOP_K, LANES), F32)
    for k in range(h2.shape[0] // LANES):
        tile = jnp.concatenate([gates[:, k * LANES:(k + 1) * LANES], pad], axis=0)
        gt_ref[k * LANES:(k + 1) * LANES, :] = tile.T


def _out_proj(attn, rec, x2, mod_l, ag, g2, w_out, layer, wr_bf, br, seq):
    n, d = x2.shape
    half = attn.shape[1]
    tm = ROW_TILE
    per_batch = seq // tm
    row = lambda w: pl.BlockSpec((tm, w), lambda i: (i, 0))
    full = lambda a: pl.BlockSpec(a.shape, lambda i: (0,) * a.ndim, pipeline_mode=pl.Buffered(1))
    ag2, g22 = ag.reshape(1, half), g2.reshape(1, d)
    return pl.pallas_call(
        _out_kernel,
        grid=(n // tm,),
        in_specs=[row(half), row(half), row(d), pl.BlockSpec((1, 6, d), lambda i: (i // per_batch, 0, 0)),
                  full(ag2), full(g22), pl.BlockSpec((1, d, d), lambda i: (layer, 0, 0), pipeline_mode=pl.Buffered(1)),
                  full(wr_bf), full(br)],
        out_specs=[row(d), row(PACK_WIDTH), row(PACK_WIDTH), pl.BlockSpec((TOP_K, tm), lambda i: (0, i)), row(LANES)],
        out_shape=[jax.ShapeDtypeStruct((n, d), F32), jax.ShapeDtypeStruct((n, PACK_WIDTH), U32),
                   jax.ShapeDtypeStruct((n, PACK_WIDTH), U32), jax.ShapeDtypeStruct((TOP_K, n), jnp.int32),
                   jax.ShapeDtypeStruct((n, LANES), F32)],
        compiler_params=_params(("parallel",)),
        name="out_proj",
    )(attn, rec, x2, mod_l, ag2, g22, w_out, wr_bf, br)


def _plan_kernel(e_ref, ut_ref, pos_ref, be_ref, bv_ref, nu_ref, cnt, off, run):
    phase = pl.program_id(0)
    i = pl.program_id(1)
    t = e_ref.shape[1]
    eio = lax.broadcasted_iota(jnp.int32, (N_EXPERTS, t), 0)
    oh1 = e_ref[0:1, :] == eio
    oh2 = e_ref[1:2, :] == eio
    oh = jnp.logical_or(oh1, oh2).astype(F32)
    tot = jnp.sum(oh, axis=1, keepdims=True)

    @pl.when(jnp.logical_and(phase == 0, i == 0))
    def _():
        cnt[...] = jnp.zeros_like(cnt)

    @pl.when(phase == 0)
    def _():
        cnt[...] += tot

    @pl.when(jnp.logical_and(phase == 1, i == 0))
    def _():
        counts = cnt[...]
        nb = jnp.floor((counts + (MOE_TILE - 1)) * (1.0 / MOE_TILE))
        er = lax.broadcasted_iota(jnp.int32, (N_EXPERTS, N_EXPERTS), 0)
        ec = lax.broadcasted_iota(jnp.int32, (N_EXPERTS, N_EXPERTS), 1)
        cum_excl = jnp.dot((ec < er).astype(BF16), nb.astype(BF16), preferred_element_type=F32)
        off[...] = cum_excl * MOE_TILE
        run[...] = jnp.zeros_like(run)
        ce, ci, cn = cum_excl[:, 0:1], (cum_excl + nb)[:, 0:1], counts[:, 0:1]
        b = lax.broadcasted_iota(jnp.int32, (N_EXPERTS, be_ref.shape[1]), 1).astype(F32)
        member = jnp.logical_and(b >= ce, b < ci)
        be = jnp.minimum(jnp.sum((b >= ci).astype(F32), axis=0, keepdims=True), N_EXPERTS - 1.0)
        bv = jnp.sum(jnp.where(member, jnp.minimum(cn - (b - ce) * MOE_TILE, float(MOE_TILE)), 0.0), axis=0, keepdims=True)
        be_ref[...] = be.astype(jnp.int32)
        bv_ref[...] = bv.astype(jnp.int32)
        nu_ref[...] = jnp.broadcast_to(ci[N_EXPERTS - 1:N_EXPERTS, :], nu_ref.shape).astype(jnp.int32)

    @pl.when(phase == 1)
    def _():
        c = jnp.dot(oh.astype(BF16), ut_ref[...], preferred_element_type=F32)
        r = off[:, 0:1] + run[:, 0:1] + c - 1.0
        pos_ref[0:1, :] = jnp.sum(jnp.where(oh1, r, 0.0), axis=0, keepdims=True).astype(jnp.int32)
        pos_ref[1:2, :] = jnp.sum(jnp.where(oh2, r, 0.0), axis=0, keepdims=True).astype(jnp.int32)
        run[...] += tot


def _plan(expert_ids_t, n_rows):
    n = expert_ids_t.shape[1]
    t = PLAN_TILE
    n_blocks = n_rows // MOE_TILE
    assert n_blocks <= PLAN_LANES and TOP_K == 2
    ut = jnp.asarray(np.triu(np.ones((t, t), np.float32)), BF16)
    tab = jax.ShapeDtypeStruct((1, PLAN_LANES), jnp.int32)
    const = lambda shape: pl.BlockSpec(shape, lambda p, i: (0, 0))
    pos, be, bv, nu = pl.pallas_call(
        _plan_kernel,
        grid=(2, n // t),
        in_specs=[pl.BlockSpec((TOP_K, t), lambda p, i: (0, i)), const((t, t))],
        out_specs=[pl.BlockSpec((TOP_K, t), lambda p, i: (0, i * p)), const((1, PLAN_LANES)), const((1, PLAN_LANES)),
                   const((1, LANES))],
        out_shape=[jax.ShapeDtypeStruct((TOP_K, n), jnp.int32), tab, tab, jax.ShapeDtypeStruct((1, LANES), jnp.int32)],
        scratch_shapes=[pltpu.VMEM((N_EXPERTS, LANES), F32)] * 3,
        compiler_params=_params(("arbitrary", "arbitrary")),
        name="plan",
    )(expert_ids_t, ut)
    return pos, be.reshape(-1), bv.reshape(-1), nu.reshape(-1)[:1]


def _sc_mesh():
    return plsc.VectorSubcoreMesh(core_axis_name="c", subcore_axis_name="s")


def _sc_scatter_rows(x, pos, n_rows):
    n, d = x.shape
    w = SC_WINDOW

    @functools.partial(pl.kernel, out_type=jax.ShapeDtypeStruct((n_rows, d), x.dtype), mesh=_sc_mesh(), scratch_types=[])
    def scatter(x_hbm, p_hbm, o_hbm):
        def body(x_vmem, i_vmem):
            for k in range(TOP_K):
                pltpu.sync_copy(x_vmem, o_hbm.at[i_vmem.at[k]])

        pltpu.emit_pipeline(
            body,
            grid=(n // w,),
            in_specs=[pl.BlockSpec((w, d), lambda i: (i, 0)), pl.BlockSpec((TOP_K, w), lambda i: (0, i))],
            out_specs=[],
            core_axis_name=("c", "s"),
            dimension_semantics=(pltpu.PARALLEL,),
        )(x_hbm, p_hbm)

    return scatter(x, pos)


def _sc_gather_rows(y, pos):
    m = pos.shape[0] * pos.shape[1]
    d = y.shape[1]
    w = SC_WINDOW

    @functools.partial(pl.kernel, out_type=jax.ShapeDtypeStruct((m, d), y.dtype), mesh=_sc_mesh(), scratch_types=[])
    def gather(y_hbm, p_hbm, o_hbm):
        def body(i_vmem, o_vmem):
            pltpu.sync_copy(y_hbm.at[i_vmem.at[0]], o_vmem)

        pltpu.emit_pipeline(
            body,
            grid=(m // w,),
            in_specs=[pl.BlockSpec((1, w), lambda i: (0, i))],
            out_specs=[pl.BlockSpec((w, d), lambda i: (i, 0))],
            core_axis_name=("c", "s"),
            dimension_semantics=(pltpu.PARALLEL,),
        )(p_hbm, o_hbm)

    return gather(y, pos.reshape(1, m))


def _expert_kernel(be_ref, bv_ref, nu_ref, xa_ref, xb_ref, w1_ref, w3_ref, w2_ref, ya_ref, yb_ref, w1b, w3b, w2b):
    i = pl.program_id(0)
    tb = xa_ref.shape[0]

    @pl.when(i < nu_ref[0])
    def _():
        e = be_ref[i]
        e_prev = be_ref[jnp.maximum(i - 1, 0)]

        @pl.when(jnp.logical_or(i == 0, e != e_prev))
        def _():
            w1b[...] = w1_ref[0, 0].astype(BF16)
            w3b[...] = w3_ref[0, 0].astype(BF16)
            w2b[...] = w2_ref[0, 0].astype(BF16)

        live = lax.broadcasted_iota(jnp.int32, (tb, 1), 0) < bv_ref[i]
        x = jnp.where(live, _unpack_rows(xa_ref[...], xb_ref[...]), 0.0).astype(BF16)
        y = None
        hc = w1b.shape[1] // FFN_SPLIT
        for c in range(FFN_SPLIT):
            cs = slice(c * hc, (c + 1) * hc)
            a = jnp.dot(x, w1b[:, cs], preferred_element_type=F32)
            b = jnp.dot(x, w3b[:, cs], preferred_element_type=F32)
            hm = (a * jax.nn.sigmoid(a) * b).astype(BF16)
            part = jnp.dot(hm, w2b[cs, :], preferred_element_type=F32)
            y = part if y is None else y + part
        ya_ref[...], yb_ref[...] = _pack_rows(y)

    @pl.when(i >= nu_ref[0])
    def _():
        ya_ref[...] = jnp.zeros_like(ya_ref)
        yb_ref[...] = jnp.zeros_like(yb_ref)


def _experts(xa, xb, w1, w3, w2, layer, block_expert, block_valid, n_used):
    n_rows = xa.shape[0]
    d, de = w1.shape[-2:]
    tb = MOE_TILE
    wspec = lambda r, c: pl.BlockSpec((1, 1, r, c), lambda i, be, bv, nu: (layer, be[i], 0, 0))
    xspec = pl.BlockSpec((tb, PACK_WIDTH), lambda i, be, bv, nu: (jnp.minimum(i, nu[0] - 1), 0))
    yspec = pl.BlockSpec((tb, PACK_WIDTH), lambda i, be, bv, nu: (i, 0))
    grid_spec = pltpu.PrefetchScalarGridSpec(
        num_scalar_prefetch=3,
        grid=(n_rows // tb,),
        in_specs=[xspec, xspec, wspec(d, de), wspec(d, de), wspec(de, d)],
        out_specs=[yspec, yspec],
        scratch_shapes=[pltpu.VMEM((d, de), BF16), pltpu.VMEM((d, de), BF16), pltpu.VMEM((de, d), BF16)],
    )
    rows = jax.ShapeDtypeStruct((n_rows, PACK_WIDTH), U32)
    return pl.pallas_call(
        _expert_kernel,
        grid_spec=grid_spec,
        out_shape=[rows, rows],
        compiler_params=_params(("arbitrary",)),
        name="experts",
    )(block_expert, block_valid, n_used, xa, xb, w1, w3, w2)


def _final_kernel(ga0, ga1, gb0, gb1, gt_ref, mod_ref, x_ref, fg_ref, o_ref):
    o_ref[...] = _rms(_moe_residual(x_ref, ga0, ga1, gb0, gb1, gt_ref, mod_ref[0][5:6, :]), fg_ref[...])


def _final(x2, ga, gb, gates, mod_l, final_g, seq):
    n, d = x2.shape
    tm = ROW_TILE
    per_batch = seq // tm
    row = lambda w: pl.BlockSpec((tm, w), lambda i: (i, 0))
    second = pl.BlockSpec((tm, PACK_WIDTH), lambda i: (i + n // tm, 0))
    return pl.pallas_call(
        _final_kernel,
        grid=(n // tm,),
        in_specs=[row(PACK_WIDTH), second, row(PACK_WIDTH), second, row(LANES),
                  pl.BlockSpec((1, 6, d), lambda i: (i // per_batch, 0, 0)), row(d), pl.BlockSpec((1, d), lambda i: (0, 0))],
        out_specs=row(d),
        out_shape=jax.ShapeDtypeStruct((n, d), F32),
        compiler_params=_params(("parallel",)),
        name="final_norm",
    )(ga, ga, gb, gb, gates, mod_l, x2, final_g.reshape(1, d))


def kernel(x, c, w_ada, b_ada, norm1_g, w_in, attn_norm_g, hgrn_lb_logits, hgrn_norm_g, w_out, norm2_g, router_group_w, router_group_b, router_expert_w, router_expert_b, moe_w1, moe_w3, moe_w2, final_g):
    bsz, seq, d = x.shape
    depth = w_ada.shape[0]
    n = bsz * seq
    attn_dim = attn_norm_g.shape[1]
    n_heads = attn_dim // ATTN_HEAD_DIM

    lb_w = jax.nn.softmax(hgrn_lb_logits.astype(F32), axis=0)
    lower_bounds = jnp.cumsum(lb_w, axis=0) - lb_w[0:1]
    mod = _adaln(c, w_ada, b_ada)
    bias = _attn_bias(n_heads)
    hconsts = _hgrn_consts()
    n_rows = n * TOP_K + N_EXPERTS * MOE_TILE

    x2 = x.reshape(n, d)
    moe = None
    for layer in range(depth):
        x2, (aq, ak, av, rq, rf, ri, rg) = _in_proj(x2, mod[layer], norm1_g[layer], w_in, layer, seq, moe)
        attn = _attention(aq, ak, av, bias, bsz, seq)
        rec = _hgrn(rq, rf, ri, rg, lower_bounds[layer], hgrn_norm_g[layer], hconsts, bsz, seq)
        gpad = EXPERTS_PER_GROUP - N_GROUPS
        wr = jnp.concatenate([router_group_w[layer].T, jnp.zeros((gpad, d), F32), router_expert_w[layer].T], axis=0)
        br = jnp.concatenate([router_group_b[layer], jnp.zeros((gpad,), F32), router_expert_b[layer]]).reshape(-1, 1)
        x2, ha, hb, eids, gates = _out_proj(attn, rec, x2, mod[layer], attn_norm_g[layer], norm2_g[layer],
                                            w_out, layer, wr.astype(BF16), br, seq)
        pos, block_expert, block_valid, n_used = _plan(eids, n_rows)
        ya, yb = _experts(_sc_scatter_rows(ha, pos, n_rows), _sc_scatter_rows(hb, pos, n_rows),
                          moe_w1, moe_w3, moe_w2, layer, block_expert, block_valid, n_used)
        moe = (_sc_gather_rows(ya, pos), _sc_gather_rows(yb, pos), gates, mod[layer])
    return _final(x2, moe[0], moe[1], moe[2], moe[3], final_g, seq).reshape(bsz, seq, d)
```

```python
import functools

import numpy as np
import jax
import jax.numpy as jnp
from jax import lax
from jax.experimental import pallas as pl
from jax.experimental.pallas import tpu as pltpu
from jax.experimental.pallas import tpu_sc as plsc

F32 = jnp.float32
BF16 = jnp.bfloat16
U32 = jnp.uint32

ATTN_HEAD_DIM = 64
ATTN_BLOCK = 128
DILATIONS = (1, 4, 16)
ALIBI_MAX_EXP = 8.0
HGRN_EXPAND = 128
HGRN_CHUNK = 128
N_GROUPS = 4
EXPERTS_PER_GROUP = 8
N_EXPERTS = N_GROUPS * EXPERTS_PER_GROUP
TOP_K = 2
NORM_EPS = 1e-6
LOG2E = 1.4426950408889634

LANES = 128
ATTN_SUPER = ATTN_BLOCK * max(DILATIONS)
ROW_TILE = 512
ATTN_UNROLL = 16
HGRN_TILE = 2048
HGRN_UNROLL = 16
MOE_TILE = 512
FFN_SPLIT = 2
PLAN_TILE = 2048
PLAN_LANES = 256
PACK_WIDTH = 256
SC_WINDOW = 128
VMEM_LIMIT = 56 << 20


def _params(semantics):
    return pltpu.CompilerParams(dimension_semantics=semantics, vmem_limit_bytes=VMEM_LIMIT)


def _rms(x, g):
    return x * lax.rsqrt(jnp.mean(x * x, axis=-1, keepdims=True) + NORM_EPS) * g


def _dot_nt(a, b):
    return lax.dot_general(a, b, (((1,), (1,)), ((), ())), preferred_element_type=F32)


def _dot_tn(a, b):
    return lax.dot_general(a, b, (((0,), (0,)), ((), ())), preferred_element_type=F32)


def _pack_pair(hi, lo):
    hb = lax.bitcast_convert_type(hi.astype(BF16).astype(F32), U32)
    lb = lax.bitcast_convert_type(lo.astype(BF16).astype(F32), U32)
    return hb | (lb >> 16)


def _unpack_pair(u):
    hi = lax.bitcast_convert_type(u & jnp.uint32(0xFFFF0000), F32)
    lo = lax.bitcast_convert_type(u << 16, F32)
    return hi, lo


def _pack_rows(y):
    w = PACK_WIDTH
    return _pack_pair(y[:, 0:w], y[:, 2 * w:3 * w]), _pack_pair(y[:, w:2 * w], y[:, 3 * w:4 * w])


def _unpack_rows(ua, ub):
    ha, la = _unpack_pair(ua)
    hb, lb = _unpack_pair(ub)
    return jnp.concatenate([ha, hb, la, lb], axis=1)


def _moe_residual(x_ref, ga0, ga1, gb0, gb1, gt_ref, gate_row):
    gt = gt_ref[...]
    y = gt[:, 0:1] * _unpack_rows(ga0[...], gb0[...]) + gt[:, 1:2] * _unpack_rows(ga1[...], gb1[...])
    return x_ref[...] + gate_row * y


def _ada_kernel(c_ref, w_ref, b_ref, o_ref):
    c = c_ref[...]
    ca = c * jax.nn.sigmoid(c)
    o_ref[0] = jnp.dot(ca, w_ref[0], precision=lax.Precision.HIGHEST, preferred_element_type=F32) + b_ref[0]


def _adaln(c, w_ada, b_ada):
    depth, d, n6 = w_ada.shape
    bsz = c.shape[0]
    rows = 8
    cp = jnp.zeros((rows, d), F32).at[:bsz].set(c)
    tn = 1536
    out = pl.pallas_call(
        _ada_kernel,
        grid=(depth, n6 // tn),
        in_specs=[
            pl.BlockSpec((rows, d), lambda l, j: (0, 0)),
            pl.BlockSpec((1, d, tn), lambda l, j: (l, 0, j)),
            pl.BlockSpec((1, 1, tn), lambda l, j: (l, 0, j)),
        ],
        out_specs=pl.BlockSpec((1, rows, tn), lambda l, j: (l, 0, j)),
        out_shape=jax.ShapeDtypeStruct((depth, rows, n6), F32),
        compiler_params=_params(("parallel", "parallel")),
        name="adaln",
    )(cp, w_ada, b_ada.reshape(depth, 1, n6))
    return out[:, :bsz].reshape(depth, bsz, 6, d)


def _in_kernel(*refs, n_moe):
    moe, (x_ref, mod_ref, g_ref, w_ref), rest = refs[:n_moe], refs[n_moe:n_moe + 4], refs[n_moe + 4:]
    hb = rest[-1]
    if n_moe:
        ga0, ga1, gb0, gb1, gt_ref, pmod_ref = moe
        x = _moe_residual(x_ref, ga0, ga1, gb0, gb1, gt_ref, pmod_ref[0][5:6, :])
        rest[0][...] = x
        outs = rest[1:-1]
    else:
        x = x_ref[...]
        outs = rest[:-1]
    mod = mod_ref[0]
    h = _rms(x, g_ref[...]) * (1.0 + mod[1:2, :]) + mod[0:1, :]
    hb[...] = h.astype(BF16)
    width = outs[0].shape[1]
    for i, o in enumerate(outs):
        w = w_ref[0, :, i * width:(i + 1) * width].astype(BF16)
        o[...] = jnp.dot(hb[...], w, preferred_element_type=F32).astype(o.dtype)


def _in_proj(x2, mod_l, g, w_in, layer, seq, moe=None):
    n, d = x2.shape
    n_out = w_in.shape[2]
    width = 512
    tm = ROW_TILE
    per_batch = seq // tm
    row = lambda w: pl.BlockSpec((tm, w), lambda i: (i, 0))
    mod_spec = pl.BlockSpec((1, 6, d), lambda i: (i // per_batch, 0, 0))
    in_specs, args, out_specs, out_shape = [], [], [], []
    if moe is not None:
        ga, gb, gates, mod_prev = moe
        second = pl.BlockSpec((tm, PACK_WIDTH), lambda i: (i + n // tm, 0))
        in_specs += [row(PACK_WIDTH), second, row(PACK_WIDTH), second, row(LANES), mod_spec]
        args += [ga, ga, gb, gb, gates, mod_prev]
        out_specs.append(row(d))
        out_shape.append(jax.ShapeDtypeStruct((n, d), F32))
    in_specs += [row(d), mod_spec, pl.BlockSpec((1, d), lambda i: (0, 0)),
                 pl.BlockSpec((1, d, n_out), lambda i: (layer, 0, 0), pipeline_mode=pl.Buffered(1))]
    args += [x2, mod_l, g.reshape(1, d), w_in]
    out_dtypes = (F32, F32, F32, BF16, F32, BF16, F32)
    out_specs += [row(width)] * len(out_dtypes)
    out_shape += [jax.ShapeDtypeStruct((n, width), dt) for dt in out_dtypes]
    outs = pl.pallas_call(
        functools.partial(_in_kernel, n_moe=0 if moe is None else 6),
        grid=(n // tm,),
        in_specs=in_specs,
        out_specs=out_specs,
        out_shape=out_shape,
        scratch_shapes=[pltpu.VMEM((tm, d), BF16)],
        compiler_params=_params(("parallel",)),
        name="in_proj",
    )(*args)
    return (x2, outs) if moe is None else (outs[0], outs[1:])


def _attn_bias(n_heads):
    qi = np.arange(ATTN_BLOCK)[:, None]
    kj = np.arange(2 * ATTN_BLOCK)[None, :]
    dist = ATTN_BLOCK + qi - kj
    valid = (dist >= 0) & (dist <= ATTN_BLOCK)
    slopes = np.exp2(-ALIBI_MAX_EXP * np.arange(1, n_heads + 1, dtype=np.float32) / n_heads)
    out = np.empty((2, len(DILATIONS), n_heads, ATTN_BLOCK, 2 * ATTN_BLOCK), np.float32)
    for p, dil in enumerate(DILATIONS):
        b = -slopes[:, None, None] * (dist * dil).astype(np.float32)[None] * np.float32(LOG2E)
        out[0, p] = np.where(valid[None], b, -np.inf)
        out[1, p] = np.where((valid & (kj >= ATTN_BLOCK))[None], b, -np.inf)
    return jnp.asarray(out)


def _attn_kernel(q_ref, k_ref, v_ref, bias_ref, o_ref, kcar, vcar, pbuf, vbuf, oscr, mscr, dscr):
    n = pl.program_id(2)
    sb = q_ref.shape[0]
    is_lo = lax.broadcasted_iota(jnp.int32, (ATTN_BLOCK, LANES), 1) < ATTN_HEAD_DIM
    scale = ATTN_HEAD_DIM ** -0.5 * LOG2E

    @pl.when(n == 0)
    def _():
        kcar[...] = jnp.zeros_like(kcar)
        vcar[...] = jnp.zeros_like(vcar)

    pbuf[...] = jnp.zeros_like(pbuf)
    vbuf[...] = jnp.zeros_like(vbuf)
    lo2 = lax.broadcasted_iota(jnp.int32, (2 * ATTN_BLOCK, LANES), 1) < ATTN_HEAD_DIM
    zero = jnp.zeros((2 * ATTN_BLOCK, LANES), BF16)
    den_lo = lo2.astype(BF16)
    den_hi = jnp.logical_not(lo2).astype(BF16)

    for p, dil in enumerate(DILATIONS):
        nb = (sb // ATTN_BLOCK) // dil

        def rows_of(it, dil=dil, nb=nb):
            start = (it % nb) * (ATTN_BLOCK * dil) + it // nb
            return pl.ds(start, ATTN_BLOCK, stride=dil) if dil > 1 else pl.ds(pl.multiple_of(start, ATTN_BLOCK), ATTN_BLOCK)

        def pv(it, probs, vaug, p=p, rows_of=rows_of):
            rows = rows_of(it)
            of = jnp.dot(probs, vaug, preferred_element_type=F32)
            oscr[p, rows, :] = of[:, :LANES]
            dscr[p, rows, :] = of[:, LANES:]

        def qk(it, p=p, nb=nb, rows_of=rows_of):
            r = it // nb
            rows = rows_of(it)
            crow = pl.ds(pl.multiple_of(r * ATTN_BLOCK, ATTN_BLOCK), ATTN_BLOCK)
            qb = q_ref[rows, :] * scale
            kc = k_ref[rows, :].astype(BF16)
            vc = v_ref[rows, :].astype(BF16)
            kcat = jnp.concatenate([kcar[p, crow, :], kc], axis=0)
            vcat = jnp.concatenate([vcar[p, crow, :], vc], axis=0)
            vaug = jnp.concatenate([jnp.concatenate([jnp.where(lo2, vcat, zero), den_lo], axis=1),
                                    jnp.concatenate([jnp.where(lo2, zero, vcat), den_hi], axis=1)], axis=0)
            kcar[p, crow, :] = kc
            vcar[p, crow, :] = vc
            first = jnp.logical_and(n == 0, it % nb == 0).astype(jnp.int32)
            probs, acc_m = [], None
            for hh in range(2):
                sel = is_lo if hh == 0 else jnp.logical_not(is_lo)
                qm = jnp.where(sel, qb, 0.0).astype(BF16)
                s = _dot_nt(qm, kcat) + bias_ref[first, p, hh]
                m = jnp.max(s, axis=-1, keepdims=True)
                probs.append(jnp.exp2(s - m).astype(BF16))
                mb = jnp.broadcast_to(m, (ATTN_BLOCK, LANES))
                acc_m = mb if hh == 0 else jnp.where(sel, mb, acc_m)
            mscr[p, rows, :] = acc_m
            return jnp.concatenate(probs, axis=1), vaug

        def body(g, carry, pv=pv, qk=qk):
            it0 = g * ATTN_UNROLL
            pend = (pbuf[...], vbuf[...])
            for u in range(ATTN_UNROLL):
                cur = qk(it0 + u)
                pv(jnp.maximum(it0 + u - 1, 0), *pend)
                pend = cur
            pbuf[...] = pend[0]
            vbuf[...] = pend[1]
            return carry

        n_it = sb // ATTN_BLOCK
        lax.fori_loop(0, n_it // ATTN_UNROLL, body, 0)
        pv(n_it - 1, pbuf[...], vbuf[...])

    ct = 256

    def combine(t, carry):
        rs = pl.ds(pl.multiple_of(t * ct, ct), ct)
        m0, m1, m2 = mscr[0, rs, :], mscr[1, rs, :], mscr[2, rs, :]
        mx = jnp.maximum(jnp.maximum(m0, m1), m2)
        w0, w1, w2 = jnp.exp2(m0 - mx), jnp.exp2(m1 - mx), jnp.exp2(m2 - mx)
        num = w0 * oscr[0, rs, :] + w1 * oscr[1, rs, :] + w2 * oscr[2, rs, :]
        den = w0 * dscr[0, rs, :] + w1 * dscr[1, rs, :] + w2 * dscr[2, rs, :]
        o_ref[rs, :] = num / den
        return carry

    lax.fori_loop(0, sb // ct, combine, 0)


def _attention(aq, ak, av, bias, bsz, seq):
    n, width = aq.shape
    sb = ATTN_SUPER
    nsb = seq // sb
    pairs = width // LANES
    npat = len(DILATIONS)
    blk = pl.BlockSpec((sb, LANES), lambda b, h, t: (b * nsb + t, h))
    return pl.pallas_call(
        _attn_kernel,
        grid=(bsz, pairs, nsb),
        in_specs=[blk, blk, blk,
                  pl.BlockSpec((2, npat, 2, ATTN_BLOCK, 2 * ATTN_BLOCK), lambda b, h, t: (0, 0, h, 0, 0))],
        out_specs=blk,
        out_shape=jax.ShapeDtypeStruct((n, width), F32),
        scratch_shapes=[
            pltpu.VMEM((npat, max(DILATIONS) * ATTN_BLOCK, LANES), BF16),
            pltpu.VMEM((npat, max(DILATIONS) * ATTN_BLOCK, LANES), BF16),
            pltpu.VMEM((ATTN_BLOCK, 4 * ATTN_BLOCK), BF16),
            pltpu.VMEM((4 * ATTN_BLOCK, 2 * LANES), BF16),
            pltpu.VMEM((npat, sb, LANES), F32),
            pltpu.VMEM((npat, sb, LANES), F32),
            pltpu.VMEM((npat, sb, LANES), F32),
        ],
        compiler_params=_params(("parallel", "parallel", "arbitrary")),
        name="dilated_attn",
    )(aq, ak, av, bias)


def _hgrn_consts():
    c = HGRN_CHUNK
    t = np.arange(c)[:, None]
    u = np.arange(c)[None, :]
    low, masks = [], []
    m = c // 2
    while m >= 1:
        mid = (t // (2 * m)) * (2 * m) + m
        is_q = t >= mid
        if m in (4, 2):
            low.append((is_q & (u >= mid) & (u <= t)) | (~is_q & (u > t) & (u < mid)))
        masks.append(((t // (2 * m)) == (u // (2 * m))) & (t % (2 * m) >= m) & (u % (2 * m) < m))
        m //= 2
    masks.append(t == u)
    f = lambda a: np.asarray(a, np.float32)
    return (jnp.asarray(f(u <= t), BF16), jnp.asarray(f(np.concatenate(low, axis=0)), BF16),
            jnp.asarray(f(np.broadcast_to(t % 2 == 1, (c, c)))), jnp.asarray(f(np.stack(masks))))


def _hgrn_kernel(q_ref, z_ref, v_ref, g_ref, lb_ref, gn_ref, tri_ref, low_ref, odd_ref, msk_ref, o_ref, st):
    c = HGRN_CHUNK

    @pl.when(pl.program_id(2) == 0)
    def _():
        st[...] = jnp.zeros_like(st)

    lb = jnp.maximum(lb_ref[...], 0.0)
    log_lb = jnp.log(lb)
    log_1m = jnp.log1p(-lb)
    one_m = 1.0 - lb
    gn = gn_ref[...]
    n_levels = msk_ref.shape[0] - 1
    coarse = [c >> (i + 1) for i in range(n_levels) if (c >> (i + 1)) >= 8]

    def chunk(ci, carry):
        rs = pl.ds(pl.multiple_of(ci * c, c), c)
        z = z_ref[rs, :]
        qb = q_ref[rs, :]
        vb = v_ref[rs, :]
        ez = jnp.exp(-jnp.abs(z))
        ls = jnp.minimum(z, 0.0) - jnp.log(1.0 + ez)
        b2 = log_1m + ls
        lf = jnp.maximum(log_lb, b2) + jnp.log(1.0 + jnp.exp(-jnp.abs(log_lb - b2)))
        rz = 1.0 / (1.0 + ez)
        kb = (one_m * jnp.where(z > 0.0, ez * rz, rz)).astype(BF16)
        l2 = lf * LOG2E
        hi = l2.astype(BF16)
        hl = jnp.concatenate([hi, (l2 - hi.astype(F32)).astype(BF16)], axis=1)
        bb = jnp.dot(tri_ref[...], hl, preferred_element_type=F32)
        b = bb[:, :HGRN_EXPAND] + bb[:, HGRN_EXPAND:]
        dl = jnp.dot(low_ref[...], hl, preferred_element_type=F32)
        dl = dl[:, :HGRN_EXPAND] + dl[:, HGRN_EXPAND:]

        def level(d):
            e = jnp.exp2(d).astype(BF16)
            return _dot_nt(qb * e, kb * e)

        a = msk_ref[n_levels] * _dot_nt(qb, kb)
        for i in range(n_levels):
            m = c >> (i + 1)
            if m >= 8:
                parts = []
                for r0 in range(0, c, 2 * m):
                    mid = b[r0 + m - 1:r0 + m, :]
                    parts += [mid - b[r0:r0 + m, :], b[r0 + m:r0 + 2 * m, :] - mid]
                d = jnp.concatenate(parts, axis=0)
            elif m > 1:
                d = dl[(i - len(coarse)) * c:(i - len(coarse) + 1) * c]
            else:
                d = l2 * odd_ref[...]
            a = a + msk_ref[i] * level(d)
        b_last = b[c - 1:c, :]
        s_t = st[...]
        o = _dot_nt(qb * jnp.exp2(b).astype(BF16), s_t.astype(BF16)) + jnp.dot(a.astype(BF16), vb, preferred_element_type=F32)
        st[...] = jnp.exp2(b_last) * s_t + _dot_tn(vb, kb * jnp.exp2(b_last - b).astype(BF16))
        g = g_ref[rs, :]
        o_ref[rs, :] = (_rms(o, gn) * (g * jax.nn.sigmoid(g))).astype(o_ref.dtype)
        return carry

    lax.fori_loop(0, q_ref.shape[0] // c, chunk, 0, unroll=HGRN_UNROLL)


def _hgrn(rq, rf, ri, rg, lb, gn, consts, bsz, seq):
    n, width = rq.shape
    heads = width // HGRN_EXPAND
    ts = HGRN_TILE
    nt = seq // ts
    blk = pl.BlockSpec((ts, HGRN_EXPAND), lambda b, h, t: (b * nt + t, h))
    vec = pl.BlockSpec((1, HGRN_EXPAND), lambda b, h, t: (0, h))
    const = lambda a: pl.BlockSpec(a.shape, lambda b, h, t: (0,) * a.ndim)
    return pl.pallas_call(
        _hgrn_kernel,
        grid=(bsz, heads, nt),
        in_specs=[blk, blk, blk, blk, vec, vec] + [const(a) for a in consts],
        out_specs=blk,
        out_shape=jax.ShapeDtypeStruct((n, width), BF16),
        scratch_shapes=[pltpu.VMEM((HGRN_EXPAND, HGRN_EXPAND), F32)],
        compiler_params=_params(("parallel", "parallel", "arbitrary")),
        name="hgrn2",
    )(rq, rf, ri, rg, lb.reshape(1, width), gn.reshape(1, width), *consts)


def _route_math(lg):
    row = lax.broadcasted_iota(jnp.int32, (EXPERTS_PER_GROUP, lg.shape[1]), 0).astype(F32)
    none = float(EXPERTS_PER_GROUP)

    def first_argmax(vals):
        mx = jnp.max(vals, axis=0, keepdims=True)
        return mx, jnp.min(jnp.where(vals == mx, row, none), axis=0, keepdims=True)

    g = lg[0:EXPERTS_PER_GROUP, :]
    is_group = row < N_GROUPS
    gmax, gidx = first_argmax(jnp.where(is_group, g, -jnp.inf))
    gprob = 1.0 / jnp.sum(jnp.where(is_group, jnp.exp(g - gmax), 0.0), axis=0, keepdims=True)
    el = lg[EXPERTS_PER_GROUP:2 * EXPERTS_PER_GROUP, :]
    for k in range(1, N_GROUPS):
        el = jnp.where(gidx == k, lg[(k + 1) * EXPERTS_PER_GROUP:(k + 2) * EXPERTS_PER_GROUP, :], el)
    t1, i1 = first_argmax(el)
    t2, i2 = first_argmax(jnp.where(row == i1, -jnp.inf, el))
    ex = jnp.exp(t2 - t1)
    base = gidx * EXPERTS_PER_GROUP
    eids = jnp.concatenate([base + i1, base + i2], axis=0).astype(jnp.int32)
    gates = jnp.concatenate([gprob / (1.0 + ex), gprob * ex / (1.0 + ex)], axis=0)
    return eids, gates


def _out_kernel(a_ref, r_ref, x_ref, mod_ref, ag_ref, g2_ref, wo_ref, wr_ref, br_ref, xo_ref, ha_ref, hb_ref, e_ref, gt_ref):
    mod = mod_ref[0]
    mix = jnp.concatenate([_rms(a_ref[...], ag_ref[...]).astype(BF16), r_ref[...]], axis=1)
    xn = x_ref[...] + mod[2:3, :] * jnp.dot(mix, wo_ref[0].astype(BF16), preferred_element_type=F32)
    xo_ref[...] = xn
    h2 = _rms(xn, g2_ref[...]) * (1.0 + mod[4:5, :]) + mod[3:4, :]
    ha_ref[...], hb_ref[...] = _pack_rows(h2)
    eids, gates = _route_math(_dot_nt(wr_ref[...], h2.astype(BF16)) + br_ref[...])
    e_ref[...] = eids
    pad = jnp.zeros((LANES - TOP_K, LANES), F32)
    for k in range(h2.shape[0] // LANES):
        tile = jnp.concatenate([gates[:, k * LANES:(k + 1) * LANES], pad], axis=0)
        gt_ref[k * LANES:(k + 1) * LANES, :] = tile.T


def _out_proj(attn, rec, x2, mod_l, ag, g2, w_out, layer, wr_bf, br, seq):
    n, d = x2.shape
    half = attn.shape[1]
    tm = ROW_TILE
    per_batch = seq // tm
    row = lambda w: pl.BlockSpec((tm, w), lambda i: (i, 0))
    full = lambda a: pl.BlockSpec(a.shape, lambda i: (0,) * a.ndim, pipeline_mode=pl.Buffered(1))
    ag2, g22 = ag.reshape(1, half), g2.reshape(1, d)
    return pl.pallas_call(
        _out_kernel,
        grid=(n // tm,),
        in_specs=[row(half), row(half), row(d), pl.BlockSpec((1, 6, d), lambda i: (i // per_batch, 0, 0)),
                  full(ag2), full(g22), pl.BlockSpec((1, d, d), lambda i: (layer, 0, 0), pipeline_mode=pl.Buffered(1)),
                  full(wr_bf), full(br)],
        out_specs=[row(d), row(PACK_WIDTH), row(PACK_WIDTH), pl.BlockSpec((TOP_K, tm), lambda i: (0, i)), row(LANES)],
        out_shape=[jax.ShapeDtypeStruct((n, d), F32), jax.ShapeDtypeStruct((n, PACK_WIDTH), U32),
                   jax.ShapeDtypeStruct((n, PACK_WIDTH), U32), jax.ShapeDtypeStruct((TOP_K, n), jnp.int32),
                   jax.ShapeDtypeStruct((n, LANES), F32)],
        compiler_params=_params(("parallel",)),
        name="out_proj",
    )(attn, rec, x2, mod_l, ag2, g22, w_out, wr_bf, br)


def _plan_kernel(e_ref, ut_ref, pos_ref, be_ref, bv_ref, bf_ref, bn_ref, nu_ref, cnt, off, run):
    phase = pl.program_id(0)
    i = pl.program_id(1)
    t = e_ref.shape[1]
    eio = lax.broadcasted_iota(jnp.int32, (N_EXPERTS, t), 0)
    oh1 = e_ref[0:1, :] == eio
    oh2 = e_ref[1:2, :] == eio
    oh = jnp.logical_or(oh1, oh2).astype(F32)
    tot = jnp.sum(oh, axis=1, keepdims=True)

    @pl.when(jnp.logical_and(phase == 0, i == 0))
    def _():
        cnt[...] = jnp.zeros_like(cnt)

    @pl.when(phase == 0)
    def _():
        cnt[...] += tot

    @pl.when(jnp.logical_and(phase == 1, i == 0))
    def _():
        counts = cnt[...]
        nb = jnp.floor((counts + (MOE_TILE - 1)) * (1.0 / MOE_TILE))
        er = lax.broadcasted_iota(jnp.int32, (N_EXPERTS, N_EXPERTS), 0)
        ec = lax.broadcasted_iota(jnp.int32, (N_EXPERTS, N_EXPERTS), 1)
        cum_excl = jnp.dot((ec < er).astype(BF16), nb.astype(BF16), preferred_element_type=F32)
        off[...] = cum_excl * MOE_TILE
        run[...] = jnp.zeros_like(run)
        ce, ci, cn = cum_excl[:, 0:1], (cum_excl + nb)[:, 0:1], counts[:, 0:1]
        b = lax.broadcasted_iota(jnp.int32, (N_EXPERTS, be_ref.shape[1]), 1).astype(F32)
        member = jnp.logical_and(b >= ce, b < ci)
        be = jnp.minimum(jnp.sum((b >= ci).astype(F32), axis=0, keepdims=True), N_EXPERTS - 1.0)
        bv = jnp.sum(jnp.where(member, jnp.minimum(cn - (b - ce) * MOE_TILE, float(MOE_TILE)), 0.0), axis=0, keepdims=True)
        be_ref[...] = be.astype(jnp.int32)
        bv_ref[...] = bv.astype(jnp.int32)
        has = nb[:, 0:1] > 0.0
        eid = lax.broadcasted_iota(jnp.int32, b.shape, 0).astype(F32)
        bf_ref[...] = jnp.sum(jnp.where(jnp.logical_and(b == ce, has), 1.0, 0.0), axis=0, keepdims=True).astype(jnp.int32)
        nxt = jnp.min(jnp.where(jnp.logical_and(eid > be, has), eid, float(N_EXPERTS)), axis=0, keepdims=True)
        bn_ref[...] = jnp.where(nxt < N_EXPERTS, nxt, -1.0).astype(jnp.int32)
        nu_ref[...] = jnp.broadcast_to(ci[N_EXPERTS - 1:N_EXPERTS, :], nu_ref.shape).astype(jnp.int32)

    @pl.when(phase == 1)
    def _():
        c = jnp.dot(oh.astype(BF16), ut_ref[...], preferred_element_type=F32)
        r = off[:, 0:1] + run[:, 0:1] + c - 1.0
        pos_ref[0:1, :] = jnp.sum(jnp.where(oh1, r, 0.0), axis=0, keepdims=True).astype(jnp.int32)
        pos_ref[1:2, :] = jnp.sum(jnp.where(oh2, r, 0.0), axis=0, keepdims=True).astype(jnp.int32)
        run[...] += tot


def _plan(expert_ids_t, n_rows):
    n = expert_ids_t.shape[1]
    t = PLAN_TILE
    n_blocks = n_rows // MOE_TILE
    assert n_blocks <= PLAN_LANES and TOP_K == 2
    ut = jnp.asarray(np.triu(np.ones((t, t), np.float32)), BF16)
    tab = jax.ShapeDtypeStruct((1, PLAN_LANES), jnp.int32)
    const = lambda shape: pl.BlockSpec(shape, lambda p, i: (0, 0))
    pos, be, bv, bf, bn, nu = pl.pallas_call(
        _plan_kernel,
        grid=(2, n // t),
        in_specs=[pl.BlockSpec((TOP_K, t), lambda p, i: (0, i)), const((t, t))],
        out_specs=[pl.BlockSpec((TOP_K, t), lambda p, i: (0, i * p))] + [const((1, PLAN_LANES))] * 4 + [const((1, LANES))],
        out_shape=[jax.ShapeDtypeStruct((TOP_K, n), jnp.int32), tab, tab, tab, tab, jax.ShapeDtypeStruct((1, LANES), jnp.int32)],
        scratch_shapes=[pltpu.VMEM((N_EXPERTS, LANES), F32)] * 3,
        compiler_params=_params(("arbitrary", "arbitrary")),
        name="plan",
    )(expert_ids_t, ut)
    return pos, (be.reshape(-1), bv.reshape(-1), bf.reshape(-1), bn.reshape(-1), nu.reshape(-1)[:1])


def _sc_mesh():
    return plsc.VectorSubcoreMesh(core_axis_name="c", subcore_axis_name="s")


def _sc_scatter_rows(x, pos, n_rows):
    n, d = x.shape
    w = SC_WINDOW

    @functools.partial(pl.kernel, out_type=jax.ShapeDtypeStruct((n_rows, d), x.dtype), mesh=_sc_mesh(), scratch_types=[])
    def scatter(x_hbm, p_hbm, o_hbm):
        def body(x_vmem, i_vmem):
            for k in range(TOP_K):
                pltpu.sync_copy(x_vmem, o_hbm.at[i_vmem.at[k]])

        pltpu.emit_pipeline(
            body,
            grid=(n // w,),
            in_specs=[pl.BlockSpec((w, d), lambda i: (i, 0)), pl.BlockSpec((TOP_K, w), lambda i: (0, i))],
            out_specs=[],
            core_axis_name=("c", "s"),
            dimension_semantics=(pltpu.PARALLEL,),
        )(x_hbm, p_hbm)

    return scatter(x, pos)


def _sc_gather_rows(y, pos):
    m = pos.shape[0] * pos.shape[1]
    d = y.shape[1]
    w = SC_WINDOW

    @functools.partial(pl.kernel, out_type=jax.ShapeDtypeStruct((m, d), y.dtype), mesh=_sc_mesh(), scratch_types=[])
    def gather(y_hbm, p_hbm, o_hbm):
        def body(i_vmem, o_vmem):
            pltpu.sync_copy(y_hbm.at[i_vmem.at[0]], o_vmem)

        pltpu.emit_pipeline(
            body,
            grid=(m // w,),
            in_specs=[pl.BlockSpec((1, w), lambda i: (0, i))],
            out_specs=[pl.BlockSpec((w, d), lambda i: (i, 0))],
            core_axis_name=("c", "s"),
            dimension_semantics=(pltpu.PARALLEL,),
        )(p_hbm, o_hbm)

    return gather(y, pos.reshape(1, m))


def _expert_kernel(be_ref, bv_ref, bf_ref, bn_ref, nu_ref, xa_ref, xb_ref, w1_hbm, w3_hbm, w2_hbm, ya_ref, yb_ref,
                   wf1, wf3, wf2, w1b, w3b, w2b, wsem, par, *, layer):
    i = pl.program_id(0)
    tb = xa_ref.shape[0]
    half = tb // 2

    def fetch(e, slot):
        return (pltpu.make_async_copy(w1_hbm.at[layer, e], wf1.at[slot], wsem.at[slot, 0]),
                pltpu.make_async_copy(w3_hbm.at[layer, e], wf3.at[slot], wsem.at[slot, 1]),
                pltpu.make_async_copy(w2_hbm.at[layer, e], wf2.at[slot], wsem.at[slot, 2]))

    def ffn(rows):
        live = lax.broadcasted_iota(jnp.int32, (rows, 1), 0) < bv_ref[i]
        x = jnp.where(live, _unpack_rows(xa_ref[0:rows, :], xb_ref[0:rows, :]), 0.0).astype(BF16)
        y = None
        hc = w1b.shape[1] // FFN_SPLIT
        for c in range(FFN_SPLIT):
            cs = slice(c * hc, (c + 1) * hc)
            a = jnp.dot(x, w1b[:, cs], preferred_element_type=F32)
            b = jnp.dot(x, w3b[:, cs], preferred_element_type=F32)
            hm = (a * jax.nn.sigmoid(a) * b).astype(BF16)
            part = jnp.dot(hm, w2b[cs, :], preferred_element_type=F32)
            y = part if y is None else y + part
        ya_ref[0:rows, :], yb_ref[0:rows, :] = _pack_rows(y)

    @pl.when(i == 0)
    def _():
        par[0] = 0
        for c in fetch(be_ref[0], 0):
            c.start()

    @pl.when(i < nu_ref[0])
    def _():
        @pl.when(bf_ref[i] == 1)
        def _():
            slot = par[0]
            for c in fetch(be_ref[i], slot):
                c.wait()
            w1b[...] = wf1[slot].astype(BF16)
            w3b[...] = wf3[slot].astype(BF16)
            w2b[...] = wf2[slot].astype(BF16)

            @pl.when(bn_ref[i] >= 0)
            def _():
                for c in fetch(bn_ref[i], 1 - slot):
                    c.start()

            par[0] = 1 - slot

        @pl.when(bv_ref[i] > half)
        def _():
            ffn(tb)

        @pl.when(bv_ref[i] <= half)
        def _():
            ffn(half)
            ya_ref[half:, :] = jnp.zeros((tb - half, ya_ref.shape[1]), ya_ref.dtype)
            yb_ref[half:, :] = jnp.zeros((tb - half, yb_ref.shape[1]), yb_ref.dtype)

    @pl.when(i >= nu_ref[0])
    def _():
        ya_ref[...] = jnp.zeros_like(ya_ref)
        yb_ref[...] = jnp.zeros_like(yb_ref)


def _experts(xa, xb, w1, w3, w2, layer, tables):
    n_rows = xa.shape[0]
    d, de = w1.shape[-2:]
    tb = MOE_TILE
    xspec = pl.BlockSpec((tb, PACK_WIDTH), lambda i, be, bv, bf, bn, nu: (jnp.minimum(i, nu[0] - 1), 0))
    yspec = pl.BlockSpec((tb, PACK_WIDTH), lambda i, be, bv, bf, bn, nu: (i, 0))
    anyspec = pl.BlockSpec(memory_space=pl.ANY)
    grid_spec = pltpu.PrefetchScalarGridSpec(
        num_scalar_prefetch=5,
        grid=(n_rows // tb,),
        in_specs=[xspec, xspec, anyspec, anyspec, anyspec],
        out_specs=[yspec, yspec],
        scratch_shapes=[pltpu.VMEM((2, d, de), F32), pltpu.VMEM((2, d, de), F32), pltpu.VMEM((2, de, d), F32),
                        pltpu.VMEM((d, de), BF16), pltpu.VMEM((d, de), BF16), pltpu.VMEM((de, d), BF16),
                        pltpu.SemaphoreType.DMA((2, 3)), pltpu.SMEM((1,), jnp.int32)],
    )
    rows = jax.ShapeDtypeStruct((n_rows, PACK_WIDTH), U32)
    return pl.pallas_call(
        functools.partial(_expert_kernel, layer=layer),
        grid_spec=grid_spec,
        out_shape=[rows, rows],
        compiler_params=_params(("arbitrary",)),
        name="experts",
    )(*tables, xa, xb, w1, w3, w2)


def _final_kernel(ga0, ga1, gb0, gb1, gt_ref, mod_ref, x_ref, fg_ref, o_ref):
    o_ref[...] = _rms(_moe_residual(x_ref, ga0, ga1, gb0, gb1, gt_ref, mod_ref[0][5:6, :]), fg_ref[...])


def _final(x2, ga, gb, gates, mod_l, final_g, seq):
    n, d = x2.shape
    tm = ROW_TILE
    per_batch = seq // tm
    row = lambda w: pl.BlockSpec((tm, w), lambda i: (i, 0))
    second = pl.BlockSpec((tm, PACK_WIDTH), lambda i: (i + n // tm, 0))
    return pl.pallas_call(
        _final_kernel,
        grid=(n // tm,),
        in_specs=[row(PACK_WIDTH), second, row(PACK_WIDTH), second, row(LANES),
                  pl.BlockSpec((1, 6, d), lambda i: (i // per_batch, 0, 0)), row(d), pl.BlockSpec((1, d), lambda i: (0, 0))],
        out_specs=row(d),
        out_shape=jax.ShapeDtypeStruct((n, d), F32),
        compiler_params=_params(("parallel",)),
        name="final_norm",
    )(ga, ga, gb, gb, gates, mod_l, x2, final_g.reshape(1, d))


def kernel(x, c, w_ada, b_ada, norm1_g, w_in, attn_norm_g, hgrn_lb_logits, hgrn_norm_g, w_out, norm2_g, router_group_w, router_group_b, router_expert_w, router_expert_b, moe_w1, moe_w3, moe_w2, final_g):
    bsz, seq, d = x.shape
    depth = w_ada.shape[0]
    n = bsz * seq
    attn_dim = attn_norm_g.shape[1]
    n_heads = attn_dim // ATTN_HEAD_DIM

    lb_w = jax.nn.softmax(hgrn_lb_logits.astype(F32), axis=0)
    lower_bounds = jnp.cumsum(lb_w, axis=0) - lb_w[0:1]
    mod = _adaln(c, w_ada, b_ada)
    bias = _attn_bias(n_heads)
    hconsts = _hgrn_consts()
    n_rows = n * TOP_K + N_EXPERTS * MOE_TILE

    x2 = x.reshape(n, d)
    moe = None
    for layer in range(depth):
        x2, (aq, ak, av, rq, rf, ri, rg) = _in_proj(x2, mod[layer], norm1_g[layer], w_in, layer, seq, moe)
        attn = _attention(aq, ak, av, bias, bsz, seq)
        rec = _hgrn(rq, rf, ri, rg, lower_bounds[layer], hgrn_norm_g[layer], hconsts, bsz, seq)
        gpad = EXPERTS_PER_GROUP - N_GROUPS
        wr = jnp.concatenate([router_group_w[layer].T, jnp.zeros((gpad, d), F32), router_expert_w[layer].T], axis=0)
        br = jnp.concatenate([router_group_b[layer], jnp.zeros((gpad,), F32), router_expert_b[layer]]).reshape(-1, 1)
        x2, ha, hb, eids, gates = _out_proj(attn, rec, x2, mod[layer], attn_norm_g[layer], norm2_g[layer],
                                            w_out, layer, wr.astype(BF16), br, seq)
        pos, tables = _plan(eids, n_rows)
        ya, yb = _experts(_sc_scatter_rows(ha, pos, n_rows), _sc_scatter_rows(hb, pos, n_rows),
                          moe_w1, moe_w3, moe_w2, layer, tables)
        moe = (_sc_gather_rows(ya, pos), _sc_gather_rows(yb, pos), gates, mod[layer])
    return _final(x2, moe[0], moe[1], moe[2], moe[3], final_g, seq).reshape(bsz, seq, d)
```

```python
import functools

import numpy as np
import jax
import jax.numpy as jnp
from jax import lax
from jax.experimental import pallas as pl
from jax.experimental.pallas import tpu as pltpu
from jax.experimental.pallas import tpu_sc as plsc

F32 = jnp.float32
BF16 = jnp.bfloat16
U32 = jnp.uint32

ATTN_HEAD_DIM = 64
ATTN_BLOCK = 128
DILATIONS = (1, 4, 16)
ALIBI_MAX_EXP = 8.0
HGRN_EXPAND = 128
HGRN_CHUNK = 128
N_GROUPS = 4
EXPERTS_PER_GROUP = 8
N_EXPERTS = N_GROUPS * EXPERTS_PER_GROUP
TOP_K = 2
NORM_EPS = 1e-6
LOG2E = 1.4426950408889634

LANES = 128
ATTN_SUPER = ATTN_BLOCK * max(DILATIONS)
ROW_TILE = 512
ATTN_UNROLL = 16
HGRN_TILE = 2048
HGRN_UNROLL = 16
MOE_TILE = 512
FFN_SPLIT = 2
PLAN_TILE = 2048
PLAN_LANES = 256
PACK_WIDTH = 256
SC_WINDOW = 128
VMEM_LIMIT = 56 << 20


def _params(semantics):
    return pltpu.CompilerParams(dimension_semantics=semantics, vmem_limit_bytes=VMEM_LIMIT)


def _rms(x, g):
    return x * lax.rsqrt(jnp.mean(x * x, axis=-1, keepdims=True) + NORM_EPS) * g


def _dot_nt(a, b):
    return lax.dot_general(a, b, (((1,), (1,)), ((), ())), preferred_element_type=F32)


def _dot_tn(a, b):
    return lax.dot_general(a, b, (((0,), (0,)), ((), ())), preferred_element_type=F32)


def _pack_pair(hi, lo):
    hb = lax.bitcast_convert_type(hi.astype(BF16).astype(F32), U32)
    lb = lax.bitcast_convert_type(lo.astype(BF16).astype(F32), U32)
    return hb | (lb >> 16)


def _unpack_pair(u):
    hi = lax.bitcast_convert_type(u & jnp.uint32(0xFFFF0000), F32)
    lo = lax.bitcast_convert_type(u << 16, F32)
    return hi, lo


def _pack_rows(y):
    w = PACK_WIDTH
    return _pack_pair(y[:, 0:w], y[:, 2 * w:3 * w]), _pack_pair(y[:, w:2 * w], y[:, 3 * w:4 * w])


def _unpack_rows(ua, ub):
    ha, la = _unpack_pair(ua)
    hb, lb = _unpack_pair(ub)
    return jnp.concatenate([ha, hb, la, lb], axis=1)


def _moe_residual(x_ref, ga0, ga1, gb0, gb1, gt_ref, gate_row):
    gt = gt_ref[...]
    y = gt[:, 0:1] * _unpack_rows(ga0[...], gb0[...]) + gt[:, 1:2] * _unpack_rows(ga1[...], gb1[...])
    return x_ref[...] + gate_row * y


def _ada_kernel(c_ref, w_ref, b_ref, o_ref):
    c = c_ref[...]
    ca = c * jax.nn.sigmoid(c)
    hi = ca.astype(BF16)
    lo = (ca - hi.astype(F32)).astype(BF16)
    w = w_ref[0].astype(BF16)
    o_ref[0] = (jnp.dot(hi, w, preferred_element_type=F32) + jnp.dot(lo, w, preferred_element_type=F32)) + b_ref[0]


def _adaln(c, w_ada, b_ada):
    depth, d, n6 = w_ada.shape
    bsz = c.shape[0]
    rows = 8
    cp = jnp.zeros((rows, d), F32).at[:bsz].set(c)
    tn = 1536
    out = pl.pallas_call(
        _ada_kernel,
        grid=(depth, n6 // tn),
        in_specs=[
            pl.BlockSpec((rows, d), lambda l, j: (0, 0)),
            pl.BlockSpec((1, d, tn), lambda l, j: (l, 0, j)),
            pl.BlockSpec((1, 1, tn), lambda l, j: (l, 0, j)),
        ],
        out_specs=pl.BlockSpec((1, rows, tn), lambda l, j: (l, 0, j)),
        out_shape=jax.ShapeDtypeStruct((depth, rows, n6), F32),
        compiler_params=_params(("parallel", "parallel")),
        name="adaln",
    )(cp, w_ada, b_ada.reshape(depth, 1, n6))
    return out[:, :bsz].reshape(depth, bsz, 6, d)


def _in_kernel(*refs, n_moe):
    moe, (x_ref, mod_ref, g_ref, w_ref), rest = refs[:n_moe], refs[n_moe:n_moe + 4], refs[n_moe + 4:]
    hb = rest[-1]
    if n_moe:
        ga0, ga1, gb0, gb1, gt_ref, pmod_ref = moe
        x = _moe_residual(x_ref, ga0, ga1, gb0, gb1, gt_ref, pmod_ref[0][5:6, :])
        rest[0][...] = x
        outs = rest[1:-1]
    else:
        x = x_ref[...]
        outs = rest[:-1]
    mod = mod_ref[0]
    h = _rms(x, g_ref[...]) * (1.0 + mod[1:2, :]) + mod[0:1, :]
    hb[...] = h.astype(BF16)
    width = outs[0].shape[1]
    for i, o in enumerate(outs):
        w = w_ref[0, :, i * width:(i + 1) * width].astype(BF16)
        o[...] = jnp.dot(hb[...], w, preferred_element_type=F32).astype(o.dtype)


def _in_proj(x2, mod_l, g, w_in, layer, seq, moe=None):
    n, d = x2.shape
    n_out = w_in.shape[2]
    width = 512
    tm = ROW_TILE
    per_batch = seq // tm
    row = lambda w: pl.BlockSpec((tm, w), lambda i: (i, 0))
    mod_spec = pl.BlockSpec((1, 6, d), lambda i: (i // per_batch, 0, 0))
    in_specs, args, out_specs, out_shape = [], [], [], []
    if moe is not None:
        ga, gb, gates, mod_prev = moe
        second = pl.BlockSpec((tm, PACK_WIDTH), lambda i: (i + n // tm, 0))
        in_specs += [row(PACK_WIDTH), second, row(PACK_WIDTH), second, row(LANES), mod_spec]
        args += [ga, ga, gb, gb, gates, mod_prev]
        out_specs.append(row(d))
        out_shape.append(jax.ShapeDtypeStruct((n, d), F32))
    in_specs += [row(d), mod_spec, pl.BlockSpec((1, d), lambda i: (0, 0)),
                 pl.BlockSpec((1, d, n_out), lambda i: (layer, 0, 0), pipeline_mode=pl.Buffered(1))]
    args += [x2, mod_l, g.reshape(1, d), w_in]
    out_dtypes = (F32, F32, F32, BF16, F32, BF16, F32)
    out_specs += [row(width)] * len(out_dtypes)
    out_shape += [jax.ShapeDtypeStruct((n, width), dt) for dt in out_dtypes]
    outs = pl.pallas_call(
        functools.partial(_in_kernel, n_moe=0 if moe is None else 6),
        grid=(n // tm,),
        in_specs=in_specs,
        out_specs=out_specs,
        out_shape=out_shape,
        scratch_shapes=[pltpu.VMEM((tm, d), BF16)],
        compiler_params=_params(("parallel",)),
        name="in_proj",
    )(*args)
    return (x2, outs) if moe is None else (outs[0], outs[1:])


def _attn_bias(n_heads):
    qi = np.arange(ATTN_BLOCK)[:, None]
    kj = np.arange(2 * ATTN_BLOCK)[None, :]
    dist = ATTN_BLOCK + qi - kj
    valid = (dist >= 0) & (dist <= ATTN_BLOCK)
    slopes = np.exp2(-ALIBI_MAX_EXP * np.arange(1, n_heads + 1, dtype=np.float32) / n_heads)
    out = np.empty((2, len(DILATIONS), n_heads, ATTN_BLOCK, 2 * ATTN_BLOCK), np.float32)
    for p, dil in enumerate(DILATIONS):
        b = -slopes[:, None, None] * (dist * dil).astype(np.float32)[None] * np.float32(LOG2E)
        out[0, p] = np.where(valid[None], b, -np.inf)
        out[1, p] = np.where((valid & (kj >= ATTN_BLOCK))[None], b, -np.inf)
    return jnp.asarray(out)


def _attn_kernel(q_ref, k_ref, v_ref, bias_ref, o_ref, kcar, vcar, pbuf, vbuf, oscr, mscr, dscr):
    n = pl.program_id(2)
    sb = q_ref.shape[0]
    is_lo = lax.broadcasted_iota(jnp.int32, (ATTN_BLOCK, LANES), 1) < ATTN_HEAD_DIM
    scale = ATTN_HEAD_DIM ** -0.5 * LOG2E

    @pl.when(n == 0)
    def _():
        kcar[...] = jnp.zeros_like(kcar)
        vcar[...] = jnp.zeros_like(vcar)

    pbuf[...] = jnp.zeros_like(pbuf)
    vbuf[...] = jnp.zeros_like(vbuf)
    lo2 = lax.broadcasted_iota(jnp.int32, (2 * ATTN_BLOCK, LANES), 1) < ATTN_HEAD_DIM
    zero = jnp.zeros((2 * ATTN_BLOCK, LANES), BF16)
    den_lo = lo2.astype(BF16)
    den_hi = jnp.logical_not(lo2).astype(BF16)

    for p, dil in enumerate(DILATIONS):
        nb = (sb // ATTN_BLOCK) // dil

        def rows_of(it, dil=dil, nb=nb):
            start = (it % nb) * (ATTN_BLOCK * dil) + it // nb
            return pl.ds(start, ATTN_BLOCK, stride=dil) if dil > 1 else pl.ds(pl.multiple_of(start, ATTN_BLOCK), ATTN_BLOCK)

        def pv(it, probs, vaug, p=p, rows_of=rows_of):
            rows = rows_of(it)
            of = jnp.dot(probs, vaug, preferred_element_type=F32)
            oscr[p, rows, :] = of[:, :LANES]
            dscr[p, rows, :] = of[:, LANES:]

        def qk(it, p=p, nb=nb, rows_of=rows_of):
            r = it // nb
            rows = rows_of(it)
            crow = pl.ds(pl.multiple_of(r * ATTN_BLOCK, ATTN_BLOCK), ATTN_BLOCK)
            qb = q_ref[rows, :] * scale
            kc = k_ref[rows, :].astype(BF16)
            vc = v_ref[rows, :].astype(BF16)
            kcat = jnp.concatenate([kcar[p, crow, :], kc], axis=0)
            vcat = jnp.concatenate([vcar[p, crow, :], vc], axis=0)
            vaug = jnp.concatenate([jnp.concatenate([jnp.where(lo2, vcat, zero), den_lo], axis=1),
                                    jnp.concatenate([jnp.where(lo2, zero, vcat), den_hi], axis=1)], axis=0)
            kcar[p, crow, :] = kc
            vcar[p, crow, :] = vc
            first = jnp.logical_and(n == 0, it % nb == 0).astype(jnp.int32)
            probs, acc_m = [], None
            for hh in range(2):
                sel = is_lo if hh == 0 else jnp.logical_not(is_lo)
                qm = jnp.where(sel, qb, 0.0).astype(BF16)
                s = _dot_nt(qm, kcat) + bias_ref[first, p, hh]
                m = jnp.max(s, axis=-1, keepdims=True)
                probs.append(jnp.exp2(s - m).astype(BF16))
                mb = jnp.broadcast_to(m, (ATTN_BLOCK, LANES))
                acc_m = mb if hh == 0 else jnp.where(sel, mb, acc_m)
            mscr[p, rows, :] = acc_m
            return jnp.concatenate(probs, axis=1), vaug

        def body(g, carry, pv=pv, qk=qk):
            it0 = g * ATTN_UNROLL
            pend = (pbuf[...], vbuf[...])
            for u in range(ATTN_UNROLL):
                cur = qk(it0 + u)
                pv(jnp.maximum(it0 + u - 1, 0), *pend)
                pend = cur
            pbuf[...] = pend[0]
            vbuf[...] = pend[1]
            return carry

        n_it = sb // ATTN_BLOCK
        lax.fori_loop(0, n_it // ATTN_UNROLL, body, 0)
        pv(n_it - 1, pbuf[...], vbuf[...])

    ct = 256

    def combine(t, carry):
        rs = pl.ds(pl.multiple_of(t * ct, ct), ct)
        m0, m1, m2 = mscr[0, rs, :], mscr[1, rs, :], mscr[2, rs, :]
        mx = jnp.maximum(jnp.maximum(m0, m1), m2)
        w0, w1, w2 = jnp.exp2(m0 - mx), jnp.exp2(m1 - mx), jnp.exp2(m2 - mx)
        num = w0 * oscr[0, rs, :] + w1 * oscr[1, rs, :] + w2 * oscr[2, rs, :]
        den = w0 * dscr[0, rs, :] + w1 * dscr[1, rs, :] + w2 * dscr[2, rs, :]
        o_ref[rs, :] = (num / den).astype(o_ref.dtype)
        return carry

    lax.fori_loop(0, sb // ct, combine, 0)


def _attention(aq, ak, av, bias, bsz, seq):
    n, width = aq.shape
    sb = ATTN_SUPER
    nsb = seq // sb
    pairs = width // LANES
    npat = len(DILATIONS)
    blk = pl.BlockSpec((sb, LANES), lambda b, h, t: (b * nsb + t, h))
    return pl.pallas_call(
        _attn_kernel,
        grid=(bsz, pairs, nsb),
        in_specs=[blk, blk, blk,
                  pl.BlockSpec((2, npat, 2, ATTN_BLOCK, 2 * ATTN_BLOCK), lambda b, h, t: (0, 0, h, 0, 0))],
        out_specs=blk,
        out_shape=jax.ShapeDtypeStruct((n, width), BF16),
        scratch_shapes=[
            pltpu.VMEM((npat, max(DILATIONS) * ATTN_BLOCK, LANES), BF16),
            pltpu.VMEM((npat, max(DILATIONS) * ATTN_BLOCK, LANES), BF16),
            pltpu.VMEM((ATTN_BLOCK, 4 * ATTN_BLOCK), BF16),
            pltpu.VMEM((4 * ATTN_BLOCK, 2 * LANES), BF16),
            pltpu.VMEM((npat, sb, LANES), F32),
            pltpu.VMEM((npat, sb, LANES), F32),
            pltpu.VMEM((npat, sb, LANES), F32),
        ],
        compiler_params=_params(("parallel", "parallel", "arbitrary")),
        name="dilated_attn",
    )(aq, ak, av, bias)


def _hgrn_consts():
    c = HGRN_CHUNK
    t = np.arange(c)[:, None]
    u = np.arange(c)[None, :]
    low, masks = [], []
    m = c // 2
    while m >= 1:
        mid = (t // (2 * m)) * (2 * m) + m
        is_q = t >= mid
        if m in (4, 2):
            low.append((is_q & (u >= mid) & (u <= t)) | (~is_q & (u > t) & (u < mid)))
        masks.append(((t // (2 * m)) == (u // (2 * m))) & (t % (2 * m) >= m) & (u % (2 * m) < m))
        m //= 2
    masks.append(t == u)
    f = lambda a: np.asarray(a, np.float32)
    return (jnp.asarray(f(u <= t), BF16), jnp.asarray(f(np.concatenate(low, axis=0)), BF16),
            jnp.asarray(f(np.broadcast_to(t % 2 == 1, (c, c)))), jnp.asarray(f(np.stack(masks))))


def _hgrn_kernel(q_ref, z_ref, v_ref, g_ref, lb_ref, gn_ref, tri_ref, low_ref, odd_ref, msk_ref, o_ref, st):
    c = HGRN_CHUNK

    @pl.when(pl.program_id(2) == 0)
    def _():
        st[...] = jnp.zeros_like(st)

    lb = jnp.maximum(lb_ref[...], 0.0)
    log_lb = jnp.log(lb)
    log_1m = jnp.log1p(-lb)
    one_m = 1.0 - lb
    gn = gn_ref[...]
    n_levels = msk_ref.shape[0] - 1
    coarse = [c >> (i + 1) for i in range(n_levels) if (c >> (i + 1)) >= 8]

    def chunk(ci, carry):
        rs = pl.ds(pl.multiple_of(ci * c, c), c)
        z = z_ref[rs, :]
        qb = q_ref[rs, :]
        vb = v_ref[rs, :]
        ez = jnp.exp(-jnp.abs(z))
        ls = jnp.minimum(z, 0.0) - jnp.log(1.0 + ez)
        b2 = log_1m + ls
        lf = jnp.maximum(log_lb, b2) + jnp.log(1.0 + jnp.exp(-jnp.abs(log_lb - b2)))
        rz = 1.0 / (1.0 + ez)
        kb = (one_m * jnp.where(z > 0.0, ez * rz, rz)).astype(BF16)
        l2 = lf * LOG2E
        hi = l2.astype(BF16)
        hl = jnp.concatenate([hi, (l2 - hi.astype(F32)).astype(BF16)], axis=1)
        bb = jnp.dot(tri_ref[...], hl, preferred_element_type=F32)
        b = bb[:, :HGRN_EXPAND] + bb[:, HGRN_EXPAND:]
        dl = jnp.dot(low_ref[...], hl, preferred_element_type=F32)
        dl = dl[:, :HGRN_EXPAND] + dl[:, HGRN_EXPAND:]

        def level(d):
            e = jnp.exp2(d).astype(BF16)
            return _dot_nt(qb * e, kb * e)

        a = msk_ref[n_levels] * _dot_nt(qb, kb)
        for i in range(n_levels):
            m = c >> (i + 1)
            if m >= 8:
                parts = []
                for r0 in range(0, c, 2 * m):
                    mid = b[r0 + m - 1:r0 + m, :]
                    parts += [mid - b[r0:r0 + m, :], b[r0 + m:r0 + 2 * m, :] - mid]
                d = jnp.concatenate(parts, axis=0)
            elif m > 1:
                d = dl[(i - len(coarse)) * c:(i - len(coarse) + 1) * c]
            else:
                d = l2 * odd_ref[...]
            a = a + msk_ref[i] * level(d)
        b_last = b[c - 1:c, :]
        s_t = st[...]
        o = _dot_nt(qb * jnp.exp2(b).astype(BF16), s_t.astype(BF16)) + jnp.dot(a.astype(BF16), vb, preferred_element_type=F32)
        st[...] = jnp.exp2(b_last) * s_t + _dot_tn(vb, kb * jnp.exp2(b_last - b).astype(BF16))
        g = g_ref[rs, :]
        o_ref[rs, :] = (_rms(o, gn) * (g * jax.nn.sigmoid(g))).astype(o_ref.dtype)
        return carry

    lax.fori_loop(0, q_ref.shape[0] // c, chunk, 0, unroll=HGRN_UNROLL)


def _hgrn(rq, rf, ri, rg, lb, gn, consts, bsz, seq):
    n, width = rq.shape
    heads = width // HGRN_EXPAND
    ts = HGRN_TILE
    nt = seq // ts
    blk = pl.BlockSpec((ts, HGRN_EXPAND), lambda b, h, t: (b * nt + t, h))
    vec = pl.BlockSpec((1, HGRN_EXPAND), lambda b, h, t: (0, h))
    const = lambda a: pl.BlockSpec(a.shape, lambda b, h, t: (0,) * a.ndim)
    return pl.pallas_call(
        _hgrn_kernel,
        grid=(bsz, heads, nt),
        in_specs=[blk, blk, blk, blk, vec, vec] + [const(a) for a in consts],
        out_specs=blk,
        out_shape=jax.ShapeDtypeStruct((n, width), BF16),
        scratch_shapes=[pltpu.VMEM((HGRN_EXPAND, HGRN_EXPAND), F32)],
        compiler_params=_params(("parallel", "parallel", "arbitrary")),
        name="hgrn2",
    )(rq, rf, ri, rg, lb.reshape(1, width), gn.reshape(1, width), *consts)


def _route_math(lg):
    row = lax.broadcasted_iota(jnp.int32, (EXPERTS_PER_GROUP, lg.shape[1]), 0).astype(F32)
    none = float(EXPERTS_PER_GROUP)

    def first_argmax(vals):
        mx = jnp.max(vals, axis=0, keepdims=True)
        return mx, jnp.min(jnp.where(vals == mx, row, none), axis=0, keepdims=True)

    g = lg[0:EXPERTS_PER_GROUP, :]
    is_group = row < N_GROUPS
    gmax, gidx = first_argmax(jnp.where(is_group, g, -jnp.inf))
    gprob = 1.0 / jnp.sum(jnp.where(is_group, jnp.exp(g - gmax), 0.0), axis=0, keepdims=True)
    el = lg[EXPERTS_PER_GROUP:2 * EXPERTS_PER_GROUP, :]
    for k in range(1, N_GROUPS):
        el = jnp.where(gidx == k, lg[(k + 1) * EXPERTS_PER_GROUP:(k + 2) * EXPERTS_PER_GROUP, :], el)
    t1, i1 = first_argmax(el)
    t2, i2 = first_argmax(jnp.where(row == i1, -jnp.inf, el))
    ex = jnp.exp(t2 - t1)
    base = gidx * EXPERTS_PER_GROUP
    eids = jnp.concatenate([base + i1, base + i2], axis=0).astype(jnp.int32)
    gates = jnp.concatenate([gprob / (1.0 + ex), gprob * ex / (1.0 + ex)], axis=0)
    return eids, gates


def _out_kernel(a_ref, r_ref, x_ref, mod_ref, ag_ref, g2_ref, wo_ref, wr_ref, br_ref, xo_ref, ha_ref, hb_ref, e_ref, gt_ref, cnt_ref):
    mod = mod_ref[0]
    mix = jnp.concatenate([_rms(a_ref[...].astype(F32), ag_ref[...]).astype(BF16), r_ref[...]], axis=1)
    xn = x_ref[...] + mod[2:3, :] * jnp.dot(mix, wo_ref[0].astype(BF16), preferred_element_type=F32)
    xo_ref[...] = xn
    h2 = _rms(xn, g2_ref[...]) * (1.0 + mod[4:5, :]) + mod[3:4, :]
    ha_ref[...], hb_ref[...] = _pack_rows(h2)
    eids, gates = _route_math(_dot_nt(wr_ref[...], h2.astype(BF16)) + br_ref[...])
    e_ref[...] = eids
    eio = lax.broadcasted_iota(jnp.int32, (N_EXPERTS, eids.shape[1]), 0)
    hit = jnp.logical_or(eids[0:1, :] == eio, eids[1:2, :] == eio).astype(F32)

    @pl.when(pl.program_id(0) == 0)
    def _():
        cnt_ref[...] = jnp.zeros_like(cnt_ref)

    cnt_ref[...] += jnp.sum(hit, axis=1, keepdims=True)
    pad = jnp.zeros((LANES - TOP_K, LANES), F32)
    for k in range(h2.shape[0] // LANES):
        tile = jnp.concatenate([gates[:, k * LANES:(k + 1) * LANES], pad], axis=0)
        gt_ref[k * LANES:(k + 1) * LANES, :] = tile.T


def _out_proj(attn, rec, x2, mod_l, ag, g2, w_out, layer, wr_bf, br, seq):
    n, d = x2.shape
    half = attn.shape[1]
    tm = ROW_TILE
    per_batch = seq // tm
    row = lambda w: pl.BlockSpec((tm, w), lambda i: (i, 0))
    full = lambda a: pl.BlockSpec(a.shape, lambda i: (0,) * a.ndim, pipeline_mode=pl.Buffered(1))
    ag2, g22 = ag.reshape(1, half), g2.reshape(1, d)
    return pl.pallas_call(
        _out_kernel,
        grid=(n // tm,),
        in_specs=[row(half), row(half), row(d), pl.BlockSpec((1, 6, d), lambda i: (i // per_batch, 0, 0)),
                  full(ag2), full(g22), pl.BlockSpec((1, d, d), lambda i: (layer, 0, 0), pipeline_mode=pl.Buffered(1)),
                  full(wr_bf), full(br)],
        out_specs=[row(d), row(PACK_WIDTH), row(PACK_WIDTH), pl.BlockSpec((TOP_K, tm), lambda i: (0, i)), row(LANES),
                   pl.BlockSpec((N_EXPERTS, LANES), lambda i: (0, 0))],
        out_shape=[jax.ShapeDtypeStruct((n, d), F32), jax.ShapeDtypeStruct((n, PACK_WIDTH), U32),
                   jax.ShapeDtypeStruct((n, PACK_WIDTH), U32), jax.ShapeDtypeStruct((TOP_K, n), jnp.int32),
                   jax.ShapeDtypeStruct((n, LANES), F32), jax.ShapeDtypeStruct((N_EXPERTS, LANES), F32)],
        compiler_params=_params(("arbitrary",)),
        name="out_proj",
    )(attn, rec, x2, mod_l, ag2, g22, w_out, wr_bf, br)


def _plan_kernel(e_ref, cnt_ref, ut_ref, pos_ref, be_ref, bv_ref, bf_ref, bn_ref, nu_ref, off, run):
    i = pl.program_id(0)
    t = e_ref.shape[1]
    eio = lax.broadcasted_iota(jnp.int32, (N_EXPERTS, t), 0)
    oh1 = e_ref[0:1, :] == eio
    oh2 = e_ref[1:2, :] == eio
    oh = jnp.logical_or(oh1, oh2).astype(F32)

    @pl.when(i == 0)
    def _():
        counts = cnt_ref[...]
        nb = jnp.floor((counts + (MOE_TILE - 1)) * (1.0 / MOE_TILE))
        er = lax.broadcasted_iota(jnp.int32, (N_EXPERTS, N_EXPERTS), 0)
        ec = lax.broadcasted_iota(jnp.int32, (N_EXPERTS, N_EXPERTS), 1)
        cum_excl = jnp.dot((ec < er).astype(BF16), nb.astype(BF16), preferred_element_type=F32)
        off[...] = cum_excl * MOE_TILE
        run[...] = jnp.zeros_like(run)
        ce, ci, cn = cum_excl[:, 0:1], (cum_excl + nb)[:, 0:1], counts[:, 0:1]
        b = lax.broadcasted_iota(jnp.int32, (N_EXPERTS, be_ref.shape[1]), 1).astype(F32)
        member = jnp.logical_and(b >= ce, b < ci)
        be = jnp.minimum(jnp.sum((b >= ci).astype(F32), axis=0, keepdims=True), N_EXPERTS - 1.0)
        bv = jnp.sum(jnp.where(member, jnp.minimum(cn - (b - ce) * MOE_TILE, float(MOE_TILE)), 0.0), axis=0, keepdims=True)
        be_ref[...] = be.astype(jnp.int32)
        bv_ref[...] = bv.astype(jnp.int32)
        has = nb[:, 0:1] > 0.0
        eid = lax.broadcasted_iota(jnp.int32, b.shape, 0).astype(F32)
        bf_ref[...] = jnp.sum(jnp.where(jnp.logical_and(b == ce, has), 1.0, 0.0), axis=0, keepdims=True).astype(jnp.int32)
        nxt = jnp.min(jnp.where(jnp.logical_and(eid > be, has), eid, float(N_EXPERTS)), axis=0, keepdims=True)
        bn_ref[...] = jnp.where(nxt < N_EXPERTS, nxt, -1.0).astype(jnp.int32)
        nu_ref[...] = jnp.broadcast_to(ci[N_EXPERTS - 1:N_EXPERTS, :], nu_ref.shape).astype(jnp.int32)

    c = jnp.dot(oh.astype(BF16), ut_ref[...], preferred_element_type=F32)
    r = off[:, 0:1] + run[:, 0:1] + c - 1.0
    pos_ref[0:1, :] = jnp.sum(jnp.where(oh1, r, 0.0), axis=0, keepdims=True).astype(jnp.int32)
    pos_ref[1:2, :] = jnp.sum(jnp.where(oh2, r, 0.0), axis=0, keepdims=True).astype(jnp.int32)
    run[...] += jnp.sum(oh, axis=1, keepdims=True)


def _plan(expert_ids_t, counts, n_rows):
    n = expert_ids_t.shape[1]
    t = PLAN_TILE
    n_blocks = n_rows // MOE_TILE
    assert n_blocks <= PLAN_LANES and TOP_K == 2
    ut = jnp.asarray(np.triu(np.ones((t, t), np.float32)), BF16)
    tab = jax.ShapeDtypeStruct((1, PLAN_LANES), jnp.int32)
    const = lambda shape: pl.BlockSpec(shape, lambda i: (0, 0))
    pos, be, bv, bf, bn, nu = pl.pallas_call(
        _plan_kernel,
        grid=(n // t,),
        in_specs=[pl.BlockSpec((TOP_K, t), lambda i: (0, i)), const((N_EXPERTS, LANES)), const((t, t))],
        out_specs=[pl.BlockSpec((TOP_K, t), lambda i: (0, i))] + [const((1, PLAN_LANES))] * 4 + [const((1, LANES))],
        out_shape=[jax.ShapeDtypeStruct((TOP_K, n), jnp.int32), tab, tab, tab, tab, jax.ShapeDtypeStruct((1, LANES), jnp.int32)],
        scratch_shapes=[pltpu.VMEM((N_EXPERTS, LANES), F32)] * 2,
        compiler_params=_params(("arbitrary",)),
        name="plan",
    )(expert_ids_t, counts, ut)
    return pos, (be.reshape(-1), bv.reshape(-1), bf.reshape(-1), bn.reshape(-1), nu.reshape(-1)[:1])


def _sc_mesh():
    return plsc.VectorSubcoreMesh(core_axis_name="c", subcore_axis_name="s")


def _sc_scatter_rows(x, pos, n_rows):
    n, d = x.shape
    w = SC_WINDOW

    @functools.partial(pl.kernel, out_type=jax.ShapeDtypeStruct((n_rows, d), x.dtype), mesh=_sc_mesh(), scratch_types=[])
    def scatter(x_hbm, p_hbm, o_hbm):
        def body(x_vmem, i_vmem):
            for k in range(TOP_K):
                pltpu.sync_copy(x_vmem, o_hbm.at[i_vmem.at[k]])

        pltpu.emit_pipeline(
            body,
            grid=(n // w,),
            in_specs=[pl.BlockSpec((w, d), lambda i: (i, 0)), pl.BlockSpec((TOP_K, w), lambda i: (0, i))],
            out_specs=[],
            core_axis_name=("c", "s"),
            dimension_semantics=(pltpu.PARALLEL,),
        )(x_hbm, p_hbm)

    return scatter(x, pos)


def _sc_gather_rows(y, pos):
    m = pos.shape[0] * pos.shape[1]
    d = y.shape[1]
    w = SC_WINDOW

    @functools.partial(pl.kernel, out_type=jax.ShapeDtypeStruct((m, d), y.dtype), mesh=_sc_mesh(), scratch_types=[])
    def gather(y_hbm, p_hbm, o_hbm):
        def body(i_vmem, o_vmem):
            pltpu.sync_copy(y_hbm.at[i_vmem.at[0]], o_vmem)

        pltpu.emit_pipeline(
            body,
            grid=(m // w,),
            in_specs=[pl.BlockSpec((1, w), lambda i: (0, i))],
            out_specs=[pl.BlockSpec((w, d), lambda i: (i, 0))],
            core_axis_name=("c", "s"),
            dimension_semantics=(pltpu.PARALLEL,),
        )(p_hbm, o_hbm)

    return gather(y, pos.reshape(1, m))


def _expert_kernel(be_ref, bv_ref, bf_ref, bn_ref, nu_ref, xa_ref, xb_ref, w1_hbm, w3_hbm, w2_hbm, ya_ref, yb_ref,
                   wf1, wf3, wf2, w1b, w3b, w2b, wsem, par, *, layer):
    i = pl.program_id(0)
    tb = xa_ref.shape[0]
    half = tb // 2

    def fetch(e, slot):
        return (pltpu.make_async_copy(w1_hbm.at[layer, e], wf1.at[slot], wsem.at[slot, 0]),
                pltpu.make_async_copy(w3_hbm.at[layer, e], wf3.at[slot], wsem.at[slot, 1]),
                pltpu.make_async_copy(w2_hbm.at[layer, e], wf2.at[slot], wsem.at[slot, 2]))

    def ffn(rows):
        live = lax.broadcasted_iota(jnp.int32, (rows, 1), 0) < bv_ref[i]
        x = jnp.where(live, _unpack_rows(xa_ref[0:rows, :], xb_ref[0:rows, :]), 0.0).astype(BF16)
        y = None
        hc = w1b.shape[1] // FFN_SPLIT
        for c in range(FFN_SPLIT):
            cs = slice(c * hc, (c + 1) * hc)
            a = jnp.dot(x, w1b[:, cs], preferred_element_type=F32)
            b = jnp.dot(x, w3b[:, cs], preferred_element_type=F32)
            hm = (a * jax.nn.sigmoid(a) * b).astype(BF16)
            part = jnp.dot(hm, w2b[cs, :], preferred_element_type=F32)
            y = part if y is None else y + part
        ya_ref[0:rows, :], yb_ref[0:rows, :] = _pack_rows(y)

    @pl.when(i == 0)
    def _():
        par[0] = 0
        for c in fetch(be_ref[0], 0):
            c.start()

    @pl.when(i < nu_ref[0])
    def _():
        @pl.when(bf_ref[i] == 1)
        def _():
            slot = par[0]
            for c in fetch(be_ref[i], slot):
                c.wait()
            w1b[...] = wf1[slot].astype(BF16)
            w3b[...] = wf3[slot].astype(BF16)
            w2b[...] = wf2[slot].astype(BF16)

            @pl.when(bn_ref[i] >= 0)
            def _():
                for c in fetch(bn_ref[i], 1 - slot):
                    c.start()

            par[0] = 1 - slot

        @pl.when(bv_ref[i] > half)
        def _():
            ffn(tb)

        @pl.when(bv_ref[i] <= half)
        def _():
            ffn(half)
            ya_ref[half:, :] = jnp.zeros((tb - half, ya_ref.shape[1]), ya_ref.dtype)
            yb_ref[half:, :] = jnp.zeros((tb - half, yb_ref.shape[1]), yb_ref.dtype)

    @pl.when(i >= nu_ref[0])
    def _():
        ya_ref[...] = jnp.zeros_like(ya_ref)
        yb_ref[...] = jnp.zeros_like(yb_ref)


def _experts(xa, xb, w1, w3, w2, layer, tables):
    n_rows = xa.shape[0]
    d, de = w1.shape[-2:]
    tb = MOE_TILE
    xspec = pl.BlockSpec((tb, PACK_WIDTH), lambda i, be, bv, bf, bn, nu: (jnp.minimum(i, nu[0] - 1), 0))
    yspec = pl.BlockSpec((tb, PACK_WIDTH), lambda i, be, bv, bf, bn, nu: (i, 0))
    anyspec = pl.BlockSpec(memory_space=pl.ANY)
    grid_spec = pltpu.PrefetchScalarGridSpec(
        num_scalar_prefetch=5,
        grid=(n_rows // tb,),
        in_specs=[xspec, xspec, anyspec, anyspec, anyspec],
        out_specs=[yspec, yspec],
        scratch_shapes=[pltpu.VMEM((2, d, de), F32), pltpu.VMEM((2, d, de), F32), pltpu.VMEM((2, de, d), F32),
                        pltpu.VMEM((d, de), BF16), pltpu.VMEM((d, de), BF16), pltpu.VMEM((de, d), BF16),
                        pltpu.SemaphoreType.DMA((2, 3)), pltpu.SMEM((1,), jnp.int32)],
    )
    rows = jax.ShapeDtypeStruct((n_rows, PACK_WIDTH), U32)
    return pl.pallas_call(
        functools.partial(_expert_kernel, layer=layer),
        grid_spec=grid_spec,
        out_shape=[rows, rows],
        compiler_params=_params(("arbitrary",)),
        name="experts",
    )(*tables, xa, xb, w1, w3, w2)


def _final_kernel(ga0, ga1, gb0, gb1, gt_ref, mod_ref, x_ref, fg_ref, o_ref):
    o_ref[...] = _rms(_moe_residual(x_ref, ga0, ga1, gb0, gb1, gt_ref, mod_ref[0][5:6, :]), fg_ref[...])


def _final(x2, ga, gb, gates, mod_l, final_g, seq):
    n, d = x2.shape
    tm = ROW_TILE
    per_batch = seq // tm
    row = lambda w: pl.BlockSpec((tm, w), lambda i: (i, 0))
    second = pl.BlockSpec((tm, PACK_WIDTH), lambda i: (i + n // tm, 0))
    return pl.pallas_call(
        _final_kernel,
        grid=(n // tm,),
        in_specs=[row(PACK_WIDTH), second, row(PACK_WIDTH), second, row(LANES),
                  pl.BlockSpec((1, 6, d), lambda i: (i // per_batch, 0, 0)), row(d), pl.BlockSpec((1, d), lambda i: (0, 0))],
        out_specs=row(d),
        out_shape=jax.ShapeDtypeStruct((n, d), F32),
        compiler_params=_params(("parallel",)),
        name="final_norm",
    )(ga, ga, gb, gb, gates, mod_l, x2, final_g.reshape(1, d))


def kernel(x, c, w_ada, b_ada, norm1_g, w_in, attn_norm_g, hgrn_lb_logits, hgrn_norm_g, w_out, norm2_g, router_group_w, router_group_b, router_expert_w, router_expert_b, moe_w1, moe_w3, moe_w2, final_g):
    bsz, seq, d = x.shape
    depth = w_ada.shape[0]
    n = bsz * seq
    attn_dim = attn_norm_g.shape[1]
    n_heads = attn_dim // ATTN_HEAD_DIM

    lb_w = jax.nn.softmax(hgrn_lb_logits.astype(F32), axis=0)
    lower_bounds = jnp.cumsum(lb_w, axis=0) - lb_w[0:1]
    mod = _adaln(c, w_ada, b_ada)
    bias = _attn_bias(n_heads)
    hconsts = _hgrn_consts()
    n_rows = n * TOP_K + N_EXPERTS * MOE_TILE

    x2 = x.reshape(n, d)
    moe = None
    for layer in range(depth):
        x2, (aq, ak, av, rq, rf, ri, rg) = _in_proj(x2, mod[layer], norm1_g[layer], w_in, layer, seq, moe)
        attn = _attention(aq, ak, av, bias, bsz, seq)
        rec = _hgrn(rq, rf, ri, rg, lower_bounds[layer], hgrn_norm_g[layer], hconsts, bsz, seq)
        gpad = EXPERTS_PER_GROUP - N_GROUPS
        wr = jnp.concatenate([router_group_w[layer].T, jnp.zeros((gpad, d), F32), router_expert_w[layer].T], axis=0)
        br = jnp.concatenate([router_group_b[layer], jnp.zeros((gpad,), F32), router_expert_b[layer]]).reshape(-1, 1)
        x2, ha, hb, eids, gates, counts = _out_proj(attn, rec, x2, mod[layer], attn_norm_g[layer], norm2_g[layer],
                                            w_out, layer, wr.astype(BF16), br, seq)
        pos, tables = _plan(eids, counts, n_rows)
        ya, yb = _experts(_sc_scatter_rows(ha, pos, n_rows), _sc_scatter_rows(hb, pos, n_rows),
                          moe_w1, moe_w3, moe_w2, layer, tables)
        moe = (_sc_gather_rows(ya, pos), _sc_gather_rows(yb, pos), gates, mod[layer])
    return _final(x2, moe[0], moe[1], moe[2], moe[3], final_g, seq).reshape(bsz, seq, d)
```

```python
import functools

import numpy as np
import jax
import jax.numpy as jnp
from jax import lax
from jax.experimental import pallas as pl
from jax.experimental.pallas import tpu as pltpu
from jax.experimental.pallas import tpu_sc as plsc

F32 = jnp.float32
BF16 = jnp.bfloat16
U32 = jnp.uint32

ATTN_HEAD_DIM = 64
ATTN_BLOCK = 128
DILATIONS = (1, 4, 16)
ALIBI_MAX_EXP = 8.0
HGRN_EXPAND = 128
HGRN_CHUNK = 128
N_GROUPS = 4
EXPERTS_PER_GROUP = 8
N_EXPERTS = N_GROUPS * EXPERTS_PER_GROUP
TOP_K = 2
NORM_EPS = 1e-6
LOG2E = 1.4426950408889634

LANES = 128
ATTN_SUPER = ATTN_BLOCK * max(DILATIONS)
ROW_TILE = 512
IN_GROUPS = 8
ATTN_UNROLL = 16
HGRN_TILE = 2048
HGRN_UNROLL = 16
MOE_TILE = 512
FFN_SPLIT = 2
PLAN_TILE = 2048
PLAN_LANES = 256
PACK_WIDTH = 256
SC_WINDOW = 128
VMEM_LIMIT = 56 << 20


def _params(semantics):
    return pltpu.CompilerParams(dimension_semantics=semantics, vmem_limit_bytes=VMEM_LIMIT)


def _rms(x, g):
    return x * lax.rsqrt(jnp.mean(x * x, axis=-1, keepdims=True) + NORM_EPS) * g


def _dot_nt(a, b):
    return lax.dot_general(a, b, (((1,), (1,)), ((), ())), preferred_element_type=F32)


def _dot_tn(a, b):
    return lax.dot_general(a, b, (((0,), (0,)), ((), ())), preferred_element_type=F32)


def _pack_pair(hi, lo):
    hb = lax.bitcast_convert_type(hi.astype(BF16).astype(F32), U32)
    lb = lax.bitcast_convert_type(lo.astype(BF16).astype(F32), U32)
    return hb | (lb >> 16)


def _unpack_pair(u):
    hi = lax.bitcast_convert_type(u & jnp.uint32(0xFFFF0000), F32)
    lo = lax.bitcast_convert_type(u << 16, F32)
    return hi, lo


def _pack_rows(y):
    w = PACK_WIDTH
    return _pack_pair(y[:, 0:w], y[:, 2 * w:3 * w]), _pack_pair(y[:, w:2 * w], y[:, 3 * w:4 * w])


def _unpack_rows(ua, ub):
    ha, la = _unpack_pair(ua)
    hb, lb = _unpack_pair(ub)
    return jnp.concatenate([ha, hb, la, lb], axis=1)


def _moe_residual(x_ref, ga0, ga1, gb0, gb1, gt_ref, gate_row):
    gt = gt_ref[...]
    y = gt[:, 0:1] * _unpack_rows(ga0[...], gb0[...]) + gt[:, 1:2] * _unpack_rows(ga1[...], gb1[...])
    return x_ref[...] + gate_row * y


def _ada_kernel(c_ref, w_ref, b_ref, o_ref):
    c = c_ref[...]
    ca = c * jax.nn.sigmoid(c)
    hi = ca.astype(BF16)
    lo = (ca - hi.astype(F32)).astype(BF16)
    w = w_ref[0].astype(BF16)
    o_ref[0] = (jnp.dot(hi, w, preferred_element_type=F32) + jnp.dot(lo, w, preferred_element_type=F32)) + b_ref[0]


def _adaln(c, w_ada, b_ada):
    depth, d, n6 = w_ada.shape
    bsz = c.shape[0]
    rows = 8
    cp = jnp.zeros((rows, d), F32).at[:bsz].set(c)
    tn = 1536
    out = pl.pallas_call(
        _ada_kernel,
        grid=(depth, n6 // tn),
        in_specs=[
            pl.BlockSpec((rows, d), lambda l, j: (0, 0)),
            pl.BlockSpec((1, d, tn), lambda l, j: (l, 0, j)),
            pl.BlockSpec((1, 1, tn), lambda l, j: (l, 0, j)),
        ],
        out_specs=pl.BlockSpec((1, rows, tn), lambda l, j: (l, 0, j)),
        out_shape=jax.ShapeDtypeStruct((depth, rows, n6), F32),
        compiler_params=_params(("parallel", "parallel")),
        name="adaln",
    )(cp, w_ada, b_ada.reshape(depth, 1, n6))
    return out[:, :bsz].reshape(depth, bsz, 6, d)


def _in_kernel(*refs, n_moe):
    moe, (x_ref, mod_ref, g_ref, w_ref), rest = refs[:n_moe], refs[n_moe:n_moe + 4], refs[n_moe + 4:]
    hb = rest[-1]
    i = pl.program_id(0)

    @pl.when(i == 0)
    def _():
        hb[1] = jnp.zeros(hb.shape[1:], hb.dtype)

    outs = rest[1:-1] if n_moe else rest[:-1]
    mod = mod_ref[0]
    prev = hb[(i + 1) % 2]
    width = outs[0].shape[1]
    tm = x_ref.shape[0]
    groups = [slice(r, r + tm // IN_GROUPS) for r in range(0, tm, tm // IN_GROUPS)]

    def prologue(rs):
        if n_moe:
            ga0, ga1, gb0, gb1, gt_ref, pmod_ref = moe
            gt = gt_ref[rs, :]
            y = gt[:, 0:1] * _unpack_rows(ga0[rs, :], gb0[rs, :]) + gt[:, 1:2] * _unpack_rows(ga1[rs, :], gb1[rs, :])
            x = x_ref[rs, :] + pmod_ref[0][5:6, :] * y
            rest[0][rs, :] = x
        else:
            x = x_ref[rs, :]
        hb[i % 2, rs, :] = (_rms(x, g_ref[...]) * (1.0 + mod[1:2, :]) + mod[0:1, :]).astype(BF16)

    per = -(-len(groups) // len(outs))
    for k, o in enumerate(outs):
        for rs in groups[k * per:(k + 1) * per]:
            prologue(rs)
        w = w_ref[0, :, k * width:(k + 1) * width].astype(BF16)
        o[...] = jnp.dot(prev, w, preferred_element_type=F32).astype(o.dtype)


def _in_proj(x2, mod_l, g, w_in, layer, seq, moe=None):
    n, d = x2.shape
    n_out = w_in.shape[2]
    width = 512
    tm = ROW_TILE
    per_batch = seq // tm
    last = n // tm - 1
    cur = lambda i: jnp.minimum(i, last)
    row = lambda w: pl.BlockSpec((tm, w), lambda i: (cur(i), 0))
    mod_spec = pl.BlockSpec((1, 6, d), lambda i: (cur(i) // per_batch, 0, 0))
    in_specs, args, out_specs, out_shape = [], [], [], []
    if moe is not None:
        ga, gb, gates, mod_prev = moe
        second = pl.BlockSpec((tm, PACK_WIDTH), lambda i: (cur(i) + n // tm, 0))
        in_specs += [row(PACK_WIDTH), second, row(PACK_WIDTH), second, row(LANES), mod_spec]
        args += [ga, ga, gb, gb, gates, mod_prev]
        out_specs.append(row(d))
        out_shape.append(jax.ShapeDtypeStruct((n, d), F32))
    in_specs += [row(d), mod_spec, pl.BlockSpec((1, d), lambda i: (0, 0)),
                 pl.BlockSpec((1, d, n_out), lambda i: (layer, 0, 0), pipeline_mode=pl.Buffered(1))]
    args += [x2, mod_l, g.reshape(1, d), w_in]
    out_dtypes = (F32, F32, F32, BF16, F32, BF16, F32)
    out_specs += [pl.BlockSpec((tm, width), lambda i: (jnp.maximum(i - 1, 0), 0))] * len(out_dtypes)
    out_shape += [jax.ShapeDtypeStruct((n, width), dt) for dt in out_dtypes]
    outs = pl.pallas_call(
        functools.partial(_in_kernel, n_moe=0 if moe is None else 6),
        grid=(n // tm + 1,),
        in_specs=in_specs,
        out_specs=out_specs,
        out_shape=out_shape,
        scratch_shapes=[pltpu.VMEM((2, tm, d), BF16)],
        compiler_params=_params(("arbitrary",)),
        name="in_proj",
    )(*args)
    return (x2, outs) if moe is None else (outs[0], outs[1:])


def _attn_bias(n_heads):
    qi = np.arange(ATTN_BLOCK)[:, None]
    kj = np.arange(2 * ATTN_BLOCK)[None, :]
    dist = ATTN_BLOCK + qi - kj
    valid = (dist >= 0) & (dist <= ATTN_BLOCK)
    slopes = np.exp2(-ALIBI_MAX_EXP * np.arange(1, n_heads + 1, dtype=np.float32) / n_heads)
    out = np.empty((2, len(DILATIONS), n_heads, ATTN_BLOCK, 2 * ATTN_BLOCK), np.float32)
    for p, dil in enumerate(DILATIONS):
        b = -slopes[:, None, None] * (dist * dil).astype(np.float32)[None] * np.float32(LOG2E)
        out[0, p] = np.where(valid[None], b, -np.inf)
        out[1, p] = np.where((valid & (kj >= ATTN_BLOCK))[None], b, -np.inf)
    return jnp.asarray(out)


def _attn_kernel(q_ref, k_ref, v_ref, bias_ref, o_ref, kcar, vcar, pbuf, vbuf, oscr, mscr, dscr):
    n = pl.program_id(2)
    sb = q_ref.shape[0]
    is_lo = lax.broadcasted_iota(jnp.int32, (ATTN_BLOCK, LANES), 1) < ATTN_HEAD_DIM
    scale = ATTN_HEAD_DIM ** -0.5 * LOG2E

    @pl.when(n == 0)
    def _():
        kcar[...] = jnp.zeros_like(kcar)
        vcar[...] = jnp.zeros_like(vcar)

    pbuf[...] = jnp.zeros_like(pbuf)
    vbuf[...] = jnp.zeros_like(vbuf)
    lo2 = lax.broadcasted_iota(jnp.int32, (2 * ATTN_BLOCK, LANES), 1) < ATTN_HEAD_DIM
    zero = jnp.zeros((2 * ATTN_BLOCK, LANES), BF16)
    den_lo = lo2.astype(BF16)
    den_hi = jnp.logical_not(lo2).astype(BF16)

    for p, dil in enumerate(DILATIONS):
        nb = (sb // ATTN_BLOCK) // dil

        def rows_of(it, dil=dil, nb=nb):
            start = (it % nb) * (ATTN_BLOCK * dil) + it // nb
            return pl.ds(start, ATTN_BLOCK, stride=dil) if dil > 1 else pl.ds(pl.multiple_of(start, ATTN_BLOCK), ATTN_BLOCK)

        def pv(it, probs, vaug, p=p, rows_of=rows_of):
            rows = rows_of(it)
            of = jnp.dot(probs, vaug, preferred_element_type=F32)
            oscr[p, rows, :] = of[:, :LANES]
            dscr[p, rows, :] = of[:, LANES:]

        def qk(it, p=p, nb=nb, rows_of=rows_of):
            r = it // nb
            rows = rows_of(it)
            crow = pl.ds(pl.multiple_of(r * ATTN_BLOCK, ATTN_BLOCK), ATTN_BLOCK)
            qb = q_ref[rows, :] * scale
            kc = k_ref[rows, :].astype(BF16)
            vc = v_ref[rows, :].astype(BF16)
            kcat = jnp.concatenate([kcar[p, crow, :], kc], axis=0)
            vcat = jnp.concatenate([vcar[p, crow, :], vc], axis=0)
            vaug = jnp.concatenate([jnp.concatenate([jnp.where(lo2, vcat, zero), den_lo], axis=1),
                                    jnp.concatenate([jnp.where(lo2, zero, vcat), den_hi], axis=1)], axis=0)
            kcar[p, crow, :] = kc
            vcar[p, crow, :] = vc
            first = jnp.logical_and(n == 0, it % nb == 0).astype(jnp.int32)
            probs, acc_m = [], None
            for hh in range(2):
                sel = is_lo if hh == 0 else jnp.logical_not(is_lo)
                qm = jnp.where(sel, qb, 0.0).astype(BF16)
                s = _dot_nt(qm, kcat) + bias_ref[first, p, hh]
                m = jnp.max(s, axis=-1, keepdims=True)
                probs.append(jnp.exp2(s - m).astype(BF16))
                mb = jnp.broadcast_to(m, (ATTN_BLOCK, LANES))
                acc_m = mb if hh == 0 else jnp.where(sel, mb, acc_m)
            mscr[p, rows, :] = acc_m
            return jnp.concatenate(probs, axis=1), vaug

        def body(g, carry, pv=pv, qk=qk):
            it0 = g * ATTN_UNROLL
            pend = (pbuf[...], vbuf[...])
            for u in range(ATTN_UNROLL):
                cur = qk(it0 + u)
                pv(jnp.maximum(it0 + u - 1, 0), *pend)
                pend = cur
            pbuf[...] = pend[0]
            vbuf[...] = pend[1]
            return carry

        n_it = sb // ATTN_BLOCK
        lax.fori_loop(0, n_it // ATTN_UNROLL, body, 0)
        pv(n_it - 1, pbuf[...], vbuf[...])

    ct = 256

    def combine(t, carry):
        rs = pl.ds(pl.multiple_of(t * ct, ct), ct)
        m0, m1, m2 = mscr[0, rs, :], mscr[1, rs, :], mscr[2, rs, :]
        mx = jnp.maximum(jnp.maximum(m0, m1), m2)
        w0, w1, w2 = jnp.exp2(m0 - mx), jnp.exp2(m1 - mx), jnp.exp2(m2 - mx)
        num = w0 * oscr[0, rs, :] + w1 * oscr[1, rs, :] + w2 * oscr[2, rs, :]
        den = w0 * dscr[0, rs, :] + w1 * dscr[1, rs, :] + w2 * dscr[2, rs, :]
        o_ref[rs, :] = (num / den).astype(o_ref.dtype)
        return carry

    lax.fori_loop(0, sb // ct, combine, 0)


def _attention(aq, ak, av, bias, bsz, seq):
    n, width = aq.shape
    sb = ATTN_SUPER
    nsb = seq // sb
    pairs = width // LANES
    npat = len(DILATIONS)
    blk = pl.BlockSpec((sb, LANES), lambda b, h, t: (b * nsb + t, h))
    return pl.pallas_call(
        _attn_kernel,
        grid=(bsz, pairs, nsb),
        in_specs=[blk, blk, blk,
                  pl.BlockSpec((2, npat, 2, ATTN_BLOCK, 2 * ATTN_BLOCK), lambda b, h, t: (0, 0, h, 0, 0))],
        out_specs=blk,
        out_shape=jax.ShapeDtypeStruct((n, width), BF16),
        scratch_shapes=[
            pltpu.VMEM((npat, max(DILATIONS) * ATTN_BLOCK, LANES), BF16),
            pltpu.VMEM((npat, max(DILATIONS) * ATTN_BLOCK, LANES), BF16),
            pltpu.VMEM((ATTN_BLOCK, 4 * ATTN_BLOCK), BF16),
            pltpu.VMEM((4 * ATTN_BLOCK, 2 * LANES), BF16),
            pltpu.VMEM((npat, sb, LANES), F32),
            pltpu.VMEM((npat, sb, LANES), F32),
            pltpu.VMEM((npat, sb, LANES), F32),
        ],
        compiler_params=_params(("parallel", "parallel", "arbitrary")),
        name="dilated_attn",
    )(aq, ak, av, bias)


def _hgrn_consts():
    c = HGRN_CHUNK
    t = np.arange(c)[:, None]
    u = np.arange(c)[None, :]
    low, masks = [], []
    m = c // 2
    while m >= 1:
        mid = (t // (2 * m)) * (2 * m) + m
        is_q = t >= mid
        if m in (4, 2):
            low.append((is_q & (u >= mid) & (u <= t)) | (~is_q & (u > t) & (u < mid)))
        masks.append(((t // (2 * m)) == (u // (2 * m))) & (t % (2 * m) >= m) & (u % (2 * m) < m))
        m //= 2
    masks.append(t == u)
    f = lambda a: np.asarray(a, np.float32)
    return (jnp.asarray(f(u <= t), BF16), jnp.asarray(f(np.concatenate(low, axis=0)), BF16),
            jnp.asarray(f(np.broadcast_to(t % 2 == 1, (c, c)))), jnp.asarray(f(np.stack(masks))))


def _hgrn_kernel(q_ref, z_ref, v_ref, g_ref, lb_ref, gn_ref, tri_ref, low_ref, odd_ref, msk_ref, o_ref, st):
    c = HGRN_CHUNK

    @pl.when(pl.program_id(2) == 0)
    def _():
        st[...] = jnp.zeros_like(st)

    lb = jnp.maximum(lb_ref[...], 0.0)
    log_lb = jnp.log(lb)
    log_1m = jnp.log1p(-lb)
    one_m = 1.0 - lb
    gn = gn_ref[...]
    n_levels = msk_ref.shape[0] - 1
    coarse = [c >> (i + 1) for i in range(n_levels) if (c >> (i + 1)) >= 8]

    def chunk(ci, carry):
        rs = pl.ds(pl.multiple_of(ci * c, c), c)
        z = z_ref[rs, :]
        qb = q_ref[rs, :]
        vb = v_ref[rs, :]
        ez = jnp.exp(-jnp.abs(z))
        ls = jnp.minimum(z, 0.0) - jnp.log(1.0 + ez)
        b2 = log_1m + ls
        lf = jnp.maximum(log_lb, b2) + jnp.log(1.0 + jnp.exp(-jnp.abs(log_lb - b2)))
        rz = 1.0 / (1.0 + ez)
        kb = (one_m * jnp.where(z > 0.0, ez * rz, rz)).astype(BF16)
        l2 = lf * LOG2E
        hi = l2.astype(BF16)
        hl = jnp.concatenate([hi, (l2 - hi.astype(F32)).astype(BF16)], axis=1)
        bb = jnp.dot(tri_ref[...], hl, preferred_element_type=F32)
        b = bb[:, :HGRN_EXPAND] + bb[:, HGRN_EXPAND:]
        dl = jnp.dot(low_ref[...], hl, preferred_element_type=F32)
        dl = dl[:, :HGRN_EXPAND] + dl[:, HGRN_EXPAND:]

        def level(d):
            e = jnp.exp2(d).astype(BF16)
            return _dot_nt(qb * e, kb * e)

        a = msk_ref[n_levels] * _dot_nt(qb, kb)
        for i in range(n_levels):
            m = c >> (i + 1)
            if m >= 8:
                parts = []
                for r0 in range(0, c, 2 * m):
                    mid = b[r0 + m - 1:r0 + m, :]
                    parts += [mid - b[r0:r0 + m, :], b[r0 + m:r0 + 2 * m, :] - mid]
                d = jnp.concatenate(parts, axis=0)
            elif m > 1:
                d = dl[(i - len(coarse)) * c:(i - len(coarse) + 1) * c]
            else:
                d = l2 * odd_ref[...]
            a = a + msk_ref[i] * level(d)
        b_last = b[c - 1:c, :]
        s_t = st[...]
        o = _dot_nt(qb * jnp.exp2(b).astype(BF16), s_t.astype(BF16)) + jnp.dot(a.astype(BF16), vb, preferred_element_type=F32)
        st[...] = jnp.exp2(b_last) * s_t + _dot_tn(vb, kb * jnp.exp2(b_last - b).astype(BF16))
        g = g_ref[rs, :]
        o_ref[rs, :] = (_rms(o, gn) * (g * jax.nn.sigmoid(g))).astype(o_ref.dtype)
        return carry

    lax.fori_loop(0, q_ref.shape[0] // c, chunk, 0, unroll=HGRN_UNROLL)


def _hgrn(rq, rf, ri, rg, lb, gn, consts, bsz, seq):
    n, width = rq.shape
    heads = width // HGRN_EXPAND
    ts = HGRN_TILE
    nt = seq // ts
    blk = pl.BlockSpec((ts, HGRN_EXPAND), lambda b, h, t: (b * nt + t, h))
    vec = pl.BlockSpec((1, HGRN_EXPAND), lambda b, h, t: (0, h))
    const = lambda a: pl.BlockSpec(a.shape, lambda b, h, t: (0,) * a.ndim)
    return pl.pallas_call(
        _hgrn_kernel,
        grid=(bsz, heads, nt),
        in_specs=[blk, blk, blk, blk, vec, vec] + [const(a) for a in consts],
        out_specs=blk,
        out_shape=jax.ShapeDtypeStruct((n, width), BF16),
        scratch_shapes=[pltpu.VMEM((HGRN_EXPAND, HGRN_EXPAND), F32)],
        compiler_params=_params(("parallel", "parallel", "arbitrary")),
        name="hgrn2",
    )(rq, rf, ri, rg, lb.reshape(1, width), gn.reshape(1, width), *consts)


def _route_math(lg):
    row = lax.broadcasted_iota(jnp.int32, (EXPERTS_PER_GROUP, lg.shape[1]), 0).astype(F32)
    none = float(EXPERTS_PER_GROUP)

    def first_argmax(vals):
        mx = jnp.max(vals, axis=0, keepdims=True)
        return mx, jnp.min(jnp.where(vals == mx, row, none), axis=0, keepdims=True)

    g = lg[0:EXPERTS_PER_GROUP, :]
    is_group = row < N_GROUPS
    gmax, gidx = first_argmax(jnp.where(is_group, g, -jnp.inf))
    gprob = 1.0 / jnp.sum(jnp.where(is_group, jnp.exp(g - gmax), 0.0), axis=0, keepdims=True)
    el = lg[EXPERTS_PER_GROUP:2 * EXPERTS_PER_GROUP, :]
    for k in range(1, N_GROUPS):
        el = jnp.where(gidx == k, lg[(k + 1) * EXPERTS_PER_GROUP:(k + 2) * EXPERTS_PER_GROUP, :], el)
    t1, i1 = first_argmax(el)
    t2, i2 = first_argmax(jnp.where(row == i1, -jnp.inf, el))
    ex = jnp.exp(t2 - t1)
    base = gidx * EXPERTS_PER_GROUP
    eids = jnp.concatenate([base + i1, base + i2], axis=0).astype(jnp.int32)
    gates = jnp.concatenate([gprob / (1.0 + ex), gprob * ex / (1.0 + ex)], axis=0)
    return eids, gates


def _out_kernel(a_ref, r_ref, x_ref, mod_ref, ag_ref, g2_ref, wo_ref, wr_ref, br_ref, xo_ref, ha_ref, hb_ref, e_ref, gt_ref, cnt_ref):
    mod = mod_ref[0]
    mix = jnp.concatenate([_rms(a_ref[...].astype(F32), ag_ref[...]).astype(BF16), r_ref[...]], axis=1)
    xn = x_ref[...] + mod[2:3, :] * jnp.dot(mix, wo_ref[0].astype(BF16), preferred_element_type=F32)
    xo_ref[...] = xn
    h2 = _rms(xn, g2_ref[...]) * (1.0 + mod[4:5, :]) + mod[3:4, :]
    ha_ref[...], hb_ref[...] = _pack_rows(h2)
    eids, gates = _route_math(_dot_nt(wr_ref[...], h2.astype(BF16)) + br_ref[...])
    e_ref[...] = eids
    eio = lax.broadcasted_iota(jnp.int32, (N_EXPERTS, eids.shape[1]), 0)
    hit = jnp.logical_or(eids[0:1, :] == eio, eids[1:2, :] == eio).astype(F32)

    @pl.when(pl.program_id(0) == 0)
    def _():
        cnt_ref[...] = jnp.zeros_like(cnt_ref)

    cnt_ref[...] += jnp.sum(hit, axis=1, keepdims=True)
    pad = jnp.zeros((LANES - TOP_K, LANES), F32)
    for k in range(h2.shape[0] // LANES):
        tile = jnp.concatenate([gates[:, k * LANES:(k + 1) * LANES], pad], axis=0)
        gt_ref[k * LANES:(k + 1) * LANES, :] = tile.T


def _out_proj(attn, rec, x2, mod_l, ag, g2, w_out, layer, wr_bf, br, seq):
    n, d = x2.shape
    half = attn.shape[1]
    tm = ROW_TILE
    per_batch = seq // tm
    row = lambda w: pl.BlockSpec((tm, w), lambda i: (i, 0))
    full = lambda a: pl.BlockSpec(a.shape, lambda i: (0,) * a.ndim, pipeline_mode=pl.Buffered(1))
    ag2, g22 = ag.reshape(1, half), g2.reshape(1, d)
    return pl.pallas_call(
        _out_kernel,
        grid=(n // tm,),
        in_specs=[row(half), row(half), row(d), pl.BlockSpec((1, 6, d), lambda i: (i // per_batch, 0, 0)),
                  full(ag2), full(g22), pl.BlockSpec((1, d, d), lambda i: (layer, 0, 0), pipeline_mode=pl.Buffered(1)),
                  full(wr_bf), full(br)],
        out_specs=[row(d), row(PACK_WIDTH), row(PACK_WIDTH), pl.BlockSpec((TOP_K, tm), lambda i: (0, i)), row(LANES),
                   pl.BlockSpec((N_EXPERTS, LANES), lambda i: (0, 0))],
        out_shape=[jax.ShapeDtypeStruct((n, d), F32), jax.ShapeDtypeStruct((n, PACK_WIDTH), U32),
                   jax.ShapeDtypeStruct((n, PACK_WIDTH), U32), jax.ShapeDtypeStruct((TOP_K, n), jnp.int32),
                   jax.ShapeDtypeStruct((n, LANES), F32), jax.ShapeDtypeStruct((N_EXPERTS, LANES), F32)],
        compiler_params=_params(("arbitrary",)),
        name="out_proj",
    )(attn, rec, x2, mod_l, ag2, g22, w_out, wr_bf, br)


def _plan_kernel(e_ref, cnt_ref, ut_ref, pos_ref, be_ref, bv_ref, bf_ref, bn_ref, nu_ref, off, run):
    i = pl.program_id(0)
    t = e_ref.shape[1]
    eio = lax.broadcasted_iota(jnp.int32, (N_EXPERTS, t), 0)
    oh1 = e_ref[0:1, :] == eio
    oh2 = e_ref[1:2, :] == eio
    oh = jnp.logical_or(oh1, oh2).astype(F32)

    @pl.when(i == 0)
    def _():
        counts = cnt_ref[...]
        nb = jnp.floor((counts + (MOE_TILE - 1)) * (1.0 / MOE_TILE))
        er = lax.broadcasted_iota(jnp.int32, (N_EXPERTS, N_EXPERTS), 0)
        ec = lax.broadcasted_iota(jnp.int32, (N_EXPERTS, N_EXPERTS), 1)
        cum_excl = jnp.dot((ec < er).astype(BF16), nb.astype(BF16), preferred_element_type=F32)
        off[...] = cum_excl * MOE_TILE
        run[...] = jnp.zeros_like(run)
        ce, ci, cn = cum_excl[:, 0:1], (cum_excl + nb)[:, 0:1], counts[:, 0:1]
        b = lax.broadcasted_iota(jnp.int32, (N_EXPERTS, be_ref.shape[1]), 1).astype(F32)
        member = jnp.logical_and(b >= ce, b < ci)
        be = jnp.minimum(jnp.sum((b >= ci).astype(F32), axis=0, keepdims=True), N_EXPERTS - 1.0)
        bv = jnp.sum(jnp.where(member, jnp.minimum(cn - (b - ce) * MOE_TILE, float(MOE_TILE)), 0.0), axis=0, keepdims=True)
        be_ref[...] = be.astype(jnp.int32)
        bv_ref[...] = bv.astype(jnp.int32)
        has = nb[:, 0:1] > 0.0
        eid = lax.broadcasted_iota(jnp.int32, b.shape, 0).astype(F32)
        bf_ref[...] = jnp.sum(jnp.where(jnp.logical_and(b == ce, has), 1.0, 0.0), axis=0, keepdims=True).astype(jnp.int32)
        nxt = jnp.min(jnp.where(jnp.logical_and(eid > be, has), eid, float(N_EXPERTS)), axis=0, keepdims=True)
        bn_ref[...] = jnp.where(nxt < N_EXPERTS, nxt, -1.0).astype(jnp.int32)
        nu_ref[...] = jnp.broadcast_to(ci[N_EXPERTS - 1:N_EXPERTS, :], nu_ref.shape).astype(jnp.int32)

    c = jnp.dot(oh.astype(BF16), ut_ref[...], preferred_element_type=F32)
    r = off[:, 0:1] + run[:, 0:1] + c - 1.0
    pos_ref[0:1, :] = jnp.sum(jnp.where(oh1, r, 0.0), axis=0, keepdims=True).astype(jnp.int32)
    pos_ref[1:2, :] = jnp.sum(jnp.where(oh2, r, 0.0), axis=0, keepdims=True).astype(jnp.int32)
    run[...] += jnp.sum(oh, axis=1, keepdims=True)


def _plan(expert_ids_t, counts, n_rows):
    n = expert_ids_t.shape[1]
    t = PLAN_TILE
    n_blocks = n_rows // MOE_TILE
    assert n_blocks <= PLAN_LANES and TOP_K == 2
    ut = jnp.asarray(np.triu(np.ones((t, t), np.float32)), BF16)
    tab = jax.ShapeDtypeStruct((1, PLAN_LANES), jnp.int32)
    const = lambda shape: pl.BlockSpec(shape, lambda i: (0, 0))
    pos, be, bv, bf, bn, nu = pl.pallas_call(
        _plan_kernel,
        grid=(n // t,),
        in_specs=[pl.BlockSpec((TOP_K, t), lambda i: (0, i)), const((N_EXPERTS, LANES)), const((t, t))],
        out_specs=[pl.BlockSpec((TOP_K, t), lambda i: (0, i))] + [const((1, PLAN_LANES))] * 4 + [const((1, LANES))],
        out_shape=[jax.ShapeDtypeStruct((TOP_K, n), jnp.int32), tab, tab, tab, tab, jax.ShapeDtypeStruct((1, LANES), jnp.int32)],
        scratch_shapes=[pltpu.VMEM((N_EXPERTS, LANES), F32)] * 2,
        compiler_params=_params(("arbitrary",)),
        name="plan",
    )(expert_ids_t, counts, ut)
    return pos, (be.reshape(-1), bv.reshape(-1), bf.reshape(-1), bn.reshape(-1), nu.reshape(-1)[:1])


def _sc_mesh():
    return plsc.VectorSubcoreMesh(core_axis_name="c", subcore_axis_name="s")


def _sc_scatter_rows(x, pos, n_rows):
    n, d = x.shape
    w = SC_WINDOW

    @functools.partial(pl.kernel, out_type=jax.ShapeDtypeStruct((n_rows, d), x.dtype), mesh=_sc_mesh(), scratch_types=[])
    def scatter(x_hbm, p_hbm, o_hbm):
        def body(x_vmem, i_vmem):
            for k in range(TOP_K):
                pltpu.sync_copy(x_vmem, o_hbm.at[i_vmem.at[k]])

        pltpu.emit_pipeline(
            body,
            grid=(n // w,),
            in_specs=[pl.BlockSpec((w, d), lambda i: (i, 0)), pl.BlockSpec((TOP_K, w), lambda i: (0, i))],
            out_specs=[],
            core_axis_name=("c", "s"),
            dimension_semantics=(pltpu.PARALLEL,),
        )(x_hbm, p_hbm)

    return scatter(x, pos)


def _sc_gather_rows(y, pos):
    m = pos.shape[0] * pos.shape[1]
    d = y.shape[1]
    w = SC_WINDOW

    @functools.partial(pl.kernel, out_type=jax.ShapeDtypeStruct((m, d), y.dtype), mesh=_sc_mesh(), scratch_types=[])
    def gather(y_hbm, p_hbm, o_hbm):
        def body(i_vmem, o_vmem):
            pltpu.sync_copy(y_hbm.at[i_vmem.at[0]], o_vmem)

        pltpu.emit_pipeline(
            body,
            grid=(m // w,),
            in_specs=[pl.BlockSpec((1, w), lambda i: (0, i))],
            out_specs=[pl.BlockSpec((w, d), lambda i: (i, 0))],
            core_axis_name=("c", "s"),
            dimension_semantics=(pltpu.PARALLEL,),
        )(p_hbm, o_hbm)

    return gather(y, pos.reshape(1, m))


def _expert_kernel(be_ref, bv_ref, bf_ref, bn_ref, nu_ref, xa_ref, xb_ref, w1_hbm, w3_hbm, w2_hbm, ya_ref, yb_ref,
                   wf1, wf3, wf2, w1b, w3b, w2b, wsem, par, *, layer):
    i = pl.program_id(0)
    tb = xa_ref.shape[0]
    half = tb // 2

    def fetch(e, slot):
        return (pltpu.make_async_copy(w1_hbm.at[layer, e], wf1.at[slot], wsem.at[slot, 0]),
                pltpu.make_async_copy(w3_hbm.at[layer, e], wf3.at[slot], wsem.at[slot, 1]),
                pltpu.make_async_copy(w2_hbm.at[layer, e], wf2.at[slot], wsem.at[slot, 2]))

    def ffn(rows):
        live = lax.broadcasted_iota(jnp.int32, (rows, 1), 0) < bv_ref[i]
        x = jnp.where(live, _unpack_rows(xa_ref[0:rows, :], xb_ref[0:rows, :]), 0.0).astype(BF16)
        y = None
        hc = w1b.shape[1] // FFN_SPLIT
        for c in range(FFN_SPLIT):
            cs = slice(c * hc, (c + 1) * hc)
            a = jnp.dot(x, w1b[:, cs], preferred_element_type=F32)
            b = jnp.dot(x, w3b[:, cs], preferred_element_type=F32)
            hm = (a * jax.nn.sigmoid(a) * b).astype(BF16)
            part = jnp.dot(hm, w2b[cs, :], preferred_element_type=F32)
            y = part if y is None else y + part
        ya_ref[0:rows, :], yb_ref[0:rows, :] = _pack_rows(y)

    @pl.when(i == 0)
    def _():
        par[0] = 0
        for c in fetch(be_ref[0], 0):
            c.start()

    @pl.when(i < nu_ref[0])
    def _():
        @pl.when(bf_ref[i] == 1)
        def _():
            slot = par[0]
            for c in fetch(be_ref[i], slot):
                c.wait()
            w1b[...] = wf1[slot].astype(BF16)
            w3b[...] = wf3[slot].astype(BF16)
            w2b[...] = wf2[slot].astype(BF16)

            @pl.when(bn_ref[i] >= 0)
            def _():
                for c in fetch(bn_ref[i], 1 - slot):
                    c.start()

            par[0] = 1 - slot

        @pl.when(bv_ref[i] > half)
        def _():
            ffn(tb)

        @pl.when(bv_ref[i] <= half)
        def _():
            ffn(half)
            ya_ref[half:, :] = jnp.zeros((tb - half, ya_ref.shape[1]), ya_ref.dtype)
            yb_ref[half:, :] = jnp.zeros((tb - half, yb_ref.shape[1]), yb_ref.dtype)

    @pl.when(i >= nu_ref[0])
    def _():
        ya_ref[...] = jnp.zeros_like(ya_ref)
        yb_ref[...] = jnp.zeros_like(yb_ref)


def _experts(xa, xb, w1, w3, w2, layer, tables):
    n_rows = xa.shape[0]
    d, de = w1.shape[-2:]
    tb = MOE_TILE
    xspec = pl.BlockSpec((tb, PACK_WIDTH), lambda i, be, bv, bf, bn, nu: (jnp.minimum(i, nu[0] - 1), 0))
    yspec = pl.BlockSpec((tb, PACK_WIDTH), lambda i, be, bv, bf, bn, nu: (i, 0))
    anyspec = pl.BlockSpec(memory_space=pl.ANY)
    grid_spec = pltpu.PrefetchScalarGridSpec(
        num_scalar_prefetch=5,
        grid=(n_rows // tb,),
        in_specs=[xspec, xspec, anyspec, anyspec, anyspec],
        out_specs=[yspec, yspec],
        scratch_shapes=[pltpu.VMEM((2, d, de), F32), pltpu.VMEM((2, d, de), F32), pltpu.VMEM((2, de, d), F32),
                        pltpu.VMEM((d, de), BF16), pltpu.VMEM((d, de), BF16), pltpu.VMEM((de, d), BF16),
                        pltpu.SemaphoreType.DMA((2, 3)), pltpu.SMEM((1,), jnp.int32)],
    )
    rows = jax.ShapeDtypeStruct((n_rows, PACK_WIDTH), U32)
    return pl.pallas_call(
        functools.partial(_expert_kernel, layer=layer),
        grid_spec=grid_spec,
        out_shape=[rows, rows],
        compiler_params=_params(("arbitrary",)),
        name="experts",
    )(*tables, xa, xb, w1, w3, w2)


def _final_kernel(ga0, ga1, gb0, gb1, gt_ref, mod_ref, x_ref, fg_ref, o_ref):
    o_ref[...] = _rms(_moe_residual(x_ref, ga0, ga1, gb0, gb1, gt_ref, mod_ref[0][5:6, :]), fg_ref[...])


def _final(x2, ga, gb, gates, mod_l, final_g, seq):
    n, d = x2.shape
    tm = ROW_TILE
    per_batch = seq // tm
    row = lambda w: pl.BlockSpec((tm, w), lambda i: (i, 0))
    second = pl.BlockSpec((tm, PACK_WIDTH), lambda i: (i + n // tm, 0))
    return pl.pallas_call(
        _final_kernel,
        grid=(n // tm,),
        in_specs=[row(PACK_WIDTH), second, row(PACK_WIDTH), second, row(LANES),
                  pl.BlockSpec((1, 6, d), lambda i: (i // per_batch, 0, 0)), row(d), pl.BlockSpec((1, d), lambda i: (0, 0))],
        out_specs=row(d),
        out_shape=jax.ShapeDtypeStruct((n, d), F32),
        compiler_params=_params(("parallel",)),
        name="final_norm",
    )(ga, ga, gb, gb, gates, mod_l, x2, final_g.reshape(1, d))


def kernel(x, c, w_ada, b_ada, norm1_g, w_in, attn_norm_g, hgrn_lb_logits, hgrn_norm_g, w_out, norm2_g, router_group_w, router_group_b, router_expert_w, router_expert_b, moe_w1, moe_w3, moe_w2, final_g):
    bsz, seq, d = x.shape
    depth = w_ada.shape[0]
    n = bsz * seq
    attn_dim = attn_norm_g.shape[1]
    n_heads = attn_dim // ATTN_HEAD_DIM

    lb_w = jax.nn.softmax(hgrn_lb_logits.astype(F32), axis=0)
    lower_bounds = jnp.cumsum(lb_w, axis=0) - lb_w[0:1]
    mod = _adaln(c, w_ada, b_ada)
    bias = _attn_bias(n_heads)
    hconsts = _hgrn_consts()
    n_rows = n * TOP_K + N_EXPERTS * MOE_TILE

    x2 = x.reshape(n, d)
    moe = None
    for layer in range(depth):
        x2, (aq, ak, av, rq, rf, ri, rg) = _in_proj(x2, mod[layer], norm1_g[layer], w_in, layer, seq, moe)
        attn = _attention(aq, ak, av, bias, bsz, seq)
        rec = _hgrn(rq, rf, ri, rg, lower_bounds[layer], hgrn_norm_g[layer], hconsts, bsz, seq)
        gpad = EXPERTS_PER_GROUP - N_GROUPS
        wr = jnp.concatenate([router_group_w[layer].T, jnp.zeros((gpad, d), F32), router_expert_w[layer].T], axis=0)
        br = jnp.concatenate([router_group_b[layer], jnp.zeros((gpad,), F32), router_expert_b[layer]]).reshape(-1, 1)
        x2, ha, hb, eids, gates, counts = _out_proj(attn, rec, x2, mod[layer], attn_norm_g[layer], norm2_g[layer],
                                            w_out, layer, wr.astype(BF16), br, seq)
        pos, tables = _plan(eids, counts, n_rows)
        ya, yb = _experts(_sc_scatter_rows(ha, pos, n_rows), _sc_scatter_rows(hb, pos, n_rows),
                          moe_w1, moe_w3, moe_w2, layer, tables)
        moe = (_sc_gather_rows(ya, pos), _sc_gather_rows(yb, pos), gates, mod[layer])
    return _final(x2, moe[0], moe[1], moe[2], moe[3], final_g, seq).reshape(bsz, seq, d)
```

```python
import functools

import numpy as np
import jax
import jax.numpy as jnp
from jax import lax
from jax.experimental import pallas as pl
from jax.experimental.pallas import tpu as pltpu
from jax.experimental.pallas import tpu_sc as plsc

F32 = jnp.float32
BF16 = jnp.bfloat16
U32 = jnp.uint32

ATTN_HEAD_DIM = 64
ATTN_BLOCK = 128
DILATIONS = (1, 4, 16)
ALIBI_MAX_EXP = 8.0
HGRN_EXPAND = 128
HGRN_CHUNK = 128
N_GROUPS = 4
EXPERTS_PER_GROUP = 8
N_EXPERTS = N_GROUPS * EXPERTS_PER_GROUP
TOP_K = 2
NORM_EPS = 1e-6
LOG2E = 1.4426950408889634

LANES = 128
ATTN_SUPER = ATTN_BLOCK * max(DILATIONS)
ROW_TILE = 512
IN_GROUPS = 8
ATTN_UNROLL = 16
HGRN_TILE = 2048
HGRN_UNROLL = 8
MOE_TILE = 512
FFN_SPLIT = 2
PLAN_TILE = 2048
PLAN_LANES = 256
PACK_WIDTH = 256
SC_WINDOW = 128
VMEM_LIMIT = 56 << 20


def _params(semantics):
    return pltpu.CompilerParams(dimension_semantics=semantics, vmem_limit_bytes=VMEM_LIMIT)


def _rms(x, g):
    return x * lax.rsqrt(jnp.mean(x * x, axis=-1, keepdims=True) + NORM_EPS) * g


def _loop(trips, body):
    if trips == 1:
        body(0, 0)
    else:
        lax.fori_loop(0, trips, body, 0)


def _round_robin(gens):
    live = list(gens)
    while live:
        for gen in list(live):
            try:
                next(gen)
            except StopIteration:
                live.remove(gen)


def _dot_nt(a, b):
    return lax.dot_general(a, b, (((1,), (1,)), ((), ())), preferred_element_type=F32)


def _dot_tn(a, b):
    return lax.dot_general(a, b, (((0,), (0,)), ((), ())), preferred_element_type=F32)


def _pack_pair(hi, lo):
    hb = lax.bitcast_convert_type(hi.astype(BF16).astype(F32), U32)
    lb = lax.bitcast_convert_type(lo.astype(BF16).astype(F32), U32)
    return hb | (lb >> 16)


def _unpack_pair(u):
    hi = lax.bitcast_convert_type(u & jnp.uint32(0xFFFF0000), F32)
    lo = lax.bitcast_convert_type(u << 16, F32)
    return hi, lo


def _pack_rows(y):
    w = PACK_WIDTH
    return _pack_pair(y[:, 0:w], y[:, 2 * w:3 * w]), _pack_pair(y[:, w:2 * w], y[:, 3 * w:4 * w])


def _unpack_rows(ua, ub):
    ha, la = _unpack_pair(ua)
    hb, lb = _unpack_pair(ub)
    return jnp.concatenate([ha, hb, la, lb], axis=1)


def _moe_residual(x_ref, ga0, ga1, gb0, gb1, gt_ref, gate_row):
    gt = gt_ref[...]
    y = gt[:, 0:1] * _unpack_rows(ga0[...], gb0[...]) + gt[:, 1:2] * _unpack_rows(ga1[...], gb1[...])
    return x_ref[...] + gate_row * y


def _ada_kernel(c_ref, w_ref, b_ref, o_ref):
    c = c_ref[...]
    ca = c * jax.nn.sigmoid(c)
    hi = ca.astype(BF16)
    lo = (ca - hi.astype(F32)).astype(BF16)
    w = w_ref[0].astype(BF16)
    o_ref[0] = (jnp.dot(hi, w, preferred_element_type=F32) + jnp.dot(lo, w, preferred_element_type=F32)) + b_ref[0]


def _adaln(c, w_ada, b_ada):
    depth, d, n6 = w_ada.shape
    bsz = c.shape[0]
    rows = 8
    cp = jnp.zeros((rows, d), F32).at[:bsz].set(c)
    tn = 1536
    out = pl.pallas_call(
        _ada_kernel,
        grid=(depth, n6 // tn),
        in_specs=[
            pl.BlockSpec((rows, d), lambda l, j: (0, 0)),
            pl.BlockSpec((1, d, tn), lambda l, j: (l, 0, j)),
            pl.BlockSpec((1, 1, tn), lambda l, j: (l, 0, j)),
        ],
        out_specs=pl.BlockSpec((1, rows, tn), lambda l, j: (l, 0, j)),
        out_shape=jax.ShapeDtypeStruct((depth, rows, n6), F32),
        compiler_params=_params(("parallel", "parallel")),
        name="adaln",
    )(cp, w_ada, b_ada.reshape(depth, 1, n6))
    return out[:, :bsz].reshape(depth, bsz, 6, d)


def _in_kernel(*refs, n_moe):
    moe, (x_ref, mod_ref, g_ref, w_ref), rest = refs[:n_moe], refs[n_moe:n_moe + 4], refs[n_moe + 4:]
    hb = rest[-1]
    i = pl.program_id(0)

    @pl.when(i == 0)
    def _():
        hb[1] = jnp.zeros(hb.shape[1:], hb.dtype)

    outs = rest[1:-1] if n_moe else rest[:-1]
    mod = mod_ref[0]
    prev = hb[(i + 1) % 2]
    width = outs[0].shape[1]
    tm = x_ref.shape[0]
    groups = [slice(r, r + tm // IN_GROUPS) for r in range(0, tm, tm // IN_GROUPS)]

    def prologue(rs):
        if n_moe:
            ga0, ga1, gb0, gb1, gt_ref, pmod_ref = moe
            gt = gt_ref[rs, :]
            y = gt[:, 0:1] * _unpack_rows(ga0[rs, :], gb0[rs, :]) + gt[:, 1:2] * _unpack_rows(ga1[rs, :], gb1[rs, :])
            x = x_ref[rs, :] + pmod_ref[0][5:6, :] * y
            rest[0][rs, :] = x
        else:
            x = x_ref[rs, :]
        hb[i % 2, rs, :] = (_rms(x, g_ref[...]) * (1.0 + mod[1:2, :]) + mod[0:1, :]).astype(BF16)

    per = -(-len(groups) // len(outs))
    for k, o in enumerate(outs):
        for rs in groups[k * per:(k + 1) * per]:
            prologue(rs)
        w = w_ref[0, :, k * width:(k + 1) * width].astype(BF16)
        o[...] = jnp.dot(prev, w, preferred_element_type=F32).astype(o.dtype)


def _in_proj(x2, mod_l, g, w_in, layer, seq, moe=None):
    n, d = x2.shape
    n_out = w_in.shape[2]
    width = 512
    tm = ROW_TILE
    per_batch = seq // tm
    last = n // tm - 1
    cur = lambda i: jnp.minimum(i, last)
    row = lambda w: pl.BlockSpec((tm, w), lambda i: (cur(i), 0))
    mod_spec = pl.BlockSpec((1, 6, d), lambda i: (cur(i) // per_batch, 0, 0))
    in_specs, args, out_specs, out_shape = [], [], [], []
    if moe is not None:
        ga, gb, gates, mod_prev = moe
        second = pl.BlockSpec((tm, PACK_WIDTH), lambda i: (cur(i) + n // tm, 0))
        in_specs += [row(PACK_WIDTH), second, row(PACK_WIDTH), second, row(LANES), mod_spec]
        args += [ga, ga, gb, gb, gates, mod_prev]
        out_specs.append(row(d))
        out_shape.append(jax.ShapeDtypeStruct((n, d), F32))
    in_specs += [row(d), mod_spec, pl.BlockSpec((1, d), lambda i: (0, 0)),
                 pl.BlockSpec((1, d, n_out), lambda i: (layer, 0, 0), pipeline_mode=pl.Buffered(1))]
    args += [x2, mod_l, g.reshape(1, d), w_in]
    out_dtypes = (F32, F32, F32, BF16, F32, BF16, F32)
    out_specs += [pl.BlockSpec((tm, width), lambda i: (jnp.maximum(i - 1, 0), 0))] * len(out_dtypes)
    out_shape += [jax.ShapeDtypeStruct((n, width), dt) for dt in out_dtypes]
    outs = pl.pallas_call(
        functools.partial(_in_kernel, n_moe=0 if moe is None else 6),
        grid=(n // tm + 1,),
        in_specs=in_specs,
        out_specs=out_specs,
        out_shape=out_shape,
        scratch_shapes=[pltpu.VMEM((2, tm, d), BF16)],
        compiler_params=_params(("arbitrary",)),
        name="in_proj",
    )(*args)
    return (x2, outs) if moe is None else (outs[0], outs[1:])


def _attn_bias(n_heads):
    qi = np.arange(ATTN_BLOCK)[:, None]
    kj = np.arange(2 * ATTN_BLOCK)[None, :]
    dist = ATTN_BLOCK + qi - kj
    valid = (dist >= 0) & (dist <= ATTN_BLOCK)
    slopes = np.exp2(-ALIBI_MAX_EXP * np.arange(1, n_heads + 1, dtype=np.float32) / n_heads)
    out = np.empty((2, len(DILATIONS), n_heads, ATTN_BLOCK, 2 * ATTN_BLOCK), np.float32)
    for p, dil in enumerate(DILATIONS):
        b = -slopes[:, None, None] * (dist * dil).astype(np.float32)[None] * np.float32(LOG2E)
        out[0, p] = np.where(valid[None], b, -np.inf)
        out[1, p] = np.where((valid & (kj >= ATTN_BLOCK))[None], b, -np.inf)
    return jnp.asarray(out)


def _attn_kernel(q_ref, k_ref, v_ref, bias_ref, o_ref, kcar, vcar, pbuf, vbuf, oscr, mscr, dscr):
    n = pl.program_id(2)
    sb = q_ref.shape[0]
    is_lo = lax.broadcasted_iota(jnp.int32, (ATTN_BLOCK, LANES), 1) < ATTN_HEAD_DIM
    scale = ATTN_HEAD_DIM ** -0.5 * LOG2E

    @pl.when(n == 0)
    def _():
        kcar[...] = jnp.zeros_like(kcar)
        vcar[...] = jnp.zeros_like(vcar)

    pbuf[...] = jnp.zeros_like(pbuf)
    vbuf[...] = jnp.zeros_like(vbuf)
    lo2 = lax.broadcasted_iota(jnp.int32, (2 * ATTN_BLOCK, LANES), 1) < ATTN_HEAD_DIM
    zero = jnp.zeros((2 * ATTN_BLOCK, LANES), BF16)
    den_lo = lo2.astype(BF16)
    den_hi = jnp.logical_not(lo2).astype(BF16)

    for p, dil in enumerate(DILATIONS):
        nb = (sb // ATTN_BLOCK) // dil

        def rows_of(it, dil=dil, nb=nb):
            start = (it % nb) * (ATTN_BLOCK * dil) + it // nb
            return pl.ds(start, ATTN_BLOCK, stride=dil) if dil > 1 else pl.ds(pl.multiple_of(start, ATTN_BLOCK), ATTN_BLOCK)

        def pv(it, probs, vaug, p=p, rows_of=rows_of):
            rows = rows_of(it)
            of = jnp.dot(probs, vaug, preferred_element_type=F32)
            oscr[p, rows, :] = of[:, :LANES]
            dscr[p, rows, :] = of[:, LANES:]

        def qk(it, p=p, nb=nb, rows_of=rows_of):
            r = it // nb
            rows = rows_of(it)
            crow = pl.ds(pl.multiple_of(r * ATTN_BLOCK, ATTN_BLOCK), ATTN_BLOCK)
            qb = q_ref[rows, :] * scale
            kc = k_ref[rows, :].astype(BF16)
            vc = v_ref[rows, :].astype(BF16)
            kcat = jnp.concatenate([kcar[p, crow, :], kc], axis=0)
            vcat = jnp.concatenate([vcar[p, crow, :], vc], axis=0)
            vaug = jnp.concatenate([jnp.concatenate([jnp.where(lo2, vcat, zero), den_lo], axis=1),
                                    jnp.concatenate([jnp.where(lo2, zero, vcat), den_hi], axis=1)], axis=0)
            kcar[p, crow, :] = kc
            vcar[p, crow, :] = vc
            first = jnp.logical_and(n == 0, it % nb == 0).astype(jnp.int32)
            probs, acc_m = [], None
            for hh in range(2):
                sel = is_lo if hh == 0 else jnp.logical_not(is_lo)
                qm = jnp.where(sel, qb, 0.0).astype(BF16)
                s = _dot_nt(qm, kcat) + bias_ref[first, p, hh]
                m = jnp.max(s, axis=-1, keepdims=True)
                probs.append(jnp.exp2(s - m).astype(BF16))
                mb = jnp.broadcast_to(m, (ATTN_BLOCK, LANES))
                acc_m = mb if hh == 0 else jnp.where(sel, mb, acc_m)
            mscr[p, rows, :] = acc_m
            return jnp.concatenate(probs, axis=1), vaug

        def body(g, carry, pv=pv, qk=qk):
            it0 = g * ATTN_UNROLL
            pend = (pbuf[...], vbuf[...])
            for u in range(ATTN_UNROLL):
                cur = qk(it0 + u)
                pv(jnp.maximum(it0 + u - 1, 0), *pend)
                pend = cur
            pbuf[...] = pend[0]
            vbuf[...] = pend[1]
            return carry

        n_it = sb // ATTN_BLOCK
        lax.fori_loop(0, n_it // ATTN_UNROLL, body, 0)
        pv(n_it - 1, pbuf[...], vbuf[...])

    ct = 256

    def combine(t, carry):
        rs = pl.ds(pl.multiple_of(t * ct, ct), ct)
        m0, m1, m2 = mscr[0, rs, :], mscr[1, rs, :], mscr[2, rs, :]
        mx = jnp.maximum(jnp.maximum(m0, m1), m2)
        w0, w1, w2 = jnp.exp2(m0 - mx), jnp.exp2(m1 - mx), jnp.exp2(m2 - mx)
        num = w0 * oscr[0, rs, :] + w1 * oscr[1, rs, :] + w2 * oscr[2, rs, :]
        den = w0 * dscr[0, rs, :] + w1 * dscr[1, rs, :] + w2 * dscr[2, rs, :]
        o_ref[rs, :] = (num / den).astype(o_ref.dtype)
        return carry

    lax.fori_loop(0, sb // ct, combine, 0)


def _attention(aq, ak, av, bias, bsz, seq):
    n, width = aq.shape
    sb = ATTN_SUPER
    nsb = seq // sb
    pairs = width // LANES
    npat = len(DILATIONS)
    blk = pl.BlockSpec((sb, LANES), lambda b, h, t: (b * nsb + t, h))
    return pl.pallas_call(
        _attn_kernel,
        grid=(bsz, pairs, nsb),
        in_specs=[blk, blk, blk,
                  pl.BlockSpec((2, npat, 2, ATTN_BLOCK, 2 * ATTN_BLOCK), lambda b, h, t: (0, 0, h, 0, 0))],
        out_specs=blk,
        out_shape=jax.ShapeDtypeStruct((n, width), BF16),
        scratch_shapes=[
            pltpu.VMEM((npat, max(DILATIONS) * ATTN_BLOCK, LANES), BF16),
            pltpu.VMEM((npat, max(DILATIONS) * ATTN_BLOCK, LANES), BF16),
            pltpu.VMEM((ATTN_BLOCK, 4 * ATTN_BLOCK), BF16),
            pltpu.VMEM((4 * ATTN_BLOCK, 2 * LANES), BF16),
            pltpu.VMEM((npat, sb, LANES), F32),
            pltpu.VMEM((npat, sb, LANES), F32),
            pltpu.VMEM((npat, sb, LANES), F32),
        ],
        compiler_params=_params(("parallel", "parallel", "arbitrary")),
        name="dilated_attn",
    )(aq, ak, av, bias)


def _hgrn_consts():
    c = HGRN_CHUNK
    t = np.arange(c)[:, None]
    u = np.arange(c)[None, :]
    low, masks = [], []
    m = c // 2
    while m >= 1:
        mid = (t // (2 * m)) * (2 * m) + m
        is_q = t >= mid
        if m in (4, 2):
            low.append((is_q & (u >= mid) & (u <= t)) | (~is_q & (u > t) & (u < mid)))
        masks.append(((t // (2 * m)) == (u // (2 * m))) & (t % (2 * m) >= m) & (u % (2 * m) < m))
        m //= 2
    masks.append(t == u)
    f = lambda a: np.asarray(a, np.float32)
    return (jnp.asarray(f(u <= t), BF16), jnp.asarray(f(np.concatenate(low, axis=0)), BF16),
            jnp.asarray(f(np.broadcast_to(t % 2 == 1, (c, c)))), jnp.asarray(f(np.stack(masks))))


def _hgrn_kernel(q_ref, z_ref, v_ref, g_ref, lb_ref, gn_ref, tri_ref, low_ref, odd_ref, msk_ref, o_ref, st):
    c = HGRN_CHUNK

    @pl.when(pl.program_id(2) == 0)
    def _():
        st[...] = jnp.zeros_like(st)

    lb = jnp.maximum(lb_ref[...], 0.0)
    log_lb = jnp.log(lb)
    log_1m = jnp.log1p(-lb)
    one_m = 1.0 - lb
    gn = gn_ref[...]
    n_levels = msk_ref.shape[0] - 1
    coarse = [c >> (i + 1) for i in range(n_levels) if (c >> (i + 1)) >= 8]

    def chunk(ci):
        rs = pl.ds(pl.multiple_of(ci * c, c), c)
        z = z_ref[rs, :]
        qb = q_ref[rs, :]
        vb = v_ref[rs, :]
        ez = jnp.exp(-jnp.abs(z))
        ls = jnp.minimum(z, 0.0) - jnp.log(1.0 + ez)
        b2 = log_1m + ls
        lf = jnp.maximum(log_lb, b2) + jnp.log(1.0 + jnp.exp(-jnp.abs(log_lb - b2)))
        rz = 1.0 / (1.0 + ez)
        kb = (one_m * jnp.where(z > 0.0, ez * rz, rz)).astype(BF16)
        l2 = lf * LOG2E
        hi = l2.astype(BF16)
        hl = jnp.concatenate([hi, (l2 - hi.astype(F32)).astype(BF16)], axis=1)
        yield
        bb = jnp.dot(tri_ref[...], hl, preferred_element_type=F32)
        b = bb[:, :HGRN_EXPAND] + bb[:, HGRN_EXPAND:]
        dl = jnp.dot(low_ref[...], hl, preferred_element_type=F32)
        dl = dl[:, :HGRN_EXPAND] + dl[:, HGRN_EXPAND:]
        yield

        def level(d):
            e = jnp.exp2(d).astype(BF16)
            return _dot_nt(qb * e, kb * e)

        a = msk_ref[n_levels] * _dot_nt(qb, kb)
        for i in range(n_levels):
            m = c >> (i + 1)
            if m >= 8:
                parts = []
                for r0 in range(0, c, 2 * m):
                    mid = b[r0 + m - 1:r0 + m, :]
                    parts += [mid - b[r0:r0 + m, :], b[r0 + m:r0 + 2 * m, :] - mid]
                d = jnp.concatenate(parts, axis=0)
            elif m > 1:
                d = dl[(i - len(coarse)) * c:(i - len(coarse) + 1) * c]
            else:
                d = l2 * odd_ref[...]
            a = a + msk_ref[i] * level(d)
            if i % 3 == 2:
                yield
        b_last = b[c - 1:c, :]
        s_t = st[...]
        o = _dot_nt(qb * jnp.exp2(b).astype(BF16), s_t.astype(BF16)) + jnp.dot(a.astype(BF16), vb, preferred_element_type=F32)
        st[...] = jnp.exp2(b_last) * s_t + _dot_tn(vb, kb * jnp.exp2(b_last - b).astype(BF16))
        yield
        g = g_ref[rs, :]
        o_ref[rs, :] = (_rms(o, gn) * (g * jax.nn.sigmoid(g))).astype(o_ref.dtype)

    def group(gi, carry):
        _round_robin([chunk(gi * HGRN_UNROLL + u) for u in range(HGRN_UNROLL)])
        return carry

    _loop(q_ref.shape[0] // (c * HGRN_UNROLL), group)


def _hgrn(rq, rf, ri, rg, lb, gn, consts, bsz, seq):
    n, width = rq.shape
    heads = width // HGRN_EXPAND
    ts = HGRN_TILE
    nt = seq // ts
    blk = pl.BlockSpec((ts, HGRN_EXPAND), lambda b, h, t: (b * nt + t, h))
    vec = pl.BlockSpec((1, HGRN_EXPAND), lambda b, h, t: (0, h))
    const = lambda a: pl.BlockSpec(a.shape, lambda b, h, t: (0,) * a.ndim)
    return pl.pallas_call(
        _hgrn_kernel,
        grid=(bsz, heads, nt),
        in_specs=[blk, blk, blk, blk, vec, vec] + [const(a) for a in consts],
        out_specs=blk,
        out_shape=jax.ShapeDtypeStruct((n, width), BF16),
        scratch_shapes=[pltpu.VMEM((HGRN_EXPAND, HGRN_EXPAND), F32)],
        compiler_params=_params(("parallel", "parallel", "arbitrary")),
        name="hgrn2",
    )(rq, rf, ri, rg, lb.reshape(1, width), gn.reshape(1, width), *consts)


def _route_math(lg):
    row = lax.broadcasted_iota(jnp.int32, (EXPERTS_PER_GROUP, lg.shape[1]), 0).astype(F32)
    none = float(EXPERTS_PER_GROUP)

    def first_argmax(vals):
        mx = jnp.max(vals, axis=0, keepdims=True)
        return mx, jnp.min(jnp.where(vals == mx, row, none), axis=0, keepdims=True)

    g = lg[0:EXPERTS_PER_GROUP, :]
    is_group = row < N_GROUPS
    gmax, gidx = first_argmax(jnp.where(is_group, g, -jnp.inf))
    gprob = 1.0 / jnp.sum(jnp.where(is_group, jnp.exp(g - gmax), 0.0), axis=0, keepdims=True)
    el = lg[EXPERTS_PER_GROUP:2 * EXPERTS_PER_GROUP, :]
    for k in range(1, N_GROUPS):
        el = jnp.where(gidx == k, lg[(k + 1) * EXPERTS_PER_GROUP:(k + 2) * EXPERTS_PER_GROUP, :], el)
    t1, i1 = first_argmax(el)
    t2, i2 = first_argmax(jnp.where(row == i1, -jnp.inf, el))
    ex = jnp.exp(t2 - t1)
    base = gidx * EXPERTS_PER_GROUP
    eids = jnp.concatenate([base + i1, base + i2], axis=0).astype(jnp.int32)
    gates = jnp.concatenate([gprob / (1.0 + ex), gprob * ex / (1.0 + ex)], axis=0)
    return eids, gates


def _out_kernel(a_ref, r_ref, x_ref, mod_ref, ag_ref, g2_ref, wo_ref, wr_ref, br_ref, xo_ref, ha_ref, hb_ref, e_ref, gt_ref, cnt_ref):
    mod = mod_ref[0]
    mix = jnp.concatenate([_rms(a_ref[...].astype(F32), ag_ref[...]).astype(BF16), r_ref[...]], axis=1)
    xn = x_ref[...] + mod[2:3, :] * jnp.dot(mix, wo_ref[0].astype(BF16), preferred_element_type=F32)
    xo_ref[...] = xn
    h2 = _rms(xn, g2_ref[...]) * (1.0 + mod[4:5, :]) + mod[3:4, :]
    ha_ref[...], hb_ref[...] = _pack_rows(h2)
    eids, gates = _route_math(_dot_nt(wr_ref[...], h2.astype(BF16)) + br_ref[...])
    e_ref[...] = eids
    eio = lax.broadcasted_iota(jnp.int32, (N_EXPERTS, eids.shape[1]), 0)
    hit = jnp.logical_or(eids[0:1, :] == eio, eids[1:2, :] == eio).astype(F32)

    @pl.when(pl.program_id(0) == 0)
    def _():
        cnt_ref[...] = jnp.zeros_like(cnt_ref)

    cnt_ref[...] += jnp.sum(hit, axis=1, keepdims=True)
    pad = jnp.zeros((LANES - TOP_K, LANES), F32)
    for k in range(h2.shape[0] // LANES):
        tile = jnp.concatenate([gates[:, k * LANES:(k + 1) * LANES], pad], axis=0)
        gt_ref[k * LANES:(k + 1) * LANES, :] = tile.T


def _out_proj(attn, rec, x2, mod_l, ag, g2, w_out, layer, wr_bf, br, seq):
    n, d = x2.shape
    half = attn.shape[1]
    tm = ROW_TILE
    per_batch = seq // tm
    row = lambda w: pl.BlockSpec((tm, w), lambda i: (i, 0))
    full = lambda a: pl.BlockSpec(a.shape, lambda i: (0,) * a.ndim, pipeline_mode=pl.Buffered(1))
    ag2, g22 = ag.reshape(1, half), g2.reshape(1, d)
    return pl.pallas_call(
        _out_kernel,
        grid=(n // tm,),
        in_specs=[row(half), row(half), row(d), pl.BlockSpec((1, 6, d), lambda i: (i // per_batch, 0, 0)),
                  full(ag2), full(g22), pl.BlockSpec((1, d, d), lambda i: (layer, 0, 0), pipeline_mode=pl.Buffered(1)),
                  full(wr_bf), full(br)],
        out_specs=[row(d), row(PACK_WIDTH), row(PACK_WIDTH), pl.BlockSpec((TOP_K, tm), lambda i: (0, i)), row(LANES),
                   pl.BlockSpec((N_EXPERTS, LANES), lambda i: (0, 0))],
        out_shape=[jax.ShapeDtypeStruct((n, d), F32), jax.ShapeDtypeStruct((n, PACK_WIDTH), U32),
                   jax.ShapeDtypeStruct((n, PACK_WIDTH), U32), jax.ShapeDtypeStruct((TOP_K, n), jnp.int32),
                   jax.ShapeDtypeStruct((n, LANES), F32), jax.ShapeDtypeStruct((N_EXPERTS, LANES), F32)],
        compiler_params=_params(("arbitrary",)),
        name="out_proj",
    )(attn, rec, x2, mod_l, ag2, g22, w_out, wr_bf, br)


def _plan_kernel(e_ref, cnt_ref, ut_ref, pos_ref, be_ref, bv_ref, bf_ref, bn_ref, nu_ref, off, run):
    i = pl.program_id(0)
    t = e_ref.shape[1]
    eio = lax.broadcasted_iota(jnp.int32, (N_EXPERTS, t), 0)
    oh1 = e_ref[0:1, :] == eio
    oh2 = e_ref[1:2, :] == eio
    oh = jnp.logical_or(oh1, oh2).astype(F32)

    @pl.when(i == 0)
    def _():
        counts = cnt_ref[...]
        nb = jnp.floor((counts + (MOE_TILE - 1)) * (1.0 / MOE_TILE))
        er = lax.broadcasted_iota(jnp.int32, (N_EXPERTS, N_EXPERTS), 0)
        ec = lax.broadcasted_iota(jnp.int32, (N_EXPERTS, N_EXPERTS), 1)
        cum_excl = jnp.dot((ec < er).astype(BF16), nb.astype(BF16), preferred_element_type=F32)
        off[...] = cum_excl * MOE_TILE
        run[...] = jnp.zeros_like(run)
        ce, ci, cn = cum_excl[:, 0:1], (cum_excl + nb)[:, 0:1], counts[:, 0:1]
        b = lax.broadcasted_iota(jnp.int32, (N_EXPERTS, be_ref.shape[1]), 1).astype(F32)
        member = jnp.logical_and(b >= ce, b < ci)
        be = jnp.minimum(jnp.sum((b >= ci).astype(F32), axis=0, keepdims=True), N_EXPERTS - 1.0)
        bv = jnp.sum(jnp.where(member, jnp.minimum(cn - (b - ce) * MOE_TILE, float(MOE_TILE)), 0.0), axis=0, keepdims=True)
        be_ref[...] = be.astype(jnp.int32)
        bv_ref[...] = bv.astype(jnp.int32)
        has = nb[:, 0:1] > 0.0
        eid = lax.broadcasted_iota(jnp.int32, b.shape, 0).astype(F32)
        bf_ref[...] = jnp.sum(jnp.where(jnp.logical_and(b == ce, has), 1.0, 0.0), axis=0, keepdims=True).astype(jnp.int32)
        nxt = jnp.min(jnp.where(jnp.logical_and(eid > be, has), eid, float(N_EXPERTS)), axis=0, keepdims=True)
        bn_ref[...] = jnp.where(nxt < N_EXPERTS, nxt, -1.0).astype(jnp.int32)
        nu_ref[...] = jnp.broadcast_to(ci[N_EXPERTS - 1:N_EXPERTS, :], nu_ref.shape).astype(jnp.int32)

    c = jnp.dot(oh.astype(BF16), ut_ref[...], preferred_element_type=F32)
    r = off[:, 0:1] + run[:, 0:1] + c - 1.0
    pos_ref[0:1, :] = jnp.sum(jnp.where(oh1, r, 0.0), axis=0, keepdims=True).astype(jnp.int32)
    pos_ref[1:2, :] = jnp.sum(jnp.where(oh2, r, 0.0), axis=0, keepdims=True).astype(jnp.int32)
    run[...] += jnp.sum(oh, axis=1, keepdims=True)


def _plan(expert_ids_t, counts, n_rows):
    n = expert_ids_t.shape[1]
    t = PLAN_TILE
    n_blocks = n_rows // MOE_TILE
    assert n_blocks <= PLAN_LANES and TOP_K == 2
    ut = jnp.asarray(np.triu(np.ones((t, t), np.float32)), BF16)
    tab = jax.ShapeDtypeStruct((1, PLAN_LANES), jnp.int32)
    const = lambda shape: pl.BlockSpec(shape, lambda i: (0, 0))
    pos, be, bv, bf, bn, nu = pl.pallas_call(
        _plan_kernel,
        grid=(n // t,),
        in_specs=[pl.BlockSpec((TOP_K, t), lambda i: (0, i)), const((N_EXPERTS, LANES)), const((t, t))],
        out_specs=[pl.BlockSpec((TOP_K, t), lambda i: (0, i))] + [const((1, PLAN_LANES))] * 4 + [const((1, LANES))],
        out_shape=[jax.ShapeDtypeStruct((TOP_K, n), jnp.int32), tab, tab, tab, tab, jax.ShapeDtypeStruct((1, LANES), jnp.int32)],
        scratch_shapes=[pltpu.VMEM((N_EXPERTS, LANES), F32)] * 2,
        compiler_params=_params(("arbitrary",)),
        name="plan",
    )(expert_ids_t, counts, ut)
    return pos, (be.reshape(-1), bv.reshape(-1), bf.reshape(-1), bn.reshape(-1), nu.reshape(-1)[:1])


def _sc_mesh():
    return plsc.VectorSubcoreMesh(core_axis_name="c", subcore_axis_name="s")


def _sc_scatter_rows(x, pos, n_rows):
    n, d = x.shape
    w = SC_WINDOW

    @functools.partial(pl.kernel, out_type=jax.ShapeDtypeStruct((n_rows, d), x.dtype), mesh=_sc_mesh(), scratch_types=[])
    def scatter(x_hbm, p_hbm, o_hbm):
        def body(x_vmem, i_vmem):
            for k in range(TOP_K):
                pltpu.sync_copy(x_vmem, o_hbm.at[i_vmem.at[k]])

        pltpu.emit_pipeline(
            body,
            grid=(n // w,),
            in_specs=[pl.BlockSpec((w, d), lambda i: (i, 0)), pl.BlockSpec((TOP_K, w), lambda i: (0, i))],
            out_specs=[],
            core_axis_name=("c", "s"),
            dimension_semantics=(pltpu.PARALLEL,),
        )(x_hbm, p_hbm)

    return scatter(x, pos)


def _sc_gather_rows(y, pos):
    m = pos.shape[0] * pos.shape[1]
    d = y.shape[1]
    w = SC_WINDOW

    @functools.partial(pl.kernel, out_type=jax.ShapeDtypeStruct((m, d), y.dtype), mesh=_sc_mesh(), scratch_types=[])
    def gather(y_hbm, p_hbm, o_hbm):
        def body(i_vmem, o_vmem):
            pltpu.sync_copy(y_hbm.at[i_vmem.at[0]], o_vmem)

        pltpu.emit_pipeline(
            body,
            grid=(m // w,),
            in_specs=[pl.BlockSpec((1, w), lambda i: (0, i))],
            out_specs=[pl.BlockSpec((w, d), lambda i: (i, 0))],
            core_axis_name=("c", "s"),
            dimension_semantics=(pltpu.PARALLEL,),
        )(p_hbm, o_hbm)

    return gather(y, pos.reshape(1, m))


def _expert_kernel(be_ref, bv_ref, bf_ref, bn_ref, nu_ref, xa_ref, xb_ref, w1_hbm, w3_hbm, w2_hbm, ya_ref, yb_ref,
                   wf1, wf3, wf2, w1b, w3b, w2b, wsem, par, *, layer):
    i = pl.program_id(0)
    tb = xa_ref.shape[0]
    half = tb // 2

    def fetch(e, slot):
        return (pltpu.make_async_copy(w1_hbm.at[layer, e], wf1.at[slot], wsem.at[slot, 0]),
                pltpu.make_async_copy(w3_hbm.at[layer, e], wf3.at[slot], wsem.at[slot, 1]),
                pltpu.make_async_copy(w2_hbm.at[layer, e], wf2.at[slot], wsem.at[slot, 2]))

    def ffn(rows):
        live = lax.broadcasted_iota(jnp.int32, (rows, 1), 0) < bv_ref[i]
        x = jnp.where(live, _unpack_rows(xa_ref[0:rows, :], xb_ref[0:rows, :]), 0.0).astype(BF16)
        y = None
        hc = w1b.shape[1] // FFN_SPLIT
        for c in range(FFN_SPLIT):
            cs = slice(c * hc, (c + 1) * hc)
            a = jnp.dot(x, w1b[:, cs], preferred_element_type=F32)
            b = jnp.dot(x, w3b[:, cs], preferred_element_type=F32)
            hm = (a * jax.nn.sigmoid(a) * b).astype(BF16)
            part = jnp.dot(hm, w2b[cs, :], preferred_element_type=F32)
            y = part if y is None else y + part
        ya_ref[0:rows, :], yb_ref[0:rows, :] = _pack_rows(y)

    @pl.when(i == 0)
    def _():
        par[0] = 0
        for c in fetch(be_ref[0], 0):
            c.start()

    @pl.when(i < nu_ref[0])
    def _():
        @pl.when(bf_ref[i] == 1)
        def _():
            slot = par[0]
            for c in fetch(be_ref[i], slot):
                c.wait()
            w1b[...] = wf1[slot].astype(BF16)
            w3b[...] = wf3[slot].astype(BF16)
            w2b[...] = wf2[slot].astype(BF16)

            @pl.when(bn_ref[i] >= 0)
            def _():
                for c in fetch(bn_ref[i], 1 - slot):
                    c.start()

            par[0] = 1 - slot

        @pl.when(bv_ref[i] > half)
        def _():
            ffn(tb)

        @pl.when(bv_ref[i] <= half)
        def _():
            ffn(half)
            ya_ref[half:, :] = jnp.zeros((tb - half, ya_ref.shape[1]), ya_ref.dtype)
            yb_ref[half:, :] = jnp.zeros((tb - half, yb_ref.shape[1]), yb_ref.dtype)

    @pl.when(i >= nu_ref[0])
    def _():
        ya_ref[...] = jnp.zeros_like(ya_ref)
        yb_ref[...] = jnp.zeros_like(yb_ref)


def _experts(xa, xb, w1, w3, w2, layer, tables):
    n_rows = xa.shape[0]
    d, de = w1.shape[-2:]
    tb = MOE_TILE
    xspec = pl.BlockSpec((tb, PACK_WIDTH), lambda i, be, bv, bf, bn, nu: (jnp.minimum(i, nu[0] - 1), 0))
    yspec = pl.BlockSpec((tb, PACK_WIDTH), lambda i, be, bv, bf, bn, nu: (i, 0))
    anyspec = pl.BlockSpec(memory_space=pl.ANY)
    grid_spec = pltpu.PrefetchScalarGridSpec(
        num_scalar_prefetch=5,
        grid=(n_rows // tb,),
        in_specs=[xspec, xspec, anyspec, anyspec, anyspec],
        out_specs=[yspec, yspec],
        scratch_shapes=[pltpu.VMEM((2, d, de), F32), pltpu.VMEM((2, d, de), F32), pltpu.VMEM((2, de, d), F32),
                        pltpu.VMEM((d, de), BF16), pltpu.VMEM((d, de), BF16), pltpu.VMEM((de, d), BF16),
                        pltpu.SemaphoreType.DMA((2, 3)), pltpu.SMEM((1,), jnp.int32)],
    )
    rows = jax.ShapeDtypeStruct((n_rows, PACK_WIDTH), U32)
    return pl.pallas_call(
        functools.partial(_expert_kernel, layer=layer),
        grid_spec=grid_spec,
        out_shape=[rows, rows],
        compiler_params=_params(("arbitrary",)),
        name="experts",
    )(*tables, xa, xb, w1, w3, w2)


def _final_kernel(ga0, ga1, gb0, gb1, gt_ref, mod_ref, x_ref, fg_ref, o_ref):
    o_ref[...] = _rms(_moe_residual(x_ref, ga0, ga1, gb0, gb1, gt_ref, mod_ref[0][5:6, :]), fg_ref[...])


def _final(x2, ga, gb, gates, mod_l, final_g, seq):
    n, d = x2.shape
    tm = ROW_TILE
    per_batch = seq // tm
    row = lambda w: pl.BlockSpec((tm, w), lambda i: (i, 0))
    second = pl.BlockSpec((tm, PACK_WIDTH), lambda i: (i + n // tm, 0))
    return pl.pallas_call(
        _final_kernel,
        grid=(n // tm,),
        in_specs=[row(PACK_WIDTH), second, row(PACK_WIDTH), second, row(LANES),
                  pl.BlockSpec((1, 6, d), lambda i: (i // per_batch, 0, 0)), row(d), pl.BlockSpec((1, d), lambda i: (0, 0))],
        out_specs=row(d),
        out_shape=jax.ShapeDtypeStruct((n, d), F32),
        compiler_params=_params(("parallel",)),
        name="final_norm",
    )(ga, ga, gb, gb, gates, mod_l, x2, final_g.reshape(1, d))


def kernel(x, c, w_ada, b_ada, norm1_g, w_in, attn_norm_g, hgrn_lb_logits, hgrn_norm_g, w_out, norm2_g, router_group_w, router_group_b, router_expert_w, router_expert_b, moe_w1, moe_w3, moe_w2, final_g):
    bsz, seq, d = x.shape
    depth = w_ada.shape[0]
    n = bsz * seq
    attn_dim = attn_norm_g.shape[1]
    n_heads = attn_dim // ATTN_HEAD_DIM

    lb_w = jax.nn.softmax(hgrn_lb_logits.astype(F32), axis=0)
    lower_bounds = jnp.cumsum(lb_w, axis=0) - lb_w[0:1]
    mod = _adaln(c, w_ada, b_ada)
    bias = _attn_bias(n_heads)
    hconsts = _hgrn_consts()
    n_rows = n * TOP_K + N_EXPERTS * MOE_TILE

    x2 = x.reshape(n, d)
    moe = None
    for layer in range(depth):
        x2, (aq, ak, av, rq, rf, ri, rg) = _in_proj(x2, mod[layer], norm1_g[layer], w_in, layer, seq, moe)
        attn = _attention(aq, ak, av, bias, bsz, seq)
        rec = _hgrn(rq, rf, ri, rg, lower_bounds[layer], hgrn_norm_g[layer], hconsts, bsz, seq)
        gpad = EXPERTS_PER_GROUP - N_GROUPS
        wr = jnp.concatenate([router_group_w[layer].T, jnp.zeros((gpad, d), F32), router_expert_w[layer].T], axis=0)
        br = jnp.concatenate([router_group_b[layer], jnp.zeros((gpad,), F32), router_expert_b[layer]]).reshape(-1, 1)
        x2, ha, hb, eids, gates, counts = _out_proj(attn, rec, x2, mod[layer], attn_norm_g[layer], norm2_g[layer],
                                            w_out, layer, wr.astype(BF16), br, seq)
        pos, tables = _plan(eids, counts, n_rows)
        ya, yb = _experts(_sc_scatter_rows(ha, pos, n_rows), _sc_scatter_rows(hb, pos, n_rows),
                          moe_w1, moe_w3, moe_w2, layer, tables)
        moe = (_sc_gather_rows(ya, pos), _sc_gather_rows(yb, pos), gates, mod[layer])
    return _final(x2, moe[0], moe[1], moe[2], moe[3], final_g, seq).reshape(bsz, seq, d)
```

```python
import functools

import numpy as np
import jax
import jax.numpy as jnp
from jax import lax
from jax.experimental import pallas as pl
from jax.experimental.pallas import tpu as pltpu
from jax.experimental.pallas import tpu_sc as plsc

F32 = jnp.float32
BF16 = jnp.bfloat16
U32 = jnp.uint32

ATTN_HEAD_DIM = 64
ATTN_BLOCK = 128
DILATIONS = (1, 4, 16)
ALIBI_MAX_EXP = 8.0
HGRN_EXPAND = 128
HGRN_CHUNK = 128
N_GROUPS = 4
EXPERTS_PER_GROUP = 8
N_EXPERTS = N_GROUPS * EXPERTS_PER_GROUP
TOP_K = 2
NORM_EPS = 1e-6
LOG2E = 1.4426950408889634

LANES = 128
ATTN_SUPER = ATTN_BLOCK * max(DILATIONS)
ROW_TILE = 512
IN_GROUPS = 8
HGRN_TILE = 2048
HGRN_UNROLL = 8
MOE_TILE = 512
FFN_SPLIT = 2
PLAN_TILE = 2048
PLAN_LANES = 256
PACK_WIDTH = 256
SC_WINDOW = 128
VMEM_LIMIT = 56 << 20


def _params(semantics):
    return pltpu.CompilerParams(dimension_semantics=semantics, vmem_limit_bytes=VMEM_LIMIT)


def _rms(x, g):
    return x * lax.rsqrt(jnp.mean(x * x, axis=-1, keepdims=True) + NORM_EPS) * g


def _loop(trips, body):
    if trips == 1:
        body(0, 0)
    else:
        lax.fori_loop(0, trips, body, 0)


def _round_robin(gens):
    live = list(gens)
    while live:
        for gen in list(live):
            try:
                next(gen)
            except StopIteration:
                live.remove(gen)


def _dot_nt(a, b):
    return lax.dot_general(a, b, (((1,), (1,)), ((), ())), preferred_element_type=F32)


def _dot_tn(a, b):
    return lax.dot_general(a, b, (((0,), (0,)), ((), ())), preferred_element_type=F32)


def _pack_pair(hi, lo):
    hb = lax.bitcast_convert_type(hi.astype(BF16).astype(F32), U32)
    lb = lax.bitcast_convert_type(lo.astype(BF16).astype(F32), U32)
    return hb | (lb >> 16)


def _unpack_pair(u):
    hi = lax.bitcast_convert_type(u & jnp.uint32(0xFFFF0000), F32)
    lo = lax.bitcast_convert_type(u << 16, F32)
    return hi, lo


def _pack_rows(y):
    w = PACK_WIDTH
    return _pack_pair(y[:, 0:w], y[:, 2 * w:3 * w]), _pack_pair(y[:, w:2 * w], y[:, 3 * w:4 * w])


def _unpack_rows(ua, ub):
    ha, la = _unpack_pair(ua)
    hb, lb = _unpack_pair(ub)
    return jnp.concatenate([ha, hb, la, lb], axis=1)


def _moe_residual(x_ref, ga0, ga1, gb0, gb1, gt_ref, gate_row):
    gt = gt_ref[...]
    y = gt[:, 0:1] * _unpack_rows(ga0[...], gb0[...]) + gt[:, 1:2] * _unpack_rows(ga1[...], gb1[...])
    return x_ref[...] + gate_row * y


def _ada_kernel(c_ref, w_ref, b_ref, o_ref):
    c = c_ref[...]
    ca = c * jax.nn.sigmoid(c)
    hi = ca.astype(BF16)
    lo = (ca - hi.astype(F32)).astype(BF16)
    w = w_ref[0].astype(BF16)
    o_ref[0] = (jnp.dot(hi, w, preferred_element_type=F32) + jnp.dot(lo, w, preferred_element_type=F32)) + b_ref[0]


def _adaln(c, w_ada, b_ada):
    depth, d, n6 = w_ada.shape
    bsz = c.shape[0]
    rows = 8
    cp = jnp.zeros((rows, d), F32).at[:bsz].set(c)
    tn = 1536
    out = pl.pallas_call(
        _ada_kernel,
        grid=(depth, n6 // tn),
        in_specs=[
            pl.BlockSpec((rows, d), lambda l, j: (0, 0)),
            pl.BlockSpec((1, d, tn), lambda l, j: (l, 0, j)),
            pl.BlockSpec((1, 1, tn), lambda l, j: (l, 0, j)),
        ],
        out_specs=pl.BlockSpec((1, rows, tn), lambda l, j: (l, 0, j)),
        out_shape=jax.ShapeDtypeStruct((depth, rows, n6), F32),
        compiler_params=_params(("parallel", "parallel")),
        name="adaln",
    )(cp, w_ada, b_ada.reshape(depth, 1, n6))
    return out[:, :bsz].reshape(depth, bsz, 6, d)


def _in_kernel(*refs, n_moe):
    moe, (x_ref, mod_ref, g_ref, w_ref), rest = refs[:n_moe], refs[n_moe:n_moe + 4], refs[n_moe + 4:]
    hb = rest[-1]
    i = pl.program_id(0)

    @pl.when(i == 0)
    def _():
        hb[1] = jnp.zeros(hb.shape[1:], hb.dtype)

    outs = rest[1:-1] if n_moe else rest[:-1]
    mod = mod_ref[0]
    prev = hb[(i + 1) % 2]
    width = outs[0].shape[1]
    tm = x_ref.shape[0]
    groups = [slice(r, r + tm // IN_GROUPS) for r in range(0, tm, tm // IN_GROUPS)]

    def prologue(rs):
        if n_moe:
            ga0, ga1, gb0, gb1, gt_ref, pmod_ref = moe
            gt = gt_ref[rs, :]
            y = gt[:, 0:1] * _unpack_rows(ga0[rs, :], gb0[rs, :]) + gt[:, 1:2] * _unpack_rows(ga1[rs, :], gb1[rs, :])
            x = x_ref[rs, :] + pmod_ref[0][5:6, :] * y
            rest[0][rs, :] = x
        else:
            x = x_ref[rs, :]
        hb[i % 2, rs, :] = (_rms(x, g_ref[...]) * (1.0 + mod[1:2, :]) + mod[0:1, :]).astype(BF16)

    per = -(-len(groups) // len(outs))
    for k, o in enumerate(outs):
        for rs in groups[k * per:(k + 1) * per]:
            prologue(rs)
        w = w_ref[0, :, k * width:(k + 1) * width].astype(BF16)
        o[...] = jnp.dot(prev, w, preferred_element_type=F32).astype(o.dtype)


def _in_proj(x2, mod_l, g, w_in, layer, seq, moe=None):
    n, d = x2.shape
    n_out = w_in.shape[2]
    width = 512
    tm = ROW_TILE
    per_batch = seq // tm
    last = n // tm - 1
    cur = lambda i: jnp.minimum(i, last)
    row = lambda w: pl.BlockSpec((tm, w), lambda i: (cur(i), 0))
    mod_spec = pl.BlockSpec((1, 6, d), lambda i: (cur(i) // per_batch, 0, 0))
    in_specs, args, out_specs, out_shape = [], [], [], []
    if moe is not None:
        ga, gb, gates, mod_prev = moe
        second = pl.BlockSpec((tm, PACK_WIDTH), lambda i: (cur(i) + n // tm, 0))
        in_specs += [row(PACK_WIDTH), second, row(PACK_WIDTH), second, row(LANES), mod_spec]
        args += [ga, ga, gb, gb, gates, mod_prev]
        out_specs.append(row(d))
        out_shape.append(jax.ShapeDtypeStruct((n, d), F32))
    in_specs += [row(d), mod_spec, pl.BlockSpec((1, d), lambda i: (0, 0)),
                 pl.BlockSpec((1, d, n_out), lambda i: (layer, 0, 0), pipeline_mode=pl.Buffered(1))]
    args += [x2, mod_l, g.reshape(1, d), w_in]
    out_dtypes = (F32, F32, F32, BF16, F32, BF16, F32)
    out_specs += [pl.BlockSpec((tm, width), lambda i: (jnp.maximum(i - 1, 0), 0))] * len(out_dtypes)
    out_shape += [jax.ShapeDtypeStruct((n, width), dt) for dt in out_dtypes]
    outs = pl.pallas_call(
        functools.partial(_in_kernel, n_moe=0 if moe is None else 6),
        grid=(n // tm + 1,),
        in_specs=in_specs,
        out_specs=out_specs,
        out_shape=out_shape,
        scratch_shapes=[pltpu.VMEM((2, tm, d), BF16)],
        compiler_params=_params(("arbitrary",)),
        name="in_proj",
    )(*args)
    return (x2, outs) if moe is None else (outs[0], outs[1:])


def _attn_bias(n_heads):
    qi = np.arange(ATTN_BLOCK)[:, None]
    kj = np.arange(2 * ATTN_BLOCK)[None, :]
    dist = ATTN_BLOCK + qi - kj
    valid = (dist >= 0) & (dist <= ATTN_BLOCK)
    slopes = np.exp2(-ALIBI_MAX_EXP * np.arange(1, n_heads + 1, dtype=np.float32) / n_heads)
    out = np.empty((2, len(DILATIONS), n_heads, ATTN_BLOCK, 2 * ATTN_BLOCK), np.float32)
    for p, dil in enumerate(DILATIONS):
        b = -slopes[:, None, None] * (dist * dil).astype(np.float32)[None] * np.float32(LOG2E)
        out[0, p] = np.where(valid[None], b, -np.inf)
        out[1, p] = np.where((valid & (kj >= ATTN_BLOCK))[None], b, -np.inf)
    return jnp.asarray(out)


def _attn_kernel(q_ref, k_ref, v_ref, bias_ref, o_ref, kcar, vcar, oscr, mscr, dscr):
    n = pl.program_id(2)
    sb = q_ref.shape[0]
    is_lo = lax.broadcasted_iota(jnp.int32, (ATTN_BLOCK, LANES), 1) < ATTN_HEAD_DIM
    scale = ATTN_HEAD_DIM ** -0.5 * LOG2E

    @pl.when(n == 0)
    def _():
        kcar[...] = jnp.zeros_like(kcar)
        vcar[...] = jnp.zeros_like(vcar)

    lo2 = lax.broadcasted_iota(jnp.int32, (2 * ATTN_BLOCK, LANES), 1) < ATTN_HEAD_DIM
    zero = jnp.zeros((2 * ATTN_BLOCK, LANES), BF16)
    den_lo = lo2.astype(BF16)
    den_hi = jnp.logical_not(lo2).astype(BF16)

    patterns = []
    for p, dil in enumerate(DILATIONS):
        nb = (sb // ATTN_BLOCK) // dil

        def rows_of(it, dil=dil, nb=nb):
            start = (it % nb) * (ATTN_BLOCK * dil) + it // nb
            return pl.ds(start, ATTN_BLOCK, stride=dil) if dil > 1 else pl.ds(pl.multiple_of(start, ATTN_BLOCK), ATTN_BLOCK)

        def pv(it, probs, vaug, p=p, rows_of=rows_of):
            rows = rows_of(it)
            of = jnp.dot(probs, vaug, preferred_element_type=F32)
            oscr[p, rows, :] = of[:, :LANES]
            dscr[p, rows, :] = of[:, LANES:]

        def qk(it, p=p, nb=nb, rows_of=rows_of):
            r = it // nb
            rows = rows_of(it)
            crow = pl.ds(pl.multiple_of(r * ATTN_BLOCK, ATTN_BLOCK), ATTN_BLOCK)
            qb = q_ref[rows, :] * scale
            kc = k_ref[rows, :].astype(BF16)
            vc = v_ref[rows, :].astype(BF16)
            kcat = jnp.concatenate([kcar[p, crow, :], kc], axis=0)
            vcat = jnp.concatenate([vcar[p, crow, :], vc], axis=0)
            vaug = jnp.concatenate([jnp.concatenate([jnp.where(lo2, vcat, zero), den_lo], axis=1),
                                    jnp.concatenate([jnp.where(lo2, zero, vcat), den_hi], axis=1)], axis=0)
            kcar[p, crow, :] = kc
            vcar[p, crow, :] = vc
            first = jnp.logical_and(n == 0, it % nb == 0).astype(jnp.int32)
            probs, acc_m = [], None
            for hh in range(2):
                sel = is_lo if hh == 0 else jnp.logical_not(is_lo)
                qm = jnp.where(sel, qb, 0.0).astype(BF16)
                s = _dot_nt(qm, kcat) + bias_ref[first, p, hh]
                m = jnp.max(s, axis=-1, keepdims=True)
                probs.append(jnp.exp2(s - m).astype(BF16))
                mb = jnp.broadcast_to(m, (ATTN_BLOCK, LANES))
                acc_m = mb if hh == 0 else jnp.where(sel, mb, acc_m)
            mscr[p, rows, :] = acc_m
            return jnp.concatenate(probs, axis=1), vaug

        n_it = sb // ATTN_BLOCK

        def pattern(pv=pv, qk=qk, n_it=n_it):
            pend = None
            for u in range(n_it):
                cur = qk(u)
                if pend is not None:
                    pv(u - 1, *pend)
                pend = cur
                yield
            pv(n_it - 1, *pend)

        patterns.append(pattern())

    _round_robin(patterns)

    ct = 256

    def combine(t, carry):
        rs = pl.ds(pl.multiple_of(t * ct, ct), ct)
        m0, m1, m2 = mscr[0, rs, :], mscr[1, rs, :], mscr[2, rs, :]
        mx = jnp.maximum(jnp.maximum(m0, m1), m2)
        w0, w1, w2 = jnp.exp2(m0 - mx), jnp.exp2(m1 - mx), jnp.exp2(m2 - mx)
        num = w0 * oscr[0, rs, :] + w1 * oscr[1, rs, :] + w2 * oscr[2, rs, :]
        den = w0 * dscr[0, rs, :] + w1 * dscr[1, rs, :] + w2 * dscr[2, rs, :]
        o_ref[rs, :] = (num / den).astype(o_ref.dtype)
        return carry

    lax.fori_loop(0, sb // ct, combine, 0)


def _attention(aq, ak, av, bias, bsz, seq):
    n, width = aq.shape
    sb = ATTN_SUPER
    nsb = seq // sb
    pairs = width // LANES
    npat = len(DILATIONS)
    blk = pl.BlockSpec((sb, LANES), lambda b, h, t: (b * nsb + t, h))
    return pl.pallas_call(
        _attn_kernel,
        grid=(bsz, pairs, nsb),
        in_specs=[blk, blk, blk,
                  pl.BlockSpec((2, npat, 2, ATTN_BLOCK, 2 * ATTN_BLOCK), lambda b, h, t: (0, 0, h, 0, 0))],
        out_specs=blk,
        out_shape=jax.ShapeDtypeStruct((n, width), BF16),
        scratch_shapes=[
            pltpu.VMEM((npat, max(DILATIONS) * ATTN_BLOCK, LANES), BF16),
            pltpu.VMEM((npat, max(DILATIONS) * ATTN_BLOCK, LANES), BF16),
            pltpu.VMEM((npat, sb, LANES), F32),
            pltpu.VMEM((npat, sb, LANES), F32),
            pltpu.VMEM((npat, sb, LANES), F32),
        ],
        compiler_params=_params(("parallel", "parallel", "arbitrary")),
        name="dilated_attn",
    )(aq, ak, av, bias)


def _hgrn_consts():
    c = HGRN_CHUNK
    t = np.arange(c)[:, None]
    u = np.arange(c)[None, :]
    low, masks = [], []
    m = c // 2
    while m >= 1:
        mid = (t // (2 * m)) * (2 * m) + m
        is_q = t >= mid
        if m in (4, 2):
            low.append((is_q & (u >= mid) & (u <= t)) | (~is_q & (u > t) & (u < mid)))
        masks.append(((t // (2 * m)) == (u // (2 * m))) & (t % (2 * m) >= m) & (u % (2 * m) < m))
        m //= 2
    masks.append(t == u)
    f = lambda a: np.asarray(a, np.float32)
    return (jnp.asarray(f(u <= t), BF16), jnp.asarray(f(np.concatenate(low, axis=0)), BF16),
            jnp.asarray(f(np.broadcast_to(t % 2 == 1, (c, c)))), jnp.asarray(f(np.stack(masks))))


def _hgrn_kernel(q_ref, z_ref, v_ref, g_ref, lb_ref, gn_ref, tri_ref, low_ref, odd_ref, msk_ref, o_ref, st):
    c = HGRN_CHUNK

    @pl.when(pl.program_id(2) == 0)
    def _():
        st[...] = jnp.zeros_like(st)

    lb = jnp.maximum(lb_ref[...], 0.0)
    log_lb = jnp.log(lb)
    log_1m = jnp.log1p(-lb)
    one_m = 1.0 - lb
    gn = gn_ref[...]
    n_levels = msk_ref.shape[0] - 1
    coarse = [c >> (i + 1) for i in range(n_levels) if (c >> (i + 1)) >= 8]

    def chunk(ci):
        rs = pl.ds(pl.multiple_of(ci * c, c), c)
        z = z_ref[rs, :]
        qb = q_ref[rs, :]
        vb = v_ref[rs, :]
        ez = jnp.exp(-jnp.abs(z))
        ls = jnp.minimum(z, 0.0) - jnp.log(1.0 + ez)
        b2 = log_1m + ls
        lf = jnp.maximum(log_lb, b2) + jnp.log(1.0 + jnp.exp(-jnp.abs(log_lb - b2)))
        rz = 1.0 / (1.0 + ez)
        kb = (one_m * jnp.where(z > 0.0, ez * rz, rz)).astype(BF16)
        l2 = lf * LOG2E
        hi = l2.astype(BF16)
        hl = jnp.concatenate([hi, (l2 - hi.astype(F32)).astype(BF16)], axis=1)
        yield
        bb = jnp.dot(tri_ref[...], hl, preferred_element_type=F32)
        b = bb[:, :HGRN_EXPAND] + bb[:, HGRN_EXPAND:]
        dl = jnp.dot(low_ref[...], hl, preferred_element_type=F32)
        dl = dl[:, :HGRN_EXPAND] + dl[:, HGRN_EXPAND:]
        yield

        def level(d):
            e = jnp.exp2(d).astype(BF16)
            return _dot_nt(qb * e, kb * e)

        a = msk_ref[n_levels] * _dot_nt(qb, kb)
        for i in range(n_levels):
            m = c >> (i + 1)
            if m >= 8:
                parts = []
                for r0 in range(0, c, 2 * m):
                    mid = b[r0 + m - 1:r0 + m, :]
                    parts += [mid - b[r0:r0 + m, :], b[r0 + m:r0 + 2 * m, :] - mid]
                d = jnp.concatenate(parts, axis=0)
            elif m > 1:
                d = dl[(i - len(coarse)) * c:(i - len(coarse) + 1) * c]
            else:
                d = l2 * odd_ref[...]
            a = a + msk_ref[i] * level(d)
            if i % 3 == 2:
                yield
        b_last = b[c - 1:c, :]
        s_t = st[...]
        o = _dot_nt(qb * jnp.exp2(b).astype(BF16), s_t.astype(BF16)) + jnp.dot(a.astype(BF16), vb, preferred_element_type=F32)
        st[...] = jnp.exp2(b_last) * s_t + _dot_tn(vb, kb * jnp.exp2(b_last - b).astype(BF16))
        yield
        g = g_ref[rs, :]
        o_ref[rs, :] = (_rms(o, gn) * (g * jax.nn.sigmoid(g))).astype(o_ref.dtype)

    def group(gi, carry):
        _round_robin([chunk(gi * HGRN_UNROLL + u) for u in range(HGRN_UNROLL)])
        return carry

    _loop(q_ref.shape[0] // (c * HGRN_UNROLL), group)


def _hgrn(rq, rf, ri, rg, lb, gn, consts, bsz, seq):
    n, width = rq.shape
    heads = width // HGRN_EXPAND
    ts = HGRN_TILE
    nt = seq // ts
    blk = pl.BlockSpec((ts, HGRN_EXPAND), lambda b, h, t: (b * nt + t, h))
    vec = pl.BlockSpec((1, HGRN_EXPAND), lambda b, h, t: (0, h))
    const = lambda a: pl.BlockSpec(a.shape, lambda b, h, t: (0,) * a.ndim)
    return pl.pallas_call(
        _hgrn_kernel,
        grid=(bsz, heads, nt),
        in_specs=[blk, blk, blk, blk, vec, vec] + [const(a) for a in consts],
        out_specs=blk,
        out_shape=jax.ShapeDtypeStruct((n, width), BF16),
        scratch_shapes=[pltpu.VMEM((HGRN_EXPAND, HGRN_EXPAND), F32)],
        compiler_params=_params(("parallel", "parallel", "arbitrary")),
        name="hgrn2",
    )(rq, rf, ri, rg, lb.reshape(1, width), gn.reshape(1, width), *consts)


def _route_math(lg):
    row = lax.broadcasted_iota(jnp.int32, (EXPERTS_PER_GROUP, lg.shape[1]), 0).astype(F32)
    none = float(EXPERTS_PER_GROUP)

    def first_argmax(vals):
        mx = jnp.max(vals, axis=0, keepdims=True)
        return mx, jnp.min(jnp.where(vals == mx, row, none), axis=0, keepdims=True)

    g = lg[0:EXPERTS_PER_GROUP, :]
    is_group = row < N_GROUPS
    gmax, gidx = first_argmax(jnp.where(is_group, g, -jnp.inf))
    gprob = 1.0 / jnp.sum(jnp.where(is_group, jnp.exp(g - gmax), 0.0), axis=0, keepdims=True)
    el = lg[EXPERTS_PER_GROUP:2 * EXPERTS_PER_GROUP, :]
    for k in range(1, N_GROUPS):
        el = jnp.where(gidx == k, lg[(k + 1) * EXPERTS_PER_GROUP:(k + 2) * EXPERTS_PER_GROUP, :], el)
    t1, i1 = first_argmax(el)
    t2, i2 = first_argmax(jnp.where(row == i1, -jnp.inf, el))
    ex = jnp.exp(t2 - t1)
    base = gidx * EXPERTS_PER_GROUP
    eids = jnp.concatenate([base + i1, base + i2], axis=0).astype(jnp.int32)
    gates = jnp.concatenate([gprob / (1.0 + ex), gprob * ex / (1.0 + ex)], axis=0)
    return eids, gates


def _out_kernel(a_ref, r_ref, x_ref, mod_ref, ag_ref, g2_ref, wo_ref, wr_ref, br_ref, xo_ref, ha_ref, hb_ref, e_ref, gt_ref, cnt_ref):
    mod = mod_ref[0]
    mix = jnp.concatenate([_rms(a_ref[...].astype(F32), ag_ref[...]).astype(BF16), r_ref[...]], axis=1)
    xn = x_ref[...] + mod[2:3, :] * jnp.dot(mix, wo_ref[0].astype(BF16), preferred_element_type=F32)
    xo_ref[...] = xn
    h2 = _rms(xn, g2_ref[...]) * (1.0 + mod[4:5, :]) + mod[3:4, :]
    ha_ref[...], hb_ref[...] = _pack_rows(h2)
    eids, gates = _route_math(_dot_nt(wr_ref[...], h2.astype(BF16)) + br_ref[...])
    e_ref[...] = eids
    eio = lax.broadcasted_iota(jnp.int32, (N_EXPERTS, eids.shape[1]), 0)
    hit = jnp.logical_or(eids[0:1, :] == eio, eids[1:2, :] == eio).astype(F32)

    @pl.when(pl.program_id(0) == 0)
    def _():
        cnt_ref[...] = jnp.zeros_like(cnt_ref)

    cnt_ref[...] += jnp.sum(hit, axis=1, keepdims=True)
    pad = jnp.zeros((LANES - TOP_K, LANES), F32)
    for k in range(h2.shape[0] // LANES):
        tile = jnp.concatenate([gates[:, k * LANES:(k + 1) * LANES], pad], axis=0)
        gt_ref[k * LANES:(k + 1) * LANES, :] = tile.T


def _out_proj(attn, rec, x2, mod_l, ag, g2, w_out, layer, wr_bf, br, seq):
    n, d = x2.shape
    half = attn.shape[1]
    tm = ROW_TILE
    per_batch = seq // tm
    row = lambda w: pl.BlockSpec((tm, w), lambda i: (i, 0))
    full = lambda a: pl.BlockSpec(a.shape, lambda i: (0,) * a.ndim, pipeline_mode=pl.Buffered(1))
    ag2, g22 = ag.reshape(1, half), g2.reshape(1, d)
    return pl.pallas_call(
        _out_kernel,
        grid=(n // tm,),
        in_specs=[row(half), row(half), row(d), pl.BlockSpec((1, 6, d), lambda i: (i // per_batch, 0, 0)),
                  full(ag2), full(g22), pl.BlockSpec((1, d, d), lambda i: (layer, 0, 0), pipeline_mode=pl.Buffered(1)),
                  full(wr_bf), full(br)],
        out_specs=[row(d), row(PACK_WIDTH), row(PACK_WIDTH), pl.BlockSpec((TOP_K, tm), lambda i: (0, i)), row(LANES),
                   pl.BlockSpec((N_EXPERTS, LANES), lambda i: (0, 0))],
        out_shape=[jax.ShapeDtypeStruct((n, d), F32), jax.ShapeDtypeStruct((n, PACK_WIDTH), U32),
                   jax.ShapeDtypeStruct((n, PACK_WIDTH), U32), jax.ShapeDtypeStruct((TOP_K, n), jnp.int32),
                   jax.ShapeDtypeStruct((n, LANES), F32), jax.ShapeDtypeStruct((N_EXPERTS, LANES), F32)],
        compiler_params=_params(("arbitrary",)),
        name="out_proj",
    )(attn, rec, x2, mod_l, ag2, g22, w_out, wr_bf, br)


def _plan_kernel(e_ref, cnt_ref, ut_ref, pos_ref, be_ref, bv_ref, bf_ref, bn_ref, nu_ref, off, run):
    i = pl.program_id(0)
    t = e_ref.shape[1]
    eio = lax.broadcasted_iota(jnp.int32, (N_EXPERTS, t), 0)
    oh1 = e_ref[0:1, :] == eio
    oh2 = e_ref[1:2, :] == eio
    oh = jnp.logical_or(oh1, oh2).astype(F32)

    @pl.when(i == 0)
    def _():
        counts = cnt_ref[...]
        nb = jnp.floor((counts + (MOE_TILE - 1)) * (1.0 / MOE_TILE))
        er = lax.broadcasted_iota(jnp.int32, (N_EXPERTS, N_EXPERTS), 0)
        ec = lax.broadcasted_iota(jnp.int32, (N_EXPERTS, N_EXPERTS), 1)
        cum_excl = jnp.dot((ec < er).astype(BF16), nb.astype(BF16), preferred_element_type=F32)
        off[...] = cum_excl * MOE_TILE
        run[...] = jnp.zeros_like(run)
        ce, ci, cn = cum_excl[:, 0:1], (cum_excl + nb)[:, 0:1], counts[:, 0:1]
        b = lax.broadcasted_iota(jnp.int32, (N_EXPERTS, be_ref.shape[1]), 1).astype(F32)
        member = jnp.logical_and(b >= ce, b < ci)
        be = jnp.minimum(jnp.sum((b >= ci).astype(F32), axis=0, keepdims=True), N_EXPERTS - 1.0)
        bv = jnp.sum(jnp.where(member, jnp.minimum(cn - (b - ce) * MOE_TILE, float(MOE_TILE)), 0.0), axis=0, keepdims=True)
        be_ref[...] = be.astype(jnp.int32)
        bv_ref[...] = bv.astype(jnp.int32)
        has = nb[:, 0:1] > 0.0
        eid = lax.broadcasted_iota(jnp.int32, b.shape, 0).astype(F32)
        bf_ref[...] = jnp.sum(jnp.where(jnp.logical_and(b == ce, has), 1.0, 0.0), axis=0, keepdims=True).astype(jnp.int32)
        nxt = jnp.min(jnp.where(jnp.logical_and(eid > be, has), eid, float(N_EXPERTS)), axis=0, keepdims=True)
        bn_ref[...] = jnp.where(nxt < N_EXPERTS, nxt, -1.0).astype(jnp.int32)
        nu_ref[...] = jnp.broadcast_to(ci[N_EXPERTS - 1:N_EXPERTS, :], nu_ref.shape).astype(jnp.int32)

    c = jnp.dot(oh.astype(BF16), ut_ref[...], preferred_element_type=F32)
    r = off[:, 0:1] + run[:, 0:1] + c - 1.0
    pos_ref[0:1, :] = jnp.sum(jnp.where(oh1, r, 0.0), axis=0, keepdims=True).astype(jnp.int32)
    pos_ref[1:2, :] = jnp.sum(jnp.where(oh2, r, 0.0), axis=0, keepdims=True).astype(jnp.int32)
    run[...] += jnp.sum(oh, axis=1, keepdims=True)


def _plan(expert_ids_t, counts, n_rows):
    n = expert_ids_t.shape[1]
    t = PLAN_TILE
    n_blocks = n_rows // MOE_TILE
    assert n_blocks <= PLAN_LANES and TOP_K == 2
    ut = jnp.asarray(np.triu(np.ones((t, t), np.float32)), BF16)
    tab = jax.ShapeDtypeStruct((1, PLAN_LANES), jnp.int32)
    const = lambda shape: pl.BlockSpec(shape, lambda i: (0, 0))
    pos, be, bv, bf, bn, nu = pl.pallas_call(
        _plan_kernel,
        grid=(n // t,),
        in_specs=[pl.BlockSpec((TOP_K, t), lambda i: (0, i)), const((N_EXPERTS, LANES)), const((t, t))],
        out_specs=[pl.BlockSpec((TOP_K, t), lambda i: (0, i))] + [const((1, PLAN_LANES))] * 4 + [const((1, LANES))],
        out_shape=[jax.ShapeDtypeStruct((TOP_K, n), jnp.int32), tab, tab, tab, tab, jax.ShapeDtypeStruct((1, LANES), jnp.int32)],
        scratch_shapes=[pltpu.VMEM((N_EXPERTS, LANES), F32)] * 2,
        compiler_params=_params(("arbitrary",)),
        name="plan",
    )(expert_ids_t, counts, ut)
    return pos, (be.reshape(-1), bv.reshape(-1), bf.reshape(-1), bn.reshape(-1), nu.reshape(-1)[:1])


def _sc_mesh():
    return plsc.VectorSubcoreMesh(core_axis_name="c", subcore_axis_name="s")


def _sc_scatter_rows(x, pos, n_rows):
    n, d = x.shape
    w = SC_WINDOW

    @functools.partial(pl.kernel, out_type=jax.ShapeDtypeStruct((n_rows, d), x.dtype), mesh=_sc_mesh(), scratch_types=[])
    def scatter(x_hbm, p_hbm, o_hbm):
        def body(x_vmem, i_vmem):
            for k in range(TOP_K):
                pltpu.sync_copy(x_vmem, o_hbm.at[i_vmem.at[k]])

        pltpu.emit_pipeline(
            body,
            grid=(n // w,),
            in_specs=[pl.BlockSpec((w, d), lambda i: (i, 0)), pl.BlockSpec((TOP_K, w), lambda i: (0, i))],
            out_specs=[],
            core_axis_name=("c", "s"),
            dimension_semantics=(pltpu.PARALLEL,),
        )(x_hbm, p_hbm)

    return scatter(x, pos)


def _sc_gather_rows(y, pos):
    m = pos.shape[0] * pos.shape[1]
    d = y.shape[1]
    w = SC_WINDOW

    @functools.partial(pl.kernel, out_type=jax.ShapeDtypeStruct((m, d), y.dtype), mesh=_sc_mesh(), scratch_types=[])
    def gather(y_hbm, p_hbm, o_hbm):
        def body(i_vmem, o_vmem):
            pltpu.sync_copy(y_hbm.at[i_vmem.at[0]], o_vmem)

        pltpu.emit_pipeline(
            body,
            grid=(m // w,),
            in_specs=[pl.BlockSpec((1, w), lambda i: (0, i))],
            out_specs=[pl.BlockSpec((w, d), lambda i: (i, 0))],
            core_axis_name=("c", "s"),
            dimension_semantics=(pltpu.PARALLEL,),
        )(p_hbm, o_hbm)

    return gather(y, pos.reshape(1, m))


def _expert_kernel(be_ref, bv_ref, bf_ref, bn_ref, nu_ref, xa_ref, xb_ref, w1_hbm, w3_hbm, w2_hbm, ya_ref, yb_ref,
                   wf1, wf3, wf2, w1b, w3b, w2b, wsem, par, *, layer):
    i = pl.program_id(0)
    tb = xa_ref.shape[0]
    half = tb // 2

    def fetch(e, slot):
        return (pltpu.make_async_copy(w1_hbm.at[layer, e], wf1.at[slot], wsem.at[slot, 0]),
                pltpu.make_async_copy(w3_hbm.at[layer, e], wf3.at[slot], wsem.at[slot, 1]),
                pltpu.make_async_copy(w2_hbm.at[layer, e], wf2.at[slot], wsem.at[slot, 2]))

    def ffn(rows):
        live = lax.broadcasted_iota(jnp.int32, (rows, 1), 0) < bv_ref[i]
        x = jnp.where(live, _unpack_rows(xa_ref[0:rows, :], xb_ref[0:rows, :]), 0.0).astype(BF16)
        y = None
        hc = w1b.shape[1] // FFN_SPLIT
        for c in range(FFN_SPLIT):
            cs = slice(c * hc, (c + 1) * hc)
            a = jnp.dot(x, w1b[:, cs], preferred_element_type=F32)
            b = jnp.dot(x, w3b[:, cs], preferred_element_type=F32)
            hm = (a * jax.nn.sigmoid(a) * b).astype(BF16)
            part = jnp.dot(hm, w2b[cs, :], preferred_element_type=F32)
            y = part if y is None else y + part
        ya_ref[0:rows, :], yb_ref[0:rows, :] = _pack_rows(y)

    @pl.when(i == 0)
    def _():
        par[0] = 0
        for c in fetch(be_ref[0], 0):
            c.start()

    @pl.when(i < nu_ref[0])
    def _():
        @pl.when(bf_ref[i] == 1)
        def _():
            slot = par[0]
            for c in fetch(be_ref[i], slot):
                c.wait()
            w1b[...] = wf1[slot].astype(BF16)
            w3b[...] = wf3[slot].astype(BF16)
            w2b[...] = wf2[slot].astype(BF16)

            @pl.when(bn_ref[i] >= 0)
            def _():
                for c in fetch(bn_ref[i], 1 - slot):
                    c.start()

            par[0] = 1 - slot

        @pl.when(bv_ref[i] > half)
        def _():
            ffn(tb)

        @pl.when(bv_ref[i] <= half)
        def _():
            ffn(half)
            ya_ref[half:, :] = jnp.zeros((tb - half, ya_ref.shape[1]), ya_ref.dtype)
            yb_ref[half:, :] = jnp.zeros((tb - half, yb_ref.shape[1]), yb_ref.dtype)

    @pl.when(i >= nu_ref[0])
    def _():
        ya_ref[...] = jnp.zeros_like(ya_ref)
        yb_ref[...] = jnp.zeros_like(yb_ref)


def _experts(xa, xb, w1, w3, w2, layer, tables):
    n_rows = xa.shape[0]
    d, de = w1.shape[-2:]
    tb = MOE_TILE
    xspec = pl.BlockSpec((tb, PACK_WIDTH), lambda i, be, bv, bf, bn, nu: (jnp.minimum(i, nu[0] - 1), 0))
    yspec = pl.BlockSpec((tb, PACK_WIDTH), lambda i, be, bv, bf, bn, nu: (i, 0))
    anyspec = pl.BlockSpec(memory_space=pl.ANY)
    grid_spec = pltpu.PrefetchScalarGridSpec(
        num_scalar_prefetch=5,
        grid=(n_rows // tb,),
        in_specs=[xspec, xspec, anyspec, anyspec, anyspec],
        out_specs=[yspec, yspec],
        scratch_shapes=[pltpu.VMEM((2, d, de), F32), pltpu.VMEM((2, d, de), F32), pltpu.VMEM((2, de, d), F32),
                        pltpu.VMEM((d, de), BF16), pltpu.VMEM((d, de), BF16), pltpu.VMEM((de, d), BF16),
                        pltpu.SemaphoreType.DMA((2, 3)), pltpu.SMEM((1,), jnp.int32)],
    )
    rows = jax.ShapeDtypeStruct((n_rows, PACK_WIDTH), U32)
    return pl.pallas_call(
        functools.partial(_expert_kernel, layer=layer),
        grid_spec=grid_spec,
        out_shape=[rows, rows],
        compiler_params=_params(("arbitrary",)),
        name="experts",
    )(*tables, xa, xb, w1, w3, w2)


def _final_kernel(ga0, ga1, gb0, gb1, gt_ref, mod_ref, x_ref, fg_ref, o_ref):
    o_ref[...] = _rms(_moe_residual(x_ref, ga0, ga1, gb0, gb1, gt_ref, mod_ref[0][5:6, :]), fg_ref[...])


def _final(x2, ga, gb, gates, mod_l, final_g, seq):
    n, d = x2.shape
    tm = ROW_TILE
    per_batch = seq // tm
    row = lambda w: pl.BlockSpec((tm, w), lambda i: (i, 0))
    second = pl.BlockSpec((tm, PACK_WIDTH), lambda i: (i + n // tm, 0))
    return pl.pallas_call(
        _final_kernel,
        grid=(n // tm,),
        in_specs=[row(PACK_WIDTH), second, row(PACK_WIDTH), second, row(LANES),
                  pl.BlockSpec((1, 6, d), lambda i: (i // per_batch, 0, 0)), row(d), pl.BlockSpec((1, d), lambda i: (0, 0))],
        out_specs=row(d),
        out_shape=jax.ShapeDtypeStruct((n, d), F32),
        compiler_params=_params(("parallel",)),
        name="final_norm",
    )(ga, ga, gb, gb, gates, mod_l, x2, final_g.reshape(1, d))


def kernel(x, c, w_ada, b_ada, norm1_g, w_in, attn_norm_g, hgrn_lb_logits, hgrn_norm_g, w_out, norm2_g, router_group_w, router_group_b, router_expert_w, router_expert_b, moe_w1, moe_w3, moe_w2, final_g):
    bsz, seq, d = x.shape
    depth = w_ada.shape[0]
    n = bsz * seq
    attn_dim = attn_norm_g.shape[1]
    n_heads = attn_dim // ATTN_HEAD_DIM

    lb_w = jax.nn.softmax(hgrn_lb_logits.astype(F32), axis=0)
    lower_bounds = jnp.cumsum(lb_w, axis=0) - lb_w[0:1]
    mod = _adaln(c, w_ada, b_ada)
    bias = _attn_bias(n_heads)
    hconsts = _hgrn_consts()
    n_rows = n * TOP_K + N_EXPERTS * MOE_TILE

    x2 = x.reshape(n, d)
    moe = None
    for layer in range(depth):
        x2, (aq, ak, av, rq, rf, ri, rg) = _in_proj(x2, mod[layer], norm1_g[layer], w_in, layer, seq, moe)
        attn = _attention(aq, ak, av, bias, bsz, seq)
        rec = _hgrn(rq, rf, ri, rg, lower_bounds[layer], hgrn_norm_g[layer], hconsts, bsz, seq)
        gpad = EXPERTS_PER_GROUP - N_GROUPS
        wr = jnp.concatenate([router_group_w[layer].T, jnp.zeros((gpad, d), F32), router_expert_w[layer].T], axis=0)
        br = jnp.concatenate([router_group_b[layer], jnp.zeros((gpad,), F32), router_expert_b[layer]]).reshape(-1, 1)
        x2, ha, hb, eids, gates, counts = _out_proj(attn, rec, x2, mod[layer], attn_norm_g[layer], norm2_g[layer],
                                            w_out, layer, wr.astype(BF16), br, seq)
        pos, tables = _plan(eids, counts, n_rows)
        ya, yb = _experts(_sc_scatter_rows(ha, pos, n_rows), _sc_scatter_rows(hb, pos, n_rows),
                          moe_w1, moe_w3, moe_w2, layer, tables)
        moe = (_sc_gather_rows(ya, pos), _sc_gather_rows(yb, pos), gates, mod[layer])
    return _final(x2, moe[0], moe[1], moe[2], moe[3], final_g, seq).reshape(bsz, seq, d)
```

```python
import functools

import numpy as np
import jax
import jax.numpy as jnp
from jax import lax
from jax.experimental import pallas as pl
from jax.experimental.pallas import tpu as pltpu
from jax.experimental.pallas import tpu_sc as plsc

F32 = jnp.float32
BF16 = jnp.bfloat16
U32 = jnp.uint32

ATTN_HEAD_DIM = 64
ATTN_BLOCK = 128
DILATIONS = (1, 4, 16)
ALIBI_MAX_EXP = 8.0
HGRN_EXPAND = 128
HGRN_CHUNK = 128
N_GROUPS = 4
EXPERTS_PER_GROUP = 8
N_EXPERTS = N_GROUPS * EXPERTS_PER_GROUP
TOP_K = 2
NORM_EPS = 1e-6
LOG2E = 1.4426950408889634

LANES = 128
ATTN_SUPER = ATTN_BLOCK * max(DILATIONS)
ROW_TILE = 512
IN_GROUPS = 16
ATTN_UNROLL = 16
HGRN_TILE = 2048
HGRN_UNROLL = 16
MOE_TILE = 512
FFN_SPLIT = 2
PLAN_TILE = 2048
PLAN_LANES = 256
PACK_WIDTH = 256
SC_WINDOW = 128
VMEM_LIMIT = 56 << 20


def _params(semantics):
    return pltpu.CompilerParams(dimension_semantics=semantics, vmem_limit_bytes=VMEM_LIMIT)


def _rms(x, g):
    return x * lax.rsqrt(jnp.mean(x * x, axis=-1, keepdims=True) + NORM_EPS) * g


def _loop(trips, body):
    if trips == 1:
        body(0, 0)
    else:
        lax.fori_loop(0, trips, body, 0)


def _round_robin(gens):
    live = list(gens)
    while live:
        for gen in list(live):
            try:
                next(gen)
            except StopIteration:
                live.remove(gen)


def _dot_nt(a, b):
    return lax.dot_general(a, b, (((1,), (1,)), ((), ())), preferred_element_type=F32)


def _dot_tn(a, b):
    return lax.dot_general(a, b, (((0,), (0,)), ((), ())), preferred_element_type=F32)


def _pack_pair(hi, lo):
    hb = lax.bitcast_convert_type(hi.astype(BF16).astype(F32), U32)
    lb = lax.bitcast_convert_type(lo.astype(BF16).astype(F32), U32)
    return hb | (lb >> 16)


def _unpack_pair(u):
    hi = lax.bitcast_convert_type(u & jnp.uint32(0xFFFF0000), F32)
    lo = lax.bitcast_convert_type(u << 16, F32)
    return hi, lo


def _pack_rows(y):
    w = PACK_WIDTH
    return _pack_pair(y[:, 0:w], y[:, 2 * w:3 * w]), _pack_pair(y[:, w:2 * w], y[:, 3 * w:4 * w])


def _unpack_rows(ua, ub):
    ha, la = _unpack_pair(ua)
    hb, lb = _unpack_pair(ub)
    return jnp.concatenate([ha, hb, la, lb], axis=1)


def _moe_residual(x_ref, ga0, ga1, gb0, gb1, gt_ref, gate_row):
    gt = gt_ref[...]
    y = gt[:, 0:1] * _unpack_rows(ga0[...], gb0[...]) + gt[:, 1:2] * _unpack_rows(ga1[...], gb1[...])
    return x_ref[...] + gate_row * y


def _ada_kernel(c_ref, w_ref, b_ref, o_ref):
    c = c_ref[...]
    ca = c * jax.nn.sigmoid(c)
    hi = ca.astype(BF16)
    lo = (ca - hi.astype(F32)).astype(BF16)
    w = w_ref[0].astype(BF16)
    o_ref[0] = (jnp.dot(hi, w, preferred_element_type=F32) + jnp.dot(lo, w, preferred_element_type=F32)) + b_ref[0]


def _adaln(c, w_ada, b_ada):
    depth, d, n6 = w_ada.shape
    bsz = c.shape[0]
    rows = 8
    cp = jnp.zeros((rows, d), F32).at[:bsz].set(c)
    tn = 1536
    out = pl.pallas_call(
        _ada_kernel,
        grid=(depth, n6 // tn),
        in_specs=[
            pl.BlockSpec((rows, d), lambda l, j: (0, 0)),
            pl.BlockSpec((1, d, tn), lambda l, j: (l, 0, j)),
            pl.BlockSpec((1, 1, tn), lambda l, j: (l, 0, j)),
        ],
        out_specs=pl.BlockSpec((1, rows, tn), lambda l, j: (l, 0, j)),
        out_shape=jax.ShapeDtypeStruct((depth, rows, n6), F32),
        compiler_params=_params(("parallel", "parallel")),
        name="adaln",
    )(cp, w_ada, b_ada.reshape(depth, 1, n6))
    return out[:, :bsz].reshape(depth, bsz, 6, d)


def _in_kernel(*refs, n_moe):
    moe, (x_ref, mod_ref, g_ref, w_ref), rest = refs[:n_moe], refs[n_moe:n_moe + 4], refs[n_moe + 4:]
    hb = rest[-1]
    i = pl.program_id(0)

    @pl.when(i == 0)
    def _():
        hb[1] = jnp.zeros(hb.shape[1:], hb.dtype)

    outs = rest[1:-1] if n_moe else rest[:-1]
    mod = mod_ref[0]
    prev = hb[(i + 1) % 2]
    width = outs[0].shape[1]
    tm = x_ref.shape[0]
    groups = [slice(r, r + tm // IN_GROUPS) for r in range(0, tm, tm // IN_GROUPS)]

    def prologue(rs):
        if n_moe:
            ga0, ga1, gb0, gb1, gt_ref, pmod_ref = moe
            gt = gt_ref[rs, :]
            y = gt[:, 0:1] * _unpack_rows(ga0[rs, :], gb0[rs, :]) + gt[:, 1:2] * _unpack_rows(ga1[rs, :], gb1[rs, :])
            x = x_ref[rs, :] + pmod_ref[0][5:6, :] * y
            rest[0][rs, :] = x
        else:
            x = x_ref[rs, :]
        hb[i % 2, rs, :] = (_rms(x, g_ref[...]) * (1.0 + mod[1:2, :]) + mod[0:1, :]).astype(BF16)

    per = -(-len(groups) // len(outs))
    for k, o in enumerate(outs):
        for rs in groups[k * per:(k + 1) * per]:
            prologue(rs)
        w = w_ref[0, :, k * width:(k + 1) * width].astype(BF16)
        o[...] = jnp.dot(prev, w, preferred_element_type=F32).astype(o.dtype)


def _in_proj(x2, mod_l, g, w_in, layer, seq, moe=None):
    n, d = x2.shape
    n_out = w_in.shape[2]
    width = 512
    tm = ROW_TILE
    per_batch = seq // tm
    last = n // tm - 1
    cur = lambda i: jnp.minimum(i, last)
    row = lambda w: pl.BlockSpec((tm, w), lambda i: (cur(i), 0))
    mod_spec = pl.BlockSpec((1, 6, d), lambda i: (cur(i) // per_batch, 0, 0))
    in_specs, args, out_specs, out_shape = [], [], [], []
    if moe is not None:
        ga, gb, gates, mod_prev = moe
        second = pl.BlockSpec((tm, PACK_WIDTH), lambda i: (cur(i) + n // tm, 0))
        in_specs += [row(PACK_WIDTH), second, row(PACK_WIDTH), second, row(LANES), mod_spec]
        args += [ga, ga, gb, gb, gates, mod_prev]
        out_specs.append(row(d))
        out_shape.append(jax.ShapeDtypeStruct((n, d), F32))
    in_specs += [row(d), mod_spec, pl.BlockSpec((1, d), lambda i: (0, 0)),
                 pl.BlockSpec((1, d, n_out), lambda i: (layer, 0, 0), pipeline_mode=pl.Buffered(1))]
    args += [x2, mod_l, g.reshape(1, d), w_in]
    out_dtypes = (F32, F32, F32, BF16, F32, BF16, F32)
    out_specs += [pl.BlockSpec((tm, width), lambda i: (jnp.maximum(i - 1, 0), 0))] * len(out_dtypes)
    out_shape += [jax.ShapeDtypeStruct((n, width), dt) for dt in out_dtypes]
    outs = pl.pallas_call(
        functools.partial(_in_kernel, n_moe=0 if moe is None else 6),
        grid=(n // tm + 1,),
        in_specs=in_specs,
        out_specs=out_specs,
        out_shape=out_shape,
        scratch_shapes=[pltpu.VMEM((2, tm, d), BF16)],
        compiler_params=_params(("arbitrary",)),
        name="in_proj",
    )(*args)
    return (x2, outs) if moe is None else (outs[0], outs[1:])


def _attn_bias(n_heads):
    qi = np.arange(ATTN_BLOCK)[:, None]
    kj = np.arange(2 * ATTN_BLOCK)[None, :]
    dist = ATTN_BLOCK + qi - kj
    valid = (dist >= 0) & (dist <= ATTN_BLOCK)
    slopes = np.exp2(-ALIBI_MAX_EXP * np.arange(1, n_heads + 1, dtype=np.float32) / n_heads)
    out = np.empty((2, len(DILATIONS), n_heads, ATTN_BLOCK, 2 * ATTN_BLOCK), np.float32)
    for p, dil in enumerate(DILATIONS):
        b = -slopes[:, None, None] * (dist * dil).astype(np.float32)[None] * np.float32(LOG2E)
        out[0, p] = np.where(valid[None], b, -np.inf)
        out[1, p] = np.where((valid & (kj >= ATTN_BLOCK))[None], b, -np.inf)
    return jnp.asarray(out)


def _attn_kernel(q_ref, k_ref, v_ref, bias_ref, o_ref, kcar, vcar, pbuf, vbuf, oscr, mscr, dscr):
    n = pl.program_id(2)
    sb = q_ref.shape[0]
    is_lo = lax.broadcasted_iota(jnp.int32, (ATTN_BLOCK, LANES), 1) < ATTN_HEAD_DIM
    scale = ATTN_HEAD_DIM ** -0.5 * LOG2E

    @pl.when(n == 0)
    def _():
        kcar[...] = jnp.zeros_like(kcar)
        vcar[...] = jnp.zeros_like(vcar)

    pbuf[...] = jnp.zeros_like(pbuf)
    vbuf[...] = jnp.zeros_like(vbuf)
    lo2 = lax.broadcasted_iota(jnp.int32, (2 * ATTN_BLOCK, LANES), 1) < ATTN_HEAD_DIM
    zero = jnp.zeros((2 * ATTN_BLOCK, LANES), BF16)
    den_lo = lo2.astype(BF16)
    den_hi = jnp.logical_not(lo2).astype(BF16)

    for p, dil in enumerate(DILATIONS):
        nb = (sb // ATTN_BLOCK) // dil

        def rows_of(it, dil=dil, nb=nb):
            start = (it % nb) * (ATTN_BLOCK * dil) + it // nb
            return pl.ds(start, ATTN_BLOCK, stride=dil) if dil > 1 else pl.ds(pl.multiple_of(start, ATTN_BLOCK), ATTN_BLOCK)

        def pv(it, probs, vaug, p=p, rows_of=rows_of):
            rows = rows_of(it)
            of = jnp.dot(probs, vaug, preferred_element_type=F32)
            oscr[p, rows, :] = of[:, :LANES]
            dscr[p, rows, :] = of[:, LANES:]

        def qk(it, p=p, nb=nb, rows_of=rows_of):
            r = it // nb
            rows = rows_of(it)
            crow = pl.ds(pl.multiple_of(r * ATTN_BLOCK, ATTN_BLOCK), ATTN_BLOCK)
            qb = q_ref[rows, :] * scale
            kc = k_ref[rows, :].astype(BF16)
            vc = v_ref[rows, :].astype(BF16)
            kcat = jnp.concatenate([kcar[p, crow, :], kc], axis=0)
            vcat = jnp.concatenate([vcar[p, crow, :], vc], axis=0)
            vaug = jnp.concatenate([jnp.concatenate([jnp.where(lo2, vcat, zero), den_lo], axis=1),
                                    jnp.concatenate([jnp.where(lo2, zero, vcat), den_hi], axis=1)], axis=0)
            kcar[p, crow, :] = kc
            vcar[p, crow, :] = vc
            first = jnp.logical_and(n == 0, it % nb == 0).astype(jnp.int32)
            probs, acc_m = [], None
            for hh in range(2):
                sel = is_lo if hh == 0 else jnp.logical_not(is_lo)
                qm = jnp.where(sel, qb, 0.0).astype(BF16)
                s = _dot_nt(qm, kcat) + bias_ref[first, p, hh]
                m = jnp.max(s, axis=-1, keepdims=True)
                probs.append(jnp.exp2(s - m).astype(BF16))
                mb = jnp.broadcast_to(m, (ATTN_BLOCK, LANES))
                acc_m = mb if hh == 0 else jnp.where(sel, mb, acc_m)
            mscr[p, rows, :] = acc_m
            return jnp.concatenate(probs, axis=1), vaug

        def body(g, carry, pv=pv, qk=qk):
            it0 = g * ATTN_UNROLL
            pend = (pbuf[...], vbuf[...])
            for u in range(ATTN_UNROLL):
                cur = qk(it0 + u)
                pv(jnp.maximum(it0 + u - 1, 0), *pend)
                pend = cur
            pbuf[...] = pend[0]
            vbuf[...] = pend[1]
            return carry

        n_it = sb // ATTN_BLOCK
        lax.fori_loop(0, n_it // ATTN_UNROLL, body, 0)
        pv(n_it - 1, pbuf[...], vbuf[...])

    ct = 256

    def combine(t, carry):
        rs = pl.ds(pl.multiple_of(t * ct, ct), ct)
        m0, m1, m2 = mscr[0, rs, :], mscr[1, rs, :], mscr[2, rs, :]
        mx = jnp.maximum(jnp.maximum(m0, m1), m2)
        w0, w1, w2 = jnp.exp2(m0 - mx), jnp.exp2(m1 - mx), jnp.exp2(m2 - mx)
        num = w0 * oscr[0, rs, :] + w1 * oscr[1, rs, :] + w2 * oscr[2, rs, :]
        den = w0 * dscr[0, rs, :] + w1 * dscr[1, rs, :] + w2 * dscr[2, rs, :]
        o_ref[rs, :] = (num / den).astype(o_ref.dtype)
        return carry

    lax.fori_loop(0, sb // ct, combine, 0)


def _attention(aq, ak, av, bias, bsz, seq):
    n, width = aq.shape
    sb = ATTN_SUPER
    nsb = seq // sb
    pairs = width // LANES
    npat = len(DILATIONS)
    blk = pl.BlockSpec((sb, LANES), lambda b, h, t: (b * nsb + t, h))
    return pl.pallas_call(
        _attn_kernel,
        grid=(bsz, pairs, nsb),
        in_specs=[blk, blk, blk,
                  pl.BlockSpec((2, npat, 2, ATTN_BLOCK, 2 * ATTN_BLOCK), lambda b, h, t: (0, 0, h, 0, 0))],
        out_specs=blk,
        out_shape=jax.ShapeDtypeStruct((n, width), BF16),
        scratch_shapes=[
            pltpu.VMEM((npat, max(DILATIONS) * ATTN_BLOCK, LANES), BF16),
            pltpu.VMEM((npat, max(DILATIONS) * ATTN_BLOCK, LANES), BF16),
            pltpu.VMEM((ATTN_BLOCK, 4 * ATTN_BLOCK), BF16),
            pltpu.VMEM((4 * ATTN_BLOCK, 2 * LANES), BF16),
            pltpu.VMEM((npat, sb, LANES), F32),
            pltpu.VMEM((npat, sb, LANES), F32),
            pltpu.VMEM((npat, sb, LANES), F32),
        ],
        compiler_params=_params(("parallel", "parallel", "arbitrary")),
        name="dilated_attn",
    )(aq, ak, av, bias)


def _hgrn_consts():
    c = HGRN_CHUNK
    t = np.arange(c)[:, None]
    u = np.arange(c)[None, :]
    low, masks = [], []
    m = c // 2
    while m >= 1:
        mid = (t // (2 * m)) * (2 * m) + m
        is_q = t >= mid
        if m in (4, 2):
            low.append((is_q & (u >= mid) & (u <= t)) | (~is_q & (u > t) & (u < mid)))
        masks.append(((t // (2 * m)) == (u // (2 * m))) & (t % (2 * m) >= m) & (u % (2 * m) < m))
        m //= 2
    masks.append(t == u)
    f = lambda a: np.asarray(a, np.float32)
    return (jnp.asarray(f(u <= t), BF16), jnp.asarray(f(np.concatenate(low, axis=0)), BF16),
            jnp.asarray(f(np.broadcast_to(t % 2 == 1, (c, c)))), jnp.asarray(f(np.stack(masks))))


def _hgrn_kernel(q_ref, z_ref, v_ref, g_ref, lb_ref, gn_ref, tri_ref, low_ref, odd_ref, msk_ref, o_ref, st):
    c = HGRN_CHUNK

    @pl.when(pl.program_id(2) == 0)
    def _():
        st[...] = jnp.zeros_like(st)

    lb = jnp.maximum(lb_ref[...], 0.0)
    log_lb = jnp.log(lb)
    log_1m = jnp.log1p(-lb)
    one_m = 1.0 - lb
    gn = gn_ref[...]
    n_levels = msk_ref.shape[0] - 1
    coarse = [c >> (i + 1) for i in range(n_levels) if (c >> (i + 1)) >= 8]

    def chunk(ci):
        rs = pl.ds(pl.multiple_of(ci * c, c), c)
        z = z_ref[rs, :]
        qb = q_ref[rs, :]
        vb = v_ref[rs, :]
        ez = jnp.exp(-jnp.abs(z))
        ls = jnp.minimum(z, 0.0) - jnp.log(1.0 + ez)
        b2 = log_1m + ls
        lf = jnp.maximum(log_lb, b2) + jnp.log(1.0 + jnp.exp(-jnp.abs(log_lb - b2)))
        rz = 1.0 / (1.0 + ez)
        kb = (one_m * jnp.where(z > 0.0, ez * rz, rz)).astype(BF16)
        l2 = lf * LOG2E
        hi = l2.astype(BF16)
        hl = jnp.concatenate([hi, (l2 - hi.astype(F32)).astype(BF16)], axis=1)
        yield
        bb = jnp.dot(tri_ref[...], hl, preferred_element_type=F32)
        b = bb[:, :HGRN_EXPAND] + bb[:, HGRN_EXPAND:]
        dl = jnp.dot(low_ref[...], hl, preferred_element_type=F32)
        dl = dl[:, :HGRN_EXPAND] + dl[:, HGRN_EXPAND:]
        yield

        def level(d):
            e = jnp.exp2(d).astype(BF16)
            return _dot_nt(qb * e, kb * e)

        a = msk_ref[n_levels] * _dot_nt(qb, kb)
        for i in range(n_levels):
            m = c >> (i + 1)
            if m >= 8:
                parts = []
                for r0 in range(0, c, 2 * m):
                    mid = b[r0 + m - 1:r0 + m, :]
                    parts += [mid - b[r0:r0 + m, :], b[r0 + m:r0 + 2 * m, :] - mid]
                d = jnp.concatenate(parts, axis=0)
            elif m > 1:
                d = dl[(i - len(coarse)) * c:(i - len(coarse) + 1) * c]
            else:
                d = l2 * odd_ref[...]
            a = a + msk_ref[i] * level(d)
            if i % 3 == 2:
                yield
        b_last = b[c - 1:c, :]
        s_t = st[...]
        o = _dot_nt(qb * jnp.exp2(b).astype(BF16), s_t.astype(BF16)) + jnp.dot(a.astype(BF16), vb, preferred_element_type=F32)
        st[...] = jnp.exp2(b_last) * s_t + _dot_tn(vb, kb * jnp.exp2(b_last - b).astype(BF16))
        yield
        g = g_ref[rs, :]
        o_ref[rs, :] = (_rms(o, gn) * (g * jax.nn.sigmoid(g))).astype(o_ref.dtype)

    def group(gi, carry):
        _round_robin([chunk(gi * HGRN_UNROLL + u) for u in range(HGRN_UNROLL)])
        return carry

    _loop(q_ref.shape[0] // (c * HGRN_UNROLL), group)


def _hgrn(rq, rf, ri, rg, lb, gn, consts, bsz, seq):
    n, width = rq.shape
    heads = width // HGRN_EXPAND
    ts = HGRN_TILE
    nt = seq // ts
    blk = pl.BlockSpec((ts, HGRN_EXPAND), lambda b, h, t: (b * nt + t, h))
    vec = pl.BlockSpec((1, HGRN_EXPAND), lambda b, h, t: (0, h))
    const = lambda a: pl.BlockSpec(a.shape, lambda b, h, t: (0,) * a.ndim)
    return pl.pallas_call(
        _hgrn_kernel,
        grid=(bsz, heads, nt),
        in_specs=[blk, blk, blk, blk, vec, vec] + [const(a) for a in consts],
        out_specs=blk,
        out_shape=jax.ShapeDtypeStruct((n, width), BF16),
        scratch_shapes=[pltpu.VMEM((HGRN_EXPAND, HGRN_EXPAND), F32)],
        compiler_params=_params(("parallel", "parallel", "arbitrary")),
        name="hgrn2",
    )(rq, rf, ri, rg, lb.reshape(1, width), gn.reshape(1, width), *consts)


def _route_math(lg):
    row = lax.broadcasted_iota(jnp.int32, (EXPERTS_PER_GROUP, lg.shape[1]), 0).astype(F32)
    none = float(EXPERTS_PER_GROUP)

    def first_argmax(vals):
        mx = jnp.max(vals, axis=0, keepdims=True)
        return mx, jnp.min(jnp.where(vals == mx, row, none), axis=0, keepdims=True)

    g = lg[0:EXPERTS_PER_GROUP, :]
    is_group = row < N_GROUPS
    gmax, gidx = first_argmax(jnp.where(is_group, g, -jnp.inf))
    gprob = 1.0 / jnp.sum(jnp.where(is_group, jnp.exp(g - gmax), 0.0), axis=0, keepdims=True)
    el = lg[EXPERTS_PER_GROUP:2 * EXPERTS_PER_GROUP, :]
    for k in range(1, N_GROUPS):
        el = jnp.where(gidx == k, lg[(k + 1) * EXPERTS_PER_GROUP:(k + 2) * EXPERTS_PER_GROUP, :], el)
    t1, i1 = first_argmax(el)
    t2, i2 = first_argmax(jnp.where(row == i1, -jnp.inf, el))
    ex = jnp.exp(t2 - t1)
    base = gidx * EXPERTS_PER_GROUP
    eids = jnp.concatenate([base + i1, base + i2], axis=0).astype(jnp.int32)
    gates = jnp.concatenate([gprob / (1.0 + ex), gprob * ex / (1.0 + ex)], axis=0)
    return eids, gates


def _out_kernel(a_ref, r_ref, x_ref, mod_ref, ag_ref, g2_ref, wo_ref, wr_ref, br_ref, xo_ref, ha_ref, hb_ref, e_ref, gt_ref, cnt_ref):
    mod = mod_ref[0]
    mix = jnp.concatenate([_rms(a_ref[...].astype(F32), ag_ref[...]).astype(BF16), r_ref[...]], axis=1)
    xn = x_ref[...] + mod[2:3, :] * jnp.dot(mix, wo_ref[0].astype(BF16), preferred_element_type=F32)
    xo_ref[...] = xn
    h2 = _rms(xn, g2_ref[...]) * (1.0 + mod[4:5, :]) + mod[3:4, :]
    ha_ref[...], hb_ref[...] = _pack_rows(h2)
    eids, gates = _route_math(_dot_nt(wr_ref[...], h2.astype(BF16)) + br_ref[...])
    e_ref[...] = eids
    eio = lax.broadcasted_iota(jnp.int32, (N_EXPERTS, eids.shape[1]), 0)
    hit = jnp.logical_or(eids[0:1, :] == eio, eids[1:2, :] == eio).astype(F32)

    @pl.when(pl.program_id(0) == 0)
    def _():
        cnt_ref[...] = jnp.zeros_like(cnt_ref)

    cnt_ref[...] += jnp.sum(hit, axis=1, keepdims=True)
    pad = jnp.zeros((LANES - TOP_K, LANES), F32)
    for k in range(h2.shape[0] // LANES):
        tile = jnp.concatenate([gates[:, k * LANES:(k + 1) * LANES], pad], axis=0)
        gt_ref[k * LANES:(k + 1) * LANES, :] = tile.T


def _out_proj(attn, rec, x2, mod_l, ag, g2, w_out, layer, wr_bf, br, seq):
    n, d = x2.shape
    half = attn.shape[1]
    tm = ROW_TILE
    per_batch = seq // tm
    row = lambda w: pl.BlockSpec((tm, w), lambda i: (i, 0))
    full = lambda a: pl.BlockSpec(a.shape, lambda i: (0,) * a.ndim, pipeline_mode=pl.Buffered(1))
    ag2, g22 = ag.reshape(1, half), g2.reshape(1, d)
    return pl.pallas_call(
        _out_kernel,
        grid=(n // tm,),
        in_specs=[row(half), row(half), row(d), pl.BlockSpec((1, 6, d), lambda i: (i // per_batch, 0, 0)),
                  full(ag2), full(g22), pl.BlockSpec((1, d, d), lambda i: (layer, 0, 0), pipeline_mode=pl.Buffered(1)),
                  full(wr_bf), full(br)],
        out_specs=[row(d), row(PACK_WIDTH), row(PACK_WIDTH), pl.BlockSpec((TOP_K, tm), lambda i: (0, i)), row(LANES),
                   pl.BlockSpec((N_EXPERTS, LANES), lambda i: (0, 0))],
        out_shape=[jax.ShapeDtypeStruct((n, d), F32), jax.ShapeDtypeStruct((n, PACK_WIDTH), U32),
                   jax.ShapeDtypeStruct((n, PACK_WIDTH), U32), jax.ShapeDtypeStruct((TOP_K, n), jnp.int32),
                   jax.ShapeDtypeStruct((n, LANES), F32), jax.ShapeDtypeStruct((N_EXPERTS, LANES), F32)],
        compiler_params=_params(("arbitrary",)),
        name="out_proj",
    )(attn, rec, x2, mod_l, ag2, g22, w_out, wr_bf, br)


def _plan_kernel(e_ref, cnt_ref, ut_ref, pos_ref, be_ref, bv_ref, bf_ref, bn_ref, nu_ref, off, run):
    i = pl.program_id(0)
    t = e_ref.shape[1]
    eio = lax.broadcasted_iota(jnp.int32, (N_EXPERTS, t), 0)
    oh1 = e_ref[0:1, :] == eio
    oh2 = e_ref[1:2, :] == eio
    oh = jnp.logical_or(oh1, oh2).astype(F32)

    @pl.when(i == 0)
    def _():
        counts = cnt_ref[...]
        nb = jnp.floor((counts + (MOE_TILE - 1)) * (1.0 / MOE_TILE))
        er = lax.broadcasted_iota(jnp.int32, (N_EXPERTS, N_EXPERTS), 0)
        ec = lax.broadcasted_iota(jnp.int32, (N_EXPERTS, N_EXPERTS), 1)
        cum_excl = jnp.dot((ec < er).astype(BF16), nb.astype(BF16), preferred_element_type=F32)
        off[...] = cum_excl * MOE_TILE
        run[...] = jnp.zeros_like(run)
        ce, ci, cn = cum_excl[:, 0:1], (cum_excl + nb)[:, 0:1], counts[:, 0:1]
        b = lax.broadcasted_iota(jnp.int32, (N_EXPERTS, be_ref.shape[1]), 1).astype(F32)
        member = jnp.logical_and(b >= ce, b < ci)
        be = jnp.minimum(jnp.sum((b >= ci).astype(F32), axis=0, keepdims=True), N_EXPERTS - 1.0)
        bv = jnp.sum(jnp.where(member, jnp.minimum(cn - (b - ce) * MOE_TILE, float(MOE_TILE)), 0.0), axis=0, keepdims=True)
        be_ref[...] = be.astype(jnp.int32)
        bv_ref[...] = bv.astype(jnp.int32)
        has = nb[:, 0:1] > 0.0
        eid = lax.broadcasted_iota(jnp.int32, b.shape, 0).astype(F32)
        bf_ref[...] = jnp.sum(jnp.where(jnp.logical_and(b == ce, has), 1.0, 0.0), axis=0, keepdims=True).astype(jnp.int32)
        nxt = jnp.min(jnp.where(jnp.logical_and(eid > be, has), eid, float(N_EXPERTS)), axis=0, keepdims=True)
        bn_ref[...] = jnp.where(nxt < N_EXPERTS, nxt, -1.0).astype(jnp.int32)
        nu_ref[...] = jnp.broadcast_to(ci[N_EXPERTS - 1:N_EXPERTS, :], nu_ref.shape).astype(jnp.int32)

    c = jnp.dot(oh.astype(BF16), ut_ref[...], preferred_element_type=F32)
    r = off[:, 0:1] + run[:, 0:1] + c - 1.0
    pos_ref[0:1, :] = jnp.sum(jnp.where(oh1, r, 0.0), axis=0, keepdims=True).astype(jnp.int32)
    pos_ref[1:2, :] = jnp.sum(jnp.where(oh2, r, 0.0), axis=0, keepdims=True).astype(jnp.int32)
    run[...] += jnp.sum(oh, axis=1, keepdims=True)


def _plan(expert_ids_t, counts, n_rows):
    n = expert_ids_t.shape[1]
    t = PLAN_TILE
    n_blocks = n_rows // MOE_TILE
    assert n_blocks <= PLAN_LANES and TOP_K == 2
    ut = jnp.asarray(np.triu(np.ones((t, t), np.float32)), BF16)
    tab = jax.ShapeDtypeStruct((1, PLAN_LANES), jnp.int32)
    const = lambda shape: pl.BlockSpec(shape, lambda i: (0, 0))
    pos, be, bv, bf, bn, nu = pl.pallas_call(
        _plan_kernel,
        grid=(n // t,),
        in_specs=[pl.BlockSpec((TOP_K, t), lambda i: (0, i)), const((N_EXPERTS, LANES)), const((t, t))],
        out_specs=[pl.BlockSpec((TOP_K, t), lambda i: (0, i))] + [const((1, PLAN_LANES))] * 4 + [const((1, LANES))],
        out_shape=[jax.ShapeDtypeStruct((TOP_K, n), jnp.int32), tab, tab, tab, tab, jax.ShapeDtypeStruct((1, LANES), jnp.int32)],
        scratch_shapes=[pltpu.VMEM((N_EXPERTS, LANES), F32)] * 2,
        compiler_params=_params(("arbitrary",)),
        name="plan",
    )(expert_ids_t, counts, ut)
    return pos, (be.reshape(-1), bv.reshape(-1), bf.reshape(-1), bn.reshape(-1), nu.reshape(-1)[:1])


def _sc_mesh():
    return plsc.VectorSubcoreMesh(core_axis_name="c", subcore_axis_name="s")


def _sc_scatter_rows(x, pos, n_rows):
    n, d = x.shape
    w = SC_WINDOW

    @functools.partial(pl.kernel, out_type=jax.ShapeDtypeStruct((n_rows, d), x.dtype), mesh=_sc_mesh(), scratch_types=[])
    def scatter(x_hbm, p_hbm, o_hbm):
        def body(x_vmem, i_vmem):
            for k in range(TOP_K):
                pltpu.sync_copy(x_vmem, o_hbm.at[i_vmem.at[k]])

        pltpu.emit_pipeline(
            body,
            grid=(n // w,),
            in_specs=[pl.BlockSpec((w, d), lambda i: (i, 0)), pl.BlockSpec((TOP_K, w), lambda i: (0, i))],
            out_specs=[],
            core_axis_name=("c", "s"),
            dimension_semantics=(pltpu.PARALLEL,),
        )(x_hbm, p_hbm)

    return scatter(x, pos)


def _sc_gather_rows(y, pos):
    m = pos.shape[0] * pos.shape[1]
    d = y.shape[1]
    w = SC_WINDOW

    @functools.partial(pl.kernel, out_type=jax.ShapeDtypeStruct((m, d), y.dtype), mesh=_sc_mesh(), scratch_types=[])
    def gather(y_hbm, p_hbm, o_hbm):
        def body(i_vmem, o_vmem):
            pltpu.sync_copy(y_hbm.at[i_vmem.at[0]], o_vmem)

        pltpu.emit_pipeline(
            body,
            grid=(m // w,),
            in_specs=[pl.BlockSpec((1, w), lambda i: (0, i))],
            out_specs=[pl.BlockSpec((w, d), lambda i: (i, 0))],
            core_axis_name=("c", "s"),
            dimension_semantics=(pltpu.PARALLEL,),
        )(p_hbm, o_hbm)

    return gather(y, pos.reshape(1, m))


def _expert_kernel(be_ref, bv_ref, bf_ref, bn_ref, nu_ref, xa_ref, xb_ref, w1_hbm, w3_hbm, w2_hbm, ya_ref, yb_ref,
                   wf1, wf3, wf2, w1b, w3b, w2b, wsem, par, *, layer):
    i = pl.program_id(0)
    tb = xa_ref.shape[0]
    half = tb // 2

    def fetch(e, slot):
        return (pltpu.make_async_copy(w1_hbm.at[layer, e], wf1.at[slot], wsem.at[slot, 0]),
                pltpu.make_async_copy(w3_hbm.at[layer, e], wf3.at[slot], wsem.at[slot, 1]),
                pltpu.make_async_copy(w2_hbm.at[layer, e], wf2.at[slot], wsem.at[slot, 2]))

    def ffn(rows):
        live = lax.broadcasted_iota(jnp.int32, (rows, 1), 0) < bv_ref[i]
        x = jnp.where(live, _unpack_rows(xa_ref[0:rows, :], xb_ref[0:rows, :]), 0.0).astype(BF16)
        y = None
        hc = w1b.shape[1] // FFN_SPLIT
        for c in range(FFN_SPLIT):
            cs = slice(c * hc, (c + 1) * hc)
            a = jnp.dot(x, w1b[:, cs], preferred_element_type=F32)
            b = jnp.dot(x, w3b[:, cs], preferred_element_type=F32)
            hm = (a * jax.nn.sigmoid(a) * b).astype(BF16)
            part = jnp.dot(hm, w2b[cs, :], preferred_element_type=F32)
            y = part if y is None else y + part
        ya_ref[0:rows, :], yb_ref[0:rows, :] = _pack_rows(y)

    @pl.when(i == 0)
    def _():
        par[0] = 0
        for c in fetch(be_ref[0], 0):
            c.start()

    @pl.when(i < nu_ref[0])
    def _():
        @pl.when(bf_ref[i] == 1)
        def _():
            slot = par[0]
            for c in fetch(be_ref[i], slot):
                c.wait()
            w1b[...] = wf1[slot].astype(BF16)
            w3b[...] = wf3[slot].astype(BF16)
            w2b[...] = wf2[slot].astype(BF16)

            @pl.when(bn_ref[i] >= 0)
            def _():
                for c in fetch(bn_ref[i], 1 - slot):
                    c.start()

            par[0] = 1 - slot

        @pl.when(bv_ref[i] > half)
        def _():
            ffn(tb)

        @pl.when(bv_ref[i] <= half)
        def _():
            ffn(half)
            ya_ref[half:, :] = jnp.zeros((tb - half, ya_ref.shape[1]), ya_ref.dtype)
            yb_ref[half:, :] = jnp.zeros((tb - half, yb_ref.shape[1]), yb_ref.dtype)

    @pl.when(i >= nu_ref[0])
    def _():
        ya_ref[...] = jnp.zeros_like(ya_ref)
        yb_ref[...] = jnp.zeros_like(yb_ref)


def _experts(xa, xb, w1, w3, w2, layer, tables):
    n_rows = xa.shape[0]
    d, de = w1.shape[-2:]
    tb = MOE_TILE
    xspec = pl.BlockSpec((tb, PACK_WIDTH), lambda i, be, bv, bf, bn, nu: (jnp.minimum(i, nu[0] - 1), 0))
    yspec = pl.BlockSpec((tb, PACK_WIDTH), lambda i, be, bv, bf, bn, nu: (i, 0))
    anyspec = pl.BlockSpec(memory_space=pl.ANY)
    grid_spec = pltpu.PrefetchScalarGridSpec(
        num_scalar_prefetch=5,
        grid=(n_rows // tb,),
        in_specs=[xspec, xspec, anyspec, anyspec, anyspec],
        out_specs=[yspec, yspec],
        scratch_shapes=[pltpu.VMEM((2, d, de), F32), pltpu.VMEM((2, d, de), F32), pltpu.VMEM((2, de, d), F32),
                        pltpu.VMEM((d, de), BF16), pltpu.VMEM((d, de), BF16), pltpu.VMEM((de, d), BF16),
                        pltpu.SemaphoreType.DMA((2, 3)), pltpu.SMEM((1,), jnp.int32)],
    )
    rows = jax.ShapeDtypeStruct((n_rows, PACK_WIDTH), U32)
    return pl.pallas_call(
        functools.partial(_expert_kernel, layer=layer),
        grid_spec=grid_spec,
        out_shape=[rows, rows],
        compiler_params=_params(("arbitrary",)),
        name="experts",
    )(*tables, xa, xb, w1, w3, w2)


def _final_kernel(ga0, ga1, gb0, gb1, gt_ref, mod_ref, x_ref, fg_ref, o_ref):
    o_ref[...] = _rms(_moe_residual(x_ref, ga0, ga1, gb0, gb1, gt_ref, mod_ref[0][5:6, :]), fg_ref[...])


def _final(x2, ga, gb, gates, mod_l, final_g, seq):
    n, d = x2.shape
    tm = ROW_TILE
    per_batch = seq // tm
    row = lambda w: pl.BlockSpec((tm, w), lambda i: (i, 0))
    second = pl.BlockSpec((tm, PACK_WIDTH), lambda i: (i + n // tm, 0))
    return pl.pallas_call(
        _final_kernel,
        grid=(n // tm,),
        in_specs=[row(PACK_WIDTH), second, row(PACK_WIDTH), second, row(LANES),
                  pl.BlockSpec((1, 6, d), lambda i: (i // per_batch, 0, 0)), row(d), pl.BlockSpec((1, d), lambda i: (0, 0))],
        out_specs=row(d),
        out_shape=jax.ShapeDtypeStruct((n, d), F32),
        compiler_params=_params(("parallel",)),
        name="final_norm",
    )(ga, ga, gb, gb, gates, mod_l, x2, final_g.reshape(1, d))


def kernel(x, c, w_ada, b_ada, norm1_g, w_in, attn_norm_g, hgrn_lb_logits, hgrn_norm_g, w_out, norm2_g, router_group_w, router_group_b, router_expert_w, router_expert_b, moe_w1, moe_w3, moe_w2, final_g):
    bsz, seq, d = x.shape
    depth = w_ada.shape[0]
    n = bsz * seq
    attn_dim = attn_norm_g.shape[1]
    n_heads = attn_dim // ATTN_HEAD_DIM

    lb_w = jax.nn.softmax(hgrn_lb_logits.astype(F32), axis=0)
    lower_bounds = jnp.cumsum(lb_w, axis=0) - lb_w[0:1]
    mod = _adaln(c, w_ada, b_ada)
    bias = _attn_bias(n_heads)
    hconsts = _hgrn_consts()
    n_rows = n * TOP_K + N_EXPERTS * MOE_TILE

    x2 = x.reshape(n, d)
    moe = None
    for layer in range(depth):
        x2, (aq, ak, av, rq, rf, ri, rg) = _in_proj(x2, mod[layer], norm1_g[layer], w_in, layer, seq, moe)
        attn = _attention(aq, ak, av, bias, bsz, seq)
        rec = _hgrn(rq, rf, ri, rg, lower_bounds[layer], hgrn_norm_g[layer], hconsts, bsz, seq)
        gpad = EXPERTS_PER_GROUP - N_GROUPS
        wr = jnp.concatenate([router_group_w[layer].T, jnp.zeros((gpad, d), F32), router_expert_w[layer].T], axis=0)
        br = jnp.concatenate([router_group_b[layer], jnp.zeros((gpad,), F32), router_expert_b[layer]]).reshape(-1, 1)
        x2, ha, hb, eids, gates, counts = _out_proj(attn, rec, x2, mod[layer], attn_norm_g[layer], norm2_g[layer],
                                            w_out, layer, wr.astype(BF16), br, seq)
        pos, tables = _plan(eids, counts, n_rows)
        ya, yb = _experts(_sc_scatter_rows(ha, pos, n_rows), _sc_scatter_rows(hb, pos, n_rows),
                          moe_w1, moe_w3, moe_w2, layer, tables)
        moe = (_sc_gather_rows(ya, pos), _sc_gather_rows(yb, pos), gates, mod[layer])
    return _final(x2, moe[0], moe[1], moe[2], moe[3], final_g, seq).reshape(bsz, seq, d)
```

```python
import functools

import numpy as np
import jax
import jax.numpy as jnp
from jax import lax
from jax.experimental import pallas as pl
from jax.experimental.pallas import tpu as pltpu
from jax.experimental.pallas import tpu_sc as plsc

F32 = jnp.float32
BF16 = jnp.bfloat16
U32 = jnp.uint32

ATTN_HEAD_DIM = 64
ATTN_BLOCK = 128
DILATIONS = (1, 4, 16)
ALIBI_MAX_EXP = 8.0
HGRN_EXPAND = 128
HGRN_CHUNK = 128
N_GROUPS = 4
EXPERTS_PER_GROUP = 8
N_EXPERTS = N_GROUPS * EXPERTS_PER_GROUP
TOP_K = 2
NORM_EPS = 1e-6
LOG2E = 1.4426950408889634

LANES = 128
ATTN_SUPER = ATTN_BLOCK * max(DILATIONS)
ROW_TILE = 512
IN_GROUPS = 16
ATTN_UNROLL = 16
HGRN_TILE = 2048
HGRN_UNROLL = 16
MOE_TILE = 512
FFN_SPLIT = 2
PLAN_TILE = 2048
PLAN_LANES = 256
PACK_WIDTH = 256
SC_WINDOW = 128
COMBINE_PARTS = 2
VMEM_LIMIT = 56 << 20


def _params(semantics):
    return pltpu.CompilerParams(dimension_semantics=semantics, vmem_limit_bytes=VMEM_LIMIT)


def _rms(x, g):
    return x * lax.rsqrt(jnp.mean(x * x, axis=-1, keepdims=True) + NORM_EPS) * g


def _loop(trips, body):
    if trips == 1:
        body(0, 0)
    else:
        lax.fori_loop(0, trips, body, 0)


def _round_robin(gens):
    live = list(gens)
    while live:
        for gen in list(live):
            try:
                next(gen)
            except StopIteration:
                live.remove(gen)


def _dot_nt(a, b):
    return lax.dot_general(a, b, (((1,), (1,)), ((), ())), preferred_element_type=F32)


def _dot_tn(a, b):
    return lax.dot_general(a, b, (((0,), (0,)), ((), ())), preferred_element_type=F32)


def _pack_pair(hi, lo):
    hb = lax.bitcast_convert_type(hi.astype(BF16).astype(F32), U32)
    lb = lax.bitcast_convert_type(lo.astype(BF16).astype(F32), U32)
    return hb | (lb >> 16)


def _unpack_pair(u):
    hi = lax.bitcast_convert_type(u & jnp.uint32(0xFFFF0000), F32)
    lo = lax.bitcast_convert_type(u << 16, F32)
    return hi, lo


def _pack_rows(y):
    w = PACK_WIDTH
    return _pack_pair(y[:, 0:w], y[:, 2 * w:3 * w]), _pack_pair(y[:, w:2 * w], y[:, 3 * w:4 * w])


def _unpack_rows(ua, ub):
    ha, la = _unpack_pair(ua)
    hb, lb = _unpack_pair(ub)
    return jnp.concatenate([ha, hb, la, lb], axis=1)


def _moe_residual(x_ref, ga0, ga1, gb0, gb1, gt_ref, gate_row):
    gt = gt_ref[...]
    y = gt[:, 0:1] * _unpack_rows(ga0[...], gb0[...]) + gt[:, 1:2] * _unpack_rows(ga1[...], gb1[...])
    return x_ref[...] + gate_row * y


def _ada_kernel(c_ref, w_ref, b_ref, o_ref):
    c = c_ref[...]
    ca = c * jax.nn.sigmoid(c)
    hi = ca.astype(BF16)
    lo = (ca - hi.astype(F32)).astype(BF16)
    w = w_ref[0].astype(BF16)
    o_ref[0] = (jnp.dot(hi, w, preferred_element_type=F32) + jnp.dot(lo, w, preferred_element_type=F32)) + b_ref[0]


def _adaln(c, w_ada, b_ada):
    depth, d, n6 = w_ada.shape
    bsz = c.shape[0]
    rows = 8
    cp = jnp.zeros((rows, d), F32).at[:bsz].set(c)
    tn = 1536
    out = pl.pallas_call(
        _ada_kernel,
        grid=(depth, n6 // tn),
        in_specs=[
            pl.BlockSpec((rows, d), lambda l, j: (0, 0)),
            pl.BlockSpec((1, d, tn), lambda l, j: (l, 0, j)),
            pl.BlockSpec((1, 1, tn), lambda l, j: (l, 0, j)),
        ],
        out_specs=pl.BlockSpec((1, rows, tn), lambda l, j: (l, 0, j)),
        out_shape=jax.ShapeDtypeStruct((depth, rows, n6), F32),
        compiler_params=_params(("parallel", "parallel")),
        name="adaln",
    )(cp, w_ada, b_ada.reshape(depth, 1, n6))
    return out[:, :bsz].reshape(depth, bsz, 6, d)


def _in_kernel(*refs, n_moe, n_alias):
    moe, (x_ref, mod_ref, g_ref, w_ref), rest = refs[:n_moe], refs[n_moe:n_moe + 4], refs[n_moe + 4 + n_alias:]
    hb = rest[-1]
    i = pl.program_id(0)

    @pl.when(i == 0)
    def _():
        hb[1] = jnp.zeros(hb.shape[1:], hb.dtype)

    outs = rest[1:-1] if n_moe else rest[:-1]
    mod = mod_ref[0]
    prev = hb[(i + 1) % 2]
    width = outs[0].shape[1]
    tm = x_ref.shape[0]
    groups = [slice(r, r + tm // IN_GROUPS) for r in range(0, tm, tm // IN_GROUPS)]

    def prologue(rs):
        if n_moe:
            ga0, ga1, gb0, gb1, gt_ref, pmod_ref = moe
            gt = gt_ref[rs, :]
            y = gt[:, 0:1] * _unpack_rows(ga0[rs, :], gb0[rs, :]) + gt[:, 1:2] * _unpack_rows(ga1[rs, :], gb1[rs, :])
            x = x_ref[rs, :] + pmod_ref[0][5:6, :] * y
            rest[0][rs, :] = x
        else:
            x = x_ref[rs, :]
        hb[i % 2, rs, :] = (_rms(x, g_ref[...]) * (1.0 + mod[1:2, :]) + mod[0:1, :]).astype(BF16)

    per = -(-len(groups) // len(outs))
    for k, o in enumerate(outs):
        for rs in groups[k * per:(k + 1) * per]:
            prologue(rs)
        w = w_ref[0, :, k * width:(k + 1) * width].astype(BF16)
        o[...] = jnp.dot(prev, w, preferred_element_type=F32).astype(o.dtype)


def _in_proj(x2, mod_l, g, w_in, layer, seq, moe=None):
    n, d = x2.shape
    n_out = w_in.shape[2]
    width = 512
    tm = ROW_TILE
    per_batch = seq // tm
    out_dtypes = (F32, F32, F32, BF16, F32, BF16, F32)
    n_parts = 1 if moe is None else len(moe[0])
    tiles = n // tm // n_parts
    outs = None
    for part in range(n_parts):
        t0 = part * tiles
        cur = lambda i, t0=t0: t0 + jnp.minimum(i, tiles - 1)
        loc = lambda i: jnp.minimum(i, tiles - 1)
        row = lambda w, cur=cur: pl.BlockSpec((tm, w), lambda i: (cur(i), 0))
        mod_spec = pl.BlockSpec((1, 6, d), lambda i, cur=cur: (cur(i) // per_batch, 0, 0))
        in_specs, args, out_specs, out_shape = [], [], [], []
        if moe is not None:
            parts, gates, mod_prev = moe
            ga, gb = parts[part]
            first = pl.BlockSpec((tm, PACK_WIDTH), lambda i: (loc(i), 0))
            second = pl.BlockSpec((tm, PACK_WIDTH), lambda i: (loc(i) + tiles, 0))
            in_specs += [first, second, first, second, row(LANES), mod_spec]
            args += [ga, ga, gb, gb, gates, mod_prev]
            out_specs.append(row(d))
            out_shape.append(jax.ShapeDtypeStruct((n, d), F32))
        in_specs += [row(d), mod_spec, pl.BlockSpec((1, d), lambda i: (0, 0)),
                     pl.BlockSpec((1, d, n_out), lambda i: (layer, 0, 0), pipeline_mode=pl.Buffered(1))]
        args += [x2, mod_l, g.reshape(1, d), w_in]
        out_specs += [pl.BlockSpec((tm, width), lambda i, t0=t0: (t0 + jnp.maximum(i - 1, 0), 0))] * len(out_dtypes)
        out_shape += [jax.ShapeDtypeStruct((n, width), dt) for dt in out_dtypes]
        aliases = {}
        if outs is not None:
            aliases = {len(args) + k: k for k in range(len(outs))}
            in_specs += [pl.BlockSpec(memory_space=pl.ANY)] * len(outs)
            args += list(outs)
        outs = pl.pallas_call(
            functools.partial(_in_kernel, n_moe=0 if moe is None else 6, n_alias=len(aliases)),
            grid=(tiles + 1,),
            in_specs=in_specs,
            out_specs=out_specs,
            out_shape=out_shape,
            input_output_aliases=aliases,
            scratch_shapes=[pltpu.VMEM((2, tm, d), BF16)],
            compiler_params=_params(("arbitrary",)),
            name="in_proj",
        )(*args)
    return (x2, outs) if moe is None else (outs[0], outs[1:])


def _attn_bias(n_heads):
    qi = np.arange(ATTN_BLOCK)[:, None]
    kj = np.arange(2 * ATTN_BLOCK)[None, :]
    dist = ATTN_BLOCK + qi - kj
    valid = (dist >= 0) & (dist <= ATTN_BLOCK)
    slopes = np.exp2(-ALIBI_MAX_EXP * np.arange(1, n_heads + 1, dtype=np.float32) / n_heads)
    out = np.empty((2, len(DILATIONS), n_heads, ATTN_BLOCK, 2 * ATTN_BLOCK), np.float32)
    for p, dil in enumerate(DILATIONS):
        b = -slopes[:, None, None] * (dist * dil).astype(np.float32)[None] * np.float32(LOG2E)
        out[0, p] = np.where(valid[None], b, -np.inf)
        out[1, p] = np.where((valid & (kj >= ATTN_BLOCK))[None], b, -np.inf)
    return jnp.asarray(out)


def _attn_kernel(q_ref, k_ref, v_ref, bias_ref, o_ref, kcar, vcar, pbuf, vbuf, oscr, mscr, dscr):
    n = pl.program_id(2)
    sb = q_ref.shape[0]
    is_lo = lax.broadcasted_iota(jnp.int32, (ATTN_BLOCK, LANES), 1) < ATTN_HEAD_DIM
    scale = ATTN_HEAD_DIM ** -0.5 * LOG2E

    @pl.when(n == 0)
    def _():
        kcar[...] = jnp.zeros_like(kcar)
        vcar[...] = jnp.zeros_like(vcar)

    pbuf[...] = jnp.zeros_like(pbuf)
    vbuf[...] = jnp.zeros_like(vbuf)
    lo2 = lax.broadcasted_iota(jnp.int32, (2 * ATTN_BLOCK, LANES), 1) < ATTN_HEAD_DIM
    zero = jnp.zeros((2 * ATTN_BLOCK, LANES), BF16)
    den_lo = lo2.astype(BF16)
    den_hi = jnp.logical_not(lo2).astype(BF16)

    for p, dil in enumerate(DILATIONS):
        nb = (sb // ATTN_BLOCK) // dil

        def rows_of(it, dil=dil, nb=nb):
            start = (it % nb) * (ATTN_BLOCK * dil) + it // nb
            return pl.ds(start, ATTN_BLOCK, stride=dil) if dil > 1 else pl.ds(pl.multiple_of(start, ATTN_BLOCK), ATTN_BLOCK)

        def pv(it, probs, vaug, p=p, rows_of=rows_of):
            rows = rows_of(it)
            of = jnp.dot(probs, vaug, preferred_element_type=F32)
            oscr[p, rows, :] = of[:, :LANES]
            dscr[p, rows, :] = of[:, LANES:]

        def qk(it, p=p, nb=nb, rows_of=rows_of):
            r = it // nb
            rows = rows_of(it)
            crow = pl.ds(pl.multiple_of(r * ATTN_BLOCK, ATTN_BLOCK), ATTN_BLOCK)
            qb = q_ref[rows, :] * scale
            kc = k_ref[rows, :].astype(BF16)
            vc = v_ref[rows, :].astype(BF16)
            kcat = jnp.concatenate([kcar[p, crow, :], kc], axis=0)
            vcat = jnp.concatenate([vcar[p, crow, :], vc], axis=0)
            vaug = jnp.concatenate([jnp.concatenate([jnp.where(lo2, vcat, zero), den_lo], axis=1),
                                    jnp.concatenate([jnp.where(lo2, zero, vcat), den_hi], axis=1)], axis=0)
            kcar[p, crow, :] = kc
            vcar[p, crow, :] = vc
            first = jnp.logical_and(n == 0, it % nb == 0).astype(jnp.int32)
            probs, acc_m = [], None
            for hh in range(2):
                sel = is_lo if hh == 0 else jnp.logical_not(is_lo)
                qm = jnp.where(sel, qb, 0.0).astype(BF16)
                s = _dot_nt(qm, kcat) + bias_ref[first, p, hh]
                m = jnp.max(s, axis=-1, keepdims=True)
                probs.append(jnp.exp2(s - m).astype(BF16))
                mb = jnp.broadcast_to(m, (ATTN_BLOCK, LANES))
                acc_m = mb if hh == 0 else jnp.where(sel, mb, acc_m)
            mscr[p, rows, :] = acc_m
            return jnp.concatenate(probs, axis=1), vaug

        def body(g, carry, pv=pv, qk=qk):
            it0 = g * ATTN_UNROLL
            pend = (pbuf[...], vbuf[...])
            for u in range(ATTN_UNROLL):
                cur = qk(it0 + u)
                pv(jnp.maximum(it0 + u - 1, 0), *pend)
                pend = cur
            pbuf[...] = pend[0]
            vbuf[...] = pend[1]
            return carry

        n_it = sb // ATTN_BLOCK
        lax.fori_loop(0, n_it // ATTN_UNROLL, body, 0)
        pv(n_it - 1, pbuf[...], vbuf[...])

    ct = 256

    def combine(t, carry):
        rs = pl.ds(pl.multiple_of(t * ct, ct), ct)
        m0, m1, m2 = mscr[0, rs, :], mscr[1, rs, :], mscr[2, rs, :]
        mx = jnp.maximum(jnp.maximum(m0, m1), m2)
        w0, w1, w2 = jnp.exp2(m0 - mx), jnp.exp2(m1 - mx), jnp.exp2(m2 - mx)
        num = w0 * oscr[0, rs, :] + w1 * oscr[1, rs, :] + w2 * oscr[2, rs, :]
        den = w0 * dscr[0, rs, :] + w1 * dscr[1, rs, :] + w2 * dscr[2, rs, :]
        o_ref[rs, :] = (num / den).astype(o_ref.dtype)
        return carry

    lax.fori_loop(0, sb // ct, combine, 0)


def _attention(aq, ak, av, bias, bsz, seq):
    n, width = aq.shape
    sb = ATTN_SUPER
    nsb = seq // sb
    pairs = width // LANES
    npat = len(DILATIONS)
    blk = pl.BlockSpec((sb, LANES), lambda b, h, t: (b * nsb + t, h))
    return pl.pallas_call(
        _attn_kernel,
        grid=(bsz, pairs, nsb),
        in_specs=[blk, blk, blk,
                  pl.BlockSpec((2, npat, 2, ATTN_BLOCK, 2 * ATTN_BLOCK), lambda b, h, t: (0, 0, h, 0, 0))],
        out_specs=blk,
        out_shape=jax.ShapeDtypeStruct((n, width), BF16),
        scratch_shapes=[
            pltpu.VMEM((npat, max(DILATIONS) * ATTN_BLOCK, LANES), BF16),
            pltpu.VMEM((npat, max(DILATIONS) * ATTN_BLOCK, LANES), BF16),
            pltpu.VMEM((ATTN_BLOCK, 4 * ATTN_BLOCK), BF16),
            pltpu.VMEM((4 * ATTN_BLOCK, 2 * LANES), BF16),
            pltpu.VMEM((npat, sb, LANES), F32),
            pltpu.VMEM((npat, sb, LANES), F32),
            pltpu.VMEM((npat, sb, LANES), F32),
        ],
        compiler_params=_params(("parallel", "parallel", "arbitrary")),
        name="dilated_attn",
    )(aq, ak, av, bias)


def _hgrn_consts():
    c = HGRN_CHUNK
    t = np.arange(c)[:, None]
    u = np.arange(c)[None, :]
    low, masks = [], []
    m = c // 2
    while m >= 1:
        mid = (t // (2 * m)) * (2 * m) + m
        is_q = t >= mid
        if m in (4, 2):
            low.append((is_q & (u >= mid) & (u <= t)) | (~is_q & (u > t) & (u < mid)))
        masks.append(((t // (2 * m)) == (u // (2 * m))) & (t % (2 * m) >= m) & (u % (2 * m) < m))
        m //= 2
    masks.append(t == u)
    f = lambda a: np.asarray(a, np.float32)
    return (jnp.asarray(f(u <= t), BF16), jnp.asarray(f(np.concatenate(low, axis=0)), BF16),
            jnp.asarray(f(np.broadcast_to(t % 2 == 1, (c, c)))), jnp.asarray(f(np.stack(masks))))


def _hgrn_kernel(q_ref, z_ref, v_ref, g_ref, lb_ref, gn_ref, tri_ref, low_ref, odd_ref, msk_ref, o_ref, st):
    c = HGRN_CHUNK

    @pl.when(pl.program_id(2) == 0)
    def _():
        st[...] = jnp.zeros_like(st)

    lb = jnp.maximum(lb_ref[...], 0.0)
    log_lb = jnp.log(lb)
    log_1m = jnp.log1p(-lb)
    one_m = 1.0 - lb
    gn = gn_ref[...]
    n_levels = msk_ref.shape[0] - 1
    coarse = [c >> (i + 1) for i in range(n_levels) if (c >> (i + 1)) >= 8]

    def chunk(ci):
        rs = pl.ds(pl.multiple_of(ci * c, c), c)
        z = z_ref[rs, :]
        qb = q_ref[rs, :]
        vb = v_ref[rs, :]
        ez = jnp.exp(-jnp.abs(z))
        ls = jnp.minimum(z, 0.0) - jnp.log(1.0 + ez)
        b2 = log_1m + ls
        lf = jnp.maximum(log_lb, b2) + jnp.log(1.0 + jnp.exp(-jnp.abs(log_lb - b2)))
        rz = 1.0 / (1.0 + ez)
        kb = (one_m * jnp.where(z > 0.0, ez * rz, rz)).astype(BF16)
        l2 = lf * LOG2E
        hi = l2.astype(BF16)
        hl = jnp.concatenate([hi, (l2 - hi.astype(F32)).astype(BF16)], axis=1)
        yield
        bb = jnp.dot(tri_ref[...], hl, preferred_element_type=F32)
        b = bb[:, :HGRN_EXPAND] + bb[:, HGRN_EXPAND:]
        dl = jnp.dot(low_ref[...], hl, preferred_element_type=F32)
        dl = dl[:, :HGRN_EXPAND] + dl[:, HGRN_EXPAND:]
        yield

        def level(d):
            e = jnp.exp2(d).astype(BF16)
            return _dot_nt(qb * e, kb * e)

        a = msk_ref[n_levels] * _dot_nt(qb, kb)
        for i in range(n_levels):
            m = c >> (i + 1)
            if m >= 8:
                parts = []
                for r0 in range(0, c, 2 * m):
                    mid = b[r0 + m - 1:r0 + m, :]
                    parts += [mid - b[r0:r0 + m, :], b[r0 + m:r0 + 2 * m, :] - mid]
                d = jnp.concatenate(parts, axis=0)
            elif m > 1:
                d = dl[(i - len(coarse)) * c:(i - len(coarse) + 1) * c]
            else:
                d = l2 * odd_ref[...]
            a = a + msk_ref[i] * level(d)
            if i % 3 == 2:
                yield
        b_last = b[c - 1:c, :]
        s_t = st[...]
        o = _dot_nt(qb * jnp.exp2(b).astype(BF16), s_t.astype(BF16)) + jnp.dot(a.astype(BF16), vb, preferred_element_type=F32)
        st[...] = jnp.exp2(b_last) * s_t + _dot_tn(vb, kb * jnp.exp2(b_last - b).astype(BF16))
        yield
        g = g_ref[rs, :]
        o_ref[rs, :] = (_rms(o, gn) * (g * jax.nn.sigmoid(g))).astype(o_ref.dtype)

    def group(gi, carry):
        _round_robin([chunk(gi * HGRN_UNROLL + u) for u in range(HGRN_UNROLL)])
        return carry

    _loop(q_ref.shape[0] // (c * HGRN_UNROLL), group)


def _hgrn(rq, rf, ri, rg, lb, gn, consts, bsz, seq):
    n, width = rq.shape
    heads = width // HGRN_EXPAND
    ts = HGRN_TILE
    nt = seq // ts
    blk = pl.BlockSpec((ts, HGRN_EXPAND), lambda b, h, t: (b * nt + t, h))
    vec = pl.BlockSpec((1, HGRN_EXPAND), lambda b, h, t: (0, h))
    const = lambda a: pl.BlockSpec(a.shape, lambda b, h, t: (0,) * a.ndim)
    return pl.pallas_call(
        _hgrn_kernel,
        grid=(bsz, heads, nt),
        in_specs=[blk, blk, blk, blk, vec, vec] + [const(a) for a in consts],
        out_specs=blk,
        out_shape=jax.ShapeDtypeStruct((n, width), BF16),
        scratch_shapes=[pltpu.VMEM((HGRN_EXPAND, HGRN_EXPAND), F32)],
        compiler_params=_params(("parallel", "parallel", "arbitrary")),
        name="hgrn2",
    )(rq, rf, ri, rg, lb.reshape(1, width), gn.reshape(1, width), *consts)


def _route_math(lg):
    row = lax.broadcasted_iota(jnp.int32, (EXPERTS_PER_GROUP, lg.shape[1]), 0).astype(F32)
    none = float(EXPERTS_PER_GROUP)

    def first_argmax(vals):
        mx = jnp.max(vals, axis=0, keepdims=True)
        return mx, jnp.min(jnp.where(vals == mx, row, none), axis=0, keepdims=True)

    g = lg[0:EXPERTS_PER_GROUP, :]
    is_group = row < N_GROUPS
    gmax, gidx = first_argmax(jnp.where(is_group, g, -jnp.inf))
    gprob = 1.0 / jnp.sum(jnp.where(is_group, jnp.exp(g - gmax), 0.0), axis=0, keepdims=True)
    el = lg[EXPERTS_PER_GROUP:2 * EXPERTS_PER_GROUP, :]
    for k in range(1, N_GROUPS):
        el = jnp.where(gidx == k, lg[(k + 1) * EXPERTS_PER_GROUP:(k + 2) * EXPERTS_PER_GROUP, :], el)
    t1, i1 = first_argmax(el)
    t2, i2 = first_argmax(jnp.where(row == i1, -jnp.inf, el))
    ex = jnp.exp(t2 - t1)
    base = gidx * EXPERTS_PER_GROUP
    eids = jnp.concatenate([base + i1, base + i2], axis=0).astype(jnp.int32)
    gates = jnp.concatenate([gprob / (1.0 + ex), gprob * ex / (1.0 + ex)], axis=0)
    return eids, gates


def _out_kernel(a_ref, r_ref, x_ref, mod_ref, ag_ref, g2_ref, wo_ref, wr_ref, br_ref, xo_ref, ha_ref, hb_ref, e_ref, gt_ref, cnt_ref):
    mod = mod_ref[0]
    mix = jnp.concatenate([_rms(a_ref[...].astype(F32), ag_ref[...]).astype(BF16), r_ref[...]], axis=1)
    xn = x_ref[...] + mod[2:3, :] * jnp.dot(mix, wo_ref[0].astype(BF16), preferred_element_type=F32)
    xo_ref[...] = xn
    h2 = _rms(xn, g2_ref[...]) * (1.0 + mod[4:5, :]) + mod[3:4, :]
    ha_ref[...], hb_ref[...] = _pack_rows(h2)
    eids, gates = _route_math(_dot_nt(wr_ref[...], h2.astype(BF16)) + br_ref[...])
    e_ref[...] = eids
    eio = lax.broadcasted_iota(jnp.int32, (N_EXPERTS, eids.shape[1]), 0)
    hit = jnp.logical_or(eids[0:1, :] == eio, eids[1:2, :] == eio).astype(F32)

    @pl.when(pl.program_id(0) == 0)
    def _():
        cnt_ref[...] = jnp.zeros_like(cnt_ref)

    cnt_ref[...] += jnp.sum(hit, axis=1, keepdims=True)
    pad = jnp.zeros((LANES - TOP_K, LANES), F32)
    for k in range(h2.shape[0] // LANES):
        tile = jnp.concatenate([gates[:, k * LANES:(k + 1) * LANES], pad], axis=0)
        gt_ref[k * LANES:(k + 1) * LANES, :] = tile.T


def _out_proj(attn, rec, x2, mod_l, ag, g2, w_out, layer, wr_bf, br, seq):
    n, d = x2.shape
    half = attn.shape[1]
    tm = ROW_TILE
    per_batch = seq // tm
    row = lambda w: pl.BlockSpec((tm, w), lambda i: (i, 0))
    full = lambda a: pl.BlockSpec(a.shape, lambda i: (0,) * a.ndim, pipeline_mode=pl.Buffered(1))
    ag2, g22 = ag.reshape(1, half), g2.reshape(1, d)
    return pl.pallas_call(
        _out_kernel,
        grid=(n // tm,),
        in_specs=[row(half), row(half), row(d), pl.BlockSpec((1, 6, d), lambda i: (i // per_batch, 0, 0)),
                  full(ag2), full(g22), pl.BlockSpec((1, d, d), lambda i: (layer, 0, 0), pipeline_mode=pl.Buffered(1)),
                  full(wr_bf), full(br)],
        out_specs=[row(d), row(PACK_WIDTH), row(PACK_WIDTH), pl.BlockSpec((TOP_K, tm), lambda i: (0, i)), row(LANES),
                   pl.BlockSpec((N_EXPERTS, LANES), lambda i: (0, 0))],
        out_shape=[jax.ShapeDtypeStruct((n, d), F32), jax.ShapeDtypeStruct((n, PACK_WIDTH), U32),
                   jax.ShapeDtypeStruct((n, PACK_WIDTH), U32), jax.ShapeDtypeStruct((TOP_K, n), jnp.int32),
                   jax.ShapeDtypeStruct((n, LANES), F32), jax.ShapeDtypeStruct((N_EXPERTS, LANES), F32)],
        compiler_params=_params(("arbitrary",)),
        name="out_proj",
    )(attn, rec, x2, mod_l, ag2, g22, w_out, wr_bf, br)


def _plan_kernel(e_ref, cnt_ref, ut_ref, pos_ref, be_ref, bv_ref, bf_ref, bn_ref, nu_ref, off, run):
    i = pl.program_id(0)
    t = e_ref.shape[1]
    eio = lax.broadcasted_iota(jnp.int32, (N_EXPERTS, t), 0)
    oh1 = e_ref[0:1, :] == eio
    oh2 = e_ref[1:2, :] == eio
    oh = jnp.logical_or(oh1, oh2).astype(F32)

    @pl.when(i == 0)
    def _():
        counts = cnt_ref[...]
        nb = jnp.floor((counts + (MOE_TILE - 1)) * (1.0 / MOE_TILE))
        er = lax.broadcasted_iota(jnp.int32, (N_EXPERTS, N_EXPERTS), 0)
        ec = lax.broadcasted_iota(jnp.int32, (N_EXPERTS, N_EXPERTS), 1)
        cum_excl = jnp.dot((ec < er).astype(BF16), nb.astype(BF16), preferred_element_type=F32)
        off[...] = cum_excl * MOE_TILE
        run[...] = jnp.zeros_like(run)
        ce, ci, cn = cum_excl[:, 0:1], (cum_excl + nb)[:, 0:1], counts[:, 0:1]
        b = lax.broadcasted_iota(jnp.int32, (N_EXPERTS, be_ref.shape[1]), 1).astype(F32)
        member = jnp.logical_and(b >= ce, b < ci)
        be = jnp.minimum(jnp.sum((b >= ci).astype(F32), axis=0, keepdims=True), N_EXPERTS - 1.0)
        bv = jnp.sum(jnp.where(member, jnp.minimum(cn - (b - ce) * MOE_TILE, float(MOE_TILE)), 0.0), axis=0, keepdims=True)
        be_ref[...] = be.astype(jnp.int32)
        bv_ref[...] = bv.astype(jnp.int32)
        has = nb[:, 0:1] > 0.0
        eid = lax.broadcasted_iota(jnp.int32, b.shape, 0).astype(F32)
        bf_ref[...] = jnp.sum(jnp.where(jnp.logical_and(b == ce, has), 1.0, 0.0), axis=0, keepdims=True).astype(jnp.int32)
        nxt = jnp.min(jnp.where(jnp.logical_and(eid > be, has), eid, float(N_EXPERTS)), axis=0, keepdims=True)
        bn_ref[...] = jnp.where(nxt < N_EXPERTS, nxt, -1.0).astype(jnp.int32)
        nu_ref[...] = jnp.broadcast_to(ci[N_EXPERTS - 1:N_EXPERTS, :], nu_ref.shape).astype(jnp.int32)

    c = jnp.dot(oh.astype(BF16), ut_ref[...], preferred_element_type=F32)
    r = off[:, 0:1] + run[:, 0:1] + c - 1.0
    pos_ref[0:1, :] = jnp.sum(jnp.where(oh1, r, 0.0), axis=0, keepdims=True).astype(jnp.int32)
    pos_ref[1:2, :] = jnp.sum(jnp.where(oh2, r, 0.0), axis=0, keepdims=True).astype(jnp.int32)
    run[...] += jnp.sum(oh, axis=1, keepdims=True)


def _plan(expert_ids_t, counts, n_rows):
    n = expert_ids_t.shape[1]
    t = PLAN_TILE
    n_blocks = n_rows // MOE_TILE
    assert n_blocks <= PLAN_LANES and TOP_K == 2
    ut = jnp.asarray(np.triu(np.ones((t, t), np.float32)), BF16)
    tab = jax.ShapeDtypeStruct((1, PLAN_LANES), jnp.int32)
    const = lambda shape: pl.BlockSpec(shape, lambda i: (0, 0))
    pos, be, bv, bf, bn, nu = pl.pallas_call(
        _plan_kernel,
        grid=(n // t,),
        in_specs=[pl.BlockSpec((TOP_K, t), lambda i: (0, i)), const((N_EXPERTS, LANES)), const((t, t))],
        out_specs=[pl.BlockSpec((TOP_K, t), lambda i: (0, i))] + [const((1, PLAN_LANES))] * 4 + [const((1, LANES))],
        out_shape=[jax.ShapeDtypeStruct((TOP_K, n), jnp.int32), tab, tab, tab, tab, jax.ShapeDtypeStruct((1, LANES), jnp.int32)],
        scratch_shapes=[pltpu.VMEM((N_EXPERTS, LANES), F32)] * 2,
        compiler_params=_params(("arbitrary",)),
        name="plan",
    )(expert_ids_t, counts, ut)
    return pos, (be.reshape(-1), bv.reshape(-1), bf.reshape(-1), bn.reshape(-1), nu.reshape(-1)[:1])


def _sc_mesh():
    return plsc.VectorSubcoreMesh(core_axis_name="c", subcore_axis_name="s")


def _sc_scatter_rows(x, pos, n_rows):
    n, d = x.shape
    w = SC_WINDOW

    @functools.partial(pl.kernel, out_type=jax.ShapeDtypeStruct((n_rows, d), x.dtype), mesh=_sc_mesh(), scratch_types=[])
    def scatter(x_hbm, p_hbm, o_hbm):
        def body(x_vmem, i_vmem):
            for k in range(TOP_K):
                pltpu.sync_copy(x_vmem, o_hbm.at[i_vmem.at[k]])

        pltpu.emit_pipeline(
            body,
            grid=(n // w,),
            in_specs=[pl.BlockSpec((w, d), lambda i: (i, 0)), pl.BlockSpec((TOP_K, w), lambda i: (0, i))],
            out_specs=[],
            core_axis_name=("c", "s"),
            dimension_semantics=(pltpu.PARALLEL,),
        )(x_hbm, p_hbm)

    return scatter(x, pos)


def _sc_gather_rows(y, pos):
    m = pos.shape[0] * pos.shape[1]
    d = y.shape[1]
    w = SC_WINDOW

    @functools.partial(pl.kernel, out_type=jax.ShapeDtypeStruct((m, d), y.dtype), mesh=_sc_mesh(), scratch_types=[])
    def gather(y_hbm, p_hbm, o_hbm):
        def body(i_vmem, o_vmem):
            pltpu.sync_copy(y_hbm.at[i_vmem.at[0]], o_vmem)

        pltpu.emit_pipeline(
            body,
            grid=(m // w,),
            in_specs=[pl.BlockSpec((1, w), lambda i: (0, i))],
            out_specs=[pl.BlockSpec((w, d), lambda i: (i, 0))],
            core_axis_name=("c", "s"),
            dimension_semantics=(pltpu.PARALLEL,),
        )(p_hbm, o_hbm)

    return gather(y, pos.reshape(1, m))


def _expert_kernel(be_ref, bv_ref, bf_ref, bn_ref, nu_ref, xa_ref, xb_ref, w1_hbm, w3_hbm, w2_hbm, ya_ref, yb_ref,
                   wf1, wf3, wf2, w1b, w3b, w2b, wsem, par, *, layer):
    i = pl.program_id(0)
    tb = xa_ref.shape[0]
    half = tb // 2

    def fetch(e, slot):
        return (pltpu.make_async_copy(w1_hbm.at[layer, e], wf1.at[slot], wsem.at[slot, 0]),
                pltpu.make_async_copy(w3_hbm.at[layer, e], wf3.at[slot], wsem.at[slot, 1]),
                pltpu.make_async_copy(w2_hbm.at[layer, e], wf2.at[slot], wsem.at[slot, 2]))

    def ffn(rows):
        live = lax.broadcasted_iota(jnp.int32, (rows, 1), 0) < bv_ref[i]
        x = jnp.where(live, _unpack_rows(xa_ref[0:rows, :], xb_ref[0:rows, :]), 0.0).astype(BF16)
        y = None
        hc = w1b.shape[1] // FFN_SPLIT
        for c in range(FFN_SPLIT):
            cs = slice(c * hc, (c + 1) * hc)
            a = jnp.dot(x, w1b[:, cs], preferred_element_type=F32)
            b = jnp.dot(x, w3b[:, cs], preferred_element_type=F32)
            hm = (a * jax.nn.sigmoid(a) * b).astype(BF16)
            part = jnp.dot(hm, w2b[cs, :], preferred_element_type=F32)
            y = part if y is None else y + part
        ya_ref[0:rows, :], yb_ref[0:rows, :] = _pack_rows(y)

    @pl.when(i == 0)
    def _():
        par[0] = 0
        for c in fetch(be_ref[0], 0):
            c.start()

    @pl.when(i < nu_ref[0])
    def _():
        @pl.when(bf_ref[i] == 1)
        def _():
            slot = par[0]
            for c in fetch(be_ref[i], slot):
                c.wait()
            w1b[...] = wf1[slot].astype(BF16)
            w3b[...] = wf3[slot].astype(BF16)
            w2b[...] = wf2[slot].astype(BF16)

            @pl.when(bn_ref[i] >= 0)
            def _():
                for c in fetch(bn_ref[i], 1 - slot):
                    c.start()

            par[0] = 1 - slot

        @pl.when(bv_ref[i] > half)
        def _():
            ffn(tb)

        @pl.when(bv_ref[i] <= half)
        def _():
            ffn(half)
            ya_ref[half:, :] = jnp.zeros((tb - half, ya_ref.shape[1]), ya_ref.dtype)
            yb_ref[half:, :] = jnp.zeros((tb - half, yb_ref.shape[1]), yb_ref.dtype)

    @pl.when(i >= nu_ref[0])
    def _():
        ya_ref[...] = jnp.zeros_like(ya_ref)
        yb_ref[...] = jnp.zeros_like(yb_ref)


def _experts(xa, xb, w1, w3, w2, layer, tables):
    n_rows = xa.shape[0]
    d, de = w1.shape[-2:]
    tb = MOE_TILE
    xspec = pl.BlockSpec((tb, PACK_WIDTH), lambda i, be, bv, bf, bn, nu: (jnp.minimum(i, nu[0] - 1), 0))
    yspec = pl.BlockSpec((tb, PACK_WIDTH), lambda i, be, bv, bf, bn, nu: (i, 0))
    anyspec = pl.BlockSpec(memory_space=pl.ANY)
    grid_spec = pltpu.PrefetchScalarGridSpec(
        num_scalar_prefetch=5,
        grid=(n_rows // tb,),
        in_specs=[xspec, xspec, anyspec, anyspec, anyspec],
        out_specs=[yspec, yspec],
        scratch_shapes=[pltpu.VMEM((2, d, de), F32), pltpu.VMEM((2, d, de), F32), pltpu.VMEM((2, de, d), F32),
                        pltpu.VMEM((d, de), BF16), pltpu.VMEM((d, de), BF16), pltpu.VMEM((de, d), BF16),
                        pltpu.SemaphoreType.DMA((2, 3)), pltpu.SMEM((1,), jnp.int32)],
    )
    rows = jax.ShapeDtypeStruct((n_rows, PACK_WIDTH), U32)
    return pl.pallas_call(
        functools.partial(_expert_kernel, layer=layer),
        grid_spec=grid_spec,
        out_shape=[rows, rows],
        compiler_params=_params(("arbitrary",)),
        name="experts",
    )(*tables, xa, xb, w1, w3, w2)


def _final_kernel(ga0, ga1, gb0, gb1, gt_ref, mod_ref, x_ref, fg_ref, *rest):
    o_ref = rest[-1]
    o_ref[...] = _rms(_moe_residual(x_ref, ga0, ga1, gb0, gb1, gt_ref, mod_ref[0][5:6, :]), fg_ref[...])


def _final(x2, parts, gates, mod_l, final_g, seq):
    n, d = x2.shape
    tm = ROW_TILE
    per_batch = seq // tm
    tiles = n // tm // len(parts)
    out = None
    for part, (ga, gb) in enumerate(parts):
        t0 = part * tiles
        row = lambda w, t0=t0: pl.BlockSpec((tm, w), lambda i: (t0 + i, 0))
        first = pl.BlockSpec((tm, PACK_WIDTH), lambda i: (i, 0))
        second = pl.BlockSpec((tm, PACK_WIDTH), lambda i: (i + tiles, 0))
        in_specs = [first, second, first, second, row(LANES),
                    pl.BlockSpec((1, 6, d), lambda i, t0=t0: ((t0 + i) // per_batch, 0, 0)), row(d),
                    pl.BlockSpec((1, d), lambda i: (0, 0))]
        args = [ga, ga, gb, gb, gates, mod_l, x2, final_g.reshape(1, d)]
        aliases = {}
        if out is not None:
            aliases = {len(args): 0}
            in_specs.append(pl.BlockSpec(memory_space=pl.ANY))
            args.append(out)
        out = pl.pallas_call(
            _final_kernel,
            grid=(tiles,),
            in_specs=in_specs,
            out_specs=row(d),
            out_shape=jax.ShapeDtypeStruct((n, d), F32),
            input_output_aliases=aliases,
            compiler_params=_params(("parallel",)),
            name="final_norm",
        )(*args)
    return out


def kernel(x, c, w_ada, b_ada, norm1_g, w_in, attn_norm_g, hgrn_lb_logits, hgrn_norm_g, w_out, norm2_g, router_group_w, router_group_b, router_expert_w, router_expert_b, moe_w1, moe_w3, moe_w2, final_g):
    bsz, seq, d = x.shape
    depth = w_ada.shape[0]
    n = bsz * seq
    attn_dim = attn_norm_g.shape[1]
    n_heads = attn_dim // ATTN_HEAD_DIM

    lb_w = jax.nn.softmax(hgrn_lb_logits.astype(F32), axis=0)
    lower_bounds = jnp.cumsum(lb_w, axis=0) - lb_w[0:1]
    mod = _adaln(c, w_ada, b_ada)
    bias = _attn_bias(n_heads)
    hconsts = _hgrn_consts()
    n_rows = n * TOP_K + N_EXPERTS * MOE_TILE

    x2 = x.reshape(n, d)
    moe = None
    for layer in range(depth):
        x2, (aq, ak, av, rq, rf, ri, rg) = _in_proj(x2, mod[layer], norm1_g[layer], w_in, layer, seq, moe)
        attn = _attention(aq, ak, av, bias, bsz, seq)
        rec = _hgrn(rq, rf, ri, rg, lower_bounds[layer], hgrn_norm_g[layer], hconsts, bsz, seq)
        gpad = EXPERTS_PER_GROUP - N_GROUPS
        wr = jnp.concatenate([router_group_w[layer].T, jnp.zeros((gpad, d), F32), router_expert_w[layer].T], axis=0)
        br = jnp.concatenate([router_group_b[layer], jnp.zeros((gpad,), F32), router_expert_b[layer]]).reshape(-1, 1)
        x2, ha, hb, eids, gates, counts = _out_proj(attn, rec, x2, mod[layer], attn_norm_g[layer], norm2_g[layer],
                                            w_out, layer, wr.astype(BF16), br, seq)
        pos, tables = _plan(eids, counts, n_rows)
        ya, yb = _experts(_sc_scatter_rows(ha, pos, n_rows), _sc_scatter_rows(hb, pos, n_rows),
                          moe_w1, moe_w3, moe_w2, layer, tables)
        span = n // COMBINE_PARTS
        parts = [tuple(_sc_gather_rows(y, pos[:, k * span:(k + 1) * span]) for y in (ya, yb)) for k in range(COMBINE_PARTS)]
        moe = (parts, gates, mod[layer])
    return _final(x2, moe[0], moe[1], moe[2], final_g, seq).reshape(bsz, seq, d)
```

```python
import functools

import numpy as np
import jax
import jax.numpy as jnp
from jax import lax
from jax.experimental import pallas as pl
from jax.experimental.pallas import tpu as pltpu
from jax.experimental.pallas import tpu_sc as plsc

F32 = jnp.float32
BF16 = jnp.bfloat16
U32 = jnp.uint32

ATTN_HEAD_DIM = 64
ATTN_BLOCK = 128
DILATIONS = (1, 4, 16)
ALIBI_MAX_EXP = 8.0
HGRN_EXPAND = 128
HGRN_CHUNK = 128
N_GROUPS = 4
EXPERTS_PER_GROUP = 8
N_EXPERTS = N_GROUPS * EXPERTS_PER_GROUP
TOP_K = 2
NORM_EPS = 1e-6
LOG2E = 1.4426950408889634

LANES = 128
ATTN_SUPER = ATTN_BLOCK * max(DILATIONS)
ROW_TILE = 512
IN_GROUPS = 16
ATTN_UNROLL = 16
HGRN_TILE = 2048
HGRN_UNROLL = 16
MOE_TILE = 1024
FFN_SPLIT = 2
PLAN_TILE = 2048
PLAN_LANES = 256
PACK_WIDTH = 256
SC_WINDOW = 128
VMEM_LIMIT = 56 << 20


def _params(semantics):
    return pltpu.CompilerParams(dimension_semantics=semantics, vmem_limit_bytes=VMEM_LIMIT)


def _rms(x, g):
    return x * lax.rsqrt(jnp.mean(x * x, axis=-1, keepdims=True) + NORM_EPS) * g


def _loop(trips, body):
    if trips == 1:
        body(0, 0)
    else:
        lax.fori_loop(0, trips, body, 0)


def _round_robin(gens):
    live = list(gens)
    while live:
        for gen in list(live):
            try:
                next(gen)
            except StopIteration:
                live.remove(gen)


def _dot_nt(a, b):
    return lax.dot_general(a, b, (((1,), (1,)), ((), ())), preferred_element_type=F32)


def _dot_tn(a, b):
    return lax.dot_general(a, b, (((0,), (0,)), ((), ())), preferred_element_type=F32)


def _pack_pair(hi, lo):
    hb = lax.bitcast_convert_type(hi.astype(BF16).astype(F32), U32)
    lb = lax.bitcast_convert_type(lo.astype(BF16).astype(F32), U32)
    return hb | (lb >> 16)


def _unpack_pair(u):
    hi = lax.bitcast_convert_type(u & jnp.uint32(0xFFFF0000), F32)
    lo = lax.bitcast_convert_type(u << 16, F32)
    return hi, lo


def _pack_rows(y):
    w = PACK_WIDTH
    return _pack_pair(y[:, 0:w], y[:, 2 * w:3 * w]), _pack_pair(y[:, w:2 * w], y[:, 3 * w:4 * w])


def _unpack_rows(ua, ub):
    ha, la = _unpack_pair(ua)
    hb, lb = _unpack_pair(ub)
    return jnp.concatenate([ha, hb, la, lb], axis=1)


def _moe_residual(x_ref, ga0, ga1, gb0, gb1, gt_ref, gate_row):
    gt = gt_ref[...]
    y = gt[:, 0:1] * _unpack_rows(ga0[...], gb0[...]) + gt[:, 1:2] * _unpack_rows(ga1[...], gb1[...])
    return x_ref[...] + gate_row * y


def _ada_kernel(c_ref, w_ref, b_ref, o_ref):
    c = c_ref[...]
    ca = c * jax.nn.sigmoid(c)
    hi = ca.astype(BF16)
    lo = (ca - hi.astype(F32)).astype(BF16)
    w = w_ref[0].astype(BF16)
    o_ref[0] = (jnp.dot(hi, w, preferred_element_type=F32) + jnp.dot(lo, w, preferred_element_type=F32)) + b_ref[0]


def _adaln(c, w_ada, b_ada):
    depth, d, n6 = w_ada.shape
    bsz = c.shape[0]
    rows = 8
    cp = jnp.zeros((rows, d), F32).at[:bsz].set(c)
    tn = 1536
    out = pl.pallas_call(
        _ada_kernel,
        grid=(depth, n6 // tn),
        in_specs=[
            pl.BlockSpec((rows, d), lambda l, j: (0, 0)),
            pl.BlockSpec((1, d, tn), lambda l, j: (l, 0, j)),
            pl.BlockSpec((1, 1, tn), lambda l, j: (l, 0, j)),
        ],
        out_specs=pl.BlockSpec((1, rows, tn), lambda l, j: (l, 0, j)),
        out_shape=jax.ShapeDtypeStruct((depth, rows, n6), F32),
        compiler_params=_params(("parallel", "parallel")),
        name="adaln",
    )(cp, w_ada, b_ada.reshape(depth, 1, n6))
    return out[:, :bsz].reshape(depth, bsz, 6, d)


def _in_kernel(*refs, n_moe):
    moe, (x_ref, mod_ref, g_ref, w_ref), rest = refs[:n_moe], refs[n_moe:n_moe + 4], refs[n_moe + 4:]
    hb = rest[-1]
    i = pl.program_id(0)

    @pl.when(i == 0)
    def _():
        hb[1] = jnp.zeros(hb.shape[1:], hb.dtype)

    outs = rest[1:-1] if n_moe else rest[:-1]
    mod = mod_ref[0]
    prev = hb[(i + 1) % 2]
    width = outs[0].shape[1]
    tm = x_ref.shape[0]
    groups = [slice(r, r + tm // IN_GROUPS) for r in range(0, tm, tm // IN_GROUPS)]

    def prologue(rs):
        if n_moe:
            ga0, ga1, gb0, gb1, gt_ref, pmod_ref = moe
            gt = gt_ref[rs, :]
            y = gt[:, 0:1] * _unpack_rows(ga0[rs, :], gb0[rs, :]) + gt[:, 1:2] * _unpack_rows(ga1[rs, :], gb1[rs, :])
            x = x_ref[rs, :] + pmod_ref[0][5:6, :] * y
            rest[0][rs, :] = x
        else:
            x = x_ref[rs, :]
        hb[i % 2, rs, :] = (_rms(x, g_ref[...]) * (1.0 + mod[1:2, :]) + mod[0:1, :]).astype(BF16)

    per = -(-len(groups) // len(outs))
    for k, o in enumerate(outs):
        for rs in groups[k * per:(k + 1) * per]:
            prologue(rs)
        w = w_ref[0, :, k * width:(k + 1) * width].astype(BF16)
        o[...] = jnp.dot(prev, w, preferred_element_type=F32).astype(o.dtype)


def _in_proj(x2, mod_l, g, w_in, layer, seq, moe=None):
    n, d = x2.shape
    n_out = w_in.shape[2]
    width = 512
    tm = ROW_TILE
    per_batch = seq // tm
    last = n // tm - 1
    cur = lambda i: jnp.minimum(i, last)
    row = lambda w: pl.BlockSpec((tm, w), lambda i: (cur(i), 0))
    mod_spec = pl.BlockSpec((1, 6, d), lambda i: (cur(i) // per_batch, 0, 0))
    in_specs, args, out_specs, out_shape = [], [], [], []
    if moe is not None:
        ga, gb, gates, mod_prev = moe
        second = pl.BlockSpec((tm, PACK_WIDTH), lambda i: (cur(i) + n // tm, 0))
        in_specs += [row(PACK_WIDTH), second, row(PACK_WIDTH), second, row(LANES), mod_spec]
        args += [ga, ga, gb, gb, gates, mod_prev]
        out_specs.append(row(d))
        out_shape.append(jax.ShapeDtypeStruct((n, d), F32))
    in_specs += [row(d), mod_spec, pl.BlockSpec((1, d), lambda i: (0, 0)),
                 pl.BlockSpec((1, d, n_out), lambda i: (layer, 0, 0), pipeline_mode=pl.Buffered(1))]
    args += [x2, mod_l, g.reshape(1, d), w_in]
    out_dtypes = (F32, F32, F32, BF16, F32, BF16, F32)
    out_specs += [pl.BlockSpec((tm, width), lambda i: (jnp.maximum(i - 1, 0), 0))] * len(out_dtypes)
    out_shape += [jax.ShapeDtypeStruct((n, width), dt) for dt in out_dtypes]
    outs = pl.pallas_call(
        functools.partial(_in_kernel, n_moe=0 if moe is None else 6),
        grid=(n // tm + 1,),
        in_specs=in_specs,
        out_specs=out_specs,
        out_shape=out_shape,
        scratch_shapes=[pltpu.VMEM((2, tm, d), BF16)],
        compiler_params=_params(("arbitrary",)),
        name="in_proj",
    )(*args)
    return (x2, outs) if moe is None else (outs[0], outs[1:])


def _attn_bias(n_heads):
    qi = np.arange(ATTN_BLOCK)[:, None]
    kj = np.arange(2 * ATTN_BLOCK)[None, :]
    dist = ATTN_BLOCK + qi - kj
    valid = (dist >= 0) & (dist <= ATTN_BLOCK)
    slopes = np.exp2(-ALIBI_MAX_EXP * np.arange(1, n_heads + 1, dtype=np.float32) / n_heads)
    out = np.empty((2, len(DILATIONS), n_heads, ATTN_BLOCK, 2 * ATTN_BLOCK), np.float32)
    for p, dil in enumerate(DILATIONS):
        b = -slopes[:, None, None] * (dist * dil).astype(np.float32)[None] * np.float32(LOG2E)
        out[0, p] = np.where(valid[None], b, -np.inf)
        out[1, p] = np.where((valid & (kj >= ATTN_BLOCK))[None], b, -np.inf)
    return jnp.asarray(out)


def _attn_kernel(q_ref, k_ref, v_ref, bias_ref, o_ref, kcar, vcar, pbuf, vbuf, oscr, mscr, dscr):
    n = pl.program_id(2)
    sb = q_ref.shape[0]
    is_lo = lax.broadcasted_iota(jnp.int32, (ATTN_BLOCK, LANES), 1) < ATTN_HEAD_DIM
    scale = ATTN_HEAD_DIM ** -0.5 * LOG2E

    @pl.when(n == 0)
    def _():
        kcar[...] = jnp.zeros_like(kcar)
        vcar[...] = jnp.zeros_like(vcar)

    pbuf[...] = jnp.zeros_like(pbuf)
    vbuf[...] = jnp.zeros_like(vbuf)
    lo2 = lax.broadcasted_iota(jnp.int32, (2 * ATTN_BLOCK, LANES), 1) < ATTN_HEAD_DIM
    zero = jnp.zeros((2 * ATTN_BLOCK, LANES), BF16)
    den_lo = lo2.astype(BF16)
    den_hi = jnp.logical_not(lo2).astype(BF16)

    for p, dil in enumerate(DILATIONS):
        nb = (sb // ATTN_BLOCK) // dil

        def rows_of(it, dil=dil, nb=nb):
            start = (it % nb) * (ATTN_BLOCK * dil) + it // nb
            return pl.ds(start, ATTN_BLOCK, stride=dil) if dil > 1 else pl.ds(pl.multiple_of(start, ATTN_BLOCK), ATTN_BLOCK)

        def pv(it, probs, vaug, p=p, rows_of=rows_of):
            rows = rows_of(it)
            of = jnp.dot(probs, vaug, preferred_element_type=F32)
            oscr[p, rows, :] = of[:, :LANES]
            dscr[p, rows, :] = of[:, LANES:]

        def qk(it, p=p, nb=nb, rows_of=rows_of):
            r = it // nb
            rows = rows_of(it)
            crow = pl.ds(pl.multiple_of(r * ATTN_BLOCK, ATTN_BLOCK), ATTN_BLOCK)
            qb = q_ref[rows, :] * scale
            kc = k_ref[rows, :].astype(BF16)
            vc = v_ref[rows, :].astype(BF16)
            kcat = jnp.concatenate([kcar[p, crow, :], kc], axis=0)
            vcat = jnp.concatenate([vcar[p, crow, :], vc], axis=0)
            vaug = jnp.concatenate([jnp.concatenate([jnp.where(lo2, vcat, zero), den_lo], axis=1),
                                    jnp.concatenate([jnp.where(lo2, zero, vcat), den_hi], axis=1)], axis=0)
            kcar[p, crow, :] = kc
            vcar[p, crow, :] = vc
            first = jnp.logical_and(n == 0, it % nb == 0).astype(jnp.int32)
            probs, acc_m = [], None
            for hh in range(2):
                sel = is_lo if hh == 0 else jnp.logical_not(is_lo)
                qm = jnp.where(sel, qb, 0.0).astype(BF16)
                s = _dot_nt(qm, kcat) + bias_ref[first, p, hh]
                m = jnp.max(s, axis=-1, keepdims=True)
                probs.append(jnp.exp2(s - m).astype(BF16))
                mb = jnp.broadcast_to(m, (ATTN_BLOCK, LANES))
                acc_m = mb if hh == 0 else jnp.where(sel, mb, acc_m)
            mscr[p, rows, :] = acc_m
            return jnp.concatenate(probs, axis=1), vaug

        def body(g, carry, pv=pv, qk=qk):
            it0 = g * ATTN_UNROLL
            pend = (pbuf[...], vbuf[...])
            for u in range(ATTN_UNROLL):
                cur = qk(it0 + u)
                pv(jnp.maximum(it0 + u - 1, 0), *pend)
                pend = cur
            pbuf[...] = pend[0]
            vbuf[...] = pend[1]
            return carry

        n_it = sb // ATTN_BLOCK
        lax.fori_loop(0, n_it // ATTN_UNROLL, body, 0)
        pv(n_it - 1, pbuf[...], vbuf[...])

    ct = 256

    def combine(t, carry):
        rs = pl.ds(pl.multiple_of(t * ct, ct), ct)
        m0, m1, m2 = mscr[0, rs, :], mscr[1, rs, :], mscr[2, rs, :]
        mx = jnp.maximum(jnp.maximum(m0, m1), m2)
        w0, w1, w2 = jnp.exp2(m0 - mx), jnp.exp2(m1 - mx), jnp.exp2(m2 - mx)
        num = w0 * oscr[0, rs, :] + w1 * oscr[1, rs, :] + w2 * oscr[2, rs, :]
        den = w0 * dscr[0, rs, :] + w1 * dscr[1, rs, :] + w2 * dscr[2, rs, :]
        o_ref[rs, :] = (num / den).astype(o_ref.dtype)
        return carry

    lax.fori_loop(0, sb // ct, combine, 0)


def _attention(aq, ak, av, bias, bsz, seq):
    n, width = aq.shape
    sb = ATTN_SUPER
    nsb = seq // sb
    pairs = width // LANES
    npat = len(DILATIONS)
    blk = pl.BlockSpec((sb, LANES), lambda b, h, t: (b * nsb + t, h))
    return pl.pallas_call(
        _attn_kernel,
        grid=(bsz, pairs, nsb),
        in_specs=[blk, blk, blk,
                  pl.BlockSpec((2, npat, 2, ATTN_BLOCK, 2 * ATTN_BLOCK), lambda b, h, t: (0, 0, h, 0, 0))],
        out_specs=blk,
        out_shape=jax.ShapeDtypeStruct((n, width), BF16),
        scratch_shapes=[
            pltpu.VMEM((npat, max(DILATIONS) * ATTN_BLOCK, LANES), BF16),
            pltpu.VMEM((npat, max(DILATIONS) * ATTN_BLOCK, LANES), BF16),
            pltpu.VMEM((ATTN_BLOCK, 4 * ATTN_BLOCK), BF16),
            pltpu.VMEM((4 * ATTN_BLOCK, 2 * LANES), BF16),
            pltpu.VMEM((npat, sb, LANES), F32),
            pltpu.VMEM((npat, sb, LANES), F32),
            pltpu.VMEM((npat, sb, LANES), F32),
        ],
        compiler_params=_params(("parallel", "parallel", "arbitrary")),
        name="dilated_attn",
    )(aq, ak, av, bias)


def _hgrn_consts():
    c = HGRN_CHUNK
    t = np.arange(c)[:, None]
    u = np.arange(c)[None, :]
    low, masks = [], []
    m = c // 2
    while m >= 1:
        mid = (t // (2 * m)) * (2 * m) + m
        is_q = t >= mid
        if m in (4, 2):
            low.append((is_q & (u >= mid) & (u <= t)) | (~is_q & (u > t) & (u < mid)))
        masks.append(((t // (2 * m)) == (u // (2 * m))) & (t % (2 * m) >= m) & (u % (2 * m) < m))
        m //= 2
    masks.append(t == u)
    f = lambda a: np.asarray(a, np.float32)
    return (jnp.asarray(f(u <= t), BF16), jnp.asarray(f(np.concatenate(low, axis=0)), BF16),
            jnp.asarray(f(np.broadcast_to(t % 2 == 1, (c, c)))), jnp.asarray(f(np.stack(masks))))


def _hgrn_kernel(q_ref, z_ref, v_ref, g_ref, lb_ref, gn_ref, tri_ref, low_ref, odd_ref, msk_ref, o_ref, st):
    c = HGRN_CHUNK

    @pl.when(pl.program_id(2) == 0)
    def _():
        st[...] = jnp.zeros_like(st)

    lb = jnp.maximum(lb_ref[...], 0.0)
    log_lb = jnp.log(lb)
    log_1m = jnp.log1p(-lb)
    one_m = 1.0 - lb
    gn = gn_ref[...]
    n_levels = msk_ref.shape[0] - 1
    coarse = [c >> (i + 1) for i in range(n_levels) if (c >> (i + 1)) >= 8]

    def chunk(ci):
        rs = pl.ds(pl.multiple_of(ci * c, c), c)
        z = z_ref[rs, :]
        qb = q_ref[rs, :]
        vb = v_ref[rs, :]
        ez = jnp.exp(-jnp.abs(z))
        ls = jnp.minimum(z, 0.0) - jnp.log(1.0 + ez)
        b2 = log_1m + ls
        lf = jnp.maximum(log_lb, b2) + jnp.log(1.0 + jnp.exp(-jnp.abs(log_lb - b2)))
        rz = 1.0 / (1.0 + ez)
        kb = (one_m * jnp.where(z > 0.0, ez * rz, rz)).astype(BF16)
        l2 = lf * LOG2E
        hi = l2.astype(BF16)
        hl = jnp.concatenate([hi, (l2 - hi.astype(F32)).astype(BF16)], axis=1)
        yield
        bb = jnp.dot(tri_ref[...], hl, preferred_element_type=F32)
        b = bb[:, :HGRN_EXPAND] + bb[:, HGRN_EXPAND:]
        dl = jnp.dot(low_ref[...], hl, preferred_element_type=F32)
        dl = dl[:, :HGRN_EXPAND] + dl[:, HGRN_EXPAND:]
        yield

        def level(d):
            e = jnp.exp2(d).astype(BF16)
            return _dot_nt(qb * e, kb * e)

        a = msk_ref[n_levels] * _dot_nt(qb, kb)
        for i in range(n_levels):
            m = c >> (i + 1)
            if m >= 8:
                parts = []
                for r0 in range(0, c, 2 * m):
                    mid = b[r0 + m - 1:r0 + m, :]
                    parts += [mid - b[r0:r0 + m, :], b[r0 + m:r0 + 2 * m, :] - mid]
                d = jnp.concatenate(parts, axis=0)
            elif m > 1:
                d = dl[(i - len(coarse)) * c:(i - len(coarse) + 1) * c]
            else:
                d = l2 * odd_ref[...]
            a = a + msk_ref[i] * level(d)
            if i % 3 == 2:
                yield
        b_last = b[c - 1:c, :]
        s_t = st[...]
        o = _dot_nt(qb * jnp.exp2(b).astype(BF16), s_t.astype(BF16)) + jnp.dot(a.astype(BF16), vb, preferred_element_type=F32)
        st[...] = jnp.exp2(b_last) * s_t + _dot_tn(vb, kb * jnp.exp2(b_last - b).astype(BF16))
        yield
        g = g_ref[rs, :]
        o_ref[rs, :] = (_rms(o, gn) * (g * jax.nn.sigmoid(g))).astype(o_ref.dtype)

    def group(gi, carry):
        _round_robin([chunk(gi * HGRN_UNROLL + u) for u in range(HGRN_UNROLL)])
        return carry

    _loop(q_ref.shape[0] // (c * HGRN_UNROLL), group)


def _hgrn(rq, rf, ri, rg, lb, gn, consts, bsz, seq):
    n, width = rq.shape
    heads = width // HGRN_EXPAND
    ts = HGRN_TILE
    nt = seq // ts
    blk = pl.BlockSpec((ts, HGRN_EXPAND), lambda b, h, t: (b * nt + t, h))
    vec = pl.BlockSpec((1, HGRN_EXPAND), lambda b, h, t: (0, h))
    const = lambda a: pl.BlockSpec(a.shape, lambda b, h, t: (0,) * a.ndim)
    return pl.pallas_call(
        _hgrn_kernel,
        grid=(bsz, heads, nt),
        in_specs=[blk, blk, blk, blk, vec, vec] + [const(a) for a in consts],
        out_specs=blk,
        out_shape=jax.ShapeDtypeStruct((n, width), BF16),
        scratch_shapes=[pltpu.VMEM((HGRN_EXPAND, HGRN_EXPAND), F32)],
        compiler_params=_params(("parallel", "parallel", "arbitrary")),
        name="hgrn2",
    )(rq, rf, ri, rg, lb.reshape(1, width), gn.reshape(1, width), *consts)


def _route_math(lg):
    row = lax.broadcasted_iota(jnp.int32, (EXPERTS_PER_GROUP, lg.shape[1]), 0).astype(F32)
    none = float(EXPERTS_PER_GROUP)

    def first_argmax(vals):
        mx = jnp.max(vals, axis=0, keepdims=True)
        return mx, jnp.min(jnp.where(vals == mx, row, none), axis=0, keepdims=True)

    g = lg[0:EXPERTS_PER_GROUP, :]
    is_group = row < N_GROUPS
    gmax, gidx = first_argmax(jnp.where(is_group, g, -jnp.inf))
    gprob = 1.0 / jnp.sum(jnp.where(is_group, jnp.exp(g - gmax), 0.0), axis=0, keepdims=True)
    el = lg[EXPERTS_PER_GROUP:2 * EXPERTS_PER_GROUP, :]
    for k in range(1, N_GROUPS):
        el = jnp.where(gidx == k, lg[(k + 1) * EXPERTS_PER_GROUP:(k + 2) * EXPERTS_PER_GROUP, :], el)
    t1, i1 = first_argmax(el)
    t2, i2 = first_argmax(jnp.where(row == i1, -jnp.inf, el))
    ex = jnp.exp(t2 - t1)
    base = gidx * EXPERTS_PER_GROUP
    eids = jnp.concatenate([base + i1, base + i2], axis=0).astype(jnp.int32)
    gates = jnp.concatenate([gprob / (1.0 + ex), gprob * ex / (1.0 + ex)], axis=0)
    return eids, gates


def _out_kernel(a_ref, r_ref, x_ref, mod_ref, ag_ref, g2_ref, wo_ref, wr_ref, br_ref, xo_ref, ha_ref, hb_ref, e_ref, gt_ref, cnt_ref):
    mod = mod_ref[0]
    mix = jnp.concatenate([_rms(a_ref[...].astype(F32), ag_ref[...]).astype(BF16), r_ref[...]], axis=1)
    xn = x_ref[...] + mod[2:3, :] * jnp.dot(mix, wo_ref[0].astype(BF16), preferred_element_type=F32)
    xo_ref[...] = xn
    h2 = _rms(xn, g2_ref[...]) * (1.0 + mod[4:5, :]) + mod[3:4, :]
    ha_ref[...], hb_ref[...] = _pack_rows(h2)
    eids, gates = _route_math(_dot_nt(wr_ref[...], h2.astype(BF16)) + br_ref[...])
    e_ref[...] = eids
    eio = lax.broadcasted_iota(jnp.int32, (N_EXPERTS, eids.shape[1]), 0)
    hit = jnp.logical_or(eids[0:1, :] == eio, eids[1:2, :] == eio).astype(F32)

    @pl.when(pl.program_id(0) == 0)
    def _():
        cnt_ref[...] = jnp.zeros_like(cnt_ref)

    cnt_ref[...] += jnp.sum(hit, axis=1, keepdims=True)
    pad = jnp.zeros((LANES - TOP_K, LANES), F32)
    for k in range(h2.shape[0] // LANES):
        tile = jnp.concatenate([gates[:, k * LANES:(k + 1) * LANES], pad], axis=0)
        gt_ref[k * LANES:(k + 1) * LANES, :] = tile.T


def _out_proj(attn, rec, x2, mod_l, ag, g2, w_out, layer, wr_bf, br, seq):
    n, d = x2.shape
    half = attn.shape[1]
    tm = ROW_TILE
    per_batch = seq // tm
    row = lambda w: pl.BlockSpec((tm, w), lambda i: (i, 0))
    full = lambda a: pl.BlockSpec(a.shape, lambda i: (0,) * a.ndim, pipeline_mode=pl.Buffered(1))
    ag2, g22 = ag.reshape(1, half), g2.reshape(1, d)
    return pl.pallas_call(
        _out_kernel,
        grid=(n // tm,),
        in_specs=[row(half), row(half), row(d), pl.BlockSpec((1, 6, d), lambda i: (i // per_batch, 0, 0)),
                  full(ag2), full(g22), pl.BlockSpec((1, d, d), lambda i: (layer, 0, 0), pipeline_mode=pl.Buffered(1)),
                  full(wr_bf), full(br)],
        out_specs=[row(d), row(PACK_WIDTH), row(PACK_WIDTH), pl.BlockSpec((TOP_K, tm), lambda i: (0, i)), row(LANES),
                   pl.BlockSpec((N_EXPERTS, LANES), lambda i: (0, 0))],
        out_shape=[jax.ShapeDtypeStruct((n, d), F32), jax.ShapeDtypeStruct((n, PACK_WIDTH), U32),
                   jax.ShapeDtypeStruct((n, PACK_WIDTH), U32), jax.ShapeDtypeStruct((TOP_K, n), jnp.int32),
                   jax.ShapeDtypeStruct((n, LANES), F32), jax.ShapeDtypeStruct((N_EXPERTS, LANES), F32)],
        compiler_params=_params(("arbitrary",)),
        name="out_proj",
    )(attn, rec, x2, mod_l, ag2, g22, w_out, wr_bf, br)


def _plan_kernel(e_ref, cnt_ref, ut_ref, pos_ref, be_ref, bv_ref, bf_ref, bn_ref, nu_ref, off, run):
    i = pl.program_id(0)
    t = e_ref.shape[1]
    eio = lax.broadcasted_iota(jnp.int32, (N_EXPERTS, t), 0)
    oh1 = e_ref[0:1, :] == eio
    oh2 = e_ref[1:2, :] == eio
    oh = jnp.logical_or(oh1, oh2).astype(F32)

    @pl.when(i == 0)
    def _():
        counts = cnt_ref[...]
        nb = jnp.floor((counts + (MOE_TILE - 1)) * (1.0 / MOE_TILE))
        er = lax.broadcasted_iota(jnp.int32, (N_EXPERTS, N_EXPERTS), 0)
        ec = lax.broadcasted_iota(jnp.int32, (N_EXPERTS, N_EXPERTS), 1)
        cum_excl = jnp.dot((ec < er).astype(BF16), nb.astype(BF16), preferred_element_type=F32)
        off[...] = cum_excl * MOE_TILE
        run[...] = jnp.zeros_like(run)
        ce, ci, cn = cum_excl[:, 0:1], (cum_excl + nb)[:, 0:1], counts[:, 0:1]
        b = lax.broadcasted_iota(jnp.int32, (N_EXPERTS, be_ref.shape[1]), 1).astype(F32)
        member = jnp.logical_and(b >= ce, b < ci)
        be = jnp.minimum(jnp.sum((b >= ci).astype(F32), axis=0, keepdims=True), N_EXPERTS - 1.0)
        bv = jnp.sum(jnp.where(member, jnp.minimum(cn - (b - ce) * MOE_TILE, float(MOE_TILE)), 0.0), axis=0, keepdims=True)
        be_ref[...] = be.astype(jnp.int32)
        bv_ref[...] = bv.astype(jnp.int32)
        has = nb[:, 0:1] > 0.0
        eid = lax.broadcasted_iota(jnp.int32, b.shape, 0).astype(F32)
        bf_ref[...] = jnp.sum(jnp.where(jnp.logical_and(b == ce, has), 1.0, 0.0), axis=0, keepdims=True).astype(jnp.int32)
        nxt = jnp.min(jnp.where(jnp.logical_and(eid > be, has), eid, float(N_EXPERTS)), axis=0, keepdims=True)
        bn_ref[...] = jnp.where(nxt < N_EXPERTS, nxt, -1.0).astype(jnp.int32)
        nu_ref[...] = jnp.broadcast_to(ci[N_EXPERTS - 1:N_EXPERTS, :], nu_ref.shape).astype(jnp.int32)

    c = jnp.dot(oh.astype(BF16), ut_ref[...], preferred_element_type=F32)
    r = off[:, 0:1] + run[:, 0:1] + c - 1.0
    pos_ref[0:1, :] = jnp.sum(jnp.where(oh1, r, 0.0), axis=0, keepdims=True).astype(jnp.int32)
    pos_ref[1:2, :] = jnp.sum(jnp.where(oh2, r, 0.0), axis=0, keepdims=True).astype(jnp.int32)
    run[...] += jnp.sum(oh, axis=1, keepdims=True)


def _plan(expert_ids_t, counts, n_rows):
    n = expert_ids_t.shape[1]
    t = PLAN_TILE
    n_blocks = n_rows // MOE_TILE
    assert n_blocks <= PLAN_LANES and TOP_K == 2
    ut = jnp.asarray(np.triu(np.ones((t, t), np.float32)), BF16)
    tab = jax.ShapeDtypeStruct((1, PLAN_LANES), jnp.int32)
    const = lambda shape: pl.BlockSpec(shape, lambda i: (0, 0))
    pos, be, bv, bf, bn, nu = pl.pallas_call(
        _plan_kernel,
        grid=(n // t,),
        in_specs=[pl.BlockSpec((TOP_K, t), lambda i: (0, i)), const((N_EXPERTS, LANES)), const((t, t))],
        out_specs=[pl.BlockSpec((TOP_K, t), lambda i: (0, i))] + [const((1, PLAN_LANES))] * 4 + [const((1, LANES))],
        out_shape=[jax.ShapeDtypeStruct((TOP_K, n), jnp.int32), tab, tab, tab, tab, jax.ShapeDtypeStruct((1, LANES), jnp.int32)],
        scratch_shapes=[pltpu.VMEM((N_EXPERTS, LANES), F32)] * 2,
        compiler_params=_params(("arbitrary",)),
        name="plan",
    )(expert_ids_t, counts, ut)
    return pos, (be.reshape(-1), bv.reshape(-1), bf.reshape(-1), bn.reshape(-1), nu.reshape(-1)[:1])


def _sc_mesh():
    return plsc.VectorSubcoreMesh(core_axis_name="c", subcore_axis_name="s")


def _sc_scatter_rows(x, pos, n_rows):
    n, d = x.shape
    w = SC_WINDOW

    @functools.partial(pl.kernel, out_type=jax.ShapeDtypeStruct((n_rows, d), x.dtype), mesh=_sc_mesh(), scratch_types=[])
    def scatter(x_hbm, p_hbm, o_hbm):
        def body(x_vmem, i_vmem):
            for k in range(TOP_K):
                pltpu.sync_copy(x_vmem, o_hbm.at[i_vmem.at[k]])

        pltpu.emit_pipeline(
            body,
            grid=(n // w,),
            in_specs=[pl.BlockSpec((w, d), lambda i: (i, 0)), pl.BlockSpec((TOP_K, w), lambda i: (0, i))],
            out_specs=[],
            core_axis_name=("c", "s"),
            dimension_semantics=(pltpu.PARALLEL,),
        )(x_hbm, p_hbm)

    return scatter(x, pos)


def _sc_gather_rows(y, pos):
    m = pos.shape[0] * pos.shape[1]
    d = y.shape[1]
    w = SC_WINDOW

    @functools.partial(pl.kernel, out_type=jax.ShapeDtypeStruct((m, d), y.dtype), mesh=_sc_mesh(), scratch_types=[])
    def gather(y_hbm, p_hbm, o_hbm):
        def body(i_vmem, o_vmem):
            pltpu.sync_copy(y_hbm.at[i_vmem.at[0]], o_vmem)

        pltpu.emit_pipeline(
            body,
            grid=(m // w,),
            in_specs=[pl.BlockSpec((1, w), lambda i: (0, i))],
            out_specs=[pl.BlockSpec((w, d), lambda i: (i, 0))],
            core_axis_name=("c", "s"),
            dimension_semantics=(pltpu.PARALLEL,),
        )(p_hbm, o_hbm)

    return gather(y, pos.reshape(1, m))


def _expert_kernel(be_ref, bv_ref, bf_ref, bn_ref, nu_ref, xa_ref, xb_ref, w1_hbm, w3_hbm, w2_hbm, ya_ref, yb_ref,
                   wf1, wf3, wf2, w1b, w3b, w2b, wsem, par, *, layer):
    i = pl.program_id(0)
    tb = xa_ref.shape[0]

    def fetch(e, slot):
        return (pltpu.make_async_copy(w1_hbm.at[layer, e], wf1.at[slot], wsem.at[slot, 0]),
                pltpu.make_async_copy(w3_hbm.at[layer, e], wf3.at[slot], wsem.at[slot, 1]),
                pltpu.make_async_copy(w2_hbm.at[layer, e], wf2.at[slot], wsem.at[slot, 2]))

    def ffn(rows):
        live = lax.broadcasted_iota(jnp.int32, (rows, 1), 0) < bv_ref[i]
        x = jnp.where(live, _unpack_rows(xa_ref[0:rows, :], xb_ref[0:rows, :]), 0.0).astype(BF16)
        y = None
        hc = w1b.shape[1] // FFN_SPLIT
        for c in range(FFN_SPLIT):
            cs = slice(c * hc, (c + 1) * hc)
            a = jnp.dot(x, w1b[:, cs], preferred_element_type=F32)
            b = jnp.dot(x, w3b[:, cs], preferred_element_type=F32)
            hm = (a * jax.nn.sigmoid(a) * b).astype(BF16)
            part = jnp.dot(hm, w2b[cs, :], preferred_element_type=F32)
            y = part if y is None else y + part
        ya_ref[0:rows, :], yb_ref[0:rows, :] = _pack_rows(y)

    @pl.when(i == 0)
    def _():
        par[0] = 0
        for c in fetch(be_ref[0], 0):
            c.start()

    @pl.when(i < nu_ref[0])
    def _():
        @pl.when(bf_ref[i] == 1)
        def _():
            slot = par[0]
            for c in fetch(be_ref[i], slot):
                c.wait()
            w1b[...] = wf1[slot].astype(BF16)
            w3b[...] = wf3[slot].astype(BF16)
            w2b[...] = wf2[slot].astype(BF16)

            @pl.when(bn_ref[i] >= 0)
            def _():
                for c in fetch(bn_ref[i], 1 - slot):
                    c.start()

            par[0] = 1 - slot

        sizes = (tb // 8, tb // 2, tb)
        for k, rows in enumerate(sizes):
            below = sizes[k - 1] if k else 0
            fits = (bv_ref[i] > below) if rows == tb else ((bv_ref[i] > below) & (bv_ref[i] <= rows))

            @pl.when(fits)
            def _(rows=rows):
                ffn(rows)
                if rows < tb:
                    ya_ref[rows:, :] = jnp.zeros((tb - rows, ya_ref.shape[1]), ya_ref.dtype)
                    yb_ref[rows:, :] = jnp.zeros((tb - rows, yb_ref.shape[1]), yb_ref.dtype)

    @pl.when(i >= nu_ref[0])
    def _():
        ya_ref[...] = jnp.zeros_like(ya_ref)
        yb_ref[...] = jnp.zeros_like(yb_ref)


def _experts(xa, xb, w1, w3, w2, layer, tables):
    n_rows = xa.shape[0]
    d, de = w1.shape[-2:]
    tb = MOE_TILE
    xspec = pl.BlockSpec((tb, PACK_WIDTH), lambda i, be, bv, bf, bn, nu: (jnp.minimum(i, nu[0] - 1), 0))
    yspec = pl.BlockSpec((tb, PACK_WIDTH), lambda i, be, bv, bf, bn, nu: (i, 0))
    anyspec = pl.BlockSpec(memory_space=pl.ANY)
    grid_spec = pltpu.PrefetchScalarGridSpec(
        num_scalar_prefetch=5,
        grid=(n_rows // tb,),
        in_specs=[xspec, xspec, anyspec, anyspec, anyspec],
        out_specs=[yspec, yspec],
        scratch_shapes=[pltpu.VMEM((2, d, de), F32), pltpu.VMEM((2, d, de), F32), pltpu.VMEM((2, de, d), F32),
                        pltpu.VMEM((d, de), BF16), pltpu.VMEM((d, de), BF16), pltpu.VMEM((de, d), BF16),
                        pltpu.SemaphoreType.DMA((2, 3)), pltpu.SMEM((1,), jnp.int32)],
    )
    rows = jax.ShapeDtypeStruct((n_rows, PACK_WIDTH), U32)
    return pl.pallas_call(
        functools.partial(_expert_kernel, layer=layer),
        grid_spec=grid_spec,
        out_shape=[rows, rows],
        compiler_params=_params(("arbitrary",)),
        name="experts",
    )(*tables, xa, xb, w1, w3, w2)


def _final_kernel(ga0, ga1, gb0, gb1, gt_ref, mod_ref, x_ref, fg_ref, o_ref):
    o_ref[...] = _rms(_moe_residual(x_ref, ga0, ga1, gb0, gb1, gt_ref, mod_ref[0][5:6, :]), fg_ref[...])


def _final(x2, ga, gb, gates, mod_l, final_g, seq):
    n, d = x2.shape
    tm = ROW_TILE
    per_batch = seq // tm
    row = lambda w: pl.BlockSpec((tm, w), lambda i: (i, 0))
    second = pl.BlockSpec((tm, PACK_WIDTH), lambda i: (i + n // tm, 0))
    return pl.pallas_call(
        _final_kernel,
        grid=(n // tm,),
        in_specs=[row(PACK_WIDTH), second, row(PACK_WIDTH), second, row(LANES),
                  pl.BlockSpec((1, 6, d), lambda i: (i // per_batch, 0, 0)), row(d), pl.BlockSpec((1, d), lambda i: (0, 0))],
        out_specs=row(d),
        out_shape=jax.ShapeDtypeStruct((n, d), F32),
        compiler_params=_params(("parallel",)),
        name="final_norm",
    )(ga, ga, gb, gb, gates, mod_l, x2, final_g.reshape(1, d))


def kernel(x, c, w_ada, b_ada, norm1_g, w_in, attn_norm_g, hgrn_lb_logits, hgrn_norm_g, w_out, norm2_g, router_group_w, router_group_b, router_expert_w, router_expert_b, moe_w1, moe_w3, moe_w2, final_g):
    bsz, seq, d = x.shape
    depth = w_ada.shape[0]
    n = bsz * seq
    attn_dim = attn_norm_g.shape[1]
    n_heads = attn_dim // ATTN_HEAD_DIM

    lb_w = jax.nn.softmax(hgrn_lb_logits.astype(F32), axis=0)
    lower_bounds = jnp.cumsum(lb_w, axis=0) - lb_w[0:1]
    mod = _adaln(c, w_ada, b_ada)
    bias = _attn_bias(n_heads)
    hconsts = _hgrn_consts()
    n_rows = n * TOP_K + N_EXPERTS * MOE_TILE

    x2 = x.reshape(n, d)
    moe = None
    for layer in range(depth):
        x2, (aq, ak, av, rq, rf, ri, rg) = _in_proj(x2, mod[layer], norm1_g[layer], w_in, layer, seq, moe)
        attn = _attention(aq, ak, av, bias, bsz, seq)
        rec = _hgrn(rq, rf, ri, rg, lower_bounds[layer], hgrn_norm_g[layer], hconsts, bsz, seq)
        gpad = EXPERTS_PER_GROUP - N_GROUPS
        wr = jnp.concatenate([router_group_w[layer].T, jnp.zeros((gpad, d), F32), router_expert_w[layer].T], axis=0)
        br = jnp.concatenate([router_group_b[layer], jnp.zeros((gpad,), F32), router_expert_b[layer]]).reshape(-1, 1)
        x2, ha, hb, eids, gates, counts = _out_proj(attn, rec, x2, mod[layer], attn_norm_g[layer], norm2_g[layer],
                                            w_out, layer, wr.astype(BF16), br, seq)
        pos, tables = _plan(eids, counts, n_rows)
        ya, yb = _experts(_sc_scatter_rows(ha, pos, n_rows), _sc_scatter_rows(hb, pos, n_rows),
                          moe_w1, moe_w3, moe_w2, layer, tables)
        moe = (_sc_gather_rows(ya, pos), _sc_gather_rows(yb, pos), gates, mod[layer])
    return _final(x2, moe[0], moe[1], moe[2], moe[3], final_g, seq).reshape(bsz, seq, d)
```

```python
import functools

import numpy as np
import jax
import jax.numpy as jnp
from jax import lax
from jax.experimental import pallas as pl
from jax.experimental.pallas import tpu as pltpu
from jax.experimental.pallas import tpu_sc as plsc

F32 = jnp.float32
BF16 = jnp.bfloat16
U32 = jnp.uint32

ATTN_HEAD_DIM = 64
ATTN_BLOCK = 128
DILATIONS = (1, 4, 16)
ALIBI_MAX_EXP = 8.0
HGRN_EXPAND = 128
HGRN_CHUNK = 128
N_GROUPS = 4
EXPERTS_PER_GROUP = 8
N_EXPERTS = N_GROUPS * EXPERTS_PER_GROUP
TOP_K = 2
NORM_EPS = 1e-6
LOG2E = 1.4426950408889634

LANES = 128
ATTN_SUPER = ATTN_BLOCK * max(DILATIONS)
ROW_TILE = 512
IN_GROUPS = 16
ATTN_UNROLL = 16
HGRN_TILE = 2048
HGRN_UNROLL = 16
MOE_TILE = 512
FFN_SPLIT = 2
PLAN_TILE = 4096
PLAN_SPAN = 256
PLAN_LANES = 256
PACK_WIDTH = 256
SC_WINDOW = 128
VMEM_LIMIT = 56 << 20


def _params(semantics):
    return pltpu.CompilerParams(dimension_semantics=semantics, vmem_limit_bytes=VMEM_LIMIT)


def _rms(x, g):
    return x * lax.rsqrt(jnp.mean(x * x, axis=-1, keepdims=True) + NORM_EPS) * g


def _loop(trips, body):
    if trips == 1:
        body(0, 0)
    else:
        lax.fori_loop(0, trips, body, 0)


def _round_robin(gens):
    live = list(gens)
    while live:
        for gen in list(live):
            try:
                next(gen)
            except StopIteration:
                live.remove(gen)


def _dot_nt(a, b):
    return lax.dot_general(a, b, (((1,), (1,)), ((), ())), preferred_element_type=F32)


def _dot_tn(a, b):
    return lax.dot_general(a, b, (((0,), (0,)), ((), ())), preferred_element_type=F32)


def _pack_pair(hi, lo):
    hb = lax.bitcast_convert_type(hi.astype(BF16).astype(F32), U32)
    lb = lax.bitcast_convert_type(lo.astype(BF16).astype(F32), U32)
    return hb | (lb >> 16)


def _unpack_pair(u):
    hi = lax.bitcast_convert_type(u & jnp.uint32(0xFFFF0000), F32)
    lo = lax.bitcast_convert_type(u << 16, F32)
    return hi, lo


def _pack_rows(y):
    w = PACK_WIDTH
    return _pack_pair(y[:, 0:w], y[:, 2 * w:3 * w]), _pack_pair(y[:, w:2 * w], y[:, 3 * w:4 * w])


def _unpack_rows(ua, ub):
    ha, la = _unpack_pair(ua)
    hb, lb = _unpack_pair(ub)
    return jnp.concatenate([ha, hb, la, lb], axis=1)


def _moe_residual(x_ref, ga0, ga1, gb0, gb1, gt_ref, gate_row):
    gt = gt_ref[...]
    y = gt[:, 0:1] * _unpack_rows(ga0[...], gb0[...]) + gt[:, 1:2] * _unpack_rows(ga1[...], gb1[...])
    return x_ref[...] + gate_row * y


def _ada_kernel(c_ref, w_ref, b_ref, o_ref):
    c = c_ref[...]
    ca = c * jax.nn.sigmoid(c)
    hi = ca.astype(BF16)
    lo = (ca - hi.astype(F32)).astype(BF16)
    w = w_ref[0].astype(BF16)
    o_ref[0] = (jnp.dot(hi, w, preferred_element_type=F32) + jnp.dot(lo, w, preferred_element_type=F32)) + b_ref[0]


def _adaln(c, w_ada, b_ada):
    depth, d, n6 = w_ada.shape
    bsz = c.shape[0]
    rows = 8
    cp = jnp.zeros((rows, d), F32).at[:bsz].set(c)
    tn = 1536
    out = pl.pallas_call(
        _ada_kernel,
        grid=(depth, n6 // tn),
        in_specs=[
            pl.BlockSpec((rows, d), lambda l, j: (0, 0)),
            pl.BlockSpec((1, d, tn), lambda l, j: (l, 0, j)),
            pl.BlockSpec((1, 1, tn), lambda l, j: (l, 0, j)),
        ],
        out_specs=pl.BlockSpec((1, rows, tn), lambda l, j: (l, 0, j)),
        out_shape=jax.ShapeDtypeStruct((depth, rows, n6), F32),
        compiler_params=_params(("parallel", "parallel")),
        name="adaln",
    )(cp, w_ada, b_ada.reshape(depth, 1, n6))
    return out[:, :bsz].reshape(depth, bsz, 6, d)


def _in_kernel(*refs, n_moe):
    moe, (x_ref, mod_ref, g_ref, w_ref), rest = refs[:n_moe], refs[n_moe:n_moe + 4], refs[n_moe + 4:]
    hb = rest[-1]
    i = pl.program_id(0)

    @pl.when(i == 0)
    def _():
        hb[1] = jnp.zeros(hb.shape[1:], hb.dtype)

    outs = rest[1:-1] if n_moe else rest[:-1]
    mod = mod_ref[0]
    prev = hb[(i + 1) % 2]
    width = outs[0].shape[1]
    tm = x_ref.shape[0]
    groups = [slice(r, r + tm // IN_GROUPS) for r in range(0, tm, tm // IN_GROUPS)]

    def prologue(rs):
        if n_moe:
            ga0, ga1, gb0, gb1, gt_ref, pmod_ref = moe
            gt = gt_ref[rs, :]
            y = gt[:, 0:1] * _unpack_rows(ga0[rs, :], gb0[rs, :]) + gt[:, 1:2] * _unpack_rows(ga1[rs, :], gb1[rs, :])
            x = x_ref[rs, :] + pmod_ref[0][5:6, :] * y
            rest[0][rs, :] = x
        else:
            x = x_ref[rs, :]
        hb[i % 2, rs, :] = (_rms(x, g_ref[...]) * (1.0 + mod[1:2, :]) + mod[0:1, :]).astype(BF16)

    per = -(-len(groups) // len(outs))
    for k, o in enumerate(outs):
        for rs in groups[k * per:(k + 1) * per]:
            prologue(rs)
        w = w_ref[0, :, k * width:(k + 1) * width].astype(BF16)
        o[...] = jnp.dot(prev, w, preferred_element_type=F32).astype(o.dtype)


def _in_proj(x2, mod_l, g, w_in, layer, seq, moe=None):
    n, d = x2.shape
    n_out = w_in.shape[2]
    width = 512
    tm = ROW_TILE
    per_batch = seq // tm
    last = n // tm - 1
    cur = lambda i: jnp.minimum(i, last)
    row = lambda w: pl.BlockSpec((tm, w), lambda i: (cur(i), 0))
    mod_spec = pl.BlockSpec((1, 6, d), lambda i: (cur(i) // per_batch, 0, 0))
    in_specs, args, out_specs, out_shape = [], [], [], []
    if moe is not None:
        ga, gb, gates, mod_prev = moe
        second = pl.BlockSpec((tm, PACK_WIDTH), lambda i: (cur(i) + n // tm, 0))
        in_specs += [row(PACK_WIDTH), second, row(PACK_WIDTH), second, row(LANES), mod_spec]
        args += [ga, ga, gb, gb, gates, mod_prev]
        out_specs.append(row(d))
        out_shape.append(jax.ShapeDtypeStruct((n, d), F32))
    in_specs += [row(d), mod_spec, pl.BlockSpec((1, d), lambda i: (0, 0)),
                 pl.BlockSpec((1, d, n_out), lambda i: (layer, 0, 0), pipeline_mode=pl.Buffered(1))]
    args += [x2, mod_l, g.reshape(1, d), w_in]
    out_dtypes = (F32, F32, F32, BF16, F32, BF16, F32)
    out_specs += [pl.BlockSpec((tm, width), lambda i: (jnp.maximum(i - 1, 0), 0))] * len(out_dtypes)
    out_shape += [jax.ShapeDtypeStruct((n, width), dt) for dt in out_dtypes]
    outs = pl.pallas_call(
        functools.partial(_in_kernel, n_moe=0 if moe is None else 6),
        grid=(n // tm + 1,),
        in_specs=in_specs,
        out_specs=out_specs,
        out_shape=out_shape,
        scratch_shapes=[pltpu.VMEM((2, tm, d), BF16)],
        compiler_params=_params(("arbitrary",)),
        name="in_proj",
    )(*args)
    return (x2, outs) if moe is None else (outs[0], outs[1:])


def _attn_bias(n_heads):
    qi = np.arange(ATTN_BLOCK)[:, None]
    kj = np.arange(2 * ATTN_BLOCK)[None, :]
    dist = ATTN_BLOCK + qi - kj
    valid = (dist >= 0) & (dist <= ATTN_BLOCK)
    slopes = np.exp2(-ALIBI_MAX_EXP * np.arange(1, n_heads + 1, dtype=np.float32) / n_heads)
    out = np.empty((2, len(DILATIONS), n_heads, ATTN_BLOCK, 2 * ATTN_BLOCK), np.float32)
    for p, dil in enumerate(DILATIONS):
        b = -slopes[:, None, None] * (dist * dil).astype(np.float32)[None] * np.float32(LOG2E)
        out[0, p] = np.where(valid[None], b, -np.inf)
        out[1, p] = np.where((valid & (kj >= ATTN_BLOCK))[None], b, -np.inf)
    return jnp.asarray(out)


def _attn_kernel(q_ref, k_ref, v_ref, bias_ref, o_ref, kcar, vcar, pbuf, vbuf, oscr, mscr, dscr):
    n = pl.program_id(2)
    sb = q_ref.shape[0]
    is_lo = lax.broadcasted_iota(jnp.int32, (ATTN_BLOCK, LANES), 1) < ATTN_HEAD_DIM
    scale = ATTN_HEAD_DIM ** -0.5 * LOG2E

    @pl.when(n == 0)
    def _():
        kcar[...] = jnp.zeros_like(kcar)
        vcar[...] = jnp.zeros_like(vcar)

    pbuf[...] = jnp.zeros_like(pbuf)
    vbuf[...] = jnp.zeros_like(vbuf)
    lo2 = lax.broadcasted_iota(jnp.int32, (2 * ATTN_BLOCK, LANES), 1) < ATTN_HEAD_DIM
    zero = jnp.zeros((2 * ATTN_BLOCK, LANES), BF16)
    den_lo = lo2.astype(BF16)
    den_hi = jnp.logical_not(lo2).astype(BF16)

    for p, dil in enumerate(DILATIONS):
        nb = (sb // ATTN_BLOCK) // dil

        def rows_of(it, dil=dil, nb=nb):
            start = (it % nb) * (ATTN_BLOCK * dil) + it // nb
            return pl.ds(start, ATTN_BLOCK, stride=dil) if dil > 1 else pl.ds(pl.multiple_of(start, ATTN_BLOCK), ATTN_BLOCK)

        def pv(it, probs, vaug, p=p, rows_of=rows_of):
            rows = rows_of(it)
            of = jnp.dot(probs, vaug, preferred_element_type=F32)
            oscr[p, rows, :] = of[:, :LANES]
            dscr[p, rows, :] = of[:, LANES:]

        def qk(it, p=p, nb=nb, rows_of=rows_of):
            r = it // nb
            rows = rows_of(it)
            crow = pl.ds(pl.multiple_of(r * ATTN_BLOCK, ATTN_BLOCK), ATTN_BLOCK)
            qb = q_ref[rows, :] * scale
            kc = k_ref[rows, :].astype(BF16)
            vc = v_ref[rows, :].astype(BF16)
            kcat = jnp.concatenate([kcar[p, crow, :], kc], axis=0)
            vcat = jnp.concatenate([vcar[p, crow, :], vc], axis=0)
            vaug = jnp.concatenate([jnp.concatenate([jnp.where(lo2, vcat, zero), den_lo], axis=1),
                                    jnp.concatenate([jnp.where(lo2, zero, vcat), den_hi], axis=1)], axis=0)
            kcar[p, crow, :] = kc
            vcar[p, crow, :] = vc
            first = jnp.logical_and(n == 0, it % nb == 0).astype(jnp.int32)
            probs, acc_m = [], None
            for hh in range(2):
                sel = is_lo if hh == 0 else jnp.logical_not(is_lo)
                qm = jnp.where(sel, qb, 0.0).astype(BF16)
                s = _dot_nt(qm, kcat) + bias_ref[first, p, hh]
                m = jnp.max(s, axis=-1, keepdims=True)
                probs.append(jnp.exp2(s - m).astype(BF16))
                mb = jnp.broadcast_to(m, (ATTN_BLOCK, LANES))
                acc_m = mb if hh == 0 else jnp.where(sel, mb, acc_m)
            mscr[p, rows, :] = acc_m
            return jnp.concatenate(probs, axis=1), vaug

        def body(g, carry, pv=pv, qk=qk):
            it0 = g * ATTN_UNROLL
            pend = (pbuf[...], vbuf[...])
            for u in range(ATTN_UNROLL):
                cur = qk(it0 + u)
                pv(jnp.maximum(it0 + u - 1, 0), *pend)
                pend = cur
            pbuf[...] = pend[0]
            vbuf[...] = pend[1]
            return carry

        n_it = sb // ATTN_BLOCK
        lax.fori_loop(0, n_it // ATTN_UNROLL, body, 0)
        pv(n_it - 1, pbuf[...], vbuf[...])

    ct = 256

    def combine(t, carry):
        rs = pl.ds(pl.multiple_of(t * ct, ct), ct)
        m0, m1, m2 = mscr[0, rs, :], mscr[1, rs, :], mscr[2, rs, :]
        mx = jnp.maximum(jnp.maximum(m0, m1), m2)
        w0, w1, w2 = jnp.exp2(m0 - mx), jnp.exp2(m1 - mx), jnp.exp2(m2 - mx)
        num = w0 * oscr[0, rs, :] + w1 * oscr[1, rs, :] + w2 * oscr[2, rs, :]
        den = w0 * dscr[0, rs, :] + w1 * dscr[1, rs, :] + w2 * dscr[2, rs, :]
        o_ref[rs, :] = (num / den).astype(o_ref.dtype)
        return carry

    lax.fori_loop(0, sb // ct, combine, 0)


def _attention(aq, ak, av, bias, bsz, seq):
    n, width = aq.shape
    sb = ATTN_SUPER
    nsb = seq // sb
    pairs = width // LANES
    npat = len(DILATIONS)
    blk = pl.BlockSpec((sb, LANES), lambda b, h, t: (b * nsb + t, h))
    return pl.pallas_call(
        _attn_kernel,
        grid=(bsz, pairs, nsb),
        in_specs=[blk, blk, blk,
                  pl.BlockSpec((2, npat, 2, ATTN_BLOCK, 2 * ATTN_BLOCK), lambda b, h, t: (0, 0, h, 0, 0))],
        out_specs=blk,
        out_shape=jax.ShapeDtypeStruct((n, width), BF16),
        scratch_shapes=[
            pltpu.VMEM((npat, max(DILATIONS) * ATTN_BLOCK, LANES), BF16),
            pltpu.VMEM((npat, max(DILATIONS) * ATTN_BLOCK, LANES), BF16),
            pltpu.VMEM((ATTN_BLOCK, 4 * ATTN_BLOCK), BF16),
            pltpu.VMEM((4 * ATTN_BLOCK, 2 * LANES), BF16),
            pltpu.VMEM((npat, sb, LANES), F32),
            pltpu.VMEM((npat, sb, LANES), F32),
            pltpu.VMEM((npat, sb, LANES), F32),
        ],
        compiler_params=_params(("parallel", "parallel", "arbitrary")),
        name="dilated_attn",
    )(aq, ak, av, bias)


def _hgrn_consts():
    c = HGRN_CHUNK
    t = np.arange(c)[:, None]
    u = np.arange(c)[None, :]
    low, masks = [], []
    m = c // 2
    while m >= 1:
        mid = (t // (2 * m)) * (2 * m) + m
        is_q = t >= mid
        if m in (4, 2):
            low.append((is_q & (u >= mid) & (u <= t)) | (~is_q & (u > t) & (u < mid)))
        masks.append(((t // (2 * m)) == (u // (2 * m))) & (t % (2 * m) >= m) & (u % (2 * m) < m))
        m //= 2
    masks.append(t == u)
    f = lambda a: np.asarray(a, np.float32)
    return (jnp.asarray(f(u <= t), BF16), jnp.asarray(f(np.concatenate(low, axis=0)), BF16),
            jnp.asarray(f(np.broadcast_to(t % 2 == 1, (c, c)))), jnp.asarray(f(np.stack(masks))))


def _hgrn_kernel(q_ref, z_ref, v_ref, g_ref, lb_ref, gn_ref, tri_ref, low_ref, odd_ref, msk_ref, o_ref, st):
    c = HGRN_CHUNK

    @pl.when(pl.program_id(2) == 0)
    def _():
        st[...] = jnp.zeros_like(st)

    lb = jnp.maximum(lb_ref[...], 0.0)
    log_lb = jnp.log(lb)
    log_1m = jnp.log1p(-lb)
    one_m = 1.0 - lb
    gn = gn_ref[...]
    n_levels = msk_ref.shape[0] - 1
    coarse = [c >> (i + 1) for i in range(n_levels) if (c >> (i + 1)) >= 8]

    def chunk(ci):
        rs = pl.ds(pl.multiple_of(ci * c, c), c)
        z = z_ref[rs, :]
        qb = q_ref[rs, :]
        vb = v_ref[rs, :]
        ez = jnp.exp(-jnp.abs(z))
        ls = jnp.minimum(z, 0.0) - jnp.log(1.0 + ez)
        b2 = log_1m + ls
        lf = jnp.maximum(log_lb, b2) + jnp.log(1.0 + jnp.exp(-jnp.abs(log_lb - b2)))
        rz = 1.0 / (1.0 + ez)
        kb = (one_m * jnp.where(z > 0.0, ez * rz, rz)).astype(BF16)
        l2 = lf * LOG2E
        hi = l2.astype(BF16)
        hl = jnp.concatenate([hi, (l2 - hi.astype(F32)).astype(BF16)], axis=1)
        yield
        bb = jnp.dot(tri_ref[...], hl, preferred_element_type=F32)
        b = bb[:, :HGRN_EXPAND] + bb[:, HGRN_EXPAND:]
        dl = jnp.dot(low_ref[...], hl, preferred_element_type=F32)
        dl = dl[:, :HGRN_EXPAND] + dl[:, HGRN_EXPAND:]
        yield

        def level(d):
            e = jnp.exp2(d).astype(BF16)
            return _dot_nt(qb * e, kb * e)

        a = msk_ref[n_levels] * _dot_nt(qb, kb)
        for i in range(n_levels):
            m = c >> (i + 1)
            if m >= 8:
                parts = []
                for r0 in range(0, c, 2 * m):
                    mid = b[r0 + m - 1:r0 + m, :]
                    parts += [mid - b[r0:r0 + m, :], b[r0 + m:r0 + 2 * m, :] - mid]
                d = jnp.concatenate(parts, axis=0)
            elif m > 1:
                d = dl[(i - len(coarse)) * c:(i - len(coarse) + 1) * c]
            else:
                d = l2 * odd_ref[...]
            a = a + msk_ref[i] * level(d)
            if i % 3 == 2:
                yield
        b_last = b[c - 1:c, :]
        s_t = st[...]
        o = _dot_nt(qb * jnp.exp2(b).astype(BF16), s_t.astype(BF16)) + jnp.dot(a.astype(BF16), vb, preferred_element_type=F32)
        st[...] = jnp.exp2(b_last) * s_t + _dot_tn(vb, kb * jnp.exp2(b_last - b).astype(BF16))
        yield
        g = g_ref[rs, :]
        o_ref[rs, :] = (_rms(o, gn) * (g * jax.nn.sigmoid(g))).astype(o_ref.dtype)

    def group(gi, carry):
        _round_robin([chunk(gi * HGRN_UNROLL + u) for u in range(HGRN_UNROLL)])
        return carry

    _loop(q_ref.shape[0] // (c * HGRN_UNROLL), group)


def _hgrn(rq, rf, ri, rg, lb, gn, consts, bsz, seq):
    n, width = rq.shape
    heads = width // HGRN_EXPAND
    ts = HGRN_TILE
    nt = seq // ts
    blk = pl.BlockSpec((ts, HGRN_EXPAND), lambda b, h, t: (b * nt + t, h))
    vec = pl.BlockSpec((1, HGRN_EXPAND), lambda b, h, t: (0, h))
    const = lambda a: pl.BlockSpec(a.shape, lambda b, h, t: (0,) * a.ndim)
    return pl.pallas_call(
        _hgrn_kernel,
        grid=(bsz, heads, nt),
        in_specs=[blk, blk, blk, blk, vec, vec] + [const(a) for a in consts],
        out_specs=blk,
        out_shape=jax.ShapeDtypeStruct((n, width), BF16),
        scratch_shapes=[pltpu.VMEM((HGRN_EXPAND, HGRN_EXPAND), F32)],
        compiler_params=_params(("parallel", "parallel", "arbitrary")),
        name="hgrn2",
    )(rq, rf, ri, rg, lb.reshape(1, width), gn.reshape(1, width), *consts)


def _route_math(lg):
    row = lax.broadcasted_iota(jnp.int32, (EXPERTS_PER_GROUP, lg.shape[1]), 0).astype(F32)
    none = float(EXPERTS_PER_GROUP)

    def first_argmax(vals):
        mx = jnp.max(vals, axis=0, keepdims=True)
        return mx, jnp.min(jnp.where(vals == mx, row, none), axis=0, keepdims=True)

    g = lg[0:EXPERTS_PER_GROUP, :]
    is_group = row < N_GROUPS
    gmax, gidx = first_argmax(jnp.where(is_group, g, -jnp.inf))
    gprob = 1.0 / jnp.sum(jnp.where(is_group, jnp.exp(g - gmax), 0.0), axis=0, keepdims=True)
    el = lg[EXPERTS_PER_GROUP:2 * EXPERTS_PER_GROUP, :]
    for k in range(1, N_GROUPS):
        el = jnp.where(gidx == k, lg[(k + 1) * EXPERTS_PER_GROUP:(k + 2) * EXPERTS_PER_GROUP, :], el)
    t1, i1 = first_argmax(el)
    t2, i2 = first_argmax(jnp.where(row == i1, -jnp.inf, el))
    ex = jnp.exp(t2 - t1)
    base = gidx * EXPERTS_PER_GROUP
    eids = jnp.concatenate([base + i1, base + i2], axis=0).astype(jnp.int32)
    gates = jnp.concatenate([gprob / (1.0 + ex), gprob * ex / (1.0 + ex)], axis=0)
    return eids, gates


def _out_kernel(a_ref, r_ref, x_ref, mod_ref, ag_ref, g2_ref, wo_ref, wr_ref, br_ref, xo_ref, ha_ref, hb_ref, e_ref, gt_ref, cnt_ref):
    mod = mod_ref[0]
    mix = jnp.concatenate([_rms(a_ref[...].astype(F32), ag_ref[...]).astype(BF16), r_ref[...]], axis=1)
    xn = x_ref[...] + mod[2:3, :] * jnp.dot(mix, wo_ref[0].astype(BF16), preferred_element_type=F32)
    xo_ref[...] = xn
    h2 = _rms(xn, g2_ref[...]) * (1.0 + mod[4:5, :]) + mod[3:4, :]
    ha_ref[...], hb_ref[...] = _pack_rows(h2)
    eids, gates = _route_math(_dot_nt(wr_ref[...], h2.astype(BF16)) + br_ref[...])
    e_ref[...] = eids
    eio = lax.broadcasted_iota(jnp.int32, (N_EXPERTS, eids.shape[1]), 0)
    hit = jnp.logical_or(eids[0:1, :] == eio, eids[1:2, :] == eio).astype(F32)

    @pl.when(pl.program_id(0) == 0)
    def _():
        cnt_ref[...] = jnp.zeros_like(cnt_ref)

    cnt_ref[...] += jnp.sum(hit, axis=1, keepdims=True)
    pad = jnp.zeros((LANES - TOP_K, LANES), F32)
    for k in range(h2.shape[0] // LANES):
        tile = jnp.concatenate([gates[:, k * LANES:(k + 1) * LANES], pad], axis=0)
        gt_ref[k * LANES:(k + 1) * LANES, :] = tile.T


def _out_proj(attn, rec, x2, mod_l, ag, g2, w_out, layer, wr_bf, br, seq):
    n, d = x2.shape
    half = attn.shape[1]
    tm = ROW_TILE
    per_batch = seq // tm
    row = lambda w: pl.BlockSpec((tm, w), lambda i: (i, 0))
    full = lambda a: pl.BlockSpec(a.shape, lambda i: (0,) * a.ndim, pipeline_mode=pl.Buffered(1))
    ag2, g22 = ag.reshape(1, half), g2.reshape(1, d)
    return pl.pallas_call(
        _out_kernel,
        grid=(n // tm,),
        in_specs=[row(half), row(half), row(d), pl.BlockSpec((1, 6, d), lambda i: (i // per_batch, 0, 0)),
                  full(ag2), full(g22), pl.BlockSpec((1, d, d), lambda i: (layer, 0, 0), pipeline_mode=pl.Buffered(1)),
                  full(wr_bf), full(br)],
        out_specs=[row(d), row(PACK_WIDTH), row(PACK_WIDTH), pl.BlockSpec((TOP_K, tm), lambda i: (0, i)), row(LANES),
                   pl.BlockSpec((N_EXPERTS, LANES), lambda i: (0, 0))],
        out_shape=[jax.ShapeDtypeStruct((n, d), F32), jax.ShapeDtypeStruct((n, PACK_WIDTH), U32),
                   jax.ShapeDtypeStruct((n, PACK_WIDTH), U32), jax.ShapeDtypeStruct((TOP_K, n), jnp.int32),
                   jax.ShapeDtypeStruct((n, LANES), F32), jax.ShapeDtypeStruct((N_EXPERTS, LANES), F32)],
        compiler_params=_params(("arbitrary",)),
        name="out_proj",
    )(attn, rec, x2, mod_l, ag2, g22, w_out, wr_bf, br)


def _plan_kernel(e_ref, cnt_ref, ut_ref, pos_ref, be_ref, bv_ref, bf_ref, bn_ref, nu_ref, off, run):
    i = pl.program_id(0)
    t = e_ref.shape[1]
    eio = lax.broadcasted_iota(jnp.int32, (N_EXPERTS, t), 0)
    oh1 = e_ref[0:1, :] == eio
    oh2 = e_ref[1:2, :] == eio
    oh = jnp.logical_or(oh1, oh2).astype(F32)

    @pl.when(i == 0)
    def _():
        counts = cnt_ref[...]
        nb = jnp.floor((counts + (MOE_TILE - 1)) * (1.0 / MOE_TILE))
        er = lax.broadcasted_iota(jnp.int32, (N_EXPERTS, N_EXPERTS), 0)
        ec = lax.broadcasted_iota(jnp.int32, (N_EXPERTS, N_EXPERTS), 1)
        cum_excl = jnp.dot((ec < er).astype(BF16), nb.astype(BF16), preferred_element_type=F32)
        off[...] = cum_excl * MOE_TILE
        run[...] = jnp.zeros_like(run)
        ce, ci, cn = cum_excl[:, 0:1], (cum_excl + nb)[:, 0:1], counts[:, 0:1]
        b = lax.broadcasted_iota(jnp.int32, (N_EXPERTS, be_ref.shape[1]), 1).astype(F32)
        member = jnp.logical_and(b >= ce, b < ci)
        be = jnp.minimum(jnp.sum((b >= ci).astype(F32), axis=0, keepdims=True), N_EXPERTS - 1.0)
        bv = jnp.sum(jnp.where(member, jnp.minimum(cn - (b - ce) * MOE_TILE, float(MOE_TILE)), 0.0), axis=0, keepdims=True)
        be_ref[...] = be.astype(jnp.int32)
        bv_ref[...] = bv.astype(jnp.int32)
        has = nb[:, 0:1] > 0.0
        eid = lax.broadcasted_iota(jnp.int32, b.shape, 0).astype(F32)
        bf_ref[...] = jnp.sum(jnp.where(jnp.logical_and(b == ce, has), 1.0, 0.0), axis=0, keepdims=True).astype(jnp.int32)
        nxt = jnp.min(jnp.where(jnp.logical_and(eid > be, has), eid, float(N_EXPERTS)), axis=0, keepdims=True)
        bn_ref[...] = jnp.where(nxt < N_EXPERTS, nxt, -1.0).astype(jnp.int32)
        nu_ref[...] = jnp.broadcast_to(ci[N_EXPERTS - 1:N_EXPERTS, :], nu_ref.shape).astype(jnp.int32)

    span = ut_ref.shape[0]
    ohb = oh.astype(BF16)
    base = off[:, 0:1] + run[:, 0:1] - 1.0
    parts = []
    for s in range(0, t, span):
        c = jnp.dot(ohb[:, s:s + span], ut_ref[...], preferred_element_type=F32)
        parts.append(base + c)
        base = base + c[:, span - 1:span]
    r = jnp.concatenate(parts, axis=1)
    pos_ref[0:1, :] = jnp.sum(jnp.where(oh1, r, 0.0), axis=0, keepdims=True).astype(jnp.int32)
    pos_ref[1:2, :] = jnp.sum(jnp.where(oh2, r, 0.0), axis=0, keepdims=True).astype(jnp.int32)
    run[...] += jnp.sum(oh, axis=1, keepdims=True)


def _plan(expert_ids_t, counts, n_rows):
    n = expert_ids_t.shape[1]
    t = PLAN_TILE
    n_blocks = n_rows // MOE_TILE
    assert n_blocks <= PLAN_LANES and TOP_K == 2
    ut = jnp.asarray(np.triu(np.ones((PLAN_SPAN, PLAN_SPAN), np.float32)), BF16)
    tab = jax.ShapeDtypeStruct((1, PLAN_LANES), jnp.int32)
    const = lambda shape: pl.BlockSpec(shape, lambda i: (0, 0))
    pos, be, bv, bf, bn, nu = pl.pallas_call(
        _plan_kernel,
        grid=(n // t,),
        in_specs=[pl.BlockSpec((TOP_K, t), lambda i: (0, i)), const((N_EXPERTS, LANES)), const((PLAN_SPAN, PLAN_SPAN))],
        out_specs=[pl.BlockSpec((TOP_K, t), lambda i: (0, i))] + [const((1, PLAN_LANES))] * 4 + [const((1, LANES))],
        out_shape=[jax.ShapeDtypeStruct((TOP_K, n), jnp.int32), tab, tab, tab, tab, jax.ShapeDtypeStruct((1, LANES), jnp.int32)],
        scratch_shapes=[pltpu.VMEM((N_EXPERTS, LANES), F32)] * 2,
        compiler_params=_params(("arbitrary",)),
        name="plan",
    )(expert_ids_t, counts, ut)
    return pos, (be.reshape(-1), bv.reshape(-1), bf.reshape(-1), bn.reshape(-1), nu.reshape(-1)[:1])


def _sc_mesh():
    return plsc.VectorSubcoreMesh(core_axis_name="c", subcore_axis_name="s")


def _sc_scatter_rows(x, pos, n_rows):
    n, d = x.shape
    w = SC_WINDOW

    @functools.partial(pl.kernel, out_type=jax.ShapeDtypeStruct((n_rows, d), x.dtype), mesh=_sc_mesh(), scratch_types=[])
    def scatter(x_hbm, p_hbm, o_hbm):
        def body(x_vmem, i_vmem):
            for k in range(TOP_K):
                pltpu.sync_copy(x_vmem, o_hbm.at[i_vmem.at[k]])

        pltpu.emit_pipeline(
            body,
            grid=(n // w,),
            in_specs=[pl.BlockSpec((w, d), lambda i: (i, 0)), pl.BlockSpec((TOP_K, w), lambda i: (0, i))],
            out_specs=[],
            core_axis_name=("c", "s"),
            dimension_semantics=(pltpu.PARALLEL,),
        )(x_hbm, p_hbm)

    return scatter(x, pos)


def _sc_gather_rows(y, pos):
    m = pos.shape[0] * pos.shape[1]
    d = y.shape[1]
    w = SC_WINDOW

    @functools.partial(pl.kernel, out_type=jax.ShapeDtypeStruct((m, d), y.dtype), mesh=_sc_mesh(), scratch_types=[])
    def gather(y_hbm, p_hbm, o_hbm):
        def body(i_vmem, o_vmem):
            pltpu.sync_copy(y_hbm.at[i_vmem.at[0]], o_vmem)

        pltpu.emit_pipeline(
            body,
            grid=(m // w,),
            in_specs=[pl.BlockSpec((1, w), lambda i: (0, i))],
            out_specs=[pl.BlockSpec((w, d), lambda i: (i, 0))],
            core_axis_name=("c", "s"),
            dimension_semantics=(pltpu.PARALLEL,),
        )(p_hbm, o_hbm)

    return gather(y, pos.reshape(1, m))


def _expert_kernel(be_ref, bv_ref, bf_ref, bn_ref, nu_ref, xa_ref, xb_ref, w1_hbm, w3_hbm, w2_hbm, ya_ref, yb_ref,
                   wf1, wf3, wf2, w1b, w3b, w2b, wsem, par, *, layer):
    i = pl.program_id(0)
    tb = xa_ref.shape[0]
    half = tb // 2

    def fetch(e, slot):
        return (pltpu.make_async_copy(w1_hbm.at[layer, e], wf1.at[slot], wsem.at[slot, 0]),
                pltpu.make_async_copy(w3_hbm.at[layer, e], wf3.at[slot], wsem.at[slot, 1]),
                pltpu.make_async_copy(w2_hbm.at[layer, e], wf2.at[slot], wsem.at[slot, 2]))

    def ffn(rows):
        live = lax.broadcasted_iota(jnp.int32, (rows, 1), 0) < bv_ref[i]
        x = jnp.where(live, _unpack_rows(xa_ref[0:rows, :], xb_ref[0:rows, :]), 0.0).astype(BF16)
        y = None
        hc = w1b.shape[1] // FFN_SPLIT
        for c in range(FFN_SPLIT):
            cs = slice(c * hc, (c + 1) * hc)
            a = jnp.dot(x, w1b[:, cs], preferred_element_type=F32)
            b = jnp.dot(x, w3b[:, cs], preferred_element_type=F32)
            hm = (a * jax.nn.sigmoid(a) * b).astype(BF16)
            part = jnp.dot(hm, w2b[cs, :], preferred_element_type=F32)
            y = part if y is None else y + part
        ya_ref[0:rows, :], yb_ref[0:rows, :] = _pack_rows(y)

    @pl.when(i == 0)
    def _():
        par[0] = 0
        for c in fetch(be_ref[0], 0):
            c.start()

    @pl.when(i < nu_ref[0])
    def _():
        @pl.when(bf_ref[i] == 1)
        def _():
            slot = par[0]
            for c in fetch(be_ref[i], slot):
                c.wait()
            w1b[...] = wf1[slot].astype(BF16)
            w3b[...] = wf3[slot].astype(BF16)
            w2b[...] = wf2[slot].astype(BF16)

            @pl.when(bn_ref[i] >= 0)
            def _():
                for c in fetch(bn_ref[i], 1 - slot):
                    c.start()

            par[0] = 1 - slot

        @pl.when(bv_ref[i] > half)
        def _():
            ffn(tb)

        @pl.when(bv_ref[i] <= half)
        def _():
            ffn(half)
            ya_ref[half:, :] = jnp.zeros((tb - half, ya_ref.shape[1]), ya_ref.dtype)
            yb_ref[half:, :] = jnp.zeros((tb - half, yb_ref.shape[1]), yb_ref.dtype)

    @pl.when(i >= nu_ref[0])
    def _():
        ya_ref[...] = jnp.zeros_like(ya_ref)
        yb_ref[...] = jnp.zeros_like(yb_ref)


def _experts(xa, xb, w1, w3, w2, layer, tables):
    n_rows = xa.shape[0]
    d, de = w1.shape[-2:]
    tb = MOE_TILE
    xspec = pl.BlockSpec((tb, PACK_WIDTH), lambda i, be, bv, bf, bn, nu: (jnp.minimum(i, nu[0] - 1), 0))
    yspec = pl.BlockSpec((tb, PACK_WIDTH), lambda i, be, bv, bf, bn, nu: (i, 0))
    anyspec = pl.BlockSpec(memory_space=pl.ANY)
    grid_spec = pltpu.PrefetchScalarGridSpec(
        num_scalar_prefetch=5,
        grid=(n_rows // tb,),
        in_specs=[xspec, xspec, anyspec, anyspec, anyspec],
        out_specs=[yspec, yspec],
        scratch_shapes=[pltpu.VMEM((2, d, de), F32), pltpu.VMEM((2, d, de), F32), pltpu.VMEM((2, de, d), F32),
                        pltpu.VMEM((d, de), BF16), pltpu.VMEM((d, de), BF16), pltpu.VMEM((de, d), BF16),
                        pltpu.SemaphoreType.DMA((2, 3)), pltpu.SMEM((1,), jnp.int32)],
    )
    rows = jax.ShapeDtypeStruct((n_rows, PACK_WIDTH), U32)
    return pl.pallas_call(
        functools.partial(_expert_kernel, layer=layer),
        grid_spec=grid_spec,
        out_shape=[rows, rows],
        compiler_params=_params(("arbitrary",)),
        name="experts",
    )(*tables, xa, xb, w1, w3, w2)


def _final_kernel(ga0, ga1, gb0, gb1, gt_ref, mod_ref, x_ref, fg_ref, o_ref):
    o_ref[...] = _rms(_moe_residual(x_ref, ga0, ga1, gb0, gb1, gt_ref, mod_ref[0][5:6, :]), fg_ref[...])


def _final(x2, ga, gb, gates, mod_l, final_g, seq):
    n, d = x2.shape
    tm = ROW_TILE
    per_batch = seq // tm
    row = lambda w: pl.BlockSpec((tm, w), lambda i: (i, 0))
    second = pl.BlockSpec((tm, PACK_WIDTH), lambda i: (i + n // tm, 0))
    return pl.pallas_call(
        _final_kernel,
        grid=(n // tm,),
        in_specs=[row(PACK_WIDTH), second, row(PACK_WIDTH), second, row(LANES),
                  pl.BlockSpec((1, 6, d), lambda i: (i // per_batch, 0, 0)), row(d), pl.BlockSpec((1, d), lambda i: (0, 0))],
        out_specs=row(d),
        out_shape=jax.ShapeDtypeStruct((n, d), F32),
        compiler_params=_params(("parallel",)),
        name="final_norm",
    )(ga, ga, gb, gb, gates, mod_l, x2, final_g.reshape(1, d))


def kernel(x, c, w_ada, b_ada, norm1_g, w_in, attn_norm_g, hgrn_lb_logits, hgrn_norm_g, w_out, norm2_g, router_group_w, router_group_b, router_expert_w, router_expert_b, moe_w1, moe_w3, moe_w2, final_g):
    bsz, seq, d = x.shape
    depth = w_ada.shape[0]
    n = bsz * seq
    attn_dim = attn_norm_g.shape[1]
    n_heads = attn_dim // ATTN_HEAD_DIM

    lb_w = jax.nn.softmax(hgrn_lb_logits.astype(F32), axis=0)
    lower_bounds = jnp.cumsum(lb_w, axis=0) - lb_w[0:1]
    mod = _adaln(c, w_ada, b_ada)
    bias = _attn_bias(n_heads)
    hconsts = _hgrn_consts()
    n_rows = n * TOP_K + N_EXPERTS * MOE_TILE

    x2 = x.reshape(n, d)
    moe = None
    for layer in range(depth):
        x2, (aq, ak, av, rq, rf, ri, rg) = _in_proj(x2, mod[layer], norm1_g[layer], w_in, layer, seq, moe)
        attn = _attention(aq, ak, av, bias, bsz, seq)
        rec = _hgrn(rq, rf, ri, rg, lower_bounds[layer], hgrn_norm_g[layer], hconsts, bsz, seq)
        gpad = EXPERTS_PER_GROUP - N_GROUPS
        wr = jnp.concatenate([router_group_w[layer].T, jnp.zeros((gpad, d), F32), router_expert_w[layer].T], axis=0)
        br = jnp.concatenate([router_group_b[layer], jnp.zeros((gpad,), F32), router_expert_b[layer]]).reshape(-1, 1)
        x2, ha, hb, eids, gates, counts = _out_proj(attn, rec, x2, mod[layer], attn_norm_g[layer], norm2_g[layer],
                                            w_out, layer, wr.astype(BF16), br, seq)
        pos, tables = _plan(eids, counts, n_rows)
        ya, yb = _experts(_sc_scatter_rows(ha, pos, n_rows), _sc_scatter_rows(hb, pos, n_rows),
                          moe_w1, moe_w3, moe_w2, layer, tables)
        moe = (_sc_gather_rows(ya, pos), _sc_gather_rows(yb, pos), gates, mod[layer])
    return _final(x2, moe[0], moe[1], moe[2], moe[3], final_g, seq).reshape(bsz, seq, d)
```

```python
import functools

import numpy as np
import jax
import jax.numpy as jnp
from jax import lax
from jax.experimental import pallas as pl
from jax.experimental.pallas import tpu as pltpu
from jax.experimental.pallas import tpu_sc as plsc

F32 = jnp.float32
BF16 = jnp.bfloat16
U32 = jnp.uint32

ATTN_HEAD_DIM = 64
ATTN_BLOCK = 128
DILATIONS = (1, 4, 16)
ALIBI_MAX_EXP = 8.0
HGRN_EXPAND = 128
HGRN_CHUNK = 128
N_GROUPS = 4
EXPERTS_PER_GROUP = 8
N_EXPERTS = N_GROUPS * EXPERTS_PER_GROUP
TOP_K = 2
NORM_EPS = 1e-6
LOG2E = 1.4426950408889634

LANES = 128
ATTN_SUPER = ATTN_BLOCK * max(DILATIONS)
ROW_TILE = 512
OUT_GROUPS = 4
IN_GROUPS = 16
ATTN_UNROLL = 16
HGRN_TILE = 2048
HGRN_UNROLL = 16
MOE_TILE = 512
FFN_SPLIT = 2
PLAN_TILE = 4096
PLAN_SPAN = 256
PLAN_LANES = 256
PACK_WIDTH = 256
SC_WINDOW = 128
VMEM_LIMIT = 56 << 20


def _params(semantics):
    return pltpu.CompilerParams(dimension_semantics=semantics, vmem_limit_bytes=VMEM_LIMIT)


def _rms(x, g):
    return x * lax.rsqrt(jnp.mean(x * x, axis=-1, keepdims=True) + NORM_EPS) * g


def _loop(trips, body):
    if trips == 1:
        body(0, 0)
    else:
        lax.fori_loop(0, trips, body, 0)


def _round_robin(gens):
    live = list(gens)
    while live:
        for gen in list(live):
            try:
                next(gen)
            except StopIteration:
                live.remove(gen)


def _dot_nt(a, b):
    return lax.dot_general(a, b, (((1,), (1,)), ((), ())), preferred_element_type=F32)


def _dot_tn(a, b):
    return lax.dot_general(a, b, (((0,), (0,)), ((), ())), preferred_element_type=F32)


def _pack_pair(hi, lo):
    hb = lax.bitcast_convert_type(hi.astype(BF16).astype(F32), U32)
    lb = lax.bitcast_convert_type(lo.astype(BF16).astype(F32), U32)
    return hb | (lb >> 16)


def _unpack_pair(u):
    hi = lax.bitcast_convert_type(u & jnp.uint32(0xFFFF0000), F32)
    lo = lax.bitcast_convert_type(u << 16, F32)
    return hi, lo


def _pack_rows(y):
    w = PACK_WIDTH
    return _pack_pair(y[:, 0:w], y[:, 2 * w:3 * w]), _pack_pair(y[:, w:2 * w], y[:, 3 * w:4 * w])


def _unpack_rows(ua, ub):
    ha, la = _unpack_pair(ua)
    hb, lb = _unpack_pair(ub)
    return jnp.concatenate([ha, hb, la, lb], axis=1)


def _moe_residual(x_ref, ga0, ga1, gb0, gb1, gt_ref, gate_row):
    gt = gt_ref[...]
    y = gt[:, 0:1] * _unpack_rows(ga0[...], gb0[...]) + gt[:, 1:2] * _unpack_rows(ga1[...], gb1[...])
    return x_ref[...] + gate_row * y


def _ada_kernel(c_ref, w_ref, b_ref, o_ref):
    c = c_ref[...]
    ca = c * jax.nn.sigmoid(c)
    hi = ca.astype(BF16)
    lo = (ca - hi.astype(F32)).astype(BF16)
    w = w_ref[0].astype(BF16)
    o_ref[0] = (jnp.dot(hi, w, preferred_element_type=F32) + jnp.dot(lo, w, preferred_element_type=F32)) + b_ref[0]


def _adaln(c, w_ada, b_ada):
    depth, d, n6 = w_ada.shape
    bsz = c.shape[0]
    rows = 8
    cp = jnp.zeros((rows, d), F32).at[:bsz].set(c)
    tn = 1536
    out = pl.pallas_call(
        _ada_kernel,
        grid=(depth, n6 // tn),
        in_specs=[
            pl.BlockSpec((rows, d), lambda l, j: (0, 0)),
            pl.BlockSpec((1, d, tn), lambda l, j: (l, 0, j)),
            pl.BlockSpec((1, 1, tn), lambda l, j: (l, 0, j)),
        ],
        out_specs=pl.BlockSpec((1, rows, tn), lambda l, j: (l, 0, j)),
        out_shape=jax.ShapeDtypeStruct((depth, rows, n6), F32),
        compiler_params=_params(("parallel", "parallel")),
        name="adaln",
    )(cp, w_ada, b_ada.reshape(depth, 1, n6))
    return out[:, :bsz].reshape(depth, bsz, 6, d)


def _in_kernel(*refs, n_moe):
    moe, (x_ref, mod_ref, g_ref, w_ref), rest = refs[:n_moe], refs[n_moe:n_moe + 4], refs[n_moe + 4:]
    hb = rest[-1]
    i = pl.program_id(0)

    @pl.when(i == 0)
    def _():
        hb[1] = jnp.zeros(hb.shape[1:], hb.dtype)

    outs = rest[1:-1] if n_moe else rest[:-1]
    mod = mod_ref[0]
    prev = hb[(i + 1) % 2]
    width = outs[0].shape[1]
    tm = x_ref.shape[0]
    groups = [slice(r, r + tm // IN_GROUPS) for r in range(0, tm, tm // IN_GROUPS)]

    def prologue(rs):
        if n_moe:
            ga0, ga1, gb0, gb1, gt_ref, pmod_ref = moe
            gt = gt_ref[rs, :]
            y = gt[:, 0:1] * _unpack_rows(ga0[rs, :], gb0[rs, :]) + gt[:, 1:2] * _unpack_rows(ga1[rs, :], gb1[rs, :])
            x = x_ref[rs, :] + pmod_ref[0][5:6, :] * y
            rest[0][rs, :] = x
        else:
            x = x_ref[rs, :]
        hb[i % 2, rs, :] = (_rms(x, g_ref[...]) * (1.0 + mod[1:2, :]) + mod[0:1, :]).astype(BF16)

    per = -(-len(groups) // len(outs))
    for k, o in enumerate(outs):
        for rs in groups[k * per:(k + 1) * per]:
            prologue(rs)
        w = w_ref[0, :, k * width:(k + 1) * width].astype(BF16)
        o[...] = jnp.dot(prev, w, preferred_element_type=F32).astype(o.dtype)


def _in_proj(x2, mod_l, g, w_in, layer, seq, moe=None):
    n, d = x2.shape
    n_out = w_in.shape[2]
    width = 512
    tm = ROW_TILE
    per_batch = seq // tm
    last = n // tm - 1
    cur = lambda i: jnp.minimum(i, last)
    row = lambda w: pl.BlockSpec((tm, w), lambda i: (cur(i), 0))
    mod_spec = pl.BlockSpec((1, 6, d), lambda i: (cur(i) // per_batch, 0, 0))
    in_specs, args, out_specs, out_shape = [], [], [], []
    if moe is not None:
        ga, gb, gates, mod_prev = moe
        second = pl.BlockSpec((tm, PACK_WIDTH), lambda i: (cur(i) + n // tm, 0))
        in_specs += [row(PACK_WIDTH), second, row(PACK_WIDTH), second, row(LANES), mod_spec]
        args += [ga, ga, gb, gb, gates, mod_prev]
        out_specs.append(row(d))
        out_shape.append(jax.ShapeDtypeStruct((n, d), F32))
    in_specs += [row(d), mod_spec, pl.BlockSpec((1, d), lambda i: (0, 0)),
                 pl.BlockSpec((1, d, n_out), lambda i: (layer, 0, 0), pipeline_mode=pl.Buffered(1))]
    args += [x2, mod_l, g.reshape(1, d), w_in]
    out_dtypes = (F32, F32, F32, BF16, F32, BF16, F32)
    out_specs += [pl.BlockSpec((tm, width), lambda i: (jnp.maximum(i - 1, 0), 0))] * len(out_dtypes)
    out_shape += [jax.ShapeDtypeStruct((n, width), dt) for dt in out_dtypes]
    outs = pl.pallas_call(
        functools.partial(_in_kernel, n_moe=0 if moe is None else 6),
        grid=(n // tm + 1,),
        in_specs=in_specs,
        out_specs=out_specs,
        out_shape=out_shape,
        scratch_shapes=[pltpu.VMEM((2, tm, d), BF16)],
        compiler_params=_params(("arbitrary",)),
        name="in_proj",
    )(*args)
    return (x2, outs) if moe is None else (outs[0], outs[1:])


def _attn_bias(n_heads):
    qi = np.arange(ATTN_BLOCK)[:, None]
    kj = np.arange(2 * ATTN_BLOCK)[None, :]
    dist = ATTN_BLOCK + qi - kj
    valid = (dist >= 0) & (dist <= ATTN_BLOCK)
    slopes = np.exp2(-ALIBI_MAX_EXP * np.arange(1, n_heads + 1, dtype=np.float32) / n_heads)
    out = np.empty((2, len(DILATIONS), n_heads, ATTN_BLOCK, 2 * ATTN_BLOCK), np.float32)
    for p, dil in enumerate(DILATIONS):
        b = -slopes[:, None, None] * (dist * dil).astype(np.float32)[None] * np.float32(LOG2E)
        out[0, p] = np.where(valid[None], b, -np.inf)
        out[1, p] = np.where((valid & (kj >= ATTN_BLOCK))[None], b, -np.inf)
    return jnp.asarray(out)


def _attn_kernel(q_ref, k_ref, v_ref, bias_ref, o_ref, kcar, vcar, pbuf, vbuf, oscr, mscr, dscr):
    n = pl.program_id(2)
    sb = q_ref.shape[0]
    is_lo = lax.broadcasted_iota(jnp.int32, (ATTN_BLOCK, LANES), 1) < ATTN_HEAD_DIM
    scale = ATTN_HEAD_DIM ** -0.5 * LOG2E

    @pl.when(n == 0)
    def _():
        kcar[...] = jnp.zeros_like(kcar)
        vcar[...] = jnp.zeros_like(vcar)

    pbuf[...] = jnp.zeros_like(pbuf)
    vbuf[...] = jnp.zeros_like(vbuf)
    lo2 = lax.broadcasted_iota(jnp.int32, (2 * ATTN_BLOCK, LANES), 1) < ATTN_HEAD_DIM
    zero = jnp.zeros((2 * ATTN_BLOCK, LANES), BF16)
    den_lo = lo2.astype(BF16)
    den_hi = jnp.logical_not(lo2).astype(BF16)

    for p, dil in enumerate(DILATIONS):
        nb = (sb // ATTN_BLOCK) // dil

        def rows_of(it, dil=dil, nb=nb):
            start = (it % nb) * (ATTN_BLOCK * dil) + it // nb
            return pl.ds(start, ATTN_BLOCK, stride=dil) if dil > 1 else pl.ds(pl.multiple_of(start, ATTN_BLOCK), ATTN_BLOCK)

        def pv(it, probs, vaug, p=p, rows_of=rows_of):
            rows = rows_of(it)
            of = jnp.dot(probs, vaug, preferred_element_type=F32)
            oscr[p, rows, :] = of[:, :LANES]
            dscr[p, rows, :] = of[:, LANES:]

        def qk(it, p=p, nb=nb, rows_of=rows_of):
            r = it // nb
            rows = rows_of(it)
            crow = pl.ds(pl.multiple_of(r * ATTN_BLOCK, ATTN_BLOCK), ATTN_BLOCK)
            qb = q_ref[rows, :] * scale
            kc = k_ref[rows, :].astype(BF16)
            vc = v_ref[rows, :].astype(BF16)
            kcat = jnp.concatenate([kcar[p, crow, :], kc], axis=0)
            vcat = jnp.concatenate([vcar[p, crow, :], vc], axis=0)
            vaug = jnp.concatenate([jnp.concatenate([jnp.where(lo2, vcat, zero), den_lo], axis=1),
                                    jnp.concatenate([jnp.where(lo2, zero, vcat), den_hi], axis=1)], axis=0)
            kcar[p, crow, :] = kc
            vcar[p, crow, :] = vc
            first = jnp.logical_and(n == 0, it % nb == 0).astype(jnp.int32)
            probs, acc_m = [], None
            for hh in range(2):
                sel = is_lo if hh == 0 else jnp.logical_not(is_lo)
                qm = jnp.where(sel, qb, 0.0).astype(BF16)
                s = _dot_nt(qm, kcat) + bias_ref[first, p, hh]
                m = jnp.max(s, axis=-1, keepdims=True)
                probs.append(jnp.exp2(s - m).astype(BF16))
                mb = jnp.broadcast_to(m, (ATTN_BLOCK, LANES))
                acc_m = mb if hh == 0 else jnp.where(sel, mb, acc_m)
            mscr[p, rows, :] = acc_m
            return jnp.concatenate(probs, axis=1), vaug

        def body(g, carry, pv=pv, qk=qk):
            it0 = g * ATTN_UNROLL
            pend = (pbuf[...], vbuf[...])
            for u in range(ATTN_UNROLL):
                cur = qk(it0 + u)
                pv(jnp.maximum(it0 + u - 1, 0), *pend)
                pend = cur
            pbuf[...] = pend[0]
            vbuf[...] = pend[1]
            return carry

        n_it = sb // ATTN_BLOCK
        lax.fori_loop(0, n_it // ATTN_UNROLL, body, 0)
        pv(n_it - 1, pbuf[...], vbuf[...])

    ct = 256

    def combine(t, carry):
        rs = pl.ds(pl.multiple_of(t * ct, ct), ct)
        m0, m1, m2 = mscr[0, rs, :], mscr[1, rs, :], mscr[2, rs, :]
        mx = jnp.maximum(jnp.maximum(m0, m1), m2)
        w0, w1, w2 = jnp.exp2(m0 - mx), jnp.exp2(m1 - mx), jnp.exp2(m2 - mx)
        num = w0 * oscr[0, rs, :] + w1 * oscr[1, rs, :] + w2 * oscr[2, rs, :]
        den = w0 * dscr[0, rs, :] + w1 * dscr[1, rs, :] + w2 * dscr[2, rs, :]
        o_ref[rs, :] = (num / den).astype(o_ref.dtype)
        return carry

    lax.fori_loop(0, sb // ct, combine, 0)


def _attention(aq, ak, av, bias, bsz, seq):
    n, width = aq.shape
    sb = ATTN_SUPER
    nsb = seq // sb
    pairs = width // LANES
    npat = len(DILATIONS)
    blk = pl.BlockSpec((sb, LANES), lambda b, h, t: (b * nsb + t, h))
    return pl.pallas_call(
        _attn_kernel,
        grid=(bsz, pairs, nsb),
        in_specs=[blk, blk, blk,
                  pl.BlockSpec((2, npat, 2, ATTN_BLOCK, 2 * ATTN_BLOCK), lambda b, h, t: (0, 0, h, 0, 0))],
        out_specs=blk,
        out_shape=jax.ShapeDtypeStruct((n, width), BF16),
        scratch_shapes=[
            pltpu.VMEM((npat, max(DILATIONS) * ATTN_BLOCK, LANES), BF16),
            pltpu.VMEM((npat, max(DILATIONS) * ATTN_BLOCK, LANES), BF16),
            pltpu.VMEM((ATTN_BLOCK, 4 * ATTN_BLOCK), BF16),
            pltpu.VMEM((4 * ATTN_BLOCK, 2 * LANES), BF16),
            pltpu.VMEM((npat, sb, LANES), F32),
            pltpu.VMEM((npat, sb, LANES), F32),
            pltpu.VMEM((npat, sb, LANES), F32),
        ],
        compiler_params=_params(("parallel", "parallel", "arbitrary")),
        name="dilated_attn",
    )(aq, ak, av, bias)


def _hgrn_consts():
    c = HGRN_CHUNK
    t = np.arange(c)[:, None]
    u = np.arange(c)[None, :]
    low, masks = [], []
    m = c // 2
    while m >= 1:
        mid = (t // (2 * m)) * (2 * m) + m
        is_q = t >= mid
        if m in (4, 2):
            low.append((is_q & (u >= mid) & (u <= t)) | (~is_q & (u > t) & (u < mid)))
        masks.append(((t // (2 * m)) == (u // (2 * m))) & (t % (2 * m) >= m) & (u % (2 * m) < m))
        m //= 2
    masks.append(t == u)
    f = lambda a: np.asarray(a, np.float32)
    return (jnp.asarray(f(u <= t), BF16), jnp.asarray(f(np.concatenate(low, axis=0)), BF16),
            jnp.asarray(f(np.broadcast_to(t % 2 == 1, (c, c)))), jnp.asarray(f(np.stack(masks))))


def _hgrn_kernel(q_ref, z_ref, v_ref, g_ref, lb_ref, gn_ref, tri_ref, low_ref, odd_ref, msk_ref, o_ref, st):
    c = HGRN_CHUNK

    @pl.when(pl.program_id(2) == 0)
    def _():
        st[...] = jnp.zeros_like(st)

    lb = jnp.maximum(lb_ref[...], 0.0)
    log_lb = jnp.log(lb)
    log_1m = jnp.log1p(-lb)
    one_m = 1.0 - lb
    gn = gn_ref[...]
    n_levels = msk_ref.shape[0] - 1
    coarse = [c >> (i + 1) for i in range(n_levels) if (c >> (i + 1)) >= 8]

    def chunk(ci):
        rs = pl.ds(pl.multiple_of(ci * c, c), c)
        z = z_ref[rs, :]
        qb = q_ref[rs, :]
        vb = v_ref[rs, :]
        ez = jnp.exp(-jnp.abs(z))
        ls = jnp.minimum(z, 0.0) - jnp.log(1.0 + ez)
        b2 = log_1m + ls
        lf = jnp.maximum(log_lb, b2) + jnp.log(1.0 + jnp.exp(-jnp.abs(log_lb - b2)))
        rz = 1.0 / (1.0 + ez)
        kb = (one_m * jnp.where(z > 0.0, ez * rz, rz)).astype(BF16)
        l2 = lf * LOG2E
        hi = l2.astype(BF16)
        hl = jnp.concatenate([hi, (l2 - hi.astype(F32)).astype(BF16)], axis=1)
        yield
        bb = jnp.dot(tri_ref[...], hl, preferred_element_type=F32)
        b = bb[:, :HGRN_EXPAND] + bb[:, HGRN_EXPAND:]
        dl = jnp.dot(low_ref[...], hl, preferred_element_type=F32)
        dl = dl[:, :HGRN_EXPAND] + dl[:, HGRN_EXPAND:]
        yield

        def level(d):
            e = jnp.exp2(d).astype(BF16)
            return _dot_nt(qb * e, kb * e)

        a = msk_ref[n_levels] * _dot_nt(qb, kb)
        for i in range(n_levels):
            m = c >> (i + 1)
            if m >= 8:
                parts = []
                for r0 in range(0, c, 2 * m):
                    mid = b[r0 + m - 1:r0 + m, :]
                    parts += [mid - b[r0:r0 + m, :], b[r0 + m:r0 + 2 * m, :] - mid]
                d = jnp.concatenate(parts, axis=0)
            elif m > 1:
                d = dl[(i - len(coarse)) * c:(i - len(coarse) + 1) * c]
            else:
                d = l2 * odd_ref[...]
            a = a + msk_ref[i] * level(d)
            if i % 3 == 2:
                yield
        b_last = b[c - 1:c, :]
        s_t = st[...]
        o = _dot_nt(qb * jnp.exp2(b).astype(BF16), s_t.astype(BF16)) + jnp.dot(a.astype(BF16), vb, preferred_element_type=F32)
        st[...] = jnp.exp2(b_last) * s_t + _dot_tn(vb, kb * jnp.exp2(b_last - b).astype(BF16))
        yield
        g = g_ref[rs, :]
        o_ref[rs, :] = (_rms(o, gn) * (g * jax.nn.sigmoid(g))).astype(o_ref.dtype)

    def group(gi, carry):
        _round_robin([chunk(gi * HGRN_UNROLL + u) for u in range(HGRN_UNROLL)])
        return carry

    _loop(q_ref.shape[0] // (c * HGRN_UNROLL), group)


def _hgrn(rq, rf, ri, rg, lb, gn, consts, bsz, seq):
    n, width = rq.shape
    heads = width // HGRN_EXPAND
    ts = HGRN_TILE
    nt = seq // ts
    blk = pl.BlockSpec((ts, HGRN_EXPAND), lambda b, h, t: (b * nt + t, h))
    vec = pl.BlockSpec((1, HGRN_EXPAND), lambda b, h, t: (0, h))
    const = lambda a: pl.BlockSpec(a.shape, lambda b, h, t: (0,) * a.ndim)
    return pl.pallas_call(
        _hgrn_kernel,
        grid=(bsz, heads, nt),
        in_specs=[blk, blk, blk, blk, vec, vec] + [const(a) for a in consts],
        out_specs=blk,
        out_shape=jax.ShapeDtypeStruct((n, width), BF16),
        scratch_shapes=[pltpu.VMEM((HGRN_EXPAND, HGRN_EXPAND), F32)],
        compiler_params=_params(("parallel", "parallel", "arbitrary")),
        name="hgrn2",
    )(rq, rf, ri, rg, lb.reshape(1, width), gn.reshape(1, width), *consts)


def _route_math(lg):
    row = lax.broadcasted_iota(jnp.int32, (EXPERTS_PER_GROUP, lg.shape[1]), 0).astype(F32)
    none = float(EXPERTS_PER_GROUP)

    def first_argmax(vals):
        mx = jnp.max(vals, axis=0, keepdims=True)
        return mx, jnp.min(jnp.where(vals == mx, row, none), axis=0, keepdims=True)

    g = lg[0:EXPERTS_PER_GROUP, :]
    is_group = row < N_GROUPS
    gmax, gidx = first_argmax(jnp.where(is_group, g, -jnp.inf))
    gprob = 1.0 / jnp.sum(jnp.where(is_group, jnp.exp(g - gmax), 0.0), axis=0, keepdims=True)
    el = lg[EXPERTS_PER_GROUP:2 * EXPERTS_PER_GROUP, :]
    for k in range(1, N_GROUPS):
        el = jnp.where(gidx == k, lg[(k + 1) * EXPERTS_PER_GROUP:(k + 2) * EXPERTS_PER_GROUP, :], el)
    t1, i1 = first_argmax(el)
    t2, i2 = first_argmax(jnp.where(row == i1, -jnp.inf, el))
    ex = jnp.exp(t2 - t1)
    base = gidx * EXPERTS_PER_GROUP
    eids = jnp.concatenate([base + i1, base + i2], axis=0).astype(jnp.int32)
    gates = jnp.concatenate([gprob / (1.0 + ex), gprob * ex / (1.0 + ex)], axis=0)
    return eids, gates


def _out_kernel(a_ref, r_ref, x_ref, mod_ref, ag_ref, g2_ref, wo_ref, wr_ref, br_ref, xo_ref, ha_ref, hb_ref, e_ref, gt_ref, cnt_ref, wb):
    @pl.when(pl.program_id(0) == 0)
    def _():
        cnt_ref[...] = jnp.zeros_like(cnt_ref)
        wb[...] = wo_ref[0].astype(BF16)

    mod = mod_ref[0]
    tm = x_ref.shape[0]
    logits = [None] * OUT_GROUPS

    def part(gi):
        rs = slice(gi * (tm // OUT_GROUPS), (gi + 1) * (tm // OUT_GROUPS))
        mix = jnp.concatenate([_rms(a_ref[rs, :].astype(F32), ag_ref[...]).astype(BF16), r_ref[rs, :]], axis=1)
        yield
        xn = x_ref[rs, :] + mod[2:3, :] * jnp.dot(mix, wb[...], preferred_element_type=F32)
        xo_ref[rs, :] = xn
        yield
        h2 = _rms(xn, g2_ref[...]) * (1.0 + mod[4:5, :]) + mod[3:4, :]
        ha_ref[rs, :], hb_ref[rs, :] = _pack_rows(h2)
        logits[gi] = _dot_nt(wr_ref[...], h2.astype(BF16))

    _round_robin([part(gi) for gi in range(OUT_GROUPS)])
    eids, gates = _route_math(jnp.concatenate(logits, axis=1) + br_ref[...])
    e_ref[...] = eids
    eio = lax.broadcasted_iota(jnp.int32, (N_EXPERTS, eids.shape[1]), 0)
    hit = jnp.logical_or(eids[0:1, :] == eio, eids[1:2, :] == eio).astype(F32)

    cnt_ref[...] += jnp.sum(hit, axis=1, keepdims=True)
    pad = jnp.zeros((LANES - TOP_K, LANES), F32)
    for k in range(tm // LANES):
        tile = jnp.concatenate([gates[:, k * LANES:(k + 1) * LANES], pad], axis=0)
        gt_ref[k * LANES:(k + 1) * LANES, :] = tile.T


def _out_proj(attn, rec, x2, mod_l, ag, g2, w_out, layer, wr_bf, br, seq):
    n, d = x2.shape
    half = attn.shape[1]
    tm = ROW_TILE
    per_batch = seq // tm
    row = lambda w: pl.BlockSpec((tm, w), lambda i: (i, 0))
    full = lambda a: pl.BlockSpec(a.shape, lambda i: (0,) * a.ndim, pipeline_mode=pl.Buffered(1))
    ag2, g22 = ag.reshape(1, half), g2.reshape(1, d)
    return pl.pallas_call(
        _out_kernel,
        grid=(n // tm,),
        in_specs=[row(half), row(half), row(d), pl.BlockSpec((1, 6, d), lambda i: (i // per_batch, 0, 0)),
                  full(ag2), full(g22), pl.BlockSpec((1, d, d), lambda i: (layer, 0, 0), pipeline_mode=pl.Buffered(1)),
                  full(wr_bf), full(br)],
        out_specs=[row(d), row(PACK_WIDTH), row(PACK_WIDTH), pl.BlockSpec((TOP_K, tm), lambda i: (0, i)), row(LANES),
                   pl.BlockSpec((N_EXPERTS, LANES), lambda i: (0, 0))],
        out_shape=[jax.ShapeDtypeStruct((n, d), F32), jax.ShapeDtypeStruct((n, PACK_WIDTH), U32),
                   jax.ShapeDtypeStruct((n, PACK_WIDTH), U32), jax.ShapeDtypeStruct((TOP_K, n), jnp.int32),
                   jax.ShapeDtypeStruct((n, LANES), F32), jax.ShapeDtypeStruct((N_EXPERTS, LANES), F32)],
        scratch_shapes=[pltpu.VMEM((d, d), BF16)],
        compiler_params=_params(("arbitrary",)),
        name="out_proj",
    )(attn, rec, x2, mod_l, ag2, g22, w_out, wr_bf, br)


def _plan_kernel(e_ref, cnt_ref, ut_ref, pos_ref, be_ref, bv_ref, bf_ref, bn_ref, nu_ref, off, run):
    i = pl.program_id(0)
    t = e_ref.shape[1]
    eio = lax.broadcasted_iota(jnp.int32, (N_EXPERTS, t), 0)
    oh1 = e_ref[0:1, :] == eio
    oh2 = e_ref[1:2, :] == eio
    oh = jnp.logical_or(oh1, oh2).astype(F32)

    @pl.when(i == 0)
    def _():
        counts = cnt_ref[...]
        nb = jnp.floor((counts + (MOE_TILE - 1)) * (1.0 / MOE_TILE))
        er = lax.broadcasted_iota(jnp.int32, (N_EXPERTS, N_EXPERTS), 0)
        ec = lax.broadcasted_iota(jnp.int32, (N_EXPERTS, N_EXPERTS), 1)
        cum_excl = jnp.dot((ec < er).astype(BF16), nb.astype(BF16), preferred_element_type=F32)
        off[...] = cum_excl * MOE_TILE
        run[...] = jnp.zeros_like(run)
        ce, ci, cn = cum_excl[:, 0:1], (cum_excl + nb)[:, 0:1], counts[:, 0:1]
        b = lax.broadcasted_iota(jnp.int32, (N_EXPERTS, be_ref.shape[1]), 1).astype(F32)
        member = jnp.logical_and(b >= ce, b < ci)
        be = jnp.minimum(jnp.sum((b >= ci).astype(F32), axis=0, keepdims=True), N_EXPERTS - 1.0)
        bv = jnp.sum(jnp.where(member, jnp.minimum(cn - (b - ce) * MOE_TILE, float(MOE_TILE)), 0.0), axis=0, keepdims=True)
        be_ref[...] = be.astype(jnp.int32)
        bv_ref[...] = bv.astype(jnp.int32)
        has = nb[:, 0:1] > 0.0
        eid = lax.broadcasted_iota(jnp.int32, b.shape, 0).astype(F32)
        bf_ref[...] = jnp.sum(jnp.where(jnp.logical_and(b == ce, has), 1.0, 0.0), axis=0, keepdims=True).astype(jnp.int32)
        nxt = jnp.min(jnp.where(jnp.logical_and(eid > be, has), eid, float(N_EXPERTS)), axis=0, keepdims=True)
        bn_ref[...] = jnp.where(nxt < N_EXPERTS, nxt, -1.0).astype(jnp.int32)
        nu_ref[...] = jnp.broadcast_to(ci[N_EXPERTS - 1:N_EXPERTS, :], nu_ref.shape).astype(jnp.int32)

    span = ut_ref.shape[0]
    ohb = oh.astype(BF16)
    base = off[:, 0:1] + run[:, 0:1] - 1.0
    parts = []
    for s in range(0, t, span):
        c = jnp.dot(ohb[:, s:s + span], ut_ref[...], preferred_element_type=F32)
        parts.append(base + c)
        base = base + c[:, span - 1:span]
    r = jnp.concatenate(parts, axis=1)
    pos_ref[0:1, :] = jnp.sum(jnp.where(oh1, r, 0.0), axis=0, keepdims=True).astype(jnp.int32)
    pos_ref[1:2, :] = jnp.sum(jnp.where(oh2, r, 0.0), axis=0, keepdims=True).astype(jnp.int32)
    run[...] += jnp.sum(oh, axis=1, keepdims=True)


def _plan(expert_ids_t, counts, n_rows):
    n = expert_ids_t.shape[1]
    t = PLAN_TILE
    n_blocks = n_rows // MOE_TILE
    assert n_blocks <= PLAN_LANES and TOP_K == 2
    ut = jnp.asarray(np.triu(np.ones((PLAN_SPAN, PLAN_SPAN), np.float32)), BF16)
    tab = jax.ShapeDtypeStruct((1, PLAN_LANES), jnp.int32)
    const = lambda shape: pl.BlockSpec(shape, lambda i: (0, 0))
    pos, be, bv, bf, bn, nu = pl.pallas_call(
        _plan_kernel,
        grid=(n // t,),
        in_specs=[pl.BlockSpec((TOP_K, t), lambda i: (0, i)), const((N_EXPERTS, LANES)), const((PLAN_SPAN, PLAN_SPAN))],
        out_specs=[pl.BlockSpec((TOP_K, t), lambda i: (0, i))] + [const((1, PLAN_LANES))] * 4 + [const((1, LANES))],
        out_shape=[jax.ShapeDtypeStruct((TOP_K, n), jnp.int32), tab, tab, tab, tab, jax.ShapeDtypeStruct((1, LANES), jnp.int32)],
        scratch_shapes=[pltpu.VMEM((N_EXPERTS, LANES), F32)] * 2,
        compiler_params=_params(("arbitrary",)),
        name="plan",
    )(expert_ids_t, counts, ut)
    return pos, (be.reshape(-1), bv.reshape(-1), bf.reshape(-1), bn.reshape(-1), nu.reshape(-1)[:1])


def _sc_mesh():
    return plsc.VectorSubcoreMesh(core_axis_name="c", subcore_axis_name="s")


def _sc_scatter_rows(x, pos, n_rows):
    n, d = x.shape
    w = SC_WINDOW

    @functools.partial(pl.kernel, out_type=jax.ShapeDtypeStruct((n_rows, d), x.dtype), mesh=_sc_mesh(), scratch_types=[])
    def scatter(x_hbm, p_hbm, o_hbm):
        def body(x_vmem, i_vmem):
            for k in range(TOP_K):
                pltpu.sync_copy(x_vmem, o_hbm.at[i_vmem.at[k]])

        pltpu.emit_pipeline(
            body,
            grid=(n // w,),
            in_specs=[pl.BlockSpec((w, d), lambda i: (i, 0)), pl.BlockSpec((TOP_K, w), lambda i: (0, i))],
            out_specs=[],
            core_axis_name=("c", "s"),
            dimension_semantics=(pltpu.PARALLEL,),
        )(x_hbm, p_hbm)

    return scatter(x, pos)


def _sc_gather_rows(y, pos):
    m = pos.shape[0] * pos.shape[1]
    d = y.shape[1]
    w = SC_WINDOW

    @functools.partial(pl.kernel, out_type=jax.ShapeDtypeStruct((m, d), y.dtype), mesh=_sc_mesh(), scratch_types=[])
    def gather(y_hbm, p_hbm, o_hbm):
        def body(i_vmem, o_vmem):
            pltpu.sync_copy(y_hbm.at[i_vmem.at[0]], o_vmem)

        pltpu.emit_pipeline(
            body,
            grid=(m // w,),
            in_specs=[pl.BlockSpec((1, w), lambda i: (0, i))],
            out_specs=[pl.BlockSpec((w, d), lambda i: (i, 0))],
            core_axis_name=("c", "s"),
            dimension_semantics=(pltpu.PARALLEL,),
        )(p_hbm, o_hbm)

    return gather(y, pos.reshape(1, m))


def _expert_kernel(be_ref, bv_ref, bf_ref, bn_ref, nu_ref, xa_ref, xb_ref, w1_hbm, w3_hbm, w2_hbm, ya_ref, yb_ref,
                   wf1, wf3, wf2, w1b, w3b, w2b, wsem, par, *, layer):
    i = pl.program_id(0)
    tb = xa_ref.shape[0]
    half = tb // 2

    def fetch(e, slot):
        return (pltpu.make_async_copy(w1_hbm.at[layer, e], wf1.at[slot], wsem.at[slot, 0]),
                pltpu.make_async_copy(w3_hbm.at[layer, e], wf3.at[slot], wsem.at[slot, 1]),
                pltpu.make_async_copy(w2_hbm.at[layer, e], wf2.at[slot], wsem.at[slot, 2]))

    def ffn(rows):
        live = lax.broadcasted_iota(jnp.int32, (rows, 1), 0) < bv_ref[i]
        x = jnp.where(live, _unpack_rows(xa_ref[0:rows, :], xb_ref[0:rows, :]), 0.0).astype(BF16)
        y = None
        hc = w1b.shape[1] // FFN_SPLIT
        for c in range(FFN_SPLIT):
            cs = slice(c * hc, (c + 1) * hc)
            a = jnp.dot(x, w1b[:, cs], preferred_element_type=F32)
            b = jnp.dot(x, w3b[:, cs], preferred_element_type=F32)
            hm = (a * jax.nn.sigmoid(a) * b).astype(BF16)
            part = jnp.dot(hm, w2b[cs, :], preferred_element_type=F32)
            y = part if y is None else y + part
        ya_ref[0:rows, :], yb_ref[0:rows, :] = _pack_rows(y)

    @pl.when(i == 0)
    def _():
        par[0] = 0
        for c in fetch(be_ref[0], 0):
            c.start()

    @pl.when(i < nu_ref[0])
    def _():
        @pl.when(bf_ref[i] == 1)
        def _():
            slot = par[0]
            for c in fetch(be_ref[i], slot):
                c.wait()
            w1b[...] = wf1[slot].astype(BF16)
            w3b[...] = wf3[slot].astype(BF16)
            w2b[...] = wf2[slot].astype(BF16)

            @pl.when(bn_ref[i] >= 0)
            def _():
                for c in fetch(bn_ref[i], 1 - slot):
                    c.start()

            par[0] = 1 - slot

        @pl.when(bv_ref[i] > half)
        def _():
            ffn(tb)

        @pl.when(bv_ref[i] <= half)
        def _():
            ffn(half)
            ya_ref[half:, :] = jnp.zeros((tb - half, ya_ref.shape[1]), ya_ref.dtype)
            yb_ref[half:, :] = jnp.zeros((tb - half, yb_ref.shape[1]), yb_ref.dtype)

    @pl.when(i >= nu_ref[0])
    def _():
        ya_ref[...] = jnp.zeros_like(ya_ref)
        yb_ref[...] = jnp.zeros_like(yb_ref)


def _experts(xa, xb, w1, w3, w2, layer, tables):
    n_rows = xa.shape[0]
    d, de = w1.shape[-2:]
    tb = MOE_TILE
    xspec = pl.BlockSpec((tb, PACK_WIDTH), lambda i, be, bv, bf, bn, nu: (jnp.minimum(i, nu[0] - 1), 0))
    yspec = pl.BlockSpec((tb, PACK_WIDTH), lambda i, be, bv, bf, bn, nu: (i, 0))
    anyspec = pl.BlockSpec(memory_space=pl.ANY)
    grid_spec = pltpu.PrefetchScalarGridSpec(
        num_scalar_prefetch=5,
        grid=(n_rows // tb,),
        in_specs=[xspec, xspec, anyspec, anyspec, anyspec],
        out_specs=[yspec, yspec],
        scratch_shapes=[pltpu.VMEM((2, d, de), F32), pltpu.VMEM((2, d, de), F32), pltpu.VMEM((2, de, d), F32),
                        pltpu.VMEM((d, de), BF16), pltpu.VMEM((d, de), BF16), pltpu.VMEM((de, d), BF16),
                        pltpu.SemaphoreType.DMA((2, 3)), pltpu.SMEM((1,), jnp.int32)],
    )
    rows = jax.ShapeDtypeStruct((n_rows, PACK_WIDTH), U32)
    return pl.pallas_call(
        functools.partial(_expert_kernel, layer=layer),
        grid_spec=grid_spec,
        out_shape=[rows, rows],
        compiler_params=_params(("arbitrary",)),
        name="experts",
    )(*tables, xa, xb, w1, w3, w2)


def _final_kernel(ga0, ga1, gb0, gb1, gt_ref, mod_ref, x_ref, fg_ref, o_ref):
    o_ref[...] = _rms(_moe_residual(x_ref, ga0, ga1, gb0, gb1, gt_ref, mod_ref[0][5:6, :]), fg_ref[...])


def _final(x2, ga, gb, gates, mod_l, final_g, seq):
    n, d = x2.shape
    tm = ROW_TILE
    per_batch = seq // tm
    row = lambda w: pl.BlockSpec((tm, w), lambda i: (i, 0))
    second = pl.BlockSpec((tm, PACK_WIDTH), lambda i: (i + n // tm, 0))
    return pl.pallas_call(
        _final_kernel,
        grid=(n // tm,),
        in_specs=[row(PACK_WIDTH), second, row(PACK_WIDTH), second, row(LANES),
                  pl.BlockSpec((1, 6, d), lambda i: (i // per_batch, 0, 0)), row(d), pl.BlockSpec((1, d), lambda i: (0, 0))],
        out_specs=row(d),
        out_shape=jax.ShapeDtypeStruct((n, d), F32),
        compiler_params=_params(("parallel",)),
        name="final_norm",
    )(ga, ga, gb, gb, gates, mod_l, x2, final_g.reshape(1, d))


def kernel(x, c, w_ada, b_ada, norm1_g, w_in, attn_norm_g, hgrn_lb_logits, hgrn_norm_g, w_out, norm2_g, router_group_w, router_group_b, router_expert_w, router_expert_b, moe_w1, moe_w3, moe_w2, final_g):
    bsz, seq, d = x.shape
    depth = w_ada.shape[0]
    n = bsz * seq
    attn_dim = attn_norm_g.shape[1]
    n_heads = attn_dim // ATTN_HEAD_DIM

    lb_w = jax.nn.softmax(hgrn_lb_logits.astype(F32), axis=0)
    lower_bounds = jnp.cumsum(lb_w, axis=0) - lb_w[0:1]
    mod = _adaln(c, w_ada, b_ada)
    bias = _attn_bias(n_heads)
    hconsts = _hgrn_consts()
    n_rows = n * TOP_K + N_EXPERTS * MOE_TILE

    x2 = x.reshape(n, d)
    moe = None
    for layer in range(depth):
        x2, (aq, ak, av, rq, rf, ri, rg) = _in_proj(x2, mod[layer], norm1_g[layer], w_in, layer, seq, moe)
        attn = _attention(aq, ak, av, bias, bsz, seq)
        rec = _hgrn(rq, rf, ri, rg, lower_bounds[layer], hgrn_norm_g[layer], hconsts, bsz, seq)
        gpad = EXPERTS_PER_GROUP - N_GROUPS
        wr = jnp.concatenate([router_group_w[layer].T, jnp.zeros((gpad, d), F32), router_expert_w[layer].T], axis=0)
        br = jnp.concatenate([router_group_b[layer], jnp.zeros((gpad,), F32), router_expert_b[layer]]).reshape(-1, 1)
        x2, ha, hb, eids, gates, counts = _out_proj(attn, rec, x2, mod[layer], attn_norm_g[layer], norm2_g[layer],
                                            w_out, layer, wr.astype(BF16), br, seq)
        pos, tables = _plan(eids, counts, n_rows)
        ya, yb = _experts(_sc_scatter_rows(ha, pos, n_rows), _sc_scatter_rows(hb, pos, n_rows),
                          moe_w1, moe_w3, moe_w2, layer, tables)
        moe = (_sc_gather_rows(ya, pos), _sc_gather_rows(yb, pos), gates, mod[layer])
    return _final(x2, moe[0], moe[1], moe[2], moe[3], final_g, seq).reshape(bsz, seq, d)
```

```python
import functools

import numpy as np
import jax
import jax.numpy as jnp
from jax import lax
from jax.experimental import pallas as pl
from jax.experimental.pallas import tpu as pltpu
from jax.experimental.pallas import tpu_sc as plsc

F32 = jnp.float32
BF16 = jnp.bfloat16
U32 = jnp.uint32

ATTN_HEAD_DIM = 64
ATTN_BLOCK = 128
DILATIONS = (1, 4, 16)
ALIBI_MAX_EXP = 8.0
HGRN_EXPAND = 128
HGRN_CHUNK = 128
N_GROUPS = 4
EXPERTS_PER_GROUP = 8
N_EXPERTS = N_GROUPS * EXPERTS_PER_GROUP
TOP_K = 2
NORM_EPS = 1e-6
LOG2E = 1.4426950408889634

LANES = 128
ATTN_SUPER = ATTN_BLOCK * max(DILATIONS)
ROW_TILE = 512
OUT_TILE = 1024
IN_GROUPS = 16
ATTN_UNROLL = 16
HGRN_TILE = 2048
HGRN_UNROLL = 16
MOE_TILE = 512
FFN_SPLIT = 2
PLAN_TILE = 4096
PLAN_SPAN = 256
PLAN_LANES = 256
PACK_WIDTH = 256
SC_WINDOW = 128
VMEM_LIMIT = 56 << 20


def _params(semantics):
    return pltpu.CompilerParams(dimension_semantics=semantics, vmem_limit_bytes=VMEM_LIMIT)


def _rms(x, g):
    return x * lax.rsqrt(jnp.mean(x * x, axis=-1, keepdims=True) + NORM_EPS) * g


def _loop(trips, body):
    if trips == 1:
        body(0, 0)
    else:
        lax.fori_loop(0, trips, body, 0)


def _round_robin(gens):
    live = list(gens)
    while live:
        for gen in list(live):
            try:
                next(gen)
            except StopIteration:
                live.remove(gen)


def _dot_nt(a, b):
    return lax.dot_general(a, b, (((1,), (1,)), ((), ())), preferred_element_type=F32)


def _dot_tn(a, b):
    return lax.dot_general(a, b, (((0,), (0,)), ((), ())), preferred_element_type=F32)


def _pack_pair(hi, lo):
    hb = lax.bitcast_convert_type(hi.astype(BF16).astype(F32), U32)
    lb = lax.bitcast_convert_type(lo.astype(BF16).astype(F32), U32)
    return hb | (lb >> 16)


def _unpack_pair(u):
    hi = lax.bitcast_convert_type(u & jnp.uint32(0xFFFF0000), F32)
    lo = lax.bitcast_convert_type(u << 16, F32)
    return hi, lo


def _pack_rows(y):
    w = PACK_WIDTH
    return _pack_pair(y[:, 0:w], y[:, 2 * w:3 * w]), _pack_pair(y[:, w:2 * w], y[:, 3 * w:4 * w])


def _unpack_rows(ua, ub):
    ha, la = _unpack_pair(ua)
    hb, lb = _unpack_pair(ub)
    return jnp.concatenate([ha, hb, la, lb], axis=1)


def _moe_residual(x_ref, ga0, ga1, gb0, gb1, gt_ref, gate_row):
    gt = gt_ref[...]
    y = gt[:, 0:1] * _unpack_rows(ga0[...], gb0[...]) + gt[:, 1:2] * _unpack_rows(ga1[...], gb1[...])
    return x_ref[...] + gate_row * y


def _ada_kernel(c_ref, w_ref, b_ref, o_ref):
    c = c_ref[...]
    ca = c * jax.nn.sigmoid(c)
    hi = ca.astype(BF16)
    lo = (ca - hi.astype(F32)).astype(BF16)
    w = w_ref[0].astype(BF16)
    o_ref[0] = (jnp.dot(hi, w, preferred_element_type=F32) + jnp.dot(lo, w, preferred_element_type=F32)) + b_ref[0]


def _adaln(c, w_ada, b_ada):
    depth, d, n6 = w_ada.shape
    bsz = c.shape[0]
    rows = 8
    cp = jnp.zeros((rows, d), F32).at[:bsz].set(c)
    tn = 1536
    out = pl.pallas_call(
        _ada_kernel,
        grid=(depth, n6 // tn),
        in_specs=[
            pl.BlockSpec((rows, d), lambda l, j: (0, 0)),
            pl.BlockSpec((1, d, tn), lambda l, j: (l, 0, j)),
            pl.BlockSpec((1, 1, tn), lambda l, j: (l, 0, j)),
        ],
        out_specs=pl.BlockSpec((1, rows, tn), lambda l, j: (l, 0, j)),
        out_shape=jax.ShapeDtypeStruct((depth, rows, n6), F32),
        compiler_params=_params(("parallel", "parallel")),
        name="adaln",
    )(cp, w_ada, b_ada.reshape(depth, 1, n6))
    return out[:, :bsz].reshape(depth, bsz, 6, d)


def _in_kernel(*refs, n_moe):
    moe, (x_ref, mod_ref, g_ref, w_ref), rest = refs[:n_moe], refs[n_moe:n_moe + 4], refs[n_moe + 4:]
    hb = rest[-1]
    i = pl.program_id(0)

    @pl.when(i == 0)
    def _():
        hb[1] = jnp.zeros(hb.shape[1:], hb.dtype)

    outs = rest[1:-1] if n_moe else rest[:-1]
    mod = mod_ref[0]
    prev = hb[(i + 1) % 2]
    width = outs[0].shape[1]
    tm = x_ref.shape[0]
    groups = [slice(r, r + tm // IN_GROUPS) for r in range(0, tm, tm // IN_GROUPS)]

    def prologue(rs):
        if n_moe:
            ga0, ga1, gb0, gb1, gt_ref, pmod_ref = moe
            gt = gt_ref[rs, :]
            y = gt[:, 0:1] * _unpack_rows(ga0[rs, :], gb0[rs, :]) + gt[:, 1:2] * _unpack_rows(ga1[rs, :], gb1[rs, :])
            x = x_ref[rs, :] + pmod_ref[0][5:6, :] * y
            rest[0][rs, :] = x
        else:
            x = x_ref[rs, :]
        hb[i % 2, rs, :] = (_rms(x, g_ref[...]) * (1.0 + mod[1:2, :]) + mod[0:1, :]).astype(BF16)

    per = -(-len(groups) // len(outs))
    for k, o in enumerate(outs):
        for rs in groups[k * per:(k + 1) * per]:
            prologue(rs)
        w = w_ref[0, :, k * width:(k + 1) * width].astype(BF16)
        o[...] = jnp.dot(prev, w, preferred_element_type=F32).astype(o.dtype)


def _in_proj(x2, mod_l, g, w_in, layer, seq, moe=None):
    n, d = x2.shape
    n_out = w_in.shape[2]
    width = 512
    tm = ROW_TILE
    per_batch = seq // tm
    last = n // tm - 1
    cur = lambda i: jnp.minimum(i, last)
    row = lambda w: pl.BlockSpec((tm, w), lambda i: (cur(i), 0))
    mod_spec = pl.BlockSpec((1, 6, d), lambda i: (cur(i) // per_batch, 0, 0))
    in_specs, args, out_specs, out_shape = [], [], [], []
    if moe is not None:
        ga, gb, gates, mod_prev = moe
        second = pl.BlockSpec((tm, PACK_WIDTH), lambda i: (cur(i) + n // tm, 0))
        in_specs += [row(PACK_WIDTH), second, row(PACK_WIDTH), second, row(LANES), mod_spec]
        args += [ga, ga, gb, gb, gates, mod_prev]
        out_specs.append(row(d))
        out_shape.append(jax.ShapeDtypeStruct((n, d), F32))
    in_specs += [row(d), mod_spec, pl.BlockSpec((1, d), lambda i: (0, 0)),
                 pl.BlockSpec((1, d, n_out), lambda i: (layer, 0, 0), pipeline_mode=pl.Buffered(1))]
    args += [x2, mod_l, g.reshape(1, d), w_in]
    out_dtypes = (F32, F32, F32, BF16, F32, BF16, F32)
    out_specs += [pl.BlockSpec((tm, width), lambda i: (jnp.maximum(i - 1, 0), 0))] * len(out_dtypes)
    out_shape += [jax.ShapeDtypeStruct((n, width), dt) for dt in out_dtypes]
    outs = pl.pallas_call(
        functools.partial(_in_kernel, n_moe=0 if moe is None else 6),
        grid=(n // tm + 1,),
        in_specs=in_specs,
        out_specs=out_specs,
        out_shape=out_shape,
        scratch_shapes=[pltpu.VMEM((2, tm, d), BF16)],
        compiler_params=_params(("arbitrary",)),
        name="in_proj",
    )(*args)
    return (x2, outs) if moe is None else (outs[0], outs[1:])


def _attn_bias(n_heads):
    qi = np.arange(ATTN_BLOCK)[:, None]
    kj = np.arange(2 * ATTN_BLOCK)[None, :]
    dist = ATTN_BLOCK + qi - kj
    valid = (dist >= 0) & (dist <= ATTN_BLOCK)
    slopes = np.exp2(-ALIBI_MAX_EXP * np.arange(1, n_heads + 1, dtype=np.float32) / n_heads)
    out = np.empty((2, len(DILATIONS), n_heads, ATTN_BLOCK, 2 * ATTN_BLOCK), np.float32)
    for p, dil in enumerate(DILATIONS):
        b = -slopes[:, None, None] * (dist * dil).astype(np.float32)[None] * np.float32(LOG2E)
        out[0, p] = np.where(valid[None], b, -np.inf)
        out[1, p] = np.where((valid & (kj >= ATTN_BLOCK))[None], b, -np.inf)
    return jnp.asarray(out)


def _attn_kernel(q_ref, k_ref, v_ref, bias_ref, o_ref, kcar, vcar, pbuf, vbuf, oscr, mscr, dscr):
    n = pl.program_id(2)
    sb = q_ref.shape[0]
    is_lo = lax.broadcasted_iota(jnp.int32, (ATTN_BLOCK, LANES), 1) < ATTN_HEAD_DIM
    scale = ATTN_HEAD_DIM ** -0.5 * LOG2E

    @pl.when(n == 0)
    def _():
        kcar[...] = jnp.zeros_like(kcar)
        vcar[...] = jnp.zeros_like(vcar)

    pbuf[...] = jnp.zeros_like(pbuf)
    vbuf[...] = jnp.zeros_like(vbuf)
    lo2 = lax.broadcasted_iota(jnp.int32, (2 * ATTN_BLOCK, LANES), 1) < ATTN_HEAD_DIM
    zero = jnp.zeros((2 * ATTN_BLOCK, LANES), BF16)
    den_lo = lo2.astype(BF16)
    den_hi = jnp.logical_not(lo2).astype(BF16)

    for p, dil in enumerate(DILATIONS):
        nb = (sb // ATTN_BLOCK) // dil

        def rows_of(it, dil=dil, nb=nb):
            start = (it % nb) * (ATTN_BLOCK * dil) + it // nb
            return pl.ds(start, ATTN_BLOCK, stride=dil) if dil > 1 else pl.ds(pl.multiple_of(start, ATTN_BLOCK), ATTN_BLOCK)

        def pv(it, probs, vaug, p=p, rows_of=rows_of):
            rows = rows_of(it)
            of = jnp.dot(probs, vaug, preferred_element_type=F32)
            oscr[p, rows, :] = of[:, :LANES]
            dscr[p, rows, :] = of[:, LANES:]

        def qk(it, p=p, nb=nb, rows_of=rows_of):
            r = it // nb
            rows = rows_of(it)
            crow = pl.ds(pl.multiple_of(r * ATTN_BLOCK, ATTN_BLOCK), ATTN_BLOCK)
            qb = q_ref[rows, :] * scale
            kc = k_ref[rows, :].astype(BF16)
            vc = v_ref[rows, :].astype(BF16)
            kcat = jnp.concatenate([kcar[p, crow, :], kc], axis=0)
            vcat = jnp.concatenate([vcar[p, crow, :], vc], axis=0)
            vaug = jnp.concatenate([jnp.concatenate([jnp.where(lo2, vcat, zero), den_lo], axis=1),
                                    jnp.concatenate([jnp.where(lo2, zero, vcat), den_hi], axis=1)], axis=0)
            kcar[p, crow, :] = kc
            vcar[p, crow, :] = vc
            first = jnp.logical_and(n == 0, it % nb == 0).astype(jnp.int32)
            probs, acc_m = [], None
            for hh in range(2):
                sel = is_lo if hh == 0 else jnp.logical_not(is_lo)
                qm = jnp.where(sel, qb, 0.0).astype(BF16)
                s = _dot_nt(qm, kcat) + bias_ref[first, p, hh]
                m = jnp.max(s, axis=-1, keepdims=True)
                probs.append(jnp.exp2(s - m).astype(BF16))
                mb = jnp.broadcast_to(m, (ATTN_BLOCK, LANES))
                acc_m = mb if hh == 0 else jnp.where(sel, mb, acc_m)
            mscr[p, rows, :] = acc_m
            return jnp.concatenate(probs, axis=1), vaug

        def body(g, carry, pv=pv, qk=qk):
            it0 = g * ATTN_UNROLL
            pend = (pbuf[...], vbuf[...])
            for u in range(ATTN_UNROLL):
                cur = qk(it0 + u)
                pv(jnp.maximum(it0 + u - 1, 0), *pend)
                pend = cur
            pbuf[...] = pend[0]
            vbuf[...] = pend[1]
            return carry

        n_it = sb // ATTN_BLOCK
        lax.fori_loop(0, n_it // ATTN_UNROLL, body, 0)
        pv(n_it - 1, pbuf[...], vbuf[...])

    ct = 256

    def combine(t, carry):
        rs = pl.ds(pl.multiple_of(t * ct, ct), ct)
        m0, m1, m2 = mscr[0, rs, :], mscr[1, rs, :], mscr[2, rs, :]
        mx = jnp.maximum(jnp.maximum(m0, m1), m2)
        w0, w1, w2 = jnp.exp2(m0 - mx), jnp.exp2(m1 - mx), jnp.exp2(m2 - mx)
        num = w0 * oscr[0, rs, :] + w1 * oscr[1, rs, :] + w2 * oscr[2, rs, :]
        den = w0 * dscr[0, rs, :] + w1 * dscr[1, rs, :] + w2 * dscr[2, rs, :]
        o_ref[rs, :] = (num / den).astype(o_ref.dtype)
        return carry

    lax.fori_loop(0, sb // ct, combine, 0)


def _attention(aq, ak, av, bias, bsz, seq):
    n, width = aq.shape
    sb = ATTN_SUPER
    nsb = seq // sb
    pairs = width // LANES
    npat = len(DILATIONS)
    blk = pl.BlockSpec((sb, LANES), lambda b, h, t: (b * nsb + t, h))
    return pl.pallas_call(
        _attn_kernel,
        grid=(bsz, pairs, nsb),
        in_specs=[blk, blk, blk,
                  pl.BlockSpec((2, npat, 2, ATTN_BLOCK, 2 * ATTN_BLOCK), lambda b, h, t: (0, 0, h, 0, 0))],
        out_specs=blk,
        out_shape=jax.ShapeDtypeStruct((n, width), BF16),
        scratch_shapes=[
            pltpu.VMEM((npat, max(DILATIONS) * ATTN_BLOCK, LANES), BF16),
            pltpu.VMEM((npat, max(DILATIONS) * ATTN_BLOCK, LANES), BF16),
            pltpu.VMEM((ATTN_BLOCK, 4 * ATTN_BLOCK), BF16),
            pltpu.VMEM((4 * ATTN_BLOCK, 2 * LANES), BF16),
            pltpu.VMEM((npat, sb, LANES), F32),
            pltpu.VMEM((npat, sb, LANES), F32),
            pltpu.VMEM((npat, sb, LANES), F32),
        ],
        compiler_params=_params(("parallel", "parallel", "arbitrary")),
        name="dilated_attn",
    )(aq, ak, av, bias)


def _hgrn_consts():
    c = HGRN_CHUNK
    t = np.arange(c)[:, None]
    u = np.arange(c)[None, :]
    low, masks = [], []
    m = c // 2
    while m >= 1:
        mid = (t // (2 * m)) * (2 * m) + m
        is_q = t >= mid
        if m in (4, 2):
            low.append((is_q & (u >= mid) & (u <= t)) | (~is_q & (u > t) & (u < mid)))
        masks.append(((t // (2 * m)) == (u // (2 * m))) & (t % (2 * m) >= m) & (u % (2 * m) < m))
        m //= 2
    masks.append(t == u)
    f = lambda a: np.asarray(a, np.float32)
    return (jnp.asarray(f(u <= t), BF16), jnp.asarray(f(np.concatenate(low, axis=0)), BF16),
            jnp.asarray(f(np.broadcast_to(t % 2 == 1, (c, c)))), jnp.asarray(f(np.stack(masks))))


def _hgrn_kernel(q_ref, z_ref, v_ref, g_ref, lb_ref, gn_ref, tri_ref, low_ref, odd_ref, msk_ref, o_ref, st):
    c = HGRN_CHUNK

    @pl.when(pl.program_id(2) == 0)
    def _():
        st[...] = jnp.zeros_like(st)

    lb = jnp.maximum(lb_ref[...], 0.0)
    log_lb = jnp.log(lb)
    log_1m = jnp.log1p(-lb)
    one_m = 1.0 - lb
    gn = gn_ref[...]
    n_levels = msk_ref.shape[0] - 1
    coarse = [c >> (i + 1) for i in range(n_levels) if (c >> (i + 1)) >= 8]

    def chunk(ci):
        rs = pl.ds(pl.multiple_of(ci * c, c), c)
        z = z_ref[rs, :]
        qb = q_ref[rs, :]
        vb = v_ref[rs, :]
        ez = jnp.exp(-jnp.abs(z))
        ls = jnp.minimum(z, 0.0) - jnp.log(1.0 + ez)
        b2 = log_1m + ls
        lf = jnp.maximum(log_lb, b2) + jnp.log(1.0 + jnp.exp(-jnp.abs(log_lb - b2)))
        rz = 1.0 / (1.0 + ez)
        kb = (one_m * jnp.where(z > 0.0, ez * rz, rz)).astype(BF16)
        l2 = lf * LOG2E
        hi = l2.astype(BF16)
        hl = jnp.concatenate([hi, (l2 - hi.astype(F32)).astype(BF16)], axis=1)
        yield
        bb = jnp.dot(tri_ref[...], hl, preferred_element_type=F32)
        b = bb[:, :HGRN_EXPAND] + bb[:, HGRN_EXPAND:]
        dl = jnp.dot(low_ref[...], hl, preferred_element_type=F32)
        dl = dl[:, :HGRN_EXPAND] + dl[:, HGRN_EXPAND:]
        yield

        def level(d):
            e = jnp.exp2(d).astype(BF16)
            return _dot_nt(qb * e, kb * e)

        a = msk_ref[n_levels] * _dot_nt(qb, kb)
        for i in range(n_levels):
            m = c >> (i + 1)
            if m >= 8:
                parts = []
                for r0 in range(0, c, 2 * m):
                    mid = b[r0 + m - 1:r0 + m, :]
                    parts += [mid - b[r0:r0 + m, :], b[r0 + m:r0 + 2 * m, :] - mid]
                d = jnp.concatenate(parts, axis=0)
            elif m > 1:
                d = dl[(i - len(coarse)) * c:(i - len(coarse) + 1) * c]
            else:
                d = l2 * odd_ref[...]
            a = a + msk_ref[i] * level(d)
            if i % 3 == 2:
                yield
        b_last = b[c - 1:c, :]
        s_t = st[...]
        o = _dot_nt(qb * jnp.exp2(b).astype(BF16), s_t.astype(BF16)) + jnp.dot(a.astype(BF16), vb, preferred_element_type=F32)
        st[...] = jnp.exp2(b_last) * s_t + _dot_tn(vb, kb * jnp.exp2(b_last - b).astype(BF16))
        yield
        g = g_ref[rs, :]
        o_ref[rs, :] = (_rms(o, gn) * (g * jax.nn.sigmoid(g))).astype(o_ref.dtype)

    def group(gi, carry):
        _round_robin([chunk(gi * HGRN_UNROLL + u) for u in range(HGRN_UNROLL)])
        return carry

    _loop(q_ref.shape[0] // (c * HGRN_UNROLL), group)


def _hgrn(rq, rf, ri, rg, lb, gn, consts, bsz, seq):
    n, width = rq.shape
    heads = width // HGRN_EXPAND
    ts = HGRN_TILE
    nt = seq // ts
    blk = pl.BlockSpec((ts, HGRN_EXPAND), lambda b, h, t: (b * nt + t, h))
    vec = pl.BlockSpec((1, HGRN_EXPAND), lambda b, h, t: (0, h))
    const = lambda a: pl.BlockSpec(a.shape, lambda b, h, t: (0,) * a.ndim)
    return pl.pallas_call(
        _hgrn_kernel,
        grid=(bsz, heads, nt),
        in_specs=[blk, blk, blk, blk, vec, vec] + [const(a) for a in consts],
        out_specs=blk,
        out_shape=jax.ShapeDtypeStruct((n, width), BF16),
        scratch_shapes=[pltpu.VMEM((HGRN_EXPAND, HGRN_EXPAND), F32)],
        compiler_params=_params(("parallel", "parallel", "arbitrary")),
        name="hgrn2",
    )(rq, rf, ri, rg, lb.reshape(1, width), gn.reshape(1, width), *consts)


def _route_math(lg):
    row = lax.broadcasted_iota(jnp.int32, (EXPERTS_PER_GROUP, lg.shape[1]), 0).astype(F32)
    none = float(EXPERTS_PER_GROUP)

    def first_argmax(vals):
        mx = jnp.max(vals, axis=0, keepdims=True)
        return mx, jnp.min(jnp.where(vals == mx, row, none), axis=0, keepdims=True)

    g = lg[0:EXPERTS_PER_GROUP, :]
    is_group = row < N_GROUPS
    gmax, gidx = first_argmax(jnp.where(is_group, g, -jnp.inf))
    gprob = 1.0 / jnp.sum(jnp.where(is_group, jnp.exp(g - gmax), 0.0), axis=0, keepdims=True)
    el = lg[EXPERTS_PER_GROUP:2 * EXPERTS_PER_GROUP, :]
    for k in range(1, N_GROUPS):
        el = jnp.where(gidx == k, lg[(k + 1) * EXPERTS_PER_GROUP:(k + 2) * EXPERTS_PER_GROUP, :], el)
    t1, i1 = first_argmax(el)
    t2, i2 = first_argmax(jnp.where(row == i1, -jnp.inf, el))
    ex = jnp.exp(t2 - t1)
    base = gidx * EXPERTS_PER_GROUP
    eids = jnp.concatenate([base + i1, base + i2], axis=0).astype(jnp.int32)
    gates = jnp.concatenate([gprob / (1.0 + ex), gprob * ex / (1.0 + ex)], axis=0)
    return eids, gates


def _out_kernel(a_ref, r_ref, x_ref, mod_ref, ag_ref, g2_ref, wo_ref, wr_ref, br_ref, xo_ref, ha_ref, hb_ref, e_ref, gt_ref, cnt_ref):
    mod = mod_ref[0]
    mix = jnp.concatenate([_rms(a_ref[...].astype(F32), ag_ref[...]).astype(BF16), r_ref[...]], axis=1)
    xn = x_ref[...] + mod[2:3, :] * jnp.dot(mix, wo_ref[0].astype(BF16), preferred_element_type=F32)
    xo_ref[...] = xn
    h2 = _rms(xn, g2_ref[...]) * (1.0 + mod[4:5, :]) + mod[3:4, :]
    ha_ref[...], hb_ref[...] = _pack_rows(h2)
    eids, gates = _route_math(_dot_nt(wr_ref[...], h2.astype(BF16)) + br_ref[...])
    e_ref[...] = eids
    eio = lax.broadcasted_iota(jnp.int32, (N_EXPERTS, eids.shape[1]), 0)
    hit = jnp.logical_or(eids[0:1, :] == eio, eids[1:2, :] == eio).astype(F32)

    @pl.when(pl.program_id(0) == 0)
    def _():
        cnt_ref[...] = jnp.zeros_like(cnt_ref)

    cnt_ref[...] += jnp.sum(hit, axis=1, keepdims=True)
    pad = jnp.zeros((LANES - TOP_K, LANES), F32)
    for k in range(h2.shape[0] // LANES):
        tile = jnp.concatenate([gates[:, k * LANES:(k + 1) * LANES], pad], axis=0)
        gt_ref[k * LANES:(k + 1) * LANES, :] = tile.T


def _out_proj(attn, rec, x2, mod_l, ag, g2, w_out, layer, wr_bf, br, seq):
    n, d = x2.shape
    half = attn.shape[1]
    tm = OUT_TILE
    per_batch = seq // tm
    row = lambda w: pl.BlockSpec((tm, w), lambda i: (i, 0))
    full = lambda a: pl.BlockSpec(a.shape, lambda i: (0,) * a.ndim, pipeline_mode=pl.Buffered(1))
    ag2, g22 = ag.reshape(1, half), g2.reshape(1, d)
    return pl.pallas_call(
        _out_kernel,
        grid=(n // tm,),
        in_specs=[row(half), row(half), row(d), pl.BlockSpec((1, 6, d), lambda i: (i // per_batch, 0, 0)),
                  full(ag2), full(g22), pl.BlockSpec((1, d, d), lambda i: (layer, 0, 0), pipeline_mode=pl.Buffered(1)),
                  full(wr_bf), full(br)],
        out_specs=[row(d), row(PACK_WIDTH), row(PACK_WIDTH), pl.BlockSpec((TOP_K, tm), lambda i: (0, i)), row(LANES),
                   pl.BlockSpec((N_EXPERTS, LANES), lambda i: (0, 0))],
        out_shape=[jax.ShapeDtypeStruct((n, d), F32), jax.ShapeDtypeStruct((n, PACK_WIDTH), U32),
                   jax.ShapeDtypeStruct((n, PACK_WIDTH), U32), jax.ShapeDtypeStruct((TOP_K, n), jnp.int32),
                   jax.ShapeDtypeStruct((n, LANES), F32), jax.ShapeDtypeStruct((N_EXPERTS, LANES), F32)],
        compiler_params=_params(("arbitrary",)),
        name="out_proj",
    )(attn, rec, x2, mod_l, ag2, g22, w_out, wr_bf, br)


def _plan_kernel(e_ref, cnt_ref, ut_ref, pos_ref, be_ref, bv_ref, bf_ref, bn_ref, nu_ref, off, run):
    i = pl.program_id(0)
    t = e_ref.shape[1]
    eio = lax.broadcasted_iota(jnp.int32, (N_EXPERTS, t), 0)
    oh1 = e_ref[0:1, :] == eio
    oh2 = e_ref[1:2, :] == eio
    oh = jnp.logical_or(oh1, oh2).astype(F32)

    @pl.when(i == 0)
    def _():
        counts = cnt_ref[...]
        nb = jnp.floor((counts + (MOE_TILE - 1)) * (1.0 / MOE_TILE))
        er = lax.broadcasted_iota(jnp.int32, (N_EXPERTS, N_EXPERTS), 0)
        ec = lax.broadcasted_iota(jnp.int32, (N_EXPERTS, N_EXPERTS), 1)
        cum_excl = jnp.dot((ec < er).astype(BF16), nb.astype(BF16), preferred_element_type=F32)
        off[...] = cum_excl * MOE_TILE
        run[...] = jnp.zeros_like(run)
        ce, ci, cn = cum_excl[:, 0:1], (cum_excl + nb)[:, 0:1], counts[:, 0:1]
        b = lax.broadcasted_iota(jnp.int32, (N_EXPERTS, be_ref.shape[1]), 1).astype(F32)
        member = jnp.logical_and(b >= ce, b < ci)
        be = jnp.minimum(jnp.sum((b >= ci).astype(F32), axis=0, keepdims=True), N_EXPERTS - 1.0)
        bv = jnp.sum(jnp.where(member, jnp.minimum(cn - (b - ce) * MOE_TILE, float(MOE_TILE)), 0.0), axis=0, keepdims=True)
        be_ref[...] = be.astype(jnp.int32)
        bv_ref[...] = bv.astype(jnp.int32)
        has = nb[:, 0:1] > 0.0
        eid = lax.broadcasted_iota(jnp.int32, b.shape, 0).astype(F32)
        bf_ref[...] = jnp.sum(jnp.where(jnp.logical_and(b == ce, has), 1.0, 0.0), axis=0, keepdims=True).astype(jnp.int32)
        nxt = jnp.min(jnp.where(jnp.logical_and(eid > be, has), eid, float(N_EXPERTS)), axis=0, keepdims=True)
        bn_ref[...] = jnp.where(nxt < N_EXPERTS, nxt, -1.0).astype(jnp.int32)
        nu_ref[...] = jnp.broadcast_to(ci[N_EXPERTS - 1:N_EXPERTS, :], nu_ref.shape).astype(jnp.int32)

    span = ut_ref.shape[0]
    ohb = oh.astype(BF16)
    base = off[:, 0:1] + run[:, 0:1] - 1.0
    parts = []
    for s in range(0, t, span):
        c = jnp.dot(ohb[:, s:s + span], ut_ref[...], preferred_element_type=F32)
        parts.append(base + c)
        base = base + c[:, span - 1:span]
    r = jnp.concatenate(parts, axis=1)
    pos_ref[0:1, :] = jnp.sum(jnp.where(oh1, r, 0.0), axis=0, keepdims=True).astype(jnp.int32)
    pos_ref[1:2, :] = jnp.sum(jnp.where(oh2, r, 0.0), axis=0, keepdims=True).astype(jnp.int32)
    run[...] += jnp.sum(oh, axis=1, keepdims=True)


def _plan(expert_ids_t, counts, n_rows):
    n = expert_ids_t.shape[1]
    t = PLAN_TILE
    n_blocks = n_rows // MOE_TILE
    assert n_blocks <= PLAN_LANES and TOP_K == 2
    ut = jnp.asarray(np.triu(np.ones((PLAN_SPAN, PLAN_SPAN), np.float32)), BF16)
    tab = jax.ShapeDtypeStruct((1, PLAN_LANES), jnp.int32)
    const = lambda shape: pl.BlockSpec(shape, lambda i: (0, 0))
    pos, be, bv, bf, bn, nu = pl.pallas_call(
        _plan_kernel,
        grid=(n // t,),
        in_specs=[pl.BlockSpec((TOP_K, t), lambda i: (0, i)), const((N_EXPERTS, LANES)), const((PLAN_SPAN, PLAN_SPAN))],
        out_specs=[pl.BlockSpec((TOP_K, t), lambda i: (0, i))] + [const((1, PLAN_LANES))] * 4 + [const((1, LANES))],
        out_shape=[jax.ShapeDtypeStruct((TOP_K, n), jnp.int32), tab, tab, tab, tab, jax.ShapeDtypeStruct((1, LANES), jnp.int32)],
        scratch_shapes=[pltpu.VMEM((N_EXPERTS, LANES), F32)] * 2,
        compiler_params=_params(("arbitrary",)),
        name="plan",
    )(expert_ids_t, counts, ut)
    return pos, (be.reshape(-1), bv.reshape(-1), bf.reshape(-1), bn.reshape(-1), nu.reshape(-1)[:1])


def _sc_mesh():
    return plsc.VectorSubcoreMesh(core_axis_name="c", subcore_axis_name="s")


def _sc_scatter_rows(x, pos, n_rows):
    n, d = x.shape
    w = SC_WINDOW

    @functools.partial(pl.kernel, out_type=jax.ShapeDtypeStruct((n_rows, d), x.dtype), mesh=_sc_mesh(), scratch_types=[])
    def scatter(x_hbm, p_hbm, o_hbm):
        def body(x_vmem, i_vmem):
            for k in range(TOP_K):
                pltpu.sync_copy(x_vmem, o_hbm.at[i_vmem.at[k]])

        pltpu.emit_pipeline(
            body,
            grid=(n // w,),
            in_specs=[pl.BlockSpec((w, d), lambda i: (i, 0)), pl.BlockSpec((TOP_K, w), lambda i: (0, i))],
            out_specs=[],
            core_axis_name=("c", "s"),
            dimension_semantics=(pltpu.PARALLEL,),
        )(x_hbm, p_hbm)

    return scatter(x, pos)


def _sc_gather_rows(y, pos):
    m = pos.shape[0] * pos.shape[1]
    d = y.shape[1]
    w = SC_WINDOW

    @functools.partial(pl.kernel, out_type=jax.ShapeDtypeStruct((m, d), y.dtype), mesh=_sc_mesh(), scratch_types=[])
    def gather(y_hbm, p_hbm, o_hbm):
        def body(i_vmem, o_vmem):
            pltpu.sync_copy(y_hbm.at[i_vmem.at[0]], o_vmem)

        pltpu.emit_pipeline(
            body,
            grid=(m // w,),
            in_specs=[pl.BlockSpec((1, w), lambda i: (0, i))],
            out_specs=[pl.BlockSpec((w, d), lambda i: (i, 0))],
            core_axis_name=("c", "s"),
            dimension_semantics=(pltpu.PARALLEL,),
        )(p_hbm, o_hbm)

    return gather(y, pos.reshape(1, m))


def _expert_kernel(be_ref, bv_ref, bf_ref, bn_ref, nu_ref, xa_ref, xb_ref, w1_hbm, w3_hbm, w2_hbm, ya_ref, yb_ref,
                   wf1, wf3, wf2, w1b, w3b, w2b, wsem, par, *, layer):
    i = pl.program_id(0)
    tb = xa_ref.shape[0]
    half = tb // 2

    def fetch(e, slot):
        return (pltpu.make_async_copy(w1_hbm.at[layer, e], wf1.at[slot], wsem.at[slot, 0]),
                pltpu.make_async_copy(w3_hbm.at[layer, e], wf3.at[slot], wsem.at[slot, 1]),
                pltpu.make_async_copy(w2_hbm.at[layer, e], wf2.at[slot], wsem.at[slot, 2]))

    def ffn(rows):
        live = lax.broadcasted_iota(jnp.int32, (rows, 1), 0) < bv_ref[i]
        x = jnp.where(live, _unpack_rows(xa_ref[0:rows, :], xb_ref[0:rows, :]), 0.0).astype(BF16)
        y = None
        hc = w1b.shape[1] // FFN_SPLIT
        for c in range(FFN_SPLIT):
            cs = slice(c * hc, (c + 1) * hc)
            a = jnp.dot(x, w1b[:, cs], preferred_element_type=F32)
            b = jnp.dot(x, w3b[:, cs], preferred_element_type=F32)
            hm = (a * jax.nn.sigmoid(a) * b).astype(BF16)
            part = jnp.dot(hm, w2b[cs, :], preferred_element_type=F32)
            y = part if y is None else y + part
        ya_ref[0:rows, :], yb_ref[0:rows, :] = _pack_rows(y)

    @pl.when(i == 0)
    def _():
        par[0] = 0
        for c in fetch(be_ref[0], 0):
            c.start()

    @pl.when(i < nu_ref[0])
    def _():
        @pl.when(bf_ref[i] == 1)
        def _():
            slot = par[0]
            for c in fetch(be_ref[i], slot):
                c.wait()
            w1b[...] = wf1[slot].astype(BF16)
            w3b[...] = wf3[slot].astype(BF16)
            w2b[...] = wf2[slot].astype(BF16)

            @pl.when(bn_ref[i] >= 0)
            def _():
                for c in fetch(bn_ref[i], 1 - slot):
                    c.start()

            par[0] = 1 - slot

        @pl.when(bv_ref[i] > half)
        def _():
            ffn(tb)

        @pl.when(bv_ref[i] <= half)
        def _():
            ffn(half)
            ya_ref[half:, :] = jnp.zeros((tb - half, ya_ref.shape[1]), ya_ref.dtype)
            yb_ref[half:, :] = jnp.zeros((tb - half, yb_ref.shape[1]), yb_ref.dtype)

    @pl.when(i >= nu_ref[0])
    def _():
        ya_ref[...] = jnp.zeros_like(ya_ref)
        yb_ref[...] = jnp.zeros_like(yb_ref)


def _experts(xa, xb, w1, w3, w2, layer, tables):
    n_rows = xa.shape[0]
    d, de = w1.shape[-2:]
    tb = MOE_TILE
    xspec = pl.BlockSpec((tb, PACK_WIDTH), lambda i, be, bv, bf, bn, nu: (jnp.minimum(i, nu[0] - 1), 0))
    yspec = pl.BlockSpec((tb, PACK_WIDTH), lambda i, be, bv, bf, bn, nu: (i, 0))
    anyspec = pl.BlockSpec(memory_space=pl.ANY)
    grid_spec = pltpu.PrefetchScalarGridSpec(
        num_scalar_prefetch=5,
        grid=(n_rows // tb,),
        in_specs=[xspec, xspec, anyspec, anyspec, anyspec],
        out_specs=[yspec, yspec],
        scratch_shapes=[pltpu.VMEM((2, d, de), F32), pltpu.VMEM((2, d, de), F32), pltpu.VMEM((2, de, d), F32),
                        pltpu.VMEM((d, de), BF16), pltpu.VMEM((d, de), BF16), pltpu.VMEM((de, d), BF16),
                        pltpu.SemaphoreType.DMA((2, 3)), pltpu.SMEM((1,), jnp.int32)],
    )
    rows = jax.ShapeDtypeStruct((n_rows, PACK_WIDTH), U32)
    return pl.pallas_call(
        functools.partial(_expert_kernel, layer=layer),
        grid_spec=grid_spec,
        out_shape=[rows, rows],
        compiler_params=_params(("arbitrary",)),
        name="experts",
    )(*tables, xa, xb, w1, w3, w2)


def _final_kernel(ga0, ga1, gb0, gb1, gt_ref, mod_ref, x_ref, fg_ref, o_ref):
    o_ref[...] = _rms(_moe_residual(x_ref, ga0, ga1, gb0, gb1, gt_ref, mod_ref[0][5:6, :]), fg_ref[...])


def _final(x2, ga, gb, gates, mod_l, final_g, seq):
    n, d = x2.shape
    tm = ROW_TILE
    per_batch = seq // tm
    row = lambda w: pl.BlockSpec((tm, w), lambda i: (i, 0))
    second = pl.BlockSpec((tm, PACK_WIDTH), lambda i: (i + n // tm, 0))
    return pl.pallas_call(
        _final_kernel,
        grid=(n // tm,),
        in_specs=[row(PACK_WIDTH), second, row(PACK_WIDTH), second, row(LANES),
                  pl.BlockSpec((1, 6, d), lambda i: (i // per_batch, 0, 0)), row(d), pl.BlockSpec((1, d), lambda i: (0, 0))],
        out_specs=row(d),
        out_shape=jax.ShapeDtypeStruct((n, d), F32),
        compiler_params=_params(("parallel",)),
        name="final_norm",
    )(ga, ga, gb, gb, gates, mod_l, x2, final_g.reshape(1, d))


def kernel(x, c, w_ada, b_ada, norm1_g, w_in, attn_norm_g, hgrn_lb_logits, hgrn_norm_g, w_out, norm2_g, router_group_w, router_group_b, router_expert_w, router_expert_b, moe_w1, moe_w3, moe_w2, final_g):
    bsz, seq, d = x.shape
    depth = w_ada.shape[0]
    n = bsz * seq
    attn_dim = attn_norm_g.shape[1]
    n_heads = attn_dim // ATTN_HEAD_DIM

    lb_w = jax.nn.softmax(hgrn_lb_logits.astype(F32), axis=0)
    lower_bounds = jnp.cumsum(lb_w, axis=0) - lb_w[0:1]
    mod = _adaln(c, w_ada, b_ada)
    bias = _attn_bias(n_heads)
    hconsts = _hgrn_consts()
    n_rows = n * TOP_K + N_EXPERTS * MOE_TILE

    x2 = x.reshape(n, d)
    moe = None
    for layer in range(depth):
        x2, (aq, ak, av, rq, rf, ri, rg) = _in_proj(x2, mod[layer], norm1_g[layer], w_in, layer, seq, moe)
        attn = _attention(aq, ak, av, bias, bsz, seq)
        rec = _hgrn(rq, rf, ri, rg, lower_bounds[layer], hgrn_norm_g[layer], hconsts, bsz, seq)
        gpad = EXPERTS_PER_GROUP - N_GROUPS
        wr = jnp.concatenate([router_group_w[layer].T, jnp.zeros((gpad, d), F32), router_expert_w[layer].T], axis=0)
        br = jnp.concatenate([router_group_b[layer], jnp.zeros((gpad,), F32), router_expert_b[layer]]).reshape(-1, 1)
        x2, ha, hb, eids, gates, counts = _out_proj(attn, rec, x2, mod[layer], attn_norm_g[layer], norm2_g[layer],
                                            w_out, layer, wr.astype(BF16), br, seq)
        pos, tables = _plan(eids, counts, n_rows)
        ya, yb = _experts(_sc_scatter_rows(ha, pos, n_rows), _sc_scatter_rows(hb, pos, n_rows),
                          moe_w1, moe_w3, moe_w2, layer, tables)
        moe = (_sc_gather_rows(ya, pos), _sc_gather_rows(yb, pos), gates, mod[layer])
    return _final(x2, moe[0], moe[1], moe[2], moe[3], final_g, seq).reshape(bsz, seq, d)
```

```python
import functools

import numpy as np
import jax
import jax.numpy as jnp
from jax import lax
from jax.experimental import pallas as pl
from jax.experimental.pallas import tpu as pltpu
from jax.experimental.pallas import tpu_sc as plsc

F32 = jnp.float32
BF16 = jnp.bfloat16
U32 = jnp.uint32

ATTN_HEAD_DIM = 64
ATTN_BLOCK = 128
DILATIONS = (1, 4, 16)
ALIBI_MAX_EXP = 8.0
HGRN_EXPAND = 128
HGRN_CHUNK = 128
N_GROUPS = 4
EXPERTS_PER_GROUP = 8
N_EXPERTS = N_GROUPS * EXPERTS_PER_GROUP
TOP_K = 2
NORM_EPS = 1e-6
LOG2E = 1.4426950408889634

LANES = 128
ATTN_SUPER = ATTN_BLOCK * max(DILATIONS)
ROW_TILE = 512
OUT_TILE = 1024
IN_GROUPS = 16
ATTN_UNROLL = 16
HGRN_TILE = 2048
HGRN_UNROLL = 16
MOE_TILE = 512
FFN_SPLIT = 2
PLAN_TILE = 4096
PLAN_SPAN = 256
PLAN_LANES = 256
PACK_WIDTH = 256
SC_WINDOW = 128
VMEM_LIMIT = 56 << 20


def _params(semantics):
    return pltpu.CompilerParams(dimension_semantics=semantics, vmem_limit_bytes=VMEM_LIMIT)


def _rms(x, g):
    return x * lax.rsqrt(jnp.mean(x * x, axis=-1, keepdims=True) + NORM_EPS) * g


def _loop(trips, body):
    if trips == 1:
        body(0, 0)
    else:
        lax.fori_loop(0, trips, body, 0)


def _round_robin(gens):
    live = list(gens)
    while live:
        for gen in list(live):
            try:
                next(gen)
            except StopIteration:
                live.remove(gen)


def _dot_nt(a, b):
    return lax.dot_general(a, b, (((1,), (1,)), ((), ())), preferred_element_type=F32)


def _dot_tn(a, b):
    return lax.dot_general(a, b, (((0,), (0,)), ((), ())), preferred_element_type=F32)


def _pack_pair(hi, lo):
    hb = lax.bitcast_convert_type(hi.astype(BF16).astype(F32), U32)
    lb = lax.bitcast_convert_type(lo.astype(BF16).astype(F32), U32)
    return hb | (lb >> 16)


def _unpack_pair(u):
    hi = lax.bitcast_convert_type(u & jnp.uint32(0xFFFF0000), F32)
    lo = lax.bitcast_convert_type(u << 16, F32)
    return hi, lo


def _pack_rows(y):
    w = PACK_WIDTH
    return _pack_pair(y[:, 0:w], y[:, 2 * w:3 * w]), _pack_pair(y[:, w:2 * w], y[:, 3 * w:4 * w])


def _unpack_rows(ua, ub):
    ha, la = _unpack_pair(ua)
    hb, lb = _unpack_pair(ub)
    return jnp.concatenate([ha, hb, la, lb], axis=1)


def _moe_residual(x_ref, ga0, ga1, gb0, gb1, gt_ref, gate_row):
    gt = gt_ref[...]
    y = gt[:, 0:1] * _unpack_rows(ga0[...], gb0[...]) + gt[:, 1:2] * _unpack_rows(ga1[...], gb1[...])
    return x_ref[...] + gate_row * y


def _ada_kernel(c_ref, w_ref, b_ref, o_ref):
    c = c_ref[...]
    ca = c * jax.nn.sigmoid(c)
    hi = ca.astype(BF16)
    lo = (ca - hi.astype(F32)).astype(BF16)
    w = w_ref[0].astype(BF16)
    o_ref[0] = (jnp.dot(hi, w, preferred_element_type=F32) + jnp.dot(lo, w, preferred_element_type=F32)) + b_ref[0]


def _adaln(c, w_ada, b_ada):
    depth, d, n6 = w_ada.shape
    bsz = c.shape[0]
    rows = 8
    cp = jnp.zeros((rows, d), F32).at[:bsz].set(c)
    tn = 1536
    out = pl.pallas_call(
        _ada_kernel,
        grid=(depth, n6 // tn),
        in_specs=[
            pl.BlockSpec((rows, d), lambda l, j: (0, 0)),
            pl.BlockSpec((1, d, tn), lambda l, j: (l, 0, j)),
            pl.BlockSpec((1, 1, tn), lambda l, j: (l, 0, j)),
        ],
        out_specs=pl.BlockSpec((1, rows, tn), lambda l, j: (l, 0, j)),
        out_shape=jax.ShapeDtypeStruct((depth, rows, n6), F32),
        compiler_params=_params(("parallel", "parallel")),
        name="adaln",
    )(cp, w_ada, b_ada.reshape(depth, 1, n6))
    return out[:, :bsz].reshape(depth, bsz, 6, d)


def _in_kernel(*refs, n_moe):
    moe, (x_ref, mod_ref, g_ref, w_ref), rest = refs[:n_moe], refs[n_moe:n_moe + 4], refs[n_moe + 4:]
    hb = rest[-1]
    i = pl.program_id(0)

    @pl.when(i == 0)
    def _():
        hb[1] = jnp.zeros(hb.shape[1:], hb.dtype)

    outs = rest[1:-1] if n_moe else rest[:-1]
    mod = mod_ref[0]
    prev = hb[(i + 1) % 2]
    width = outs[0].shape[1]
    tm = x_ref.shape[0]
    groups = [slice(r, r + tm // IN_GROUPS) for r in range(0, tm, tm // IN_GROUPS)]

    def prologue(rs):
        if n_moe:
            ga0, ga1, gb0, gb1, gt_ref, pmod_ref = moe
            gt = gt_ref[rs, :]
            y = gt[:, 0:1] * _unpack_rows(ga0[rs, :], gb0[rs, :]) + gt[:, 1:2] * _unpack_rows(ga1[rs, :], gb1[rs, :])
            x = x_ref[rs, :] + pmod_ref[0][5:6, :] * y
            rest[0][rs, :] = x
        else:
            x = x_ref[rs, :]
        hb[i % 2, rs, :] = (_rms(x, g_ref[...]) * (1.0 + mod[1:2, :]) + mod[0:1, :]).astype(BF16)

    per = -(-len(groups) // len(outs))
    for k, o in enumerate(outs):
        for rs in groups[k * per:(k + 1) * per]:
            prologue(rs)
        w = w_ref[0, :, k * width:(k + 1) * width].astype(BF16)
        o[...] = jnp.dot(prev, w, preferred_element_type=F32).astype(o.dtype)


def _in_proj(x2, mod_l, g, w_in, layer, seq, moe=None):
    n, d = x2.shape
    n_out = w_in.shape[2]
    width = 512
    tm = ROW_TILE
    per_batch = seq // tm
    last = n // tm - 1
    cur = lambda i: jnp.minimum(i, last)
    row = lambda w: pl.BlockSpec((tm, w), lambda i: (cur(i), 0))
    mod_spec = pl.BlockSpec((1, 6, d), lambda i: (cur(i) // per_batch, 0, 0))
    in_specs, args, out_specs, out_shape = [], [], [], []
    if moe is not None:
        ga, gb, gates, mod_prev = moe
        second = pl.BlockSpec((tm, PACK_WIDTH), lambda i: (cur(i) + n // tm, 0))
        in_specs += [row(PACK_WIDTH), second, row(PACK_WIDTH), second, row(LANES), mod_spec]
        args += [ga, ga, gb, gb, gates, mod_prev]
        out_specs.append(row(d))
        out_shape.append(jax.ShapeDtypeStruct((n, d), F32))
    in_specs += [row(d), mod_spec, pl.BlockSpec((1, d), lambda i: (0, 0)),
                 pl.BlockSpec((1, d, n_out), lambda i: (layer, 0, 0), pipeline_mode=pl.Buffered(1))]
    args += [x2, mod_l, g.reshape(1, d), w_in]
    out_dtypes = (F32, F32, F32, BF16, F32, BF16, F32)
    out_specs += [pl.BlockSpec((tm, width), lambda i: (jnp.maximum(i - 1, 0), 0))] * len(out_dtypes)
    out_shape += [jax.ShapeDtypeStruct((n, width), dt) for dt in out_dtypes]
    outs = pl.pallas_call(
        functools.partial(_in_kernel, n_moe=0 if moe is None else 6),
        grid=(n // tm + 1,),
        in_specs=in_specs,
        out_specs=out_specs,
        out_shape=out_shape,
        scratch_shapes=[pltpu.VMEM((2, tm, d), BF16)],
        compiler_params=_params(("arbitrary",)),
        name="in_proj",
    )(*args)
    return (x2, outs) if moe is None else (outs[0], outs[1:])


def _attn_bias(n_heads):
    qi = np.arange(ATTN_BLOCK)[:, None]
    kj = np.arange(2 * ATTN_BLOCK)[None, :]
    dist = ATTN_BLOCK + qi - kj
    valid = (dist >= 0) & (dist <= ATTN_BLOCK)
    slopes = np.exp2(-ALIBI_MAX_EXP * np.arange(1, n_heads + 1, dtype=np.float32) / n_heads)
    out = np.empty((2, len(DILATIONS), n_heads, ATTN_BLOCK, 2 * ATTN_BLOCK), np.float32)
    for p, dil in enumerate(DILATIONS):
        b = -slopes[:, None, None] * (dist * dil).astype(np.float32)[None] * np.float32(LOG2E)
        out[0, p] = np.where(valid[None], b, -np.inf)
        out[1, p] = np.where((valid & (kj >= ATTN_BLOCK))[None], b, -np.inf)
    return jnp.asarray(out)


def _attn_kernel(q_ref, k_ref, v_ref, bias_ref, o_ref, kcar, vcar, pbuf, vbuf, oscr, mscr, dscr):
    n = pl.program_id(2)
    sb = q_ref.shape[0]
    is_lo = lax.broadcasted_iota(jnp.int32, (ATTN_BLOCK, LANES), 1) < ATTN_HEAD_DIM
    scale = ATTN_HEAD_DIM ** -0.5 * LOG2E

    @pl.when(n == 0)
    def _():
        kcar[...] = jnp.zeros_like(kcar)
        vcar[...] = jnp.zeros_like(vcar)

    pbuf[...] = jnp.zeros_like(pbuf)
    vbuf[...] = jnp.zeros_like(vbuf)
    lo2 = lax.broadcasted_iota(jnp.int32, (2 * ATTN_BLOCK, LANES), 1) < ATTN_HEAD_DIM
    zero = jnp.zeros((2 * ATTN_BLOCK, LANES), BF16)
    den_lo = lo2.astype(BF16)
    den_hi = jnp.logical_not(lo2).astype(BF16)

    for p, dil in enumerate(DILATIONS):
        nb = (sb // ATTN_BLOCK) // dil

        def rows_of(it, dil=dil, nb=nb):
            start = (it % nb) * (ATTN_BLOCK * dil) + it // nb
            return pl.ds(start, ATTN_BLOCK, stride=dil) if dil > 1 else pl.ds(pl.multiple_of(start, ATTN_BLOCK), ATTN_BLOCK)

        def pv(it, probs, vaug, p=p, rows_of=rows_of):
            rows = rows_of(it)
            of = jnp.dot(probs, vaug, preferred_element_type=F32)
            oscr[p, rows, :] = of[:, :LANES]
            dscr[p, rows, :] = of[:, LANES:]

        def qk(it, p=p, nb=nb, rows_of=rows_of):
            r = it // nb
            rows = rows_of(it)
            crow = pl.ds(pl.multiple_of(r * ATTN_BLOCK, ATTN_BLOCK), ATTN_BLOCK)
            qb = q_ref[rows, :] * scale
            kc = k_ref[rows, :].astype(BF16)
            vc = v_ref[rows, :].astype(BF16)
            kcat = jnp.concatenate([kcar[p, crow, :], kc], axis=0)
            vcat = jnp.concatenate([vcar[p, crow, :], vc], axis=0)
            vaug = jnp.concatenate([jnp.concatenate([jnp.where(lo2, vcat, zero), den_lo], axis=1),
                                    jnp.concatenate([jnp.where(lo2, zero, vcat), den_hi], axis=1)], axis=0)
            kcar[p, crow, :] = kc
            vcar[p, crow, :] = vc
            first = jnp.logical_and(n == 0, it % nb == 0).astype(jnp.int32)
            probs, acc_m = [], None
            for hh in range(2):
                sel = is_lo if hh == 0 else jnp.logical_not(is_lo)
                qm = jnp.where(sel, qb, 0.0).astype(BF16)
                s = _dot_nt(qm, kcat) + bias_ref[first, p, hh]
                m = jnp.max(s, axis=-1, keepdims=True)
                probs.append(jnp.exp2(s - m).astype(BF16))
                mb = jnp.broadcast_to(m, (ATTN_BLOCK, LANES))
                acc_m = mb if hh == 0 else jnp.where(sel, mb, acc_m)
            mscr[p, rows, :] = acc_m
            return jnp.concatenate(probs, axis=1), vaug

        def body(g, carry, pv=pv, qk=qk):
            it0 = g * ATTN_UNROLL
            pend = (pbuf[...], vbuf[...])
            for u in range(ATTN_UNROLL):
                cur = qk(it0 + u)
                pv(jnp.maximum(it0 + u - 1, 0), *pend)
                pend = cur
            pbuf[...] = pend[0]
            vbuf[...] = pend[1]
            return carry

        n_it = sb // ATTN_BLOCK
        lax.fori_loop(0, n_it // ATTN_UNROLL, body, 0)
        pv(n_it - 1, pbuf[...], vbuf[...])

    ct = 256

    def combine(t, carry):
        rs = pl.ds(pl.multiple_of(t * ct, ct), ct)
        m0, m1, m2 = mscr[0, rs, :], mscr[1, rs, :], mscr[2, rs, :]
        mx = jnp.maximum(jnp.maximum(m0, m1), m2)
        w0, w1, w2 = jnp.exp2(m0 - mx), jnp.exp2(m1 - mx), jnp.exp2(m2 - mx)
        num = w0 * oscr[0, rs, :] + w1 * oscr[1, rs, :] + w2 * oscr[2, rs, :]
        den = w0 * dscr[0, rs, :] + w1 * dscr[1, rs, :] + w2 * dscr[2, rs, :]
        o_ref[rs, :] = (num / den).astype(o_ref.dtype)
        return carry

    lax.fori_loop(0, sb // ct, combine, 0)


def _attention(aq, ak, av, bias, bsz, seq):
    n, width = aq.shape
    sb = ATTN_SUPER
    nsb = seq // sb
    pairs = width // LANES
    npat = len(DILATIONS)
    blk = pl.BlockSpec((sb, LANES), lambda b, h, t: (b * nsb + t, h))
    return pl.pallas_call(
        _attn_kernel,
        grid=(bsz, pairs, nsb),
        in_specs=[blk, blk, blk,
                  pl.BlockSpec((2, npat, 2, ATTN_BLOCK, 2 * ATTN_BLOCK), lambda b, h, t: (0, 0, h, 0, 0))],
        out_specs=blk,
        out_shape=jax.ShapeDtypeStruct((n, width), BF16),
        scratch_shapes=[
            pltpu.VMEM((npat, max(DILATIONS) * ATTN_BLOCK, LANES), BF16),
            pltpu.VMEM((npat, max(DILATIONS) * ATTN_BLOCK, LANES), BF16),
            pltpu.VMEM((ATTN_BLOCK, 4 * ATTN_BLOCK), BF16),
            pltpu.VMEM((4 * ATTN_BLOCK, 2 * LANES), BF16),
            pltpu.VMEM((npat, sb, LANES), F32),
            pltpu.VMEM((npat, sb, LANES), F32),
            pltpu.VMEM((npat, sb, LANES), F32),
        ],
        compiler_params=_params(("parallel", "parallel", "arbitrary")),
        name="dilated_attn",
    )(aq, ak, av, bias)


def _hgrn_consts():
    c = HGRN_CHUNK
    t = np.arange(c)[:, None]
    u = np.arange(c)[None, :]
    low, masks = [], []
    m = c // 2
    while m >= 1:
        mid = (t // (2 * m)) * (2 * m) + m
        is_q = t >= mid
        if m in (4, 2):
            low.append((is_q & (u >= mid) & (u <= t)) | (~is_q & (u > t) & (u < mid)))
        masks.append(((t // (2 * m)) == (u // (2 * m))) & (t % (2 * m) >= m) & (u % (2 * m) < m))
        m //= 2
    masks.append(t == u)
    f = lambda a: np.asarray(a, np.float32)
    return (jnp.asarray(f(u <= t), BF16), jnp.asarray(f(np.concatenate(low, axis=0)), BF16),
            jnp.asarray(f(np.broadcast_to(t % 2 == 1, (c, c)))), jnp.asarray(f(np.stack(masks))))


def _hgrn_kernel(q_ref, z_ref, v_ref, g_ref, lb_ref, gn_ref, tri_ref, low_ref, odd_ref, msk_ref, o_ref, st):
    c = HGRN_CHUNK

    @pl.when(pl.program_id(2) == 0)
    def _():
        st[...] = jnp.zeros_like(st)

    lb = jnp.maximum(lb_ref[...], 0.0)
    log_lb = jnp.log(lb)
    log_1m = jnp.log1p(-lb)
    one_m = 1.0 - lb
    gn = gn_ref[...]
    n_levels = msk_ref.shape[0] - 1
    coarse = [c >> (i + 1) for i in range(n_levels) if (c >> (i + 1)) >= 8]

    def chunk(ci):
        rs = pl.ds(pl.multiple_of(ci * c, c), c)
        z = z_ref[rs, :]
        qb = q_ref[rs, :]
        vb = v_ref[rs, :]
        ez = jnp.exp(-jnp.abs(z))
        ls = jnp.minimum(z, 0.0) - jnp.log(1.0 + ez)
        b2 = log_1m + ls
        lf = jnp.maximum(log_lb, b2) + jnp.log(1.0 + jnp.exp(-jnp.abs(log_lb - b2)))
        rz = 1.0 / (1.0 + ez)
        kb = (one_m * jnp.where(z > 0.0, ez * rz, rz)).astype(BF16)
        l2 = lf * LOG2E
        hi = l2.astype(BF16)
        hl = jnp.concatenate([hi, (l2 - hi.astype(F32)).astype(BF16)], axis=1)
        yield
        bb = jnp.dot(tri_ref[...], hl, preferred_element_type=F32)
        b = bb[:, :HGRN_EXPAND] + bb[:, HGRN_EXPAND:]
        dl = jnp.dot(low_ref[...], hl, preferred_element_type=F32)
        dl = dl[:, :HGRN_EXPAND] + dl[:, HGRN_EXPAND:]
        yield

        def level(d):
            e = jnp.exp2(d).astype(BF16)
            return _dot_nt(qb * e, kb * e)

        a = msk_ref[n_levels] * _dot_nt(qb, kb)
        for i in range(n_levels):
            m = c >> (i + 1)
            if m >= 8:
                parts = []
                for r0 in range(0, c, 2 * m):
                    mid = b[r0 + m - 1:r0 + m, :]
                    parts += [mid - b[r0:r0 + m, :], b[r0 + m:r0 + 2 * m, :] - mid]
                d = jnp.concatenate(parts, axis=0)
            elif m > 1:
                d = dl[(i - len(coarse)) * c:(i - len(coarse) + 1) * c]
            else:
                d = l2 * odd_ref[...]
            a = a + msk_ref[i] * level(d)
            if i % 3 == 2:
                yield
        b_last = b[c - 1:c, :]
        s_t = st[...]
        o = _dot_nt(qb * jnp.exp2(b).astype(BF16), s_t.astype(BF16)) + jnp.dot(a.astype(BF16), vb, preferred_element_type=F32)
        st[...] = jnp.exp2(b_last) * s_t + _dot_tn(vb, kb * jnp.exp2(b_last - b).astype(BF16))
        yield
        g = g_ref[rs, :]
        o_ref[rs, :] = (_rms(o, gn) * (g * jax.nn.sigmoid(g))).astype(o_ref.dtype)

    def group(gi, carry):
        _round_robin([chunk(gi * HGRN_UNROLL + u) for u in range(HGRN_UNROLL)])
        return carry

    _loop(q_ref.shape[0] // (c * HGRN_UNROLL), group)


def _hgrn(rq, rf, ri, rg, lb, gn, consts, bsz, seq):
    n, width = rq.shape
    heads = width // HGRN_EXPAND
    ts = HGRN_TILE
    nt = seq // ts
    blk = pl.BlockSpec((ts, HGRN_EXPAND), lambda b, h, t: (b * nt + t, h))
    vec = pl.BlockSpec((1, HGRN_EXPAND), lambda b, h, t: (0, h))
    const = lambda a: pl.BlockSpec(a.shape, lambda b, h, t: (0,) * a.ndim)
    return pl.pallas_call(
        _hgrn_kernel,
        grid=(bsz, heads, nt),
        in_specs=[blk, blk, blk, blk, vec, vec] + [const(a) for a in consts],
        out_specs=blk,
        out_shape=jax.ShapeDtypeStruct((n, width), BF16),
        scratch_shapes=[pltpu.VMEM((HGRN_EXPAND, HGRN_EXPAND), F32)],
        compiler_params=_params(("parallel", "parallel", "arbitrary")),
        name="hgrn2",
    )(rq, rf, ri, rg, lb.reshape(1, width), gn.reshape(1, width), *consts)


def _route_math(lg):
    row = lax.broadcasted_iota(jnp.int32, (EXPERTS_PER_GROUP, lg.shape[1]), 0).astype(F32)
    none = float(EXPERTS_PER_GROUP)

    def first_argmax(vals):
        mx = jnp.max(vals, axis=0, keepdims=True)
        return mx, jnp.min(jnp.where(vals == mx, row, none), axis=0, keepdims=True)

    g = lg[0:EXPERTS_PER_GROUP, :]
    is_group = row < N_GROUPS
    gmax, gidx = first_argmax(jnp.where(is_group, g, -jnp.inf))
    gprob = 1.0 / jnp.sum(jnp.where(is_group, jnp.exp(g - gmax), 0.0), axis=0, keepdims=True)
    el = lg[EXPERTS_PER_GROUP:2 * EXPERTS_PER_GROUP, :]
    for k in range(1, N_GROUPS):
        el = jnp.where(gidx == k, lg[(k + 1) * EXPERTS_PER_GROUP:(k + 2) * EXPERTS_PER_GROUP, :], el)
    t1, i1 = first_argmax(el)
    t2, i2 = first_argmax(jnp.where(row == i1, -jnp.inf, el))
    ex = jnp.exp(t2 - t1)
    base = gidx * EXPERTS_PER_GROUP
    eids = jnp.concatenate([base + i1, base + i2], axis=0).astype(jnp.int32)
    gates = jnp.concatenate([gprob / (1.0 + ex), gprob * ex / (1.0 + ex)], axis=0)
    return eids, gates


def _out_kernel(a_ref, r_ref, x_ref, mod_ref, ag_ref, g2_ref, wo_ref, wr_ref, br_ref, xo_ref, ha_ref, hb_ref, e_ref, gt_ref, cnt_ref):
    mod = mod_ref[0]
    mix = jnp.concatenate([_rms(a_ref[...].astype(F32), ag_ref[...]).astype(BF16), r_ref[...]], axis=1)
    xn = x_ref[...] + mod[2:3, :] * jnp.dot(mix, wo_ref[0].astype(BF16), preferred_element_type=F32)
    xo_ref[...] = xn
    h2 = _rms(xn, g2_ref[...]) * (1.0 + mod[4:5, :]) + mod[3:4, :]
    ha_ref[...], hb_ref[...] = _pack_rows(h2)
    eids, gates = _route_math(_dot_nt(wr_ref[...], h2.astype(BF16)) + br_ref[...])
    e_ref[...] = eids
    eio = lax.broadcasted_iota(jnp.int32, (N_EXPERTS, eids.shape[1]), 0)
    hit = jnp.logical_or(eids[0:1, :] == eio, eids[1:2, :] == eio).astype(F32)

    @pl.when(pl.program_id(0) == 0)
    def _():
        cnt_ref[...] = jnp.zeros_like(cnt_ref)

    cnt_ref[...] += jnp.sum(hit, axis=1, keepdims=True)
    pad = jnp.zeros((LANES - TOP_K, LANES), F32)
    for k in range(h2.shape[0] // LANES):
        tile = jnp.concatenate([gates[:, k * LANES:(k + 1) * LANES], pad], axis=0)
        gt_ref[k * LANES:(k + 1) * LANES, :] = tile.T


def _out_proj(attn, rec, x2, mod_l, ag, g2, w_out, layer, wr_bf, br, seq):
    n, d = x2.shape
    half = attn.shape[1]
    tm = OUT_TILE
    per_batch = seq // tm
    row = lambda w: pl.BlockSpec((tm, w), lambda i: (i, 0))
    full = lambda a: pl.BlockSpec(a.shape, lambda i: (0,) * a.ndim, pipeline_mode=pl.Buffered(1))
    ag2, g22 = ag.reshape(1, half), g2.reshape(1, d)
    return pl.pallas_call(
        _out_kernel,
        grid=(n // tm,),
        in_specs=[row(half), row(half), row(d), pl.BlockSpec((1, 6, d), lambda i: (i // per_batch, 0, 0)),
                  full(ag2), full(g22), pl.BlockSpec((1, d, d), lambda i: (layer, 0, 0), pipeline_mode=pl.Buffered(1)),
                  full(wr_bf), full(br)],
        out_specs=[row(d), row(PACK_WIDTH), row(PACK_WIDTH), pl.BlockSpec((TOP_K, tm), lambda i: (0, i)), row(LANES),
                   pl.BlockSpec((N_EXPERTS, LANES), lambda i: (0, 0))],
        out_shape=[jax.ShapeDtypeStruct((n, d), F32), jax.ShapeDtypeStruct((n, PACK_WIDTH), U32),
                   jax.ShapeDtypeStruct((n, PACK_WIDTH), U32), jax.ShapeDtypeStruct((TOP_K, n), jnp.int32),
                   jax.ShapeDtypeStruct((n, LANES), F32), jax.ShapeDtypeStruct((N_EXPERTS, LANES), F32)],
        compiler_params=_params(("arbitrary",)),
        name="out_proj",
    )(attn, rec, x2, mod_l, ag2, g22, w_out, wr_bf, br)


def _plan_kernel(e_ref, cnt_ref, ut_ref, pos_ref, be_ref, bv_ref, bf_ref, bn_ref, nu_ref, off, run):
    i = pl.program_id(0)
    t = e_ref.shape[1]
    eio = lax.broadcasted_iota(jnp.int32, (N_EXPERTS, t), 0)
    oh1 = e_ref[0:1, :] == eio
    oh2 = e_ref[1:2, :] == eio
    oh = jnp.logical_or(oh1, oh2).astype(F32)

    @pl.when(i == 0)
    def _():
        counts = cnt_ref[...]
        nb = jnp.floor((counts + (MOE_TILE - 1)) * (1.0 / MOE_TILE))
        er = lax.broadcasted_iota(jnp.int32, (N_EXPERTS, N_EXPERTS), 0)
        ec = lax.broadcasted_iota(jnp.int32, (N_EXPERTS, N_EXPERTS), 1)
        cum_excl = jnp.dot((ec < er).astype(BF16), nb.astype(BF16), preferred_element_type=F32)
        off[...] = cum_excl * MOE_TILE
        run[...] = jnp.zeros_like(run)
        ce, ci, cn = cum_excl[:, 0:1], (cum_excl + nb)[:, 0:1], counts[:, 0:1]
        b = lax.broadcasted_iota(jnp.int32, (N_EXPERTS, be_ref.shape[1]), 1).astype(F32)
        member = jnp.logical_and(b >= ce, b < ci)
        be = jnp.minimum(jnp.sum((b >= ci).astype(F32), axis=0, keepdims=True), N_EXPERTS - 1.0)
        bv = jnp.sum(jnp.where(member, jnp.minimum(cn - (b - ce) * MOE_TILE, float(MOE_TILE)), 0.0), axis=0, keepdims=True)
        be_ref[...] = be.astype(jnp.int32)
        bv_ref[...] = bv.astype(jnp.int32)
        has = nb[:, 0:1] > 0.0
        eid = lax.broadcasted_iota(jnp.int32, b.shape, 0).astype(F32)
        bf_ref[...] = jnp.sum(jnp.where(jnp.logical_and(b == ce, has), 1.0, 0.0), axis=0, keepdims=True).astype(jnp.int32)
        nxt = jnp.min(jnp.where(jnp.logical_and(eid > be, has), eid, float(N_EXPERTS)), axis=0, keepdims=True)
        bn_ref[...] = jnp.where(nxt < N_EXPERTS, nxt, -1.0).astype(jnp.int32)
        nu_ref[...] = jnp.broadcast_to(ci[N_EXPERTS - 1:N_EXPERTS, :], nu_ref.shape).astype(jnp.int32)

    span = ut_ref.shape[0]
    ohb = oh.astype(BF16)
    base = off[:, 0:1] + run[:, 0:1] - 1.0
    parts = []
    for s in range(0, t, span):
        c = jnp.dot(ohb[:, s:s + span], ut_ref[...], preferred_element_type=F32)
        parts.append(base + c)
        base = base + c[:, span - 1:span]
    r = jnp.concatenate(parts, axis=1)
    pos_ref[0:1, :] = jnp.sum(jnp.where(oh1, r, 0.0), axis=0, keepdims=True).astype(jnp.int32)
    pos_ref[1:2, :] = jnp.sum(jnp.where(oh2, r, 0.0), axis=0, keepdims=True).astype(jnp.int32)
    run[...] += jnp.sum(oh, axis=1, keepdims=True)


def _plan(expert_ids_t, counts, n_rows):
    n = expert_ids_t.shape[1]
    t = PLAN_TILE
    n_blocks = n_rows // MOE_TILE
    assert n_blocks <= PLAN_LANES and TOP_K == 2
    ut = jnp.asarray(np.triu(np.ones((PLAN_SPAN, PLAN_SPAN), np.float32)), BF16)
    tab = jax.ShapeDtypeStruct((1, PLAN_LANES), jnp.int32)
    const = lambda shape: pl.BlockSpec(shape, lambda i: (0, 0))
    pos, be, bv, bf, bn, nu = pl.pallas_call(
        _plan_kernel,
        grid=(n // t,),
        in_specs=[pl.BlockSpec((TOP_K, t), lambda i: (0, i)), const((N_EXPERTS, LANES)), const((PLAN_SPAN, PLAN_SPAN))],
        out_specs=[pl.BlockSpec((TOP_K, t), lambda i: (0, i))] + [const((1, PLAN_LANES))] * 4 + [const((1, LANES))],
        out_shape=[jax.ShapeDtypeStruct((TOP_K, n), jnp.int32), tab, tab, tab, tab, jax.ShapeDtypeStruct((1, LANES), jnp.int32)],
        scratch_shapes=[pltpu.VMEM((N_EXPERTS, LANES), F32)] * 2,
        compiler_params=_params(("arbitrary",)),
        name="plan",
    )(expert_ids_t, counts, ut)
    return pos, (be.reshape(-1), bv.reshape(-1), bf.reshape(-1), bn.reshape(-1), nu.reshape(-1)[:1])


def _sc_mesh():
    return plsc.VectorSubcoreMesh(core_axis_name="c", subcore_axis_name="s")


def _sc_scatter_rows(x, pos, n_rows):
    n, d = x.shape
    w = SC_WINDOW

    @functools.partial(pl.kernel, out_type=jax.ShapeDtypeStruct((n_rows, d), x.dtype), mesh=_sc_mesh(), scratch_types=[])
    def scatter(x_hbm, p_hbm, o_hbm):
        def body(x_vmem, i_vmem):
            for k in range(TOP_K):
                pltpu.sync_copy(x_vmem, o_hbm.at[i_vmem.at[k]])

        pltpu.emit_pipeline(
            body,
            grid=(n // w,),
            in_specs=[pl.BlockSpec((w, d), lambda i: (i, 0)), pl.BlockSpec((TOP_K, w), lambda i: (0, i))],
            out_specs=[],
            core_axis_name=("c", "s"),
            dimension_semantics=(pltpu.PARALLEL,),
        )(x_hbm, p_hbm)

    return scatter(x, pos)


def _sc_gather_rows(y, pos):
    m = pos.shape[0] * pos.shape[1]
    d = y.shape[1]
    w = SC_WINDOW

    @functools.partial(pl.kernel, out_type=jax.ShapeDtypeStruct((m, d), y.dtype), mesh=_sc_mesh(), scratch_types=[])
    def gather(y_hbm, p_hbm, o_hbm):
        def body(i_vmem, o_vmem):
            pltpu.sync_copy(y_hbm.at[i_vmem.at[0]], o_vmem)

        pltpu.emit_pipeline(
            body,
            grid=(m // w,),
            in_specs=[pl.BlockSpec((1, w), lambda i: (0, i))],
            out_specs=[pl.BlockSpec((w, d), lambda i: (i, 0))],
            core_axis_name=("c", "s"),
            dimension_semantics=(pltpu.PARALLEL,),
        )(p_hbm, o_hbm)

    return gather(y, pos.reshape(1, m))


def _expert_kernel(be_ref, bv_ref, bf_ref, bn_ref, nu_ref, xa_ref, xb_ref, w1_hbm, w3_hbm, w2_hbm, ya_ref, yb_ref,
                   wf1, wf3, wf2, w1b, w3b, w2b, wsem, par, *, layer):
    i = pl.program_id(0)
    tb = xa_ref.shape[0]

    def fetch(e, slot):
        return (pltpu.make_async_copy(w1_hbm.at[layer, e], wf1.at[slot], wsem.at[slot, 0]),
                pltpu.make_async_copy(w3_hbm.at[layer, e], wf3.at[slot], wsem.at[slot, 1]),
                pltpu.make_async_copy(w2_hbm.at[layer, e], wf2.at[slot], wsem.at[slot, 2]))

    def ffn(rows):
        live = lax.broadcasted_iota(jnp.int32, (rows, 1), 0) < bv_ref[i]
        x = jnp.where(live, _unpack_rows(xa_ref[0:rows, :], xb_ref[0:rows, :]), 0.0).astype(BF16)
        y = None
        hc = w1b.shape[1] // FFN_SPLIT
        for c in range(FFN_SPLIT):
            cs = slice(c * hc, (c + 1) * hc)
            a = jnp.dot(x, w1b[:, cs], preferred_element_type=F32)
            b = jnp.dot(x, w3b[:, cs], preferred_element_type=F32)
            hm = (a * jax.nn.sigmoid(a) * b).astype(BF16)
            part = jnp.dot(hm, w2b[cs, :], preferred_element_type=F32)
            y = part if y is None else y + part
        ya_ref[0:rows, :], yb_ref[0:rows, :] = _pack_rows(y)

    @pl.when(i == 0)
    def _():
        par[0] = 0
        for c in fetch(be_ref[0], 0):
            c.start()

    @pl.when(i < nu_ref[0])
    def _():
        @pl.when(bf_ref[i] == 1)
        def _():
            slot = par[0]
            for c in fetch(be_ref[i], slot):
                c.wait()
            w1b[...] = wf1[slot].astype(BF16)
            w3b[...] = wf3[slot].astype(BF16)
            w2b[...] = wf2[slot].astype(BF16)

            @pl.when(bn_ref[i] >= 0)
            def _():
                for c in fetch(bn_ref[i], 1 - slot):
                    c.start()

            par[0] = 1 - slot

        sizes = (tb // 4, tb // 2, tb)
        for k, rows in enumerate(sizes):
            below = sizes[k - 1] if k else 0
            fits = (bv_ref[i] > below) if rows == tb else ((bv_ref[i] > below) & (bv_ref[i] <= rows))

            @pl.when(fits)
            def _(rows=rows):
                ffn(rows)
                if rows < tb:
                    ya_ref[rows:, :] = jnp.zeros((tb - rows, ya_ref.shape[1]), ya_ref.dtype)
                    yb_ref[rows:, :] = jnp.zeros((tb - rows, yb_ref.shape[1]), yb_ref.dtype)

    @pl.when(i >= nu_ref[0])
    def _():
        ya_ref[...] = jnp.zeros_like(ya_ref)
        yb_ref[...] = jnp.zeros_like(yb_ref)


def _experts(xa, xb, w1, w3, w2, layer, tables):
    n_rows = xa.shape[0]
    d, de = w1.shape[-2:]
    tb = MOE_TILE
    xspec = pl.BlockSpec((tb, PACK_WIDTH), lambda i, be, bv, bf, bn, nu: (jnp.minimum(i, nu[0] - 1), 0))
    yspec = pl.BlockSpec((tb, PACK_WIDTH), lambda i, be, bv, bf, bn, nu: (i, 0))
    anyspec = pl.BlockSpec(memory_space=pl.ANY)
    grid_spec = pltpu.PrefetchScalarGridSpec(
        num_scalar_prefetch=5,
        grid=(n_rows // tb,),
        in_specs=[xspec, xspec, anyspec, anyspec, anyspec],
        out_specs=[yspec, yspec],
        scratch_shapes=[pltpu.VMEM((2, d, de), F32), pltpu.VMEM((2, d, de), F32), pltpu.VMEM((2, de, d), F32),
                        pltpu.VMEM((d, de), BF16), pltpu.VMEM((d, de), BF16), pltpu.VMEM((de, d), BF16),
                        pltpu.SemaphoreType.DMA((2, 3)), pltpu.SMEM((1,), jnp.int32)],
    )
    rows = jax.ShapeDtypeStruct((n_rows, PACK_WIDTH), U32)
    return pl.pallas_call(
        functools.partial(_expert_kernel, layer=layer),
        grid_spec=grid_spec,
        out_shape=[rows, rows],
        compiler_params=_params(("arbitrary",)),
        name="experts",
    )(*tables, xa, xb, w1, w3, w2)


def _final_kernel(ga0, ga1, gb0, gb1, gt_ref, mod_ref, x_ref, fg_ref, o_ref):
    o_ref[...] = _rms(_moe_residual(x_ref, ga0, ga1, gb0, gb1, gt_ref, mod_ref[0][5:6, :]), fg_ref[...])


def _final(x2, ga, gb, gates, mod_l, final_g, seq):
    n, d = x2.shape
    tm = ROW_TILE
    per_batch = seq // tm
    row = lambda w: pl.BlockSpec((tm, w), lambda i: (i, 0))
    second = pl.BlockSpec((tm, PACK_WIDTH), lambda i: (i + n // tm, 0))
    return pl.pallas_call(
        _final_kernel,
        grid=(n // tm,),
        in_specs=[row(PACK_WIDTH), second, row(PACK_WIDTH), second, row(LANES),
                  pl.BlockSpec((1, 6, d), lambda i: (i // per_batch, 0, 0)), row(d), pl.BlockSpec((1, d), lambda i: (0, 0))],
        out_specs=row(d),
        out_shape=jax.ShapeDtypeStruct((n, d), F32),
        compiler_params=_params(("parallel",)),
        name="final_norm",
    )(ga, ga, gb, gb, gates, mod_l, x2, final_g.reshape(1, d))


def kernel(x, c, w_ada, b_ada, norm1_g, w_in, attn_norm_g, hgrn_lb_logits, hgrn_norm_g, w_out, norm2_g, router_group_w, router_group_b, router_expert_w, router_expert_b, moe_w1, moe_w3, moe_w2, final_g):
    bsz, seq, d = x.shape
    depth = w_ada.shape[0]
    n = bsz * seq
    attn_dim = attn_norm_g.shape[1]
    n_heads = attn_dim // ATTN_HEAD_DIM

    lb_w = jax.nn.softmax(hgrn_lb_logits.astype(F32), axis=0)
    lower_bounds = jnp.cumsum(lb_w, axis=0) - lb_w[0:1]
    mod = _adaln(c, w_ada, b_ada)
    bias = _attn_bias(n_heads)
    hconsts = _hgrn_consts()
    n_rows = n * TOP_K + N_EXPERTS * MOE_TILE

    x2 = x.reshape(n, d)
    moe = None
    for layer in range(depth):
        x2, (aq, ak, av, rq, rf, ri, rg) = _in_proj(x2, mod[layer], norm1_g[layer], w_in, layer, seq, moe)
        attn = _attention(aq, ak, av, bias, bsz, seq)
        rec = _hgrn(rq, rf, ri, rg, lower_bounds[layer], hgrn_norm_g[layer], hconsts, bsz, seq)
        gpad = EXPERTS_PER_GROUP - N_GROUPS
        wr = jnp.concatenate([router_group_w[layer].T, jnp.zeros((gpad, d), F32), router_expert_w[layer].T], axis=0)
        br = jnp.concatenate([router_group_b[layer], jnp.zeros((gpad,), F32), router_expert_b[layer]]).reshape(-1, 1)
        x2, ha, hb, eids, gates, counts = _out_proj(attn, rec, x2, mod[layer], attn_norm_g[layer], norm2_g[layer],
                                            w_out, layer, wr.astype(BF16), br, seq)
        pos, tables = _plan(eids, counts, n_rows)
        ya, yb = _experts(_sc_scatter_rows(ha, pos, n_rows), _sc_scatter_rows(hb, pos, n_rows),
                          moe_w1, moe_w3, moe_w2, layer, tables)
        moe = (_sc_gather_rows(ya, pos), _sc_gather_rows(yb, pos), gates, mod[layer])
    return _final(x2, moe[0], moe[1], moe[2], moe[3], final_g, seq).reshape(bsz, seq, d)
```

```python
import functools

import numpy as np
import jax
import jax.numpy as jnp
from jax import lax
from jax.experimental import pallas as pl
from jax.experimental.pallas import tpu as pltpu
from jax.experimental.pallas import tpu_sc as plsc

F32 = jnp.float32
BF16 = jnp.bfloat16
U32 = jnp.uint32

ATTN_HEAD_DIM = 64
ATTN_BLOCK = 128
DILATIONS = (1, 4, 16)
ALIBI_MAX_EXP = 8.0
HGRN_EXPAND = 128
HGRN_CHUNK = 128
N_GROUPS = 4
EXPERTS_PER_GROUP = 8
N_EXPERTS = N_GROUPS * EXPERTS_PER_GROUP
TOP_K = 2
NORM_EPS = 1e-6
LOG2E = 1.4426950408889634

LANES = 128
ATTN_SUPER = ATTN_BLOCK * max(DILATIONS)
ROW_TILE = 512
OUT_TILE = 1024
IN_GROUPS = 16
ATTN_UNROLL = 16
HGRN_TILE = 4096
HGRN_UNROLL = 16
MOE_TILE = 512
FFN_SPLIT = 2
PLAN_TILE = 4096
PLAN_SPAN = 256
PLAN_LANES = 256
PACK_WIDTH = 256
SC_WINDOW = 128
VMEM_LIMIT = 56 << 20


def _params(semantics):
    return pltpu.CompilerParams(dimension_semantics=semantics, vmem_limit_bytes=VMEM_LIMIT)


def _rms(x, g):
    return x * lax.rsqrt(jnp.mean(x * x, axis=-1, keepdims=True) + NORM_EPS) * g


def _loop(trips, body):
    if trips == 1:
        body(0, 0)
    else:
        lax.fori_loop(0, trips, body, 0)


def _round_robin(gens):
    live = list(gens)
    while live:
        for gen in list(live):
            try:
                next(gen)
            except StopIteration:
                live.remove(gen)


def _dot_nt(a, b):
    return lax.dot_general(a, b, (((1,), (1,)), ((), ())), preferred_element_type=F32)


def _dot_tn(a, b):
    return lax.dot_general(a, b, (((0,), (0,)), ((), ())), preferred_element_type=F32)


def _pack_pair(hi, lo):
    hb = lax.bitcast_convert_type(hi.astype(BF16).astype(F32), U32)
    lb = lax.bitcast_convert_type(lo.astype(BF16).astype(F32), U32)
    return hb | (lb >> 16)


def _unpack_pair(u):
    hi = lax.bitcast_convert_type(u & jnp.uint32(0xFFFF0000), F32)
    lo = lax.bitcast_convert_type(u << 16, F32)
    return hi, lo


def _pack_rows(y):
    w = PACK_WIDTH
    return _pack_pair(y[:, 0:w], y[:, 2 * w:3 * w]), _pack_pair(y[:, w:2 * w], y[:, 3 * w:4 * w])


def _unpack_rows(ua, ub):
    ha, la = _unpack_pair(ua)
    hb, lb = _unpack_pair(ub)
    return jnp.concatenate([ha, hb, la, lb], axis=1)


def _moe_residual(x_ref, ga0, ga1, gb0, gb1, gt_ref, gate_row):
    gt = gt_ref[...]
    y = gt[:, 0:1] * _unpack_rows(ga0[...], gb0[...]) + gt[:, 1:2] * _unpack_rows(ga1[...], gb1[...])
    return x_ref[...] + gate_row * y


def _ada_kernel(c_ref, w_ref, b_ref, o_ref):
    c = c_ref[...]
    ca = c * jax.nn.sigmoid(c)
    hi = ca.astype(BF16)
    lo = (ca - hi.astype(F32)).astype(BF16)
    w = w_ref[0].astype(BF16)
    o_ref[0] = (jnp.dot(hi, w, preferred_element_type=F32) + jnp.dot(lo, w, preferred_element_type=F32)) + b_ref[0]


def _adaln(c, w_ada, b_ada):
    depth, d, n6 = w_ada.shape
    bsz = c.shape[0]
    rows = 8
    cp = jnp.zeros((rows, d), F32).at[:bsz].set(c)
    tn = 1536
    out = pl.pallas_call(
        _ada_kernel,
        grid=(depth, n6 // tn),
        in_specs=[
            pl.BlockSpec((rows, d), lambda l, j: (0, 0)),
            pl.BlockSpec((1, d, tn), lambda l, j: (l, 0, j)),
            pl.BlockSpec((1, 1, tn), lambda l, j: (l, 0, j)),
        ],
        out_specs=pl.BlockSpec((1, rows, tn), lambda l, j: (l, 0, j)),
        out_shape=jax.ShapeDtypeStruct((depth, rows, n6), F32),
        compiler_params=_params(("parallel", "parallel")),
        name="adaln",
    )(cp, w_ada, b_ada.reshape(depth, 1, n6))
    return out[:, :bsz].reshape(depth, bsz, 6, d)


def _in_kernel(*refs, n_moe):
    moe, (x_ref, mod_ref, g_ref, w_ref), rest = refs[:n_moe], refs[n_moe:n_moe + 4], refs[n_moe + 4:]
    hb = rest[-1]
    i = pl.program_id(0)

    @pl.when(i == 0)
    def _():
        hb[1] = jnp.zeros(hb.shape[1:], hb.dtype)

    outs = rest[1:-1] if n_moe else rest[:-1]
    mod = mod_ref[0]
    prev = hb[(i + 1) % 2]
    width = outs[0].shape[1]
    tm = x_ref.shape[0]
    groups = [slice(r, r + tm // IN_GROUPS) for r in range(0, tm, tm // IN_GROUPS)]

    def prologue(rs):
        if n_moe:
            ga0, ga1, gb0, gb1, gt_ref, pmod_ref = moe
            gt = gt_ref[rs, :]
            y = gt[:, 0:1] * _unpack_rows(ga0[rs, :], gb0[rs, :]) + gt[:, 1:2] * _unpack_rows(ga1[rs, :], gb1[rs, :])
            x = x_ref[rs, :] + pmod_ref[0][5:6, :] * y
            rest[0][rs, :] = x
        else:
            x = x_ref[rs, :]
        hb[i % 2, rs, :] = (_rms(x, g_ref[...]) * (1.0 + mod[1:2, :]) + mod[0:1, :]).astype(BF16)

    per = -(-len(groups) // len(outs))
    for k, o in enumerate(outs):
        for rs in groups[k * per:(k + 1) * per]:
            prologue(rs)
        w = w_ref[0, :, k * width:(k + 1) * width].astype(BF16)
        o[...] = jnp.dot(prev, w, preferred_element_type=F32).astype(o.dtype)


def _in_proj(x2, mod_l, g, w_in, layer, seq, moe=None):
    n, d = x2.shape
    n_out = w_in.shape[2]
    width = 512
    tm = ROW_TILE
    per_batch = seq // tm
    last = n // tm - 1
    cur = lambda i: jnp.minimum(i, last)
    row = lambda w: pl.BlockSpec((tm, w), lambda i: (cur(i), 0))
    mod_spec = pl.BlockSpec((1, 6, d), lambda i: (cur(i) // per_batch, 0, 0))
    in_specs, args, out_specs, out_shape = [], [], [], []
    if moe is not None:
        ga, gb, gates, mod_prev = moe
        second = pl.BlockSpec((tm, PACK_WIDTH), lambda i: (cur(i) + n // tm, 0))
        in_specs += [row(PACK_WIDTH), second, row(PACK_WIDTH), second, row(LANES), mod_spec]
        args += [ga, ga, gb, gb, gates, mod_prev]
        out_specs.append(row(d))
        out_shape.append(jax.ShapeDtypeStruct((n, d), F32))
    in_specs += [row(d), mod_spec, pl.BlockSpec((1, d), lambda i: (0, 0)),
                 pl.BlockSpec((1, d, n_out), lambda i: (layer, 0, 0), pipeline_mode=pl.Buffered(1))]
    args += [x2, mod_l, g.reshape(1, d), w_in]
    out_dtypes = (F32, F32, F32, BF16, F32, BF16, F32)
    out_specs += [pl.BlockSpec((tm, width), lambda i: (jnp.maximum(i - 1, 0), 0))] * len(out_dtypes)
    out_shape += [jax.ShapeDtypeStruct((n, width), dt) for dt in out_dtypes]
    outs = pl.pallas_call(
        functools.partial(_in_kernel, n_moe=0 if moe is None else 6),
        grid=(n // tm + 1,),
        in_specs=in_specs,
        out_specs=out_specs,
        out_shape=out_shape,
        scratch_shapes=[pltpu.VMEM((2, tm, d), BF16)],
        compiler_params=_params(("arbitrary",)),
        name="in_proj",
    )(*args)
    return (x2, outs) if moe is None else (outs[0], outs[1:])


def _attn_bias(n_heads):
    qi = np.arange(ATTN_BLOCK)[:, None]
    kj = np.arange(2 * ATTN_BLOCK)[None, :]
    dist = ATTN_BLOCK + qi - kj
    valid = (dist >= 0) & (dist <= ATTN_BLOCK)
    slopes = np.exp2(-ALIBI_MAX_EXP * np.arange(1, n_heads + 1, dtype=np.float32) / n_heads)
    out = np.empty((2, len(DILATIONS), n_heads, ATTN_BLOCK, 2 * ATTN_BLOCK), np.float32)
    for p, dil in enumerate(DILATIONS):
        b = -slopes[:, None, None] * (dist * dil).astype(np.float32)[None] * np.float32(LOG2E)
        out[0, p] = np.where(valid[None], b, -np.inf)
        out[1, p] = np.where((valid & (kj >= ATTN_BLOCK))[None], b, -np.inf)
    return jnp.asarray(out)


def _attn_kernel(q_ref, k_ref, v_ref, bias_ref, o_ref, kcar, vcar, pbuf, vbuf, oscr, mscr, dscr):
    n = pl.program_id(2)
    sb = q_ref.shape[0]
    is_lo = lax.broadcasted_iota(jnp.int32, (ATTN_BLOCK, LANES), 1) < ATTN_HEAD_DIM
    scale = ATTN_HEAD_DIM ** -0.5 * LOG2E

    @pl.when(n == 0)
    def _():
        kcar[...] = jnp.zeros_like(kcar)
        vcar[...] = jnp.zeros_like(vcar)

    pbuf[...] = jnp.zeros_like(pbuf)
    vbuf[...] = jnp.zeros_like(vbuf)
    lo2 = lax.broadcasted_iota(jnp.int32, (2 * ATTN_BLOCK, LANES), 1) < ATTN_HEAD_DIM
    zero = jnp.zeros((2 * ATTN_BLOCK, LANES), BF16)
    den_lo = lo2.astype(BF16)
    den_hi = jnp.logical_not(lo2).astype(BF16)

    for p, dil in enumerate(DILATIONS):
        nb = (sb // ATTN_BLOCK) // dil

        def rows_of(it, dil=dil, nb=nb):
            start = (it % nb) * (ATTN_BLOCK * dil) + it // nb
            return pl.ds(start, ATTN_BLOCK, stride=dil) if dil > 1 else pl.ds(pl.multiple_of(start, ATTN_BLOCK), ATTN_BLOCK)

        def pv(it, probs, vaug, p=p, rows_of=rows_of):
            rows = rows_of(it)
            of = jnp.dot(probs, vaug, preferred_element_type=F32)
            oscr[p, rows, :] = of[:, :LANES]
            dscr[p, rows, :] = of[:, LANES:]

        def qk(it, p=p, nb=nb, rows_of=rows_of):
            r = it // nb
            rows = rows_of(it)
            crow = pl.ds(pl.multiple_of(r * ATTN_BLOCK, ATTN_BLOCK), ATTN_BLOCK)
            qb = q_ref[rows, :] * scale
            kc = k_ref[rows, :].astype(BF16)
            vc = v_ref[rows, :].astype(BF16)
            kcat = jnp.concatenate([kcar[p, crow, :], kc], axis=0)
            vcat = jnp.concatenate([vcar[p, crow, :], vc], axis=0)
            vaug = jnp.concatenate([jnp.concatenate([jnp.where(lo2, vcat, zero), den_lo], axis=1),
                                    jnp.concatenate([jnp.where(lo2, zero, vcat), den_hi], axis=1)], axis=0)
            kcar[p, crow, :] = kc
            vcar[p, crow, :] = vc
            first = jnp.logical_and(n == 0, it % nb == 0).astype(jnp.int32)
            probs, acc_m = [], None
            for hh in range(2):
                sel = is_lo if hh == 0 else jnp.logical_not(is_lo)
                qm = jnp.where(sel, qb, 0.0).astype(BF16)
                s = _dot_nt(qm, kcat) + bias_ref[first, p, hh]
                m = jnp.max(s, axis=-1, keepdims=True)
                probs.append(jnp.exp2(s - m).astype(BF16))
                mb = jnp.broadcast_to(m, (ATTN_BLOCK, LANES))
                acc_m = mb if hh == 0 else jnp.where(sel, mb, acc_m)
            mscr[p, rows, :] = acc_m
            return jnp.concatenate(probs, axis=1), vaug

        def body(g, carry, pv=pv, qk=qk):
            it0 = g * ATTN_UNROLL
            pend = (pbuf[...], vbuf[...])
            for u in range(ATTN_UNROLL):
                cur = qk(it0 + u)
                pv(jnp.maximum(it0 + u - 1, 0), *pend)
                pend = cur
            pbuf[...] = pend[0]
            vbuf[...] = pend[1]
            return carry

        n_it = sb // ATTN_BLOCK
        lax.fori_loop(0, n_it // ATTN_UNROLL, body, 0)
        pv(n_it - 1, pbuf[...], vbuf[...])

    ct = 256

    def combine(t, carry):
        rs = pl.ds(pl.multiple_of(t * ct, ct), ct)
        m0, m1, m2 = mscr[0, rs, :], mscr[1, rs, :], mscr[2, rs, :]
        mx = jnp.maximum(jnp.maximum(m0, m1), m2)
        w0, w1, w2 = jnp.exp2(m0 - mx), jnp.exp2(m1 - mx), jnp.exp2(m2 - mx)
        num = w0 * oscr[0, rs, :] + w1 * oscr[1, rs, :] + w2 * oscr[2, rs, :]
        den = w0 * dscr[0, rs, :] + w1 * dscr[1, rs, :] + w2 * dscr[2, rs, :]
        o_ref[rs, :] = (num / den).astype(o_ref.dtype)
        return carry

    lax.fori_loop(0, sb // ct, combine, 0)


def _attention(aq, ak, av, bias, bsz, seq):
    n, width = aq.shape
    sb = ATTN_SUPER
    nsb = seq // sb
    pairs = width // LANES
    npat = len(DILATIONS)
    blk = pl.BlockSpec((sb, LANES), lambda b, h, t: (b * nsb + t, h))
    return pl.pallas_call(
        _attn_kernel,
        grid=(bsz, pairs, nsb),
        in_specs=[blk, blk, blk,
                  pl.BlockSpec((2, npat, 2, ATTN_BLOCK, 2 * ATTN_BLOCK), lambda b, h, t: (0, 0, h, 0, 0))],
        out_specs=blk,
        out_shape=jax.ShapeDtypeStruct((n, width), BF16),
        scratch_shapes=[
            pltpu.VMEM((npat, max(DILATIONS) * ATTN_BLOCK, LANES), BF16),
            pltpu.VMEM((npat, max(DILATIONS) * ATTN_BLOCK, LANES), BF16),
            pltpu.VMEM((ATTN_BLOCK, 4 * ATTN_BLOCK), BF16),
            pltpu.VMEM((4 * ATTN_BLOCK, 2 * LANES), BF16),
            pltpu.VMEM((npat, sb, LANES), F32),
            pltpu.VMEM((npat, sb, LANES), F32),
            pltpu.VMEM((npat, sb, LANES), F32),
        ],
        compiler_params=_params(("parallel", "parallel", "arbitrary")),
        name="dilated_attn",
    )(aq, ak, av, bias)


def _hgrn_consts():
    c = HGRN_CHUNK
    t = np.arange(c)[:, None]
    u = np.arange(c)[None, :]
    low, masks = [], []
    m = c // 2
    while m >= 1:
        mid = (t // (2 * m)) * (2 * m) + m
        is_q = t >= mid
        if m in (4, 2):
            low.append((is_q & (u >= mid) & (u <= t)) | (~is_q & (u > t) & (u < mid)))
        masks.append(((t // (2 * m)) == (u // (2 * m))) & (t % (2 * m) >= m) & (u % (2 * m) < m))
        m //= 2
    masks.append(t == u)
    f = lambda a: np.asarray(a, np.float32)
    return (jnp.asarray(f(u <= t), BF16), jnp.asarray(f(np.concatenate(low, axis=0)), BF16),
            jnp.asarray(f(np.broadcast_to(t % 2 == 1, (c, c)))), jnp.asarray(f(np.stack(masks))))


def _hgrn_kernel(q_ref, z_ref, v_ref, g_ref, lb_ref, gn_ref, tri_ref, low_ref, odd_ref, msk_ref, o_ref, st):
    c = HGRN_CHUNK

    @pl.when(pl.program_id(2) == 0)
    def _():
        st[...] = jnp.zeros_like(st)

    lb = jnp.maximum(lb_ref[...], 0.0)
    log_lb = jnp.log(lb)
    log_1m = jnp.log1p(-lb)
    one_m = 1.0 - lb
    gn = gn_ref[...]
    n_levels = msk_ref.shape[0] - 1
    coarse = [c >> (i + 1) for i in range(n_levels) if (c >> (i + 1)) >= 8]

    def chunk(ci):
        rs = pl.ds(pl.multiple_of(ci * c, c), c)
        z = z_ref[rs, :]
        qb = q_ref[rs, :]
        vb = v_ref[rs, :]
        ez = jnp.exp(-jnp.abs(z))
        ls = jnp.minimum(z, 0.0) - jnp.log(1.0 + ez)
        b2 = log_1m + ls
        lf = jnp.maximum(log_lb, b2) + jnp.log(1.0 + jnp.exp(-jnp.abs(log_lb - b2)))
        rz = 1.0 / (1.0 + ez)
        kb = (one_m * jnp.where(z > 0.0, ez * rz, rz)).astype(BF16)
        l2 = lf * LOG2E
        hi = l2.astype(BF16)
        hl = jnp.concatenate([hi, (l2 - hi.astype(F32)).astype(BF16)], axis=1)
        yield
        bb = jnp.dot(tri_ref[...], hl, preferred_element_type=F32)
        b = bb[:, :HGRN_EXPAND] + bb[:, HGRN_EXPAND:]
        dl = jnp.dot(low_ref[...], hl, preferred_element_type=F32)
        dl = dl[:, :HGRN_EXPAND] + dl[:, HGRN_EXPAND:]
        yield

        def level(d):
            e = jnp.exp2(d).astype(BF16)
            return _dot_nt(qb * e, kb * e)

        a = msk_ref[n_levels] * _dot_nt(qb, kb)
        for i in range(n_levels):
            m = c >> (i + 1)
            if m >= 8:
                parts = []
                for r0 in range(0, c, 2 * m):
                    mid = b[r0 + m - 1:r0 + m, :]
                    parts += [mid - b[r0:r0 + m, :], b[r0 + m:r0 + 2 * m, :] - mid]
                d = jnp.concatenate(parts, axis=0)
            elif m > 1:
                d = dl[(i - len(coarse)) * c:(i - len(coarse) + 1) * c]
            else:
                d = l2 * odd_ref[...]
            a = a + msk_ref[i] * level(d)
            if i % 3 == 2:
                yield
        b_last = b[c - 1:c, :]
        s_t = st[...]
        o = _dot_nt(qb * jnp.exp2(b).astype(BF16), s_t.astype(BF16)) + jnp.dot(a.astype(BF16), vb, preferred_element_type=F32)
        st[...] = jnp.exp2(b_last) * s_t + _dot_tn(vb, kb * jnp.exp2(b_last - b).astype(BF16))
        yield
        g = g_ref[rs, :]
        o_ref[rs, :] = (_rms(o, gn) * (g * jax.nn.sigmoid(g))).astype(o_ref.dtype)

    def group(gi, carry):
        _round_robin([chunk(gi * HGRN_UNROLL + u) for u in range(HGRN_UNROLL)])
        return carry

    _loop(q_ref.shape[0] // (c * HGRN_UNROLL), group)


def _hgrn(rq, rf, ri, rg, lb, gn, consts, bsz, seq):
    n, width = rq.shape
    heads = width // HGRN_EXPAND
    ts = HGRN_TILE
    nt = seq // ts
    blk = pl.BlockSpec((ts, HGRN_EXPAND), lambda b, h, t: (b * nt + t, h))
    vec = pl.BlockSpec((1, HGRN_EXPAND), lambda b, h, t: (0, h))
    const = lambda a: pl.BlockSpec(a.shape, lambda b, h, t: (0,) * a.ndim)
    return pl.pallas_call(
        _hgrn_kernel,
        grid=(bsz, heads, nt),
        in_specs=[blk, blk, blk, blk, vec, vec] + [const(a) for a in consts],
        out_specs=blk,
        out_shape=jax.ShapeDtypeStruct((n, width), BF16),
        scratch_shapes=[pltpu.VMEM((HGRN_EXPAND, HGRN_EXPAND), F32)],
        compiler_params=_params(("parallel", "parallel", "arbitrary")),
        name="hgrn2",
    )(rq, rf, ri, rg, lb.reshape(1, width), gn.reshape(1, width), *consts)


def _route_math(lg):
    row = lax.broadcasted_iota(jnp.int32, (EXPERTS_PER_GROUP, lg.shape[1]), 0).astype(F32)
    none = float(EXPERTS_PER_GROUP)

    def first_argmax(vals):
        mx = jnp.max(vals, axis=0, keepdims=True)
        return mx, jnp.min(jnp.where(vals == mx, row, none), axis=0, keepdims=True)

    g = lg[0:EXPERTS_PER_GROUP, :]
    is_group = row < N_GROUPS
    gmax, gidx = first_argmax(jnp.where(is_group, g, -jnp.inf))
    gprob = 1.0 / jnp.sum(jnp.where(is_group, jnp.exp(g - gmax), 0.0), axis=0, keepdims=True)
    el = lg[EXPERTS_PER_GROUP:2 * EXPERTS_PER_GROUP, :]
    for k in range(1, N_GROUPS):
        el = jnp.where(gidx == k, lg[(k + 1) * EXPERTS_PER_GROUP:(k + 2) * EXPERTS_PER_GROUP, :], el)
    t1, i1 = first_argmax(el)
    t2, i2 = first_argmax(jnp.where(row == i1, -jnp.inf, el))
    ex = jnp.exp(t2 - t1)
    base = gidx * EXPERTS_PER_GROUP
    eids = jnp.concatenate([base + i1, base + i2], axis=0).astype(jnp.int32)
    gates = jnp.concatenate([gprob / (1.0 + ex), gprob * ex / (1.0 + ex)], axis=0)
    return eids, gates


def _out_kernel(a_ref, r_ref, x_ref, mod_ref, ag_ref, g2_ref, wo_ref, wr_ref, br_ref, xo_ref, ha_ref, hb_ref, e_ref, gt_ref, cnt_ref):
    mod = mod_ref[0]
    mix = jnp.concatenate([_rms(a_ref[...].astype(F32), ag_ref[...]).astype(BF16), r_ref[...]], axis=1)
    xn = x_ref[...] + mod[2:3, :] * jnp.dot(mix, wo_ref[0].astype(BF16), preferred_element_type=F32)
    xo_ref[...] = xn
    h2 = _rms(xn, g2_ref[...]) * (1.0 + mod[4:5, :]) + mod[3:4, :]
    ha_ref[...], hb_ref[...] = _pack_rows(h2)
    eids, gates = _route_math(_dot_nt(wr_ref[...], h2.astype(BF16)) + br_ref[...])
    e_ref[...] = eids
    eio = lax.broadcasted_iota(jnp.int32, (N_EXPERTS, eids.shape[1]), 0)
    hit = jnp.logical_or(eids[0:1, :] == eio, eids[1:2, :] == eio).astype(F32)

    @pl.when(pl.program_id(0) == 0)
    def _():
        cnt_ref[...] = jnp.zeros_like(cnt_ref)

    cnt_ref[...] += jnp.sum(hit, axis=1, keepdims=True)
    pad = jnp.zeros((LANES - TOP_K, LANES), F32)
    for k in range(h2.shape[0] // LANES):
        tile = jnp.concatenate([gates[:, k * LANES:(k + 1) * LANES], pad], axis=0)
        gt_ref[k * LANES:(k + 1) * LANES, :] = tile.T


def _out_proj(attn, rec, x2, mod_l, ag, g2, w_out, layer, wr_bf, br, seq):
    n, d = x2.shape
    half = attn.shape[1]
    tm = OUT_TILE
    per_batch = seq // tm
    row = lambda w: pl.BlockSpec((tm, w), lambda i: (i, 0))
    full = lambda a: pl.BlockSpec(a.shape, lambda i: (0,) * a.ndim, pipeline_mode=pl.Buffered(1))
    ag2, g22 = ag.reshape(1, half), g2.reshape(1, d)
    return pl.pallas_call(
        _out_kernel,
        grid=(n // tm,),
        in_specs=[row(half), row(half), row(d), pl.BlockSpec((1, 6, d), lambda i: (i // per_batch, 0, 0)),
                  full(ag2), full(g22), pl.BlockSpec((1, d, d), lambda i: (layer, 0, 0), pipeline_mode=pl.Buffered(1)),
                  full(wr_bf), full(br)],
        out_specs=[row(d), row(PACK_WIDTH), row(PACK_WIDTH), pl.BlockSpec((TOP_K, tm), lambda i: (0, i)), row(LANES),
                   pl.BlockSpec((N_EXPERTS, LANES), lambda i: (0, 0))],
        out_shape=[jax.ShapeDtypeStruct((n, d), F32), jax.ShapeDtypeStruct((n, PACK_WIDTH), U32),
                   jax.ShapeDtypeStruct((n, PACK_WIDTH), U32), jax.ShapeDtypeStruct((TOP_K, n), jnp.int32),
                   jax.ShapeDtypeStruct((n, LANES), F32), jax.ShapeDtypeStruct((N_EXPERTS, LANES), F32)],
        compiler_params=_params(("arbitrary",)),
        name="out_proj",
    )(attn, rec, x2, mod_l, ag2, g22, w_out, wr_bf, br)


def _plan_kernel(e_ref, cnt_ref, ut_ref, pos_ref, be_ref, bv_ref, bf_ref, bn_ref, nu_ref, off, run):
    i = pl.program_id(0)
    t = e_ref.shape[1]
    eio = lax.broadcasted_iota(jnp.int32, (N_EXPERTS, t), 0)
    oh1 = e_ref[0:1, :] == eio
    oh2 = e_ref[1:2, :] == eio
    oh = jnp.logical_or(oh1, oh2).astype(F32)

    @pl.when(i == 0)
    def _():
        counts = cnt_ref[...]
        nb = jnp.floor((counts + (MOE_TILE - 1)) * (1.0 / MOE_TILE))
        er = lax.broadcasted_iota(jnp.int32, (N_EXPERTS, N_EXPERTS), 0)
        ec = lax.broadcasted_iota(jnp.int32, (N_EXPERTS, N_EXPERTS), 1)
        cum_excl = jnp.dot((ec < er).astype(BF16), nb.astype(BF16), preferred_element_type=F32)
        off[...] = cum_excl * MOE_TILE
        run[...] = jnp.zeros_like(run)
        ce, ci, cn = cum_excl[:, 0:1], (cum_excl + nb)[:, 0:1], counts[:, 0:1]
        b = lax.broadcasted_iota(jnp.int32, (N_EXPERTS, be_ref.shape[1]), 1).astype(F32)
        member = jnp.logical_and(b >= ce, b < ci)
        be = jnp.minimum(jnp.sum((b >= ci).astype(F32), axis=0, keepdims=True), N_EXPERTS - 1.0)
        bv = jnp.sum(jnp.where(member, jnp.minimum(cn - (b - ce) * MOE_TILE, float(MOE_TILE)), 0.0), axis=0, keepdims=True)
        be_ref[...] = be.astype(jnp.int32)
        bv_ref[...] = bv.astype(jnp.int32)
        has = nb[:, 0:1] > 0.0
        eid = lax.broadcasted_iota(jnp.int32, b.shape, 0).astype(F32)
        bf_ref[...] = jnp.sum(jnp.where(jnp.logical_and(b == ce, has), 1.0, 0.0), axis=0, keepdims=True).astype(jnp.int32)
        nxt = jnp.min(jnp.where(jnp.logical_and(eid > be, has), eid, float(N_EXPERTS)), axis=0, keepdims=True)
        bn_ref[...] = jnp.where(nxt < N_EXPERTS, nxt, -1.0).astype(jnp.int32)
        nu_ref[...] = jnp.broadcast_to(ci[N_EXPERTS - 1:N_EXPERTS, :], nu_ref.shape).astype(jnp.int32)

    span = ut_ref.shape[0]
    ohb = oh.astype(BF16)
    base = off[:, 0:1] + run[:, 0:1] - 1.0
    parts = []
    for s in range(0, t, span):
        c = jnp.dot(ohb[:, s:s + span], ut_ref[...], preferred_element_type=F32)
        parts.append(base + c)
        base = base + c[:, span - 1:span]
    r = jnp.concatenate(parts, axis=1)
    pos_ref[0:1, :] = jnp.sum(jnp.where(oh1, r, 0.0), axis=0, keepdims=True).astype(jnp.int32)
    pos_ref[1:2, :] = jnp.sum(jnp.where(oh2, r, 0.0), axis=0, keepdims=True).astype(jnp.int32)
    run[...] += jnp.sum(oh, axis=1, keepdims=True)


def _plan(expert_ids_t, counts, n_rows):
    n = expert_ids_t.shape[1]
    t = PLAN_TILE
    n_blocks = n_rows // MOE_TILE
    assert n_blocks <= PLAN_LANES and TOP_K == 2
    ut = jnp.asarray(np.triu(np.ones((PLAN_SPAN, PLAN_SPAN), np.float32)), BF16)
    tab = jax.ShapeDtypeStruct((1, PLAN_LANES), jnp.int32)
    const = lambda shape: pl.BlockSpec(shape, lambda i: (0, 0))
    pos, be, bv, bf, bn, nu = pl.pallas_call(
        _plan_kernel,
        grid=(n // t,),
        in_specs=[pl.BlockSpec((TOP_K, t), lambda i: (0, i)), const((N_EXPERTS, LANES)), const((PLAN_SPAN, PLAN_SPAN))],
        out_specs=[pl.BlockSpec((TOP_K, t), lambda i: (0, i))] + [const((1, PLAN_LANES))] * 4 + [const((1, LANES))],
        out_shape=[jax.ShapeDtypeStruct((TOP_K, n), jnp.int32), tab, tab, tab, tab, jax.ShapeDtypeStruct((1, LANES), jnp.int32)],
        scratch_shapes=[pltpu.VMEM((N_EXPERTS, LANES), F32)] * 2,
        compiler_params=_params(("arbitrary",)),
        name="plan",
    )(expert_ids_t, counts, ut)
    return pos, (be.reshape(-1), bv.reshape(-1), bf.reshape(-1), bn.reshape(-1), nu.reshape(-1)[:1])


def _sc_mesh():
    return plsc.VectorSubcoreMesh(core_axis_name="c", subcore_axis_name="s")


def _sc_scatter_rows(x, pos, n_rows):
    n, d = x.shape
    w = SC_WINDOW

    @functools.partial(pl.kernel, out_type=jax.ShapeDtypeStruct((n_rows, d), x.dtype), mesh=_sc_mesh(), scratch_types=[])
    def scatter(x_hbm, p_hbm, o_hbm):
        def body(x_vmem, i_vmem):
            for k in range(TOP_K):
                pltpu.sync_copy(x_vmem, o_hbm.at[i_vmem.at[k]])

        pltpu.emit_pipeline(
            body,
            grid=(n // w,),
            in_specs=[pl.BlockSpec((w, d), lambda i: (i, 0)), pl.BlockSpec((TOP_K, w), lambda i: (0, i))],
            out_specs=[],
            core_axis_name=("c", "s"),
            dimension_semantics=(pltpu.PARALLEL,),
        )(x_hbm, p_hbm)

    return scatter(x, pos)


def _sc_gather_rows(y, pos):
    m = pos.shape[0] * pos.shape[1]
    d = y.shape[1]
    w = SC_WINDOW

    @functools.partial(pl.kernel, out_type=jax.ShapeDtypeStruct((m, d), y.dtype), mesh=_sc_mesh(), scratch_types=[])
    def gather(y_hbm, p_hbm, o_hbm):
        def body(i_vmem, o_vmem):
            pltpu.sync_copy(y_hbm.at[i_vmem.at[0]], o_vmem)

        pltpu.emit_pipeline(
            body,
            grid=(m // w,),
            in_specs=[pl.BlockSpec((1, w), lambda i: (0, i))],
            out_specs=[pl.BlockSpec((w, d), lambda i: (i, 0))],
            core_axis_name=("c", "s"),
            dimension_semantics=(pltpu.PARALLEL,),
        )(p_hbm, o_hbm)

    return gather(y, pos.reshape(1, m))


def _expert_kernel(be_ref, bv_ref, bf_ref, bn_ref, nu_ref, xa_ref, xb_ref, w1_hbm, w3_hbm, w2_hbm, ya_ref, yb_ref,
                   wf1, wf3, wf2, w1b, w3b, w2b, wsem, par, *, layer):
    i = pl.program_id(0)
    tb = xa_ref.shape[0]
    half = tb // 2

    def fetch(e, slot):
        return (pltpu.make_async_copy(w1_hbm.at[layer, e], wf1.at[slot], wsem.at[slot, 0]),
                pltpu.make_async_copy(w3_hbm.at[layer, e], wf3.at[slot], wsem.at[slot, 1]),
                pltpu.make_async_copy(w2_hbm.at[layer, e], wf2.at[slot], wsem.at[slot, 2]))

    def ffn(rows):
        live = lax.broadcasted_iota(jnp.int32, (rows, 1), 0) < bv_ref[i]
        x = jnp.where(live, _unpack_rows(xa_ref[0:rows, :], xb_ref[0:rows, :]), 0.0).astype(BF16)
        y = None
        hc = w1b.shape[1] // FFN_SPLIT
        for c in range(FFN_SPLIT):
            cs = slice(c * hc, (c + 1) * hc)
            a = jnp.dot(x, w1b[:, cs], preferred_element_type=F32)
            b = jnp.dot(x, w3b[:, cs], preferred_element_type=F32)
            hm = (a * jax.nn.sigmoid(a) * b).astype(BF16)
            part = jnp.dot(hm, w2b[cs, :], preferred_element_type=F32)
            y = part if y is None else y + part
        ya_ref[0:rows, :], yb_ref[0:rows, :] = _pack_rows(y)

    @pl.when(i == 0)
    def _():
        par[0] = 0
        for c in fetch(be_ref[0], 0):
            c.start()

    @pl.when(i < nu_ref[0])
    def _():
        @pl.when(bf_ref[i] == 1)
        def _():
            slot = par[0]
            for c in fetch(be_ref[i], slot):
                c.wait()
            w1b[...] = wf1[slot].astype(BF16)
            w3b[...] = wf3[slot].astype(BF16)
            w2b[...] = wf2[slot].astype(BF16)

            @pl.when(bn_ref[i] >= 0)
            def _():
                for c in fetch(bn_ref[i], 1 - slot):
                    c.start()

            par[0] = 1 - slot

        @pl.when(bv_ref[i] > half)
        def _():
            ffn(tb)

        @pl.when(bv_ref[i] <= half)
        def _():
            ffn(half)
            ya_ref[half:, :] = jnp.zeros((tb - half, ya_ref.shape[1]), ya_ref.dtype)
            yb_ref[half:, :] = jnp.zeros((tb - half, yb_ref.shape[1]), yb_ref.dtype)

    @pl.when(i >= nu_ref[0])
    def _():
        ya_ref[...] = jnp.zeros_like(ya_ref)
        yb_ref[...] = jnp.zeros_like(yb_ref)


def _experts(xa, xb, w1, w3, w2, layer, tables):
    n_rows = xa.shape[0]
    d, de = w1.shape[-2:]
    tb = MOE_TILE
    xspec = pl.BlockSpec((tb, PACK_WIDTH), lambda i, be, bv, bf, bn, nu: (jnp.minimum(i, nu[0] - 1), 0))
    yspec = pl.BlockSpec((tb, PACK_WIDTH), lambda i, be, bv, bf, bn, nu: (i, 0))
    anyspec = pl.BlockSpec(memory_space=pl.ANY)
    grid_spec = pltpu.PrefetchScalarGridSpec(
        num_scalar_prefetch=5,
        grid=(n_rows // tb,),
        in_specs=[xspec, xspec, anyspec, anyspec, anyspec],
        out_specs=[yspec, yspec],
        scratch_shapes=[pltpu.VMEM((2, d, de), F32), pltpu.VMEM((2, d, de), F32), pltpu.VMEM((2, de, d), F32),
                        pltpu.VMEM((d, de), BF16), pltpu.VMEM((d, de), BF16), pltpu.VMEM((de, d), BF16),
                        pltpu.SemaphoreType.DMA((2, 3)), pltpu.SMEM((1,), jnp.int32)],
    )
    rows = jax.ShapeDtypeStruct((n_rows, PACK_WIDTH), U32)
    return pl.pallas_call(
        functools.partial(_expert_kernel, layer=layer),
        grid_spec=grid_spec,
        out_shape=[rows, rows],
        compiler_params=_params(("arbitrary",)),
        name="experts",
    )(*tables, xa, xb, w1, w3, w2)


def _final_kernel(ga0, ga1, gb0, gb1, gt_ref, mod_ref, x_ref, fg_ref, o_ref):
    o_ref[...] = _rms(_moe_residual(x_ref, ga0, ga1, gb0, gb1, gt_ref, mod_ref[0][5:6, :]), fg_ref[...])


def _final(x2, ga, gb, gates, mod_l, final_g, seq):
    n, d = x2.shape
    tm = OUT_TILE
    per_batch = seq // tm
    row = lambda w: pl.BlockSpec((tm, w), lambda i: (i, 0))
    second = pl.BlockSpec((tm, PACK_WIDTH), lambda i: (i + n // tm, 0))
    return pl.pallas_call(
        _final_kernel,
        grid=(n // tm,),
        in_specs=[row(PACK_WIDTH), second, row(PACK_WIDTH), second, row(LANES),
                  pl.BlockSpec((1, 6, d), lambda i: (i // per_batch, 0, 0)), row(d), pl.BlockSpec((1, d), lambda i: (0, 0))],
        out_specs=row(d),
        out_shape=jax.ShapeDtypeStruct((n, d), F32),
        compiler_params=_params(("parallel",)),
        name="final_norm",
    )(ga, ga, gb, gb, gates, mod_l, x2, final_g.reshape(1, d))


def kernel(x, c, w_ada, b_ada, norm1_g, w_in, attn_norm_g, hgrn_lb_logits, hgrn_norm_g, w_out, norm2_g, router_group_w, router_group_b, router_expert_w, router_expert_b, moe_w1, moe_w3, moe_w2, final_g):
    bsz, seq, d = x.shape
    depth = w_ada.shape[0]
    n = bsz * seq
    attn_dim = attn_norm_g.shape[1]
    n_heads = attn_dim // ATTN_HEAD_DIM

    lb_w = jax.nn.softmax(hgrn_lb_logits.astype(F32), axis=0)
    lower_bounds = jnp.cumsum(lb_w, axis=0) - lb_w[0:1]
    mod = _adaln(c, w_ada, b_ada)
    bias = _attn_bias(n_heads)
    hconsts = _hgrn_consts()
    n_rows = n * TOP_K + N_EXPERTS * MOE_TILE

    x2 = x.reshape(n, d)
    moe = None
    for layer in range(depth):
        x2, (aq, ak, av, rq, rf, ri, rg) = _in_proj(x2, mod[layer], norm1_g[layer], w_in, layer, seq, moe)
        attn = _attention(aq, ak, av, bias, bsz, seq)
        rec = _hgrn(rq, rf, ri, rg, lower_bounds[layer], hgrn_norm_g[layer], hconsts, bsz, seq)
        gpad = EXPERTS_PER_GROUP - N_GROUPS
        wr = jnp.concatenate([router_group_w[layer].T, jnp.zeros((gpad, d), F32), router_expert_w[layer].T], axis=0)
        br = jnp.concatenate([router_group_b[layer], jnp.zeros((gpad,), F32), router_expert_b[layer]]).reshape(-1, 1)
        x2, ha, hb, eids, gates, counts = _out_proj(attn, rec, x2, mod[layer], attn_norm_g[layer], norm2_g[layer],
                                            w_out, layer, wr.astype(BF16), br, seq)
        pos, tables = _plan(eids, counts, n_rows)
        ya, yb = _experts(_sc_scatter_rows(ha, pos, n_rows), _sc_scatter_rows(hb, pos, n_rows),
                          moe_w1, moe_w3, moe_w2, layer, tables)
        moe = (_sc_gather_rows(ya, pos), _sc_gather_rows(yb, pos), gates, mod[layer])
    return _final(x2, moe[0], moe[1], moe[2], moe[3], final_g, seq).reshape(bsz, seq, d)
```

```python
import functools

import numpy as np
import jax
import jax.numpy as jnp
from jax import lax
from jax.experimental import pallas as pl
from jax.experimental.pallas import tpu as pltpu
from jax.experimental.pallas import tpu_sc as plsc

F32 = jnp.float32
BF16 = jnp.bfloat16
U32 = jnp.uint32

ATTN_HEAD_DIM = 64
ATTN_BLOCK = 128
DILATIONS = (1, 4, 16)
ALIBI_MAX_EXP = 8.0
HGRN_EXPAND = 128
HGRN_CHUNK = 128
N_GROUPS = 4
EXPERTS_PER_GROUP = 8
N_EXPERTS = N_GROUPS * EXPERTS_PER_GROUP
TOP_K = 2
NORM_EPS = 1e-6
LOG2E = 1.4426950408889634

LANES = 128
ATTN_SUPER = ATTN_BLOCK * max(DILATIONS)
ROW_TILE = 512
PROJ_WIDTH = 512
OUT_TILE = 1024
IN_GROUPS = 16
ATTN_UNROLL = 16
HGRN_TILE = 2048
HGRN_UNROLL = 16
MOE_TILE = 512
FFN_SPLIT = 2
PLAN_TILE = 4096
PLAN_SPAN = 256
PLAN_LANES = 256
PACK_WIDTH = 256
SC_WINDOW = 128
VMEM_LIMIT = 56 << 20


def _params(semantics):
    return pltpu.CompilerParams(dimension_semantics=semantics, vmem_limit_bytes=VMEM_LIMIT)


def _rms(x, g):
    return x * lax.rsqrt(jnp.mean(x * x, axis=-1, keepdims=True) + NORM_EPS) * g


def _loop(trips, body):
    if trips == 1:
        body(0, 0)
    else:
        lax.fori_loop(0, trips, body, 0)


def _round_robin(gens):
    live = list(gens)
    while live:
        for gen in list(live):
            try:
                next(gen)
            except StopIteration:
                live.remove(gen)


def _dot_nt(a, b):
    return lax.dot_general(a, b, (((1,), (1,)), ((), ())), preferred_element_type=F32)


def _dot_tn(a, b):
    return lax.dot_general(a, b, (((0,), (0,)), ((), ())), preferred_element_type=F32)


def _pack_pair(hi, lo):
    hb = lax.bitcast_convert_type(hi.astype(BF16).astype(F32), U32)
    lb = lax.bitcast_convert_type(lo.astype(BF16).astype(F32), U32)
    return hb | (lb >> 16)


def _unpack_pair(u):
    hi = lax.bitcast_convert_type(u & jnp.uint32(0xFFFF0000), F32)
    lo = lax.bitcast_convert_type(u << 16, F32)
    return hi, lo


def _pack_rows(y):
    w = PACK_WIDTH
    return _pack_pair(y[:, 0:w], y[:, 2 * w:3 * w]), _pack_pair(y[:, w:2 * w], y[:, 3 * w:4 * w])


def _unpack_rows(ua, ub):
    ha, la = _unpack_pair(ua)
    hb, lb = _unpack_pair(ub)
    return jnp.concatenate([ha, hb, la, lb], axis=1)


def _moe_residual(x_ref, ga0, ga1, gb0, gb1, gt_ref, gate_row):
    gt = gt_ref[...]
    y = gt[:, 0:1] * _unpack_rows(ga0[...], gb0[...]) + gt[:, 1:2] * _unpack_rows(ga1[...], gb1[...])
    return x_ref[...] + gate_row * y


def _ada_kernel(c_ref, w_ref, b_ref, o_ref):
    c = c_ref[...]
    ca = c * jax.nn.sigmoid(c)
    hi = ca.astype(BF16)
    lo = (ca - hi.astype(F32)).astype(BF16)
    w = w_ref[0].astype(BF16)
    o_ref[0] = (jnp.dot(hi, w, preferred_element_type=F32) + jnp.dot(lo, w, preferred_element_type=F32)) + b_ref[0]


def _adaln(c, w_ada, b_ada):
    depth, d, n6 = w_ada.shape
    bsz = c.shape[0]
    rows = 8
    cp = jnp.zeros((rows, d), F32).at[:bsz].set(c)
    tn = 1536
    out = pl.pallas_call(
        _ada_kernel,
        grid=(depth, n6 // tn),
        in_specs=[
            pl.BlockSpec((rows, d), lambda l, j: (0, 0)),
            pl.BlockSpec((1, d, tn), lambda l, j: (l, 0, j)),
            pl.BlockSpec((1, 1, tn), lambda l, j: (l, 0, j)),
        ],
        out_specs=pl.BlockSpec((1, rows, tn), lambda l, j: (l, 0, j)),
        out_shape=jax.ShapeDtypeStruct((depth, rows, n6), F32),
        compiler_params=_params(("parallel", "parallel")),
        name="adaln",
    )(cp, w_ada, b_ada.reshape(depth, 1, n6))
    return out[:, :bsz].reshape(depth, bsz, 6, d)


def _in_kernel(*refs, n_moe):
    moe, (x_ref, mod_ref, g_ref, w_ref), rest = refs[:n_moe], refs[n_moe:n_moe + 4], refs[n_moe + 4:]
    hb = rest[-1]
    i = pl.program_id(0)

    @pl.when(i == 0)
    def _():
        hb[1] = jnp.zeros(hb.shape[1:], hb.dtype)

    of, ob = rest[1:-1] if n_moe else rest[:-1]
    outs = [(of, 0), (of, 1), (of, 2), (ob, 0), (of, 3), (ob, 1), (of, 4)]
    mod = mod_ref[0]
    prev = hb[(i + 1) % 2]
    width = PROJ_WIDTH
    tm = x_ref.shape[0]
    groups = [slice(r, r + tm // IN_GROUPS) for r in range(0, tm, tm // IN_GROUPS)]

    def prologue(rs):
        if n_moe:
            ga0, ga1, gb0, gb1, gt_ref, pmod_ref = moe
            gt = gt_ref[rs, :]
            y = gt[:, 0:1] * _unpack_rows(ga0[rs, :], gb0[rs, :]) + gt[:, 1:2] * _unpack_rows(ga1[rs, :], gb1[rs, :])
            x = x_ref[rs, :] + pmod_ref[0][5:6, :] * y
            rest[0][rs, :] = x
        else:
            x = x_ref[rs, :]
        hb[i % 2, rs, :] = (_rms(x, g_ref[...]) * (1.0 + mod[1:2, :]) + mod[0:1, :]).astype(BF16)

    per = -(-len(groups) // len(outs))
    for k, (o, j) in enumerate(outs):
        for rs in groups[k * per:(k + 1) * per]:
            prologue(rs)
        w = w_ref[0, :, k * width:(k + 1) * width].astype(BF16)
        o[:, j * width:(j + 1) * width] = jnp.dot(prev, w, preferred_element_type=F32).astype(o.dtype)


def _in_proj(x2, mod_l, g, w_in, layer, seq, moe=None):
    n, d = x2.shape
    n_out = w_in.shape[2]
    width = PROJ_WIDTH
    tm = ROW_TILE
    per_batch = seq // tm
    last = n // tm - 1
    cur = lambda i: jnp.minimum(i, last)
    row = lambda w: pl.BlockSpec((tm, w), lambda i: (cur(i), 0))
    mod_spec = pl.BlockSpec((1, 6, d), lambda i: (cur(i) // per_batch, 0, 0))
    in_specs, args, out_specs, out_shape = [], [], [], []
    if moe is not None:
        ga, gb, gates, mod_prev = moe
        second = pl.BlockSpec((tm, PACK_WIDTH), lambda i: (cur(i) + n // tm, 0))
        in_specs += [row(PACK_WIDTH), second, row(PACK_WIDTH), second, row(LANES), mod_spec]
        args += [ga, ga, gb, gb, gates, mod_prev]
        out_specs.append(row(d))
        out_shape.append(jax.ShapeDtypeStruct((n, d), F32))
    in_specs += [row(d), mod_spec, pl.BlockSpec((1, d), lambda i: (0, 0)),
                 pl.BlockSpec((1, d, n_out), lambda i: (layer, 0, 0), pipeline_mode=pl.Buffered(1))]
    args += [x2, mod_l, g.reshape(1, d), w_in]
    for parts, dt in ((5, F32), (2, BF16)):
        out_specs.append(pl.BlockSpec((tm, parts * width), lambda i: (jnp.maximum(i - 1, 0), 0)))
        out_shape.append(jax.ShapeDtypeStruct((n, parts * width), dt))
    outs = pl.pallas_call(
        functools.partial(_in_kernel, n_moe=0 if moe is None else 6),
        grid=(n // tm + 1,),
        in_specs=in_specs,
        out_specs=out_specs,
        out_shape=out_shape,
        scratch_shapes=[pltpu.VMEM((2, tm, d), BF16)],
        compiler_params=_params(("arbitrary",)),
        name="in_proj",
    )(*args)
    return (x2, outs) if moe is None else (outs[0], outs[1:])


def _attn_bias(n_heads):
    qi = np.arange(ATTN_BLOCK)[:, None]
    kj = np.arange(2 * ATTN_BLOCK)[None, :]
    dist = ATTN_BLOCK + qi - kj
    valid = (dist >= 0) & (dist <= ATTN_BLOCK)
    slopes = np.exp2(-ALIBI_MAX_EXP * np.arange(1, n_heads + 1, dtype=np.float32) / n_heads)
    out = np.empty((2, len(DILATIONS), n_heads, ATTN_BLOCK, 2 * ATTN_BLOCK), np.float32)
    for p, dil in enumerate(DILATIONS):
        b = -slopes[:, None, None] * (dist * dil).astype(np.float32)[None] * np.float32(LOG2E)
        out[0, p] = np.where(valid[None], b, -np.inf)
        out[1, p] = np.where((valid & (kj >= ATTN_BLOCK))[None], b, -np.inf)
    return jnp.asarray(out)


def _attn_kernel(q_ref, k_ref, v_ref, bias_ref, o_ref, kcar, vcar, pbuf, vbuf, oscr, mscr, dscr):
    n = pl.program_id(2)
    sb = q_ref.shape[0]
    is_lo = lax.broadcasted_iota(jnp.int32, (ATTN_BLOCK, LANES), 1) < ATTN_HEAD_DIM
    scale = ATTN_HEAD_DIM ** -0.5 * LOG2E

    @pl.when(n == 0)
    def _():
        kcar[...] = jnp.zeros_like(kcar)
        vcar[...] = jnp.zeros_like(vcar)

    pbuf[...] = jnp.zeros_like(pbuf)
    vbuf[...] = jnp.zeros_like(vbuf)
    lo2 = lax.broadcasted_iota(jnp.int32, (2 * ATTN_BLOCK, LANES), 1) < ATTN_HEAD_DIM
    zero = jnp.zeros((2 * ATTN_BLOCK, LANES), BF16)
    den_lo = lo2.astype(BF16)
    den_hi = jnp.logical_not(lo2).astype(BF16)

    for p, dil in enumerate(DILATIONS):
        nb = (sb // ATTN_BLOCK) // dil

        def rows_of(it, dil=dil, nb=nb):
            start = (it % nb) * (ATTN_BLOCK * dil) + it // nb
            return pl.ds(start, ATTN_BLOCK, stride=dil) if dil > 1 else pl.ds(pl.multiple_of(start, ATTN_BLOCK), ATTN_BLOCK)

        def pv(it, probs, vaug, p=p, rows_of=rows_of):
            rows = rows_of(it)
            of = jnp.dot(probs, vaug, preferred_element_type=F32)
            oscr[p, rows, :] = of[:, :LANES]
            dscr[p, rows, :] = of[:, LANES:]

        def qk(it, p=p, nb=nb, rows_of=rows_of):
            r = it // nb
            rows = rows_of(it)
            crow = pl.ds(pl.multiple_of(r * ATTN_BLOCK, ATTN_BLOCK), ATTN_BLOCK)
            qb = q_ref[rows, :] * scale
            kc = k_ref[rows, :].astype(BF16)
            vc = v_ref[rows, :].astype(BF16)
            kcat = jnp.concatenate([kcar[p, crow, :], kc], axis=0)
            vcat = jnp.concatenate([vcar[p, crow, :], vc], axis=0)
            vaug = jnp.concatenate([jnp.concatenate([jnp.where(lo2, vcat, zero), den_lo], axis=1),
                                    jnp.concatenate([jnp.where(lo2, zero, vcat), den_hi], axis=1)], axis=0)
            kcar[p, crow, :] = kc
            vcar[p, crow, :] = vc
            first = jnp.logical_and(n == 0, it % nb == 0).astype(jnp.int32)
            probs, acc_m = [], None
            for hh in range(2):
                sel = is_lo if hh == 0 else jnp.logical_not(is_lo)
                qm = jnp.where(sel, qb, 0.0).astype(BF16)
                s = _dot_nt(qm, kcat) + bias_ref[first, p, hh]
                m = jnp.max(s, axis=-1, keepdims=True)
                probs.append(jnp.exp2(s - m).astype(BF16))
                mb = jnp.broadcast_to(m, (ATTN_BLOCK, LANES))
                acc_m = mb if hh == 0 else jnp.where(sel, mb, acc_m)
            mscr[p, rows, :] = acc_m
            return jnp.concatenate(probs, axis=1), vaug

        def body(g, carry, pv=pv, qk=qk):
            it0 = g * ATTN_UNROLL
            pend = (pbuf[...], vbuf[...])
            for u in range(ATTN_UNROLL):
                cur = qk(it0 + u)
                pv(jnp.maximum(it0 + u - 1, 0), *pend)
                pend = cur
            pbuf[...] = pend[0]
            vbuf[...] = pend[1]
            return carry

        n_it = sb // ATTN_BLOCK
        lax.fori_loop(0, n_it // ATTN_UNROLL, body, 0)
        pv(n_it - 1, pbuf[...], vbuf[...])

    ct = 256

    def combine(t, carry):
        rs = pl.ds(pl.multiple_of(t * ct, ct), ct)
        m0, m1, m2 = mscr[0, rs, :], mscr[1, rs, :], mscr[2, rs, :]
        mx = jnp.maximum(jnp.maximum(m0, m1), m2)
        w0, w1, w2 = jnp.exp2(m0 - mx), jnp.exp2(m1 - mx), jnp.exp2(m2 - mx)
        num = w0 * oscr[0, rs, :] + w1 * oscr[1, rs, :] + w2 * oscr[2, rs, :]
        den = w0 * dscr[0, rs, :] + w1 * dscr[1, rs, :] + w2 * dscr[2, rs, :]
        o_ref[rs, :] = (num / den).astype(o_ref.dtype)
        return carry

    lax.fori_loop(0, sb // ct, combine, 0)


def _attention(proj, bias, bsz, seq):
    n, width = proj.shape[0], PROJ_WIDTH
    sb = ATTN_SUPER
    nsb = seq // sb
    pairs = width // LANES
    npat = len(DILATIONS)
    col = lambda j: pl.BlockSpec((sb, LANES), lambda b, h, t: (b * nsb + t, j * pairs + h))
    blk = col(0)
    return pl.pallas_call(
        _attn_kernel,
        grid=(bsz, pairs, nsb),
        in_specs=[col(0), col(1), col(2),
                  pl.BlockSpec((2, npat, 2, ATTN_BLOCK, 2 * ATTN_BLOCK), lambda b, h, t: (0, 0, h, 0, 0))],
        out_specs=blk,
        out_shape=jax.ShapeDtypeStruct((n, width), BF16),
        scratch_shapes=[
            pltpu.VMEM((npat, max(DILATIONS) * ATTN_BLOCK, LANES), BF16),
            pltpu.VMEM((npat, max(DILATIONS) * ATTN_BLOCK, LANES), BF16),
            pltpu.VMEM((ATTN_BLOCK, 4 * ATTN_BLOCK), BF16),
            pltpu.VMEM((4 * ATTN_BLOCK, 2 * LANES), BF16),
            pltpu.VMEM((npat, sb, LANES), F32),
            pltpu.VMEM((npat, sb, LANES), F32),
            pltpu.VMEM((npat, sb, LANES), F32),
        ],
        compiler_params=_params(("parallel", "parallel", "arbitrary")),
        name="dilated_attn",
    )(proj, proj, proj, bias)


def _hgrn_consts():
    c = HGRN_CHUNK
    t = np.arange(c)[:, None]
    u = np.arange(c)[None, :]
    low, masks = [], []
    m = c // 2
    while m >= 1:
        mid = (t // (2 * m)) * (2 * m) + m
        is_q = t >= mid
        if m in (4, 2):
            low.append((is_q & (u >= mid) & (u <= t)) | (~is_q & (u > t) & (u < mid)))
        masks.append(((t // (2 * m)) == (u // (2 * m))) & (t % (2 * m) >= m) & (u % (2 * m) < m))
        m //= 2
    masks.append(t == u)
    f = lambda a: np.asarray(a, np.float32)
    return (jnp.asarray(f(u <= t), BF16), jnp.asarray(f(np.concatenate(low, axis=0)), BF16),
            jnp.asarray(f(np.broadcast_to(t % 2 == 1, (c, c)))), jnp.asarray(f(np.stack(masks))))


def _hgrn_kernel(q_ref, z_ref, v_ref, g_ref, lb_ref, gn_ref, tri_ref, low_ref, odd_ref, msk_ref, o_ref, st):
    c = HGRN_CHUNK

    @pl.when(pl.program_id(2) == 0)
    def _():
        st[...] = jnp.zeros_like(st)

    lb = jnp.maximum(lb_ref[...], 0.0)
    log_lb = jnp.log(lb)
    log_1m = jnp.log1p(-lb)
    one_m = 1.0 - lb
    gn = gn_ref[...]
    n_levels = msk_ref.shape[0] - 1
    coarse = [c >> (i + 1) for i in range(n_levels) if (c >> (i + 1)) >= 8]

    def chunk(ci):
        rs = pl.ds(pl.multiple_of(ci * c, c), c)
        z = z_ref[rs, :]
        qb = q_ref[rs, :]
        vb = v_ref[rs, :]
        ez = jnp.exp(-jnp.abs(z))
        ls = jnp.minimum(z, 0.0) - jnp.log(1.0 + ez)
        b2 = log_1m + ls
        lf = jnp.maximum(log_lb, b2) + jnp.log(1.0 + jnp.exp(-jnp.abs(log_lb - b2)))
        rz = 1.0 / (1.0 + ez)
        kb = (one_m * jnp.where(z > 0.0, ez * rz, rz)).astype(BF16)
        l2 = lf * LOG2E
        hi = l2.astype(BF16)
        hl = jnp.concatenate([hi, (l2 - hi.astype(F32)).astype(BF16)], axis=1)
        yield
        bb = jnp.dot(tri_ref[...], hl, preferred_element_type=F32)
        b = bb[:, :HGRN_EXPAND] + bb[:, HGRN_EXPAND:]
        dl = jnp.dot(low_ref[...], hl, preferred_element_type=F32)
        dl = dl[:, :HGRN_EXPAND] + dl[:, HGRN_EXPAND:]
        yield

        def level(d):
            e = jnp.exp2(d).astype(BF16)
            return _dot_nt(qb * e, kb * e)

        a = msk_ref[n_levels] * _dot_nt(qb, kb)
        for i in range(n_levels):
            m = c >> (i + 1)
            if m >= 8:
                parts = []
                for r0 in range(0, c, 2 * m):
                    mid = b[r0 + m - 1:r0 + m, :]
                    parts += [mid - b[r0:r0 + m, :], b[r0 + m:r0 + 2 * m, :] - mid]
                d = jnp.concatenate(parts, axis=0)
            elif m > 1:
                d = dl[(i - len(coarse)) * c:(i - len(coarse) + 1) * c]
            else:
                d = l2 * odd_ref[...]
            a = a + msk_ref[i] * level(d)
            if i % 3 == 2:
                yield
        b_last = b[c - 1:c, :]
        s_t = st[...]
        o = _dot_nt(qb * jnp.exp2(b).astype(BF16), s_t.astype(BF16)) + jnp.dot(a.astype(BF16), vb, preferred_element_type=F32)
        st[...] = jnp.exp2(b_last) * s_t + _dot_tn(vb, kb * jnp.exp2(b_last - b).astype(BF16))
        yield
        g = g_ref[rs, :]
        o_ref[rs, :] = (_rms(o, gn) * (g * jax.nn.sigmoid(g))).astype(o_ref.dtype)

    def group(gi, carry):
        _round_robin([chunk(gi * HGRN_UNROLL + u) for u in range(HGRN_UNROLL)])
        return carry

    _loop(q_ref.shape[0] // (c * HGRN_UNROLL), group)


def _hgrn(pf, pb, lb, gn, consts, bsz, seq):
    n, width = pf.shape[0], PROJ_WIDTH
    heads = width // HGRN_EXPAND
    ts = HGRN_TILE
    nt = seq // ts
    col = lambda j: pl.BlockSpec((ts, HGRN_EXPAND), lambda b, h, t: (b * nt + t, j * heads + h))
    blk = col(0)
    vec = pl.BlockSpec((1, HGRN_EXPAND), lambda b, h, t: (0, h))
    const = lambda a: pl.BlockSpec(a.shape, lambda b, h, t: (0,) * a.ndim)
    return pl.pallas_call(
        _hgrn_kernel,
        grid=(bsz, heads, nt),
        in_specs=[col(0), col(3), col(1), col(4), vec, vec] + [const(a) for a in consts],
        out_specs=blk,
        out_shape=jax.ShapeDtypeStruct((n, width), BF16),
        scratch_shapes=[pltpu.VMEM((HGRN_EXPAND, HGRN_EXPAND), F32)],
        compiler_params=_params(("parallel", "parallel", "arbitrary")),
        name="hgrn2",
    )(pb, pf, pb, pf, lb.reshape(1, width), gn.reshape(1, width), *consts)


def _route_math(lg):
    row = lax.broadcasted_iota(jnp.int32, (EXPERTS_PER_GROUP, lg.shape[1]), 0).astype(F32)
    none = float(EXPERTS_PER_GROUP)

    def first_argmax(vals):
        mx = jnp.max(vals, axis=0, keepdims=True)
        return mx, jnp.min(jnp.where(vals == mx, row, none), axis=0, keepdims=True)

    g = lg[0:EXPERTS_PER_GROUP, :]
    is_group = row < N_GROUPS
    gmax, gidx = first_argmax(jnp.where(is_group, g, -jnp.inf))
    gprob = 1.0 / jnp.sum(jnp.where(is_group, jnp.exp(g - gmax), 0.0), axis=0, keepdims=True)
    el = lg[EXPERTS_PER_GROUP:2 * EXPERTS_PER_GROUP, :]
    for k in range(1, N_GROUPS):
        el = jnp.where(gidx == k, lg[(k + 1) * EXPERTS_PER_GROUP:(k + 2) * EXPERTS_PER_GROUP, :], el)
    t1, i1 = first_argmax(el)
    t2, i2 = first_argmax(jnp.where(row == i1, -jnp.inf, el))
    ex = jnp.exp(t2 - t1)
    base = gidx * EXPERTS_PER_GROUP
    eids = jnp.concatenate([base + i1, base + i2], axis=0).astype(jnp.int32)
    gates = jnp.concatenate([gprob / (1.0 + ex), gprob * ex / (1.0 + ex)], axis=0)
    return eids, gates


def _out_kernel(a_ref, r_ref, x_ref, mod_ref, ag_ref, g2_ref, wo_ref, wr_ref, br_ref, xo_ref, ha_ref, hb_ref, e_ref, gt_ref, cnt_ref):
    mod = mod_ref[0]
    mix = jnp.concatenate([_rms(a_ref[...].astype(F32), ag_ref[...]).astype(BF16), r_ref[...]], axis=1)
    xn = x_ref[...] + mod[2:3, :] * jnp.dot(mix, wo_ref[0].astype(BF16), preferred_element_type=F32)
    xo_ref[...] = xn
    h2 = _rms(xn, g2_ref[...]) * (1.0 + mod[4:5, :]) + mod[3:4, :]
    ha_ref[...], hb_ref[...] = _pack_rows(h2)
    eids, gates = _route_math(_dot_nt(wr_ref[...], h2.astype(BF16)) + br_ref[...])
    e_ref[...] = eids
    eio = lax.broadcasted_iota(jnp.int32, (N_EXPERTS, eids.shape[1]), 0)
    hit = jnp.logical_or(eids[0:1, :] == eio, eids[1:2, :] == eio).astype(F32)

    @pl.when(pl.program_id(0) == 0)
    def _():
        cnt_ref[...] = jnp.zeros_like(cnt_ref)

    cnt_ref[...] += jnp.sum(hit, axis=1, keepdims=True)
    pad = jnp.zeros((LANES - TOP_K, LANES), F32)
    for k in range(h2.shape[0] // LANES):
        tile = jnp.concatenate([gates[:, k * LANES:(k + 1) * LANES], pad], axis=0)
        gt_ref[k * LANES:(k + 1) * LANES, :] = tile.T


def _out_proj(attn, rec, x2, mod_l, ag, g2, w_out, layer, wr_bf, br, seq):
    n, d = x2.shape
    half = attn.shape[1]
    tm = OUT_TILE
    per_batch = seq // tm
    row = lambda w: pl.BlockSpec((tm, w), lambda i: (i, 0))
    full = lambda a: pl.BlockSpec(a.shape, lambda i: (0,) * a.ndim, pipeline_mode=pl.Buffered(1))
    ag2, g22 = ag.reshape(1, half), g2.reshape(1, d)
    return pl.pallas_call(
        _out_kernel,
        grid=(n // tm,),
        in_specs=[row(half), row(half), row(d), pl.BlockSpec((1, 6, d), lambda i: (i // per_batch, 0, 0)),
                  full(ag2), full(g22), pl.BlockSpec((1, d, d), lambda i: (layer, 0, 0), pipeline_mode=pl.Buffered(1)),
                  full(wr_bf), full(br)],
        out_specs=[row(d), row(PACK_WIDTH), row(PACK_WIDTH), pl.BlockSpec((TOP_K, tm), lambda i: (0, i)), row(LANES),
                   pl.BlockSpec((N_EXPERTS, LANES), lambda i: (0, 0))],
        out_shape=[jax.ShapeDtypeStruct((n, d), F32), jax.ShapeDtypeStruct((n, PACK_WIDTH), U32),
                   jax.ShapeDtypeStruct((n, PACK_WIDTH), U32), jax.ShapeDtypeStruct((TOP_K, n), jnp.int32),
                   jax.ShapeDtypeStruct((n, LANES), F32), jax.ShapeDtypeStruct((N_EXPERTS, LANES), F32)],
        compiler_params=_params(("arbitrary",)),
        name="out_proj",
    )(attn, rec, x2, mod_l, ag2, g22, w_out, wr_bf, br)


def _plan_kernel(e_ref, cnt_ref, ut_ref, pos_ref, be_ref, bv_ref, bf_ref, bn_ref, nu_ref, off, run):
    i = pl.program_id(0)
    t = e_ref.shape[1]
    eio = lax.broadcasted_iota(jnp.int32, (N_EXPERTS, t), 0)
    oh1 = e_ref[0:1, :] == eio
    oh2 = e_ref[1:2, :] == eio
    oh = jnp.logical_or(oh1, oh2).astype(F32)

    @pl.when(i == 0)
    def _():
        counts = cnt_ref[...]
        nb = jnp.floor((counts + (MOE_TILE - 1)) * (1.0 / MOE_TILE))
        er = lax.broadcasted_iota(jnp.int32, (N_EXPERTS, N_EXPERTS), 0)
        ec = lax.broadcasted_iota(jnp.int32, (N_EXPERTS, N_EXPERTS), 1)
        cum_excl = jnp.dot((ec < er).astype(BF16), nb.astype(BF16), preferred_element_type=F32)
        off[...] = cum_excl * MOE_TILE
        run[...] = jnp.zeros_like(run)
        ce, ci, cn = cum_excl[:, 0:1], (cum_excl + nb)[:, 0:1], counts[:, 0:1]
        b = lax.broadcasted_iota(jnp.int32, (N_EXPERTS, be_ref.shape[1]), 1).astype(F32)
        member = jnp.logical_and(b >= ce, b < ci)
        be = jnp.minimum(jnp.sum((b >= ci).astype(F32), axis=0, keepdims=True), N_EXPERTS - 1.0)
        bv = jnp.sum(jnp.where(member, jnp.minimum(cn - (b - ce) * MOE_TILE, float(MOE_TILE)), 0.0), axis=0, keepdims=True)
        be_ref[...] = be.astype(jnp.int32)
        bv_ref[...] = bv.astype(jnp.int32)
        has = nb[:, 0:1] > 0.0
        eid = lax.broadcasted_iota(jnp.int32, b.shape, 0).astype(F32)
        bf_ref[...] = jnp.sum(jnp.where(jnp.logical_and(b == ce, has), 1.0, 0.0), axis=0, keepdims=True).astype(jnp.int32)
        nxt = jnp.min(jnp.where(jnp.logical_and(eid > be, has), eid, float(N_EXPERTS)), axis=0, keepdims=True)
        bn_ref[...] = jnp.where(nxt < N_EXPERTS, nxt, -1.0).astype(jnp.int32)
        nu_ref[...] = jnp.broadcast_to(ci[N_EXPERTS - 1:N_EXPERTS, :], nu_ref.shape).astype(jnp.int32)

    span = ut_ref.shape[0]
    ohb = oh.astype(BF16)
    base = off[:, 0:1] + run[:, 0:1] - 1.0
    parts = []
    for s in range(0, t, span):
        c = jnp.dot(ohb[:, s:s + span], ut_ref[...], preferred_element_type=F32)
        parts.append(base + c)
        base = base + c[:, span - 1:span]
    r = jnp.concatenate(parts, axis=1)
    pos_ref[0:1, :] = jnp.sum(jnp.where(oh1, r, 0.0), axis=0, keepdims=True).astype(jnp.int32)
    pos_ref[1:2, :] = jnp.sum(jnp.where(oh2, r, 0.0), axis=0, keepdims=True).astype(jnp.int32)
    run[...] += jnp.sum(oh, axis=1, keepdims=True)


def _plan(expert_ids_t, counts, n_rows):
    n = expert_ids_t.shape[1]
    t = PLAN_TILE
    n_blocks = n_rows // MOE_TILE
    assert n_blocks <= PLAN_LANES and TOP_K == 2
    ut = jnp.asarray(np.triu(np.ones((PLAN_SPAN, PLAN_SPAN), np.float32)), BF16)
    tab = jax.ShapeDtypeStruct((1, PLAN_LANES), jnp.int32)
    const = lambda shape: pl.BlockSpec(shape, lambda i: (0, 0))
    pos, be, bv, bf, bn, nu = pl.pallas_call(
        _plan_kernel,
        grid=(n // t,),
        in_specs=[pl.BlockSpec((TOP_K, t), lambda i: (0, i)), const((N_EXPERTS, LANES)), const((PLAN_SPAN, PLAN_SPAN))],
        out_specs=[pl.BlockSpec((TOP_K, t), lambda i: (0, i))] + [const((1, PLAN_LANES))] * 4 + [const((1, LANES))],
        out_shape=[jax.ShapeDtypeStruct((TOP_K, n), jnp.int32), tab, tab, tab, tab, jax.ShapeDtypeStruct((1, LANES), jnp.int32)],
        scratch_shapes=[pltpu.VMEM((N_EXPERTS, LANES), F32)] * 2,
        compiler_params=_params(("arbitrary",)),
        name="plan",
    )(expert_ids_t, counts, ut)
    return pos, (be.reshape(-1), bv.reshape(-1), bf.reshape(-1), bn.reshape(-1), nu.reshape(-1)[:1])


def _sc_mesh():
    return plsc.VectorSubcoreMesh(core_axis_name="c", subcore_axis_name="s")


def _sc_scatter_rows(x, pos, n_rows):
    n, d = x.shape
    w = SC_WINDOW

    @functools.partial(pl.kernel, out_type=jax.ShapeDtypeStruct((n_rows, d), x.dtype), mesh=_sc_mesh(), scratch_types=[])
    def scatter(x_hbm, p_hbm, o_hbm):
        def body(x_vmem, i_vmem):
            for k in range(TOP_K):
                pltpu.sync_copy(x_vmem, o_hbm.at[i_vmem.at[k]])

        pltpu.emit_pipeline(
            body,
            grid=(n // w,),
            in_specs=[pl.BlockSpec((w, d), lambda i: (i, 0)), pl.BlockSpec((TOP_K, w), lambda i: (0, i))],
            out_specs=[],
            core_axis_name=("c", "s"),
            dimension_semantics=(pltpu.PARALLEL,),
        )(x_hbm, p_hbm)

    return scatter(x, pos)


def _sc_gather_rows(y, pos):
    m = pos.shape[0] * pos.shape[1]
    d = y.shape[1]
    w = SC_WINDOW

    @functools.partial(pl.kernel, out_type=jax.ShapeDtypeStruct((m, d), y.dtype), mesh=_sc_mesh(), scratch_types=[])
    def gather(y_hbm, p_hbm, o_hbm):
        def body(i_vmem, o_vmem):
            pltpu.sync_copy(y_hbm.at[i_vmem.at[0]], o_vmem)

        pltpu.emit_pipeline(
            body,
            grid=(m // w,),
            in_specs=[pl.BlockSpec((1, w), lambda i: (0, i))],
            out_specs=[pl.BlockSpec((w, d), lambda i: (i, 0))],
            core_axis_name=("c", "s"),
            dimension_semantics=(pltpu.PARALLEL,),
        )(p_hbm, o_hbm)

    return gather(y, pos.reshape(1, m))


def _expert_kernel(be_ref, bv_ref, bf_ref, bn_ref, nu_ref, xa_ref, xb_ref, w1_hbm, w3_hbm, w2_hbm, ya_ref, yb_ref,
                   wf1, wf3, wf2, w1b, w3b, w2b, wsem, par, *, layer):
    i = pl.program_id(0)
    tb = xa_ref.shape[0]
    half = tb // 2

    def fetch(e, slot):
        return (pltpu.make_async_copy(w1_hbm.at[layer, e], wf1.at[slot], wsem.at[slot, 0]),
                pltpu.make_async_copy(w3_hbm.at[layer, e], wf3.at[slot], wsem.at[slot, 1]),
                pltpu.make_async_copy(w2_hbm.at[layer, e], wf2.at[slot], wsem.at[slot, 2]))

    def ffn(rows):
        live = lax.broadcasted_iota(jnp.int32, (rows, 1), 0) < bv_ref[i]
        x = jnp.where(live, _unpack_rows(xa_ref[0:rows, :], xb_ref[0:rows, :]), 0.0).astype(BF16)
        y = None
        hc = w1b.shape[1] // FFN_SPLIT
        for c in range(FFN_SPLIT):
            cs = slice(c * hc, (c + 1) * hc)
            a = jnp.dot(x, w1b[:, cs], preferred_element_type=F32)
            b = jnp.dot(x, w3b[:, cs], preferred_element_type=F32)
            hm = (a * jax.nn.sigmoid(a) * b).astype(BF16)
            part = jnp.dot(hm, w2b[cs, :], preferred_element_type=F32)
            y = part if y is None else y + part
        ya_ref[0:rows, :], yb_ref[0:rows, :] = _pack_rows(y)

    @pl.when(i == 0)
    def _():
        par[0] = 0
        for c in fetch(be_ref[0], 0):
            c.start()

    @pl.when(i < nu_ref[0])
    def _():
        @pl.when(bf_ref[i] == 1)
        def _():
            slot = par[0]
            for c in fetch(be_ref[i], slot):
                c.wait()
            w1b[...] = wf1[slot].astype(BF16)
            w3b[...] = wf3[slot].astype(BF16)
            w2b[...] = wf2[slot].astype(BF16)

            @pl.when(bn_ref[i] >= 0)
            def _():
                for c in fetch(bn_ref[i], 1 - slot):
                    c.start()

            par[0] = 1 - slot

        @pl.when(bv_ref[i] > half)
        def _():
            ffn(tb)

        @pl.when(bv_ref[i] <= half)
        def _():
            ffn(half)
            ya_ref[half:, :] = jnp.zeros((tb - half, ya_ref.shape[1]), ya_ref.dtype)
            yb_ref[half:, :] = jnp.zeros((tb - half, yb_ref.shape[1]), yb_ref.dtype)

    @pl.when(i >= nu_ref[0])
    def _():
        ya_ref[...] = jnp.zeros_like(ya_ref)
        yb_ref[...] = jnp.zeros_like(yb_ref)


def _experts(xa, xb, w1, w3, w2, layer, tables):
    n_rows = xa.shape[0]
    d, de = w1.shape[-2:]
    tb = MOE_TILE
    xspec = pl.BlockSpec((tb, PACK_WIDTH), lambda i, be, bv, bf, bn, nu: (jnp.minimum(i, nu[0] - 1), 0))
    yspec = pl.BlockSpec((tb, PACK_WIDTH), lambda i, be, bv, bf, bn, nu: (i, 0))
    anyspec = pl.BlockSpec(memory_space=pl.ANY)
    grid_spec = pltpu.PrefetchScalarGridSpec(
        num_scalar_prefetch=5,
        grid=(n_rows // tb,),
        in_specs=[xspec, xspec, anyspec, anyspec, anyspec],
        out_specs=[yspec, yspec],
        scratch_shapes=[pltpu.VMEM((2, d, de), F32), pltpu.VMEM((2, d, de), F32), pltpu.VMEM((2, de, d), F32),
                        pltpu.VMEM((d, de), BF16), pltpu.VMEM((d, de), BF16), pltpu.VMEM((de, d), BF16),
                        pltpu.SemaphoreType.DMA((2, 3)), pltpu.SMEM((1,), jnp.int32)],
    )
    rows = jax.ShapeDtypeStruct((n_rows, PACK_WIDTH), U32)
    return pl.pallas_call(
        functools.partial(_expert_kernel, layer=layer),
        grid_spec=grid_spec,
        out_shape=[rows, rows],
        compiler_params=_params(("arbitrary",)),
        name="experts",
    )(*tables, xa, xb, w1, w3, w2)


def _final_kernel(ga0, ga1, gb0, gb1, gt_ref, mod_ref, x_ref, fg_ref, o_ref):
    o_ref[...] = _rms(_moe_residual(x_ref, ga0, ga1, gb0, gb1, gt_ref, mod_ref[0][5:6, :]), fg_ref[...])


def _final(x2, ga, gb, gates, mod_l, final_g, seq):
    n, d = x2.shape
    tm = ROW_TILE
    per_batch = seq // tm
    row = lambda w: pl.BlockSpec((tm, w), lambda i: (i, 0))
    second = pl.BlockSpec((tm, PACK_WIDTH), lambda i: (i + n // tm, 0))
    return pl.pallas_call(
        _final_kernel,
        grid=(n // tm,),
        in_specs=[row(PACK_WIDTH), second, row(PACK_WIDTH), second, row(LANES),
                  pl.BlockSpec((1, 6, d), lambda i: (i // per_batch, 0, 0)), row(d), pl.BlockSpec((1, d), lambda i: (0, 0))],
        out_specs=row(d),
        out_shape=jax.ShapeDtypeStruct((n, d), F32),
        compiler_params=_params(("parallel",)),
        name="final_norm",
    )(ga, ga, gb, gb, gates, mod_l, x2, final_g.reshape(1, d))


def kernel(x, c, w_ada, b_ada, norm1_g, w_in, attn_norm_g, hgrn_lb_logits, hgrn_norm_g, w_out, norm2_g, router_group_w, router_group_b, router_expert_w, router_expert_b, moe_w1, moe_w3, moe_w2, final_g):
    bsz, seq, d = x.shape
    depth = w_ada.shape[0]
    n = bsz * seq
    attn_dim = attn_norm_g.shape[1]
    n_heads = attn_dim // ATTN_HEAD_DIM

    lb_w = jax.nn.softmax(hgrn_lb_logits.astype(F32), axis=0)
    lower_bounds = jnp.cumsum(lb_w, axis=0) - lb_w[0:1]
    mod = _adaln(c, w_ada, b_ada)
    bias = _attn_bias(n_heads)
    hconsts = _hgrn_consts()
    n_rows = n * TOP_K + N_EXPERTS * MOE_TILE

    x2 = x.reshape(n, d)
    moe = None
    for layer in range(depth):
        x2, (pf, pb) = _in_proj(x2, mod[layer], norm1_g[layer], w_in, layer, seq, moe)
        attn = _attention(pf, bias, bsz, seq)
        rec = _hgrn(pf, pb, lower_bounds[layer], hgrn_norm_g[layer], hconsts, bsz, seq)
        gpad = EXPERTS_PER_GROUP - N_GROUPS
        wr = jnp.concatenate([router_group_w[layer].T, jnp.zeros((gpad, d), F32), router_expert_w[layer].T], axis=0)
        br = jnp.concatenate([router_group_b[layer], jnp.zeros((gpad,), F32), router_expert_b[layer]]).reshape(-1, 1)
        x2, ha, hb, eids, gates, counts = _out_proj(attn, rec, x2, mod[layer], attn_norm_g[layer], norm2_g[layer],
                                            w_out, layer, wr.astype(BF16), br, seq)
        pos, tables = _plan(eids, counts, n_rows)
        ya, yb = _experts(_sc_scatter_rows(ha, pos, n_rows), _sc_scatter_rows(hb, pos, n_rows),
                          moe_w1, moe_w3, moe_w2, layer, tables)
        moe = (_sc_gather_rows(ya, pos), _sc_gather_rows(yb, pos), gates, mod[layer])
    return _final(x2, moe[0], moe[1], moe[2], moe[3], final_g, seq).reshape(bsz, seq, d)
```
